```python
import jax, jax.numpy as jnp
from jax import lax
import numpy as np

D_MODEL = 1024
BATCH = 8
SEQ = 2048
DEPTH = 4
DEC_BATCH = 128
DEC_SEQ = 8
PAST_LEN = 16384
PAGE_SIZE = 128

N_MIXERS = 2
N_A = (DEPTH + 1) // 2
N_B = DEPTH // 2
CHUNK = 128
SGU_WIDTH = 2 * D_MODEL
SGU_GROUPS = 8
SGU_GROUP_DIM = SGU_WIDTH // SGU_GROUPS
RET_HEADS = 8
RET_DK = D_MODEL // RET_HEADS
RET_DV = 2 * RET_DK
RET_CHUNK = 128
ROPE_BASE = 10000.0
N_EXPERTS = 32
TOP_K = 4
D_EXPERT = D_MODEL
SWIGLU_ALPHA = 1.702
SWIGLU_LIMIT = 7.0
MOE_BLOCK = 128
EPS = 1e-6

kernel_name = "hybrid_sgu_retention_moe_step"


def rms_norm(x, g):
    x32 = x.astype(jnp.float32)
    y = x32 * lax.rsqrt(jnp.mean(x32 * x32, axis=-1, keepdims=True) + EPS)
    return y.astype(x.dtype) * g


def layer_norm(x, g, b):
    x32 = x.astype(jnp.float32)
    mu = jnp.mean(x32, axis=-1, keepdims=True)
    xc = x32 - mu
    y = xc * lax.rsqrt(jnp.mean(xc * xc, axis=-1, keepdims=True) + EPS)
    return y.astype(x.dtype) * g + b


def sgu_mixer(h, w_in, ln_g, ln_b, w_s, b_s, w_out):
    bsz, L, _ = h.shape
    cl = min(L, CHUNK)
    z = jax.nn.gelu(h @ w_in, approximate=False)
    u, v = jnp.split(z, 2, axis=-1)
    v = layer_norm(v, ln_g, ln_b)
    causal = jnp.tril(jnp.ones((cl, cl), dtype=bool))
    ws = jnp.where(causal[None], w_s[:, :cl, :cl], 0.0).astype(v.dtype)
    vc = v.reshape(bsz, L // cl, cl, SGU_GROUPS, SGU_GROUP_DIM)
    mixed = jnp.einsum('gij,bnjgc->bnigc', ws, vc) + b_s[:, :cl].T[None, None, :, :, None]
    y = u * mixed.reshape(bsz, L, SGU_WIDTH)
    return y @ w_out, v


def ret_log_gamma():
    return jnp.log(1.0 - 2.0 ** (-5.0 - jnp.arange(RET_HEADS, dtype=jnp.float32)))


def rotary(x, pos):
    half = x.shape[-1] // 2
    inv = 1.0 / (ROPE_BASE ** jnp.linspace(0.0, 1.0, half, dtype=jnp.float32))
    ang = pos.astype(jnp.float32)[:, None] * inv[None, :]
    cos, sin = jnp.cos(ang), jnp.sin(ang)
    x1, x2 = x[..., :half], x[..., half:]
    return jnp.concatenate([x1 * cos - x2 * sin, x1 * sin + x2 * cos], axis=-1)


def retention_chunk(S, q, k, v):
    C = q.shape[2]
    idx = jnp.arange(C, dtype=jnp.float32)
    lg = ret_log_gamma()[:, None]
    diff = idx[:, None] - idx[None, :]
    decay = jnp.where(diff[None] >= 0, jnp.exp(lg[:, :, None] * jnp.maximum(diff, 0.0)[None]), 0.0)
    scores = jnp.einsum('bhid,bhjd->bhij', q, k) * decay[None]
    o = jnp.einsum('bhij,bhje->bhie', scores, v)
    o = o + jnp.einsum('bhid,bhde->bhie', q * jnp.exp(lg * (idx + 1.0))[None, :, :, None], S)
    k_dec = k * jnp.exp(lg * (C - 1.0 - idx))[None, :, :, None]
    S_new = jnp.exp(lg * C)[None, :, :, None] * S + jnp.einsum('bhjd,bhje->bhde', k_dec, v)
    return S_new, o


def retention_mixer(h, S0, pos, w_in, norm_g, w_out):
    bsz, L, _ = h.shape
    qd, vd = RET_HEADS * RET_DK, RET_HEADS * RET_DV
    proj = h @ w_in
    q, k, v, g = jnp.split(proj, [qd, 2 * qd, 2 * qd + vd], axis=-1)
    def heads(t, d):
        return t.reshape(bsz, L, RET_HEADS, d).transpose(0, 2, 1, 3).astype(jnp.float32)
    q = rotary(heads(q, RET_DK), pos)
    k = rotary(heads(k, RET_DK), pos) * (RET_DK ** -0.5)
    v = heads(v, RET_DV)
    cl = min(L, RET_CHUNK)
    n = L // cl
    def chunked(t):
        return t.reshape(bsz, RET_HEADS, n, cl, t.shape[-1]).transpose(2, 0, 1, 3, 4)
    S, o = lax.scan(lambda s, xs: retention_chunk(s, xs[0], xs[1], xs[2]),
                    S0.astype(jnp.float32), (chunked(q), chunked(k), chunked(v)))
    o = o.transpose(1, 2, 0, 3, 4).reshape(bsz, RET_HEADS, L, RET_DV)
    o = o * lax.rsqrt(jnp.mean(o * o, axis=-1, keepdims=True) + EPS)
    o = o.transpose(0, 2, 1, 3).reshape(bsz, L, vd).astype(h.dtype) * norm_g
    return (jax.nn.silu(g) * o) @ w_out, S


def moe(h, w_router, b_router, w_up, b_up, w_down, b_down):
    bsz, L, D = h.shape
    x = h.reshape(-1, D)
    T = x.shape[0]
    TK = T * TOP_K
    logits = (x @ w_router + b_router).astype(jnp.float32)
    top_val, top_idx = lax.top_k(logits, TOP_K)
    gates = jax.nn.softmax(top_val, axis=-1).astype(h.dtype)
    flat_e = top_idx.reshape(-1).astype(jnp.int32)
    order = jnp.argsort(flat_e).astype(jnp.int32)
    sorted_e = flat_e[order]
    counts = jnp.bincount(flat_e, length=N_EXPERTS)
    padded = (counts + MOE_BLOCK - 1) // MOE_BLOCK * MOE_BLOCK
    pad_end = jnp.cumsum(padded)
    pad_start = pad_end - padded
    start = jnp.cumsum(counts) - counts
    rank = jnp.arange(TK, dtype=jnp.int32) - start[sorted_e]
    dest_sorted = (pad_start[sorted_e] + rank).astype(jnp.int32)
    dest = jnp.zeros((TK,), jnp.int32).at[order].set(dest_sorted)
    n_blocks = -(-(TK + N_EXPERTS * (MOE_BLOCK - 1)) // MOE_BLOCK)
    n_rows = n_blocks * MOE_BLOCK
    row_tok = jnp.full((n_rows,), T, jnp.int32).at[dest].set(jnp.arange(TK, dtype=jnp.int32) // TOP_K)
    x_pad = jnp.concatenate([x, jnp.zeros((1, D), x.dtype)], axis=0)
    xb = x_pad[row_tok].reshape(n_blocks, MOE_BLOCK, D)
    blk_e = jnp.minimum(jnp.searchsorted(pad_end, jnp.arange(n_blocks) * MOE_BLOCK, side='right'),
                        N_EXPERTS - 1).astype(jnp.int32)
    def expert_block(args):
        xe, e = args
        z = xe @ w_up[e] + b_up[e]
        glu, lin = jnp.split(z, 2, axis=-1)
        glu = jnp.minimum(glu, SWIGLU_LIMIT)
        lin = jnp.clip(lin, -SWIGLU_LIMIT, SWIGLU_LIMIT)
        act = glu * jax.nn.sigmoid(SWIGLU_ALPHA * glu) * (lin + 1.0)
        return act @ w_down[e] + b_down[e]
    yb = lax.map(expert_block, (xb, blk_e)).reshape(n_rows, D)
    y = yb[dest].reshape(T, TOP_K, D)
    return jnp.einsum('tkd,tk->td', y, gates).reshape(bsz, L, D)


def trunk(x, c, pos, ret_state0, w_mod, b_mod, norm1_g, norm2_g, sgu_w_in, sgu_ln_g, sgu_ln_b,
          sgu_w_s, sgu_b_s, sgu_w_out, ret_w_in, ret_norm_g, ret_w_out, moe_w_router, moe_b_router,
          moe_w_up, moe_b_up, moe_w_down, moe_b_down, final_g):
    ret_states, sgu_rows = [], []
    cs = jax.nn.silu(c)
    for i in range(DEPTH):
        mod = cs @ w_mod[i] + b_mod[i]
        sh1, sc1, g1, sh2, sc2, g2 = jnp.split(mod[:, None, :], 6, axis=-1)
        h = rms_norm(x, norm1_g[i]) * (1.0 + sc1) + sh1
        j = i // N_MIXERS
        if i % N_MIXERS == 0:
            y, v_rows = sgu_mixer(h, sgu_w_in[j], sgu_ln_g[j], sgu_ln_b[j], sgu_w_s[j], sgu_b_s[j], sgu_w_out[j])
            sgu_rows.append(v_rows)
        else:
            y, s = retention_mixer(h, ret_state0[j], pos, ret_w_in[j], ret_norm_g[j], ret_w_out[j])
            ret_states.append(s.astype(x.dtype))
        x = x + g1 * y
        h = rms_norm(x, norm2_g[i]) * (1.0 + sc2) + sh2
        x = x + g2 * moe(h, moe_w_router[i], moe_b_router[i], moe_w_up[i], moe_b_up[i], moe_w_down[i], moe_b_down[i])
    return rms_norm(x, final_g), ret_states, sgu_rows


def setup_inputs(seed: int = 0) -> dict:
    key = jax.random.key(seed)
    ks = jax.random.split(key, 26)
    D, F = D_MODEL, D_EXPERT
    nrm = jax.random.normal
    f32 = jnp.float32
    ret_in = 2 * RET_HEADS * RET_DK + 2 * RET_HEADS * RET_DV
    return {
        'x_prompt': nrm(ks[0], (BATCH, SEQ, D), f32),
        'x_sample': nrm(ks[1], (DEC_BATCH, DEC_SEQ, D), f32),
        'c_prompt': nrm(ks[2], (BATCH, D), f32),
        'c_sample': nrm(ks[3], (DEC_BATCH, D), f32),
        'state_ret': 0.1 * nrm(ks[4], (N_B, DEC_BATCH, RET_HEADS, RET_DK, RET_DV), f32),
        'w_mod': 0.5 * D ** -0.5 * nrm(ks[5], (DEPTH, D, 6 * D), f32),
        'b_mod': 0.01 * nrm(ks[6], (DEPTH, 6 * D), f32),
        'norm1_g': 1.0 + 0.01 * nrm(ks[7], (DEPTH, D), f32),
        'norm2_g': 1.0 + 0.01 * nrm(ks[8], (DEPTH, D), f32),
        'sgu_w_in': D ** -0.5 * nrm(ks[9], (N_A, D, 2 * SGU_WIDTH), f32),
        'sgu_ln_g': 1.0 + 0.01 * nrm(ks[10], (N_A, SGU_WIDTH), f32),
        'sgu_ln_b': 0.01 * nrm(ks[11], (N_A, SGU_WIDTH), f32),
        'sgu_w_s': 0.5 * CHUNK ** -0.5 * nrm(ks[12], (N_A, SGU_GROUPS, CHUNK, CHUNK), f32),
        'sgu_b_s': 1.0 + 0.01 * nrm(ks[13], (N_A, SGU_GROUPS, CHUNK), f32),
        'sgu_w_out': SGU_WIDTH ** -0.5 * nrm(ks[14], (N_A, SGU_WIDTH, D), f32),
        'ret_w_in': D ** -0.5 * nrm(ks[15], (N_B, D, ret_in), f32),
        'ret_norm_g': 1.0 + 0.01 * nrm(ks[16], (N_B, RET_HEADS * RET_DV), f32),
        'ret_w_out': (RET_HEADS * RET_DV) ** -0.5 * nrm(ks[17], (N_B, RET_HEADS * RET_DV, D), f32),
        'moe_w_router': D ** -0.5 * nrm(ks[18], (DEPTH, D, N_EXPERTS), f32),
        'moe_b_router': 0.01 * nrm(ks[19], (DEPTH, N_EXPERTS), f32),
        'moe_w_up': D ** -0.5 * nrm(ks[20], (DEPTH, N_EXPERTS, D, 2 * F), f32),
        'moe_b_up': 0.01 * nrm(ks[21], (DEPTH, N_EXPERTS, 2 * F), f32),
        'moe_w_down': F ** -0.5 * nrm(ks[22], (DEPTH, N_EXPERTS, F, D), f32),
        'moe_b_down': 0.01 * nrm(ks[23], (DEPTH, N_EXPERTS, D), f32),
        'final_g': 1.0 + 0.01 * nrm(ks[24], (D,), f32),
    }


def reference(x_prompt, x_sample, c_prompt, c_sample, state_ret, w_mod, b_mod, norm1_g, norm2_g,
              sgu_w_in, sgu_ln_g, sgu_ln_b, sgu_w_s, sgu_b_s, sgu_w_out, ret_w_in, ret_norm_g, ret_w_out,
              moe_w_router, moe_b_router, moe_w_up, moe_b_up, moe_w_down, moe_b_down, final_g):
    pos_prompt = jnp.arange(SEQ, dtype=jnp.int32)
    pos_sample = PAST_LEN + jnp.arange(DEC_SEQ, dtype=jnp.int32)
    state0 = jnp.zeros((N_B, x_prompt.shape[0], RET_HEADS, RET_DK, RET_DV), x_prompt.dtype)
    y_prompt, ret_p, _ = trunk(x_prompt, c_prompt, pos_prompt, state0, w_mod, b_mod, norm1_g, norm2_g,
                               sgu_w_in, sgu_ln_g, sgu_ln_b, sgu_w_s, sgu_b_s, sgu_w_out, ret_w_in,
                               ret_norm_g, ret_w_out, moe_w_router, moe_b_router, moe_w_up, moe_b_up,
                               moe_w_down, moe_b_down, final_g)
    y_sample, ret_s, rows_s = trunk(x_sample, c_sample, pos_sample, state_ret, w_mod, b_mod, norm1_g, norm2_g,
                                    sgu_w_in, sgu_ln_g, sgu_ln_b, sgu_w_s, sgu_b_s, sgu_w_out, ret_w_in,
                                    ret_norm_g, ret_w_out, moe_w_router, moe_b_router, moe_w_up, moe_b_up,
                                    moe_w_down, moe_b_down, final_g)
    state_ret_prompt = jnp.stack(ret_p)
    state_ret_sample = jnp.stack(ret_s)
    sgu_v_sample = jnp.stack(rows_s)
    return (y_prompt, y_sample, state_ret_prompt, state_ret_sample, sgu_v_sample)
```

```python
import functools

import jax
import jax.numpy as jnp
from jax import lax
from jax.experimental import pallas as pl
from jax.experimental.pallas import tpu as pltpu

F32 = jnp.float32
BF16 = jnp.bfloat16
I32 = jnp.int32

D = 1024
NB_P, L_P = 8, 2048
NB_S, L_S = 128, 8
PAST = 16384
DEPTH = 4
T_P = NB_P * L_P
T_S = NB_S * L_S
T = T_P + T_S
SGU_W = 2 * D
SGU_G = 8
SGU_GD = SGU_W // SGU_G
CHUNK = 128
H = 8
DK = D // H
DV = 2 * DK
QD = H * DK
VD = H * DV
RET_IN = 2 * QD + 2 * VD
ROPE_BASE = 10000.0
NE = 32
TOPK = 4
FE = D
ALPHA = 1.702
LIMIT = 7.0
EPS = 1e-6

LANES = 128
TM = 256
NPT = T_P // TM
NST = T_S // TM
NT = NPT + NST
TPB = L_P // TM
BLK = 256
TK = T * TOPK
N_BLOCKS = -(-(TK + NE * (BLK - 1)) // BLK)
N_ROWS = N_BLOCKS * BLK
GROWS = 2048
SB = 16
VMEM_LIMIT = 56 * 1024 * 1024


def _cparams(n_axes):
    return pltpu.CompilerParams(dimension_semantics=("arbitrary",) * n_axes,
                                vmem_limit_bytes=VMEM_LIMIT)


def _rms(x, g):
    return (x * lax.rsqrt(jnp.mean(x * x, axis=-1, keepdims=True) + EPS)) * g


def _pick(is_p, p_ref, s_ref):
    return jnp.where(is_p, p_ref[0], s_ref[...])


def _mod_specs(j):
    return [
        pl.BlockSpec((1, 1, D), lambda t, j=j: (jnp.minimum(t // TPB, NB_P - 1), 0, j)),
        pl.BlockSpec((TM, D), lambda t, j=j: (jnp.maximum(t - NPT, 0), j)),
    ]


def _tile_spec(width):
    return pl.BlockSpec((TM, width), lambda t: (t, 0))


def _full_spec(shape):
    return pl.BlockSpec(shape, lambda *_: (0,) * len(shape))


def _mod_kernel(c_ref, w_ref, b_ref, o_ref):
    c = c_ref[...]
    cs = (c * jax.nn.sigmoid(c)).astype(BF16)
    o_ref[0] = jnp.dot(cs, w_ref[0].astype(BF16), preferred_element_type=F32) + b_ref[0]


def _modulation(c_all, w_mod, b_mod):
    tn = 1536
    nb = c_all.shape[0]
    return pl.pallas_call(
        _mod_kernel,
        grid=(DEPTH, 6 * D // tn),
        in_specs=[
            pl.BlockSpec((nb, D), lambda l, n: (0, 0)),
            pl.BlockSpec((1, D, tn), lambda l, n: (l, 0, n)),
            pl.BlockSpec((1, 1, tn), lambda l, n: (l, 0, n)),
        ],
        out_specs=pl.BlockSpec((1, nb, tn), lambda l, n: (l, 0, n)),
        out_shape=jax.ShapeDtypeStruct((DEPTH, nb, 6 * D), F32),
        compiler_params=_cparams(2),
        name="modulation",
    )(c_all, w_mod, b_mod.reshape(DEPTH, 1, 6 * D))


def _tail(is_p, x, y, g1p, g1s, sh2p, sh2s, sc2p, sc2s, n2g, wr, br, xo_ref, h2_ref, idx_ref, gate_ref):
    xn = x + _pick(is_p, g1p, g1s) * y
    xo_ref[...] = xn
    h2 = _rms(xn, n2g[...]) * (1.0 + _pick(is_p, sc2p, sc2s)) + _pick(is_p, sh2p, sh2s)
    h2_ref[...] = h2
    logit = jnp.dot(h2.astype(BF16), wr[...], preferred_element_type=F32) + br[...]
    lane = lax.broadcasted_iota(I32, logit.shape, 1)
    vals, ids = [], []
    for _ in range(TOPK):
        m = jnp.max(logit, axis=-1, keepdims=True)
        sel = jnp.min(jnp.where(logit == m, lane, LANES), axis=-1, keepdims=True)
        vals.append(m)
        ids.append(sel)
        logit = jnp.where(lane == sel, -jnp.inf, logit)
    es = [jnp.exp(v - vals[0]) for v in vals]
    tot = (es[0] + es[1]) + (es[2] + es[3])
    idx_out = jnp.zeros(logit.shape, I32)
    gate_out = jnp.zeros(logit.shape, F32)
    for k in range(TOPK):
        idx_out = jnp.where(lane == k, ids[k], idx_out)
        gate_out = jnp.where(lane == k, es[k] / tot, gate_out)
    idx_ref[...] = idx_out
    gate_ref[...] = gate_out


def _tail_in_specs():
    return (_mod_specs(2) + _mod_specs(3) + _mod_specs(4)
            + [_full_spec((1, D)), _full_spec((D, LANES)), _full_spec((1, LANES))])


def _tail_out_specs():
    return [_tile_spec(D), _tile_spec(D), _tile_spec(LANES), _tile_spec(LANES)]


def _tail_out_shapes():
    return [jax.ShapeDtypeStruct((T, D), F32), jax.ShapeDtypeStruct((T, D), F32),
            jax.ShapeDtypeStruct((T, LANES), I32), jax.ShapeDtypeStruct((T, LANES), F32)]


def _sgu_kernel(x_ref, sh1p, sh1s, sc1p, sc1s, n1g, win, lng, lnb, mixw, mixb, wout,
                g1p, g1s, sh2p, sh2s, sc2p, sc2s, n2g, wr, br,
                xo_ref, h2_ref, idx_ref, gate_ref, v_ref, y_scr):
    t = pl.program_id(0)
    is_p = t < NPT
    x = x_ref[...]
    h = _rms(x, n1g[...]) * (1.0 + _pick(is_p, sc1p, sc1s)) + _pick(is_p, sh1p, sh1s)
    z = jnp.dot(h.astype(BF16), win[...], preferred_element_type=F32)
    z = 0.5 * z * (1.0 + lax.erf(z * (0.5 ** 0.5)))
    u = z[:, :SGU_W]
    v = z[:, SGU_W:]
    vc = v - jnp.mean(v, axis=-1, keepdims=True)
    vn = vc * lax.rsqrt(jnp.mean(vc * vc, axis=-1, keepdims=True) + EPS) * lng[...] + lnb[...]

    @pl.when(t >= NPT)
    def _():
        v_ref[...] = vn

    vb = vn.astype(BF16)
    ri = lax.broadcasted_iota(I32, (CHUNK, CHUNK), 0)
    ci = lax.broadcasted_iota(I32, (CHUNK, CHUNK), 1)
    causal = ci <= ri
    shift = jnp.broadcast_to(jnp.where(is_p, 7, 3), ri.shape)
    keep = causal & (lax.shift_right_logical(ri, shift) == lax.shift_right_logical(ci, shift))
    for g in range(SGU_G):
        wg = jnp.where(keep, mixw[0, g], 0.0).astype(BF16)
        for c in range(TM // CHUNK):
            rows = slice(c * CHUNK, (c + 1) * CHUNK)
            cols = slice(g * SGU_GD, (g + 1) * SGU_GD)
            mixed = jnp.dot(wg, vb[rows, cols], preferred_element_type=F32) + mixb[0, :, cols]
            y_scr[rows, cols] = (u[rows, cols] * mixed).astype(BF16)
    y = jnp.dot(y_scr[...], wout[...], preferred_element_type=F32)
    _tail(is_p, x, y, g1p, g1s, sh2p, sh2s, sc2p, sc2s, n2g, wr, br, xo_ref, h2_ref, idx_ref, gate_ref)


def _sgu_layer(x, mod_p, mod_s, n1g, n2g, win, lng, lnb, mixw, mixb, wout, wr, br):
    sel = lambda t: (jnp.where(t < NPT, 0, 1), 0, 0, 0)
    in_specs = ([_tile_spec(D)] + _mod_specs(0) + _mod_specs(1)
                + [_full_spec((1, D)), _full_spec((D, 2 * SGU_W)), _full_spec((1, SGU_W)),
                   _full_spec((1, SGU_W)),
                   pl.BlockSpec((1, SGU_G, CHUNK, CHUNK), sel),
                   pl.BlockSpec((1, CHUNK, SGU_W), lambda t: (jnp.where(t < NPT, 0, 1), 0, 0)),
                   _full_spec((SGU_W, D))]
                + _tail_in_specs())
    out_specs = _tail_out_specs() + [pl.BlockSpec((TM, SGU_W), lambda t: (jnp.maximum(t - NPT, 0), 0))]
    out_shape = _tail_out_shapes() + [jax.ShapeDtypeStruct((T_S, SGU_W), F32)]
    return pl.pallas_call(
        _sgu_kernel, grid=(NT,), in_specs=in_specs, out_specs=out_specs, out_shape=out_shape,
        scratch_shapes=[pltpu.VMEM((TM, SGU_W), BF16)],
        compiler_params=_cparams(1), name="sgu_layer",
    )(x, mod_p, mod_s, mod_p, mod_s, n1g, win, lng, lnb, mixw, mixb, wout,
      mod_p, mod_s, mod_p, mod_s, mod_p, mod_s, n2g, wr, br)


def _ret_proj_kernel(x_ref, sh1p, sh1s, sc1p, sc1s, n1g, win, cos_ref, sin_ref,
                     q_ref, k_ref, v_ref, sg_ref):
    t = pl.program_id(0)
    is_p = t < NPT
    x = x_ref[...]
    h = _rms(x, n1g[...]) * (1.0 + _pick(is_p, sc1p, sc1s)) + _pick(is_p, sh1p, sh1s)
    p = jnp.dot(h.astype(BF16), win[...], preferred_element_type=F32)
    cos = cos_ref[...]
    sin = sin_ref[...]
    for hd in range(H):
        cq = slice(hd * DK, (hd + 1) * DK)
        ck = slice(QD + hd * DK, QD + (hd + 1) * DK)
        qh = p[:, cq]
        kh = p[:, ck]
        q_ref[:, cq] = (qh * cos + pltpu.roll(qh, DK // 2, 1) * sin).astype(BF16)
        k_ref[:, cq] = ((kh * cos + pltpu.roll(kh, DK // 2, 1) * sin) * (DK ** -0.5)).astype(BF16)
    v_ref[...] = p[:, 2 * QD:2 * QD + VD].astype(BF16)
    g = p[:, 2 * QD + VD:]
    sg_ref[...] = (g * jax.nn.sigmoid(g)).astype(BF16)


def _ret_proj(x, mod_p, mod_s, n1g, win, cos_tab, sin_tab):
    in_specs = ([_tile_spec(D)] + _mod_specs(0) + _mod_specs(1)
                + [_full_spec((1, D)), _full_spec((D, RET_IN)), _tile_spec(DK), _tile_spec(DK)])
    out_specs = [_tile_spec(QD), _tile_spec(QD), _tile_spec(VD), _tile_spec(VD)]
    out_shape = [jax.ShapeDtypeStruct((T, QD), BF16), jax.ShapeDtypeStruct((T, QD), BF16),
                 jax.ShapeDtypeStruct((T, VD), BF16), jax.ShapeDtypeStruct((T, VD), BF16)]
    return pl.pallas_call(
        _ret_proj_kernel, grid=(NT,), in_specs=in_specs, out_specs=out_specs, out_shape=out_shape,
        compiler_params=_cparams(1), name="ret_proj",
    )(x, mod_p, mod_s, mod_p, mod_s, n1g, win, cos_tab, sin_tab)


def _ret_intra(q, k, v, decay, qdec, kdec):
    s = lax.dot_general(q, k, (((1,), (1,)), ((), ())), preferred_element_type=F32) * decay
    o = jnp.dot(s.astype(BF16), v, preferred_element_type=F32)
    return o, q.astype(F32) * qdec, k.astype(F32) * kdec


def _ret_finish(o, sg, ng):
    on = o * lax.rsqrt(jnp.mean(o * o, axis=-1, keepdims=True) + EPS)
    return (sg.astype(F32) * (on * ng)).astype(BF16)


def _ret_prompt_kernel(gam_ref, q_ref, k_ref, v_ref, sg_ref, dec_ref, qdec_ref, kdec_ref, ng_ref,
                       y_ref, s_ref):
    c = pl.program_id(1)

    @pl.when(c == 0)
    def _():
        s_ref[...] = jnp.zeros(s_ref.shape, F32)

    for hd in range(H):
        ck = slice(hd * DK, (hd + 1) * DK)
        cv = slice(hd * DV, (hd + 1) * DV)
        v = v_ref[:, cv]
        o, qd, kd = _ret_intra(q_ref[:, ck], k_ref[:, ck], v, dec_ref[hd], qdec_ref[:, ck], kdec_ref[:, ck])
        s_old = s_ref[0, hd]
        o = o + jnp.dot(qd.astype(BF16), s_old.astype(BF16), preferred_element_type=F32)
        s_ref[0, hd] = gam_ref[hd] * s_old + lax.dot_general(
            kd.astype(BF16), v, (((0,), (0,)), ((), ())), preferred_element_type=F32)
        y_ref[:, cv] = _ret_finish(o, sg_ref[:, cv], ng_ref[:, cv])


def _ret_prompt(q, k, v, sg, tabs, ng):
    nc = L_P // CHUNK
    row = lambda b, c: (b * nc + c, 0)
    smem = pl.BlockSpec(memory_space=pltpu.SMEM)
    in_specs = [smem,
                pl.BlockSpec((CHUNK, QD), row), pl.BlockSpec((CHUNK, QD), row),
                pl.BlockSpec((CHUNK, VD), row), pl.BlockSpec((CHUNK, VD), row),
                _full_spec((H, CHUNK, CHUNK)), _full_spec((CHUNK, QD)), _full_spec((CHUNK, QD)),
                _full_spec((1, VD))]
    out_specs = [pl.BlockSpec((CHUNK, VD), row),
                 pl.BlockSpec((1, H, DK, DV), lambda b, c: (b, 0, 0, 0))]
    out_shape = [jax.ShapeDtypeStruct((T, VD), BF16), jax.ShapeDtypeStruct((NB_P, H, DK, DV), F32)]
    return pl.pallas_call(
        _ret_prompt_kernel, grid=(NB_P, nc), in_specs=in_specs, out_specs=out_specs, out_shape=out_shape,
        compiler_params=_cparams(2), name="ret_prompt",
    )(tabs["gam"], q, k, v, sg, tabs["decay"], tabs["qdec"], tabs["kdec"], ng)


def _ret_sample_kernel(gam_ref, q_ref, k_ref, v_ref, sg_ref, dec_ref, qdec_ref, kdec_ref, ng_ref,
                       s_in_ref, y_in_ref, y_ref, s_out_ref):
    del y_in_ref
    hd = pl.program_id(1)
    v = v_ref[...]
    o, qd, kd = _ret_intra(q_ref[...], k_ref[...], v, dec_ref[0], qdec_ref[...], kdec_ref[...])
    gam = gam_ref[hd]
    row_batch = lax.broadcasted_iota(I32, (CHUNK, DK), 0) // L_S
    cross = []
    for b in range(SB):
        s_old = s_in_ref[b, 0]
        rows = slice(b * L_S, (b + 1) * L_S)
        cross.append(jnp.dot(qd[rows].astype(BF16), s_old.astype(BF16), preferred_element_type=F32))
        kb = jnp.where(row_batch == b, kd, 0.0).astype(BF16)
        s_out_ref[b, 0] = gam * s_old + lax.dot_general(
            kb, v, (((0,), (0,)), ((), ())), preferred_element_type=F32)
    o = o + jnp.concatenate(cross, axis=0)
    y_ref[...] = _ret_finish(o, sg_ref[...], ng_ref[...])


def _ret_sample(q, k, v, sg, tabs, ng, s_in, y_prev):
    base = T_P // CHUNK
    rq = lambda g, h: (base + g, h)
    smem = pl.BlockSpec(memory_space=pltpu.SMEM)
    in_specs = [smem,
                pl.BlockSpec((CHUNK, DK), rq), pl.BlockSpec((CHUNK, DK), rq),
                pl.BlockSpec((CHUNK, DV), rq), pl.BlockSpec((CHUNK, DV), rq),
                pl.BlockSpec((1, CHUNK, CHUNK), lambda g, h: (h, 0, 0)),
                pl.BlockSpec((CHUNK, DK), lambda g, h: (0, h)),
                pl.BlockSpec((CHUNK, DK), lambda g, h: (0, h)),
                pl.BlockSpec((1, DV), lambda g, h: (0, h)),
                pl.BlockSpec((SB, 1, DK, DV), lambda g, h: (g, h, 0, 0)),
                pl.BlockSpec(memory_space=pl.ANY)]
    out_specs = [pl.BlockSpec((CHUNK, DV), rq),
                 pl.BlockSpec((SB, 1, DK, DV), lambda g, h: (g, h, 0, 0))]
    out_shape = [jax.ShapeDtypeStruct((T, VD), BF16), jax.ShapeDtypeStruct((NB_S, H, DK, DV), F32)]
    return pl.pallas_call(
        _ret_sample_kernel, grid=(NB_S // SB, H), in_specs=in_specs, out_specs=out_specs,
        out_shape=out_shape, input_output_aliases={10: 0},
        compiler_params=_cparams(2), name="ret_sample",
    )(tabs["gam"], q, k, v, sg, tabs["decay"], tabs["qdec"], tabs["kdec"], ng, s_in, y_prev)


def _ret_out_kernel(y_ref, wout, x_ref, g1p, g1s, sh2p, sh2s, sc2p, sc2s, n2g, wr, br,
                    xo_ref, h2_ref, idx_ref, gate_ref):
    is_p = pl.program_id(0) < NPT
    y = jnp.dot(y_ref[...], wout[...], preferred_element_type=F32)
    _tail(is_p, x_ref[...], y, g1p, g1s, sh2p, sh2s, sc2p, sc2s, n2g, wr, br,
          xo_ref, h2_ref, idx_ref, gate_ref)


def _ret_out(y, wout, x, mod_p, mod_s, n2g, wr, br):
    in_specs = [_tile_spec(VD), _full_spec((VD, D)), _tile_spec(D)] + _tail_in_specs()
    return pl.pallas_call(
        _ret_out_kernel, grid=(NT,), in_specs=in_specs, out_specs=_tail_out_specs(),
        out_shape=_tail_out_shapes(), compiler_params=_cparams(1), name="ret_out",
    )(y, wout, x, mod_p, mod_s, mod_p, mod_s, mod_p, mod_s, n2g, wr, br)


def _row_copy(src, dst, r_src, r_dst, n, sem):
    return pltpu.make_async_copy(src.at[pl.ds(r_src, n)], dst.at[pl.ds(r_dst, n)], sem)


def _gather_kernel(idx_ref, src, dst, sem):
    base = pl.program_id(0) * GROWS

    def issue(i, carry):
        _row_copy(src, dst, idx_ref[i], base + i, 1, sem).start()
        return carry

    lax.fori_loop(0, GROWS, issue, 0, unroll=8)

    def drain(i, carry):
        _row_copy(src, dst, 0, base + i, 1, sem).wait()
        return carry

    lax.fori_loop(0, GROWS, drain, 0, unroll=8)


def _gather_rows(src, idx):
    n = idx.shape[0]
    assert n % GROWS == 0
    return pl.pallas_call(
        _gather_kernel, grid=(n // GROWS,),
        in_specs=[pl.BlockSpec((GROWS,), lambda s: (s,), memory_space=pltpu.SMEM),
                  pl.BlockSpec(memory_space=pl.ANY)],
        out_specs=pl.BlockSpec(memory_space=pl.ANY),
        out_shape=jax.ShapeDtypeStruct((n,) + src.shape[1:], src.dtype),
        scratch_shapes=[pltpu.SemaphoreType.DMA],
        compiler_params=_cparams(1), name="gather_rows",
    )(idx, src)


def _expert_kernel(be_ref, nu_ref, xb_ref, wu_ref, bu_ref, wd_ref, bd_ref, yb_ref, wub, wdb):
    i = pl.program_id(0)
    used = i < nu_ref[0]

    @pl.when(used)
    def _():
        fresh = (i == 0) | (be_ref[i] != be_ref[jnp.maximum(i - 1, 0)])

        @pl.when(fresh)
        def _():
            wub[...] = wu_ref[0].astype(BF16)
            wdb[...] = wd_ref[0].astype(BF16)

        z = jnp.dot(xb_ref[...].astype(BF16), wub[...], preferred_element_type=F32) + bu_ref[0]
        glu = jnp.minimum(z[:, :FE], LIMIT)
        lin = jnp.clip(z[:, FE:], -LIMIT, LIMIT)
        act = glu * jax.nn.sigmoid(ALPHA * glu) * (lin + 1.0)
        yb_ref[...] = jnp.dot(act.astype(BF16), wdb[...], preferred_element_type=F32) + bd_ref[0]

    @pl.when(jnp.logical_not(used))
    def _():
        yb_ref[...] = jnp.zeros(yb_ref.shape, F32)


def _experts(xb, blk_e, n_used, wu, bu, wd, bd):
    last = lambda i, be, nu: jnp.minimum(i, nu[0] - 1)
    grid_spec = pltpu.PrefetchScalarGridSpec(
        num_scalar_prefetch=2, grid=(N_BLOCKS,),
        in_specs=[pl.BlockSpec((BLK, D), lambda i, be, nu: (last(i, be, nu), 0)),
                  pl.BlockSpec((1, D, 2 * FE), lambda i, be, nu: (be[i], 0, 0)),
                  pl.BlockSpec((1, 1, 2 * FE), lambda i, be, nu: (be[i], 0, 0)),
                  pl.BlockSpec((1, FE, D), lambda i, be, nu: (be[i], 0, 0)),
                  pl.BlockSpec((1, 1, D), lambda i, be, nu: (be[i], 0, 0))],
        out_specs=pl.BlockSpec((BLK, D), lambda i, be, nu: (i, 0)),
        scratch_shapes=[pltpu.VMEM((D, 2 * FE), BF16), pltpu.VMEM((FE, D), BF16)])
    return pl.pallas_call(
        _expert_kernel, grid_spec=grid_spec,
        out_shape=jax.ShapeDtypeStruct((N_ROWS, D), F32),
        compiler_params=_cparams(1), name="experts",
    )(blk_e, n_used, xb, wu, bu.reshape(NE, 1, 2 * FE), wd, bd.reshape(NE, 1, D))


def _combine_kernel(x_ref, yk_ref, gate_ref, g2p, g2s, fg_ref, o_ref, *, final):
    is_p = pl.program_id(0) < NPT
    gates = gate_ref[...]
    acc = gates[:, 0:1] * yk_ref[0]
    for k in range(1, TOPK):
        acc = acc + gates[:, k:k + 1] * yk_ref[k]
    xn = x_ref[...] + _pick(is_p, g2p, g2s) * acc
    o_ref[...] = _rms(xn, fg_ref[...]) if final else xn


def _combine(x, yk, gates, mod_p, mod_s, final_g, final):
    in_specs = ([_tile_spec(D), pl.BlockSpec((TOPK, TM, D), lambda t: (0, t, 0)), _tile_spec(LANES)]
                + _mod_specs(5) + [_full_spec((1, D))])
    return pl.pallas_call(
        functools.partial(_combine_kernel, final=final), grid=(NT,), in_specs=in_specs,
        out_specs=_tile_spec(D), out_shape=jax.ShapeDtypeStruct((T, D), F32),
        compiler_params=_cparams(1), name="moe_combine",
    )(x, yk, gates, mod_p, mod_s, final_g)


def _route(idx):
    flat_e = idx.reshape(-1)
    order = jnp.argsort(flat_e, stable=True).astype(I32)
    sorted_e = flat_e[order]
    counts = jnp.bincount(flat_e, length=NE).astype(I32)
    padded = (counts + BLK - 1) // BLK * BLK
    pad_end = jnp.cumsum(padded)
    pad_start = pad_end - padded
    start = jnp.cumsum(counts) - counts
    rank = jnp.arange(TK, dtype=I32) - start[sorted_e]
    dest_sorted = (pad_start[sorted_e] + rank).astype(I32)
    dest = jnp.zeros((TK,), I32).at[order].set(dest_sorted)
    row_tok = jnp.zeros((N_ROWS,), I32).at[dest_sorted].set(order // TOPK)
    blk_e = jnp.minimum(jnp.searchsorted(pad_end, jnp.arange(N_BLOCKS, dtype=I32) * BLK, side="right"),
                        NE - 1).astype(I32)
    n_used = (pad_end[-1:] // BLK).astype(I32)
    return dest, row_tok, blk_e, n_used


def _moe(x, h2, idx, gates, mod_p, mod_s, wu, bu, wd, bd, final_g, final):
    dest, row_tok, blk_e, n_used = _route(idx[:, :TOPK])
    xb = _gather_rows(h2, row_tok)
    yb = _experts(xb, blk_e, n_used, wu, bu, wd, bd)
    slot_major = dest.reshape(T, TOPK).T.reshape(-1)
    yk = _gather_rows(yb, slot_major).reshape(TOPK, T, D)
    return _combine(x, yk, gates, mod_p, mod_s, final_g, final)


def _rope_tables():
    half = DK // 2
    inv = 1.0 / (ROPE_BASE ** jnp.linspace(0.0, 1.0, half, dtype=F32))

    def tab(pos):
        ang = pos.astype(F32)[:, None] * inv[None, :]
        cos, sin = jnp.cos(ang), jnp.sin(ang)
        return jnp.concatenate([cos, cos], -1), jnp.concatenate([-sin, sin], -1)

    cp, sp = tab(jnp.arange(L_P, dtype=I32))
    cs, ss = tab(PAST + jnp.arange(L_S, dtype=I32))
    cos = jnp.concatenate([jnp.tile(cp, (NB_P, 1)), jnp.tile(cs, (NB_S, 1))], 0)
    sin = jnp.concatenate([jnp.tile(sp, (NB_P, 1)), jnp.tile(ss, (NB_S, 1))], 0)
    return cos, sin


def _decay_tables(cl):
    lg = jnp.log(1.0 - 2.0 ** (-5.0 - jnp.arange(H, dtype=F32)))
    r = jnp.arange(CHUNK)
    idx = (r % cl).astype(F32)
    diff = idx[:, None] - idx[None, :]
    same = (r[:, None] // cl) == (r[None, :] // cl)
    decay = jnp.where((same & (diff >= 0))[None],
                      jnp.exp(lg[:, None, None] * jnp.maximum(diff, 0.0)[None]), 0.0)
    qdec = jnp.exp(lg[None, :] * (idx[:, None] + 1.0))
    kdec = jnp.exp(lg[None, :] * (cl - 1.0 - idx[:, None]))
    wide = lambda a: jnp.repeat(a, DK, axis=1)
    return {"decay": decay, "qdec": wide(qdec), "kdec": wide(kdec), "gam": jnp.exp(lg * cl)}


def kernel(x_prompt, x_sample, c_prompt, c_sample, state_ret, w_mod, b_mod, norm1_g, norm2_g,
           sgu_w_in, sgu_ln_g, sgu_ln_b, sgu_w_s, sgu_b_s, sgu_w_out, ret_w_in, ret_norm_g, ret_w_out,
           moe_w_router, moe_b_router, moe_w_up, moe_b_up, moe_w_down, moe_b_down, final_g):
    x = jnp.concatenate([x_prompt.reshape(T_P, D), x_sample.reshape(T_S, D)], 0)
    mod = _modulation(jnp.concatenate([c_prompt, c_sample], 0), w_mod, b_mod)
    cos_tab, sin_tab = _rope_tables()
    tabs_p = _decay_tables(CHUNK)
    tabs_s = _decay_tables(L_S)
    wr_pad = jnp.pad(moe_w_router, ((0, 0), (0, 0), (0, LANES - NE))).astype(BF16)
    br_pad = jnp.pad(moe_b_router, ((0, 0), (0, LANES - NE)), constant_values=-1e30)
    fg = final_g.reshape(1, D)

    ret_p, ret_s, v_rows = [], [], []
    for i in range(DEPTH):
        j = i // 2
        mod_p = mod[i, :NB_P].reshape(NB_P, 1, 6 * D)
        mod_s = jnp.repeat(mod[i, NB_P:], L_S, axis=0)
        n1g = norm1_g[i].reshape(1, D)
        n2g = norm2_g[i].reshape(1, D)
        wr = wr_pad[i]
        br = br_pad[i].reshape(1, LANES)
        if i % 2 == 0:
            mixw = jnp.stack([sgu_w_s[j], jnp.tile(sgu_w_s[j][:, :L_S, :L_S], (1, SB, SB))])
            bias_p = jnp.repeat(sgu_b_s[j].T, SGU_GD, axis=1)
            bias_s = jnp.tile(bias_p[:L_S], (SB, 1))
            x, h2, idx, gates, v = _sgu_layer(
                x, mod_p, mod_s, n1g, n2g, sgu_w_in[j].astype(BF16), sgu_ln_g[j].reshape(1, SGU_W),
                sgu_ln_b[j].reshape(1, SGU_W), mixw, jnp.stack([bias_p, bias_s]),
                sgu_w_out[j].astype(BF16), wr, br)
            v_rows.append(v.reshape(NB_S, L_S, SGU_W))
        else:
            q, k, v, sg = _ret_proj(x, mod_p, mod_s, n1g, ret_w_in[j].astype(BF16), cos_tab, sin_tab)
            ng = ret_norm_g[j].reshape(1, VD)
            y, s_p = _ret_prompt(q, k, v, sg, tabs_p, ng)
            y, s_s = _ret_sample(q, k, v, sg, tabs_s, ng, state_ret[j], y)
            ret_p.append(s_p)
            ret_s.append(s_s)
            x, h2, idx, gates = _ret_out(y, ret_w_out[j].astype(BF16), x, mod_p, mod_s, n2g, wr, br)
        x = _moe(x, h2, idx, gates, mod_p, mod_s, moe_w_up[i], moe_b_up[i], moe_w_down[i], moe_b_down[i],
                 fg, final=(i == DEPTH - 1))
    y_prompt = x[:T_P].reshape(NB_P, L_P, D)
    y_sample = x[T_P:].reshape(NB_S, L_S, D)
    return (y_prompt, y_sample, jnp.stack(ret_p), jnp.stack(ret_s), jnp.stack(v_rows))
```

```python
import functools

import jax
import jax.numpy as jnp
from jax import lax
from jax.experimental import pallas as pl
from jax.experimental.pallas import tpu as pltpu

F32 = jnp.float32
BF16 = jnp.bfloat16
I32 = jnp.int32

D = 1024
NB_P, L_P = 8, 2048
NB_S, L_S = 128, 8
PAST = 16384
DEPTH = 4
T_P = NB_P * L_P
T_S = NB_S * L_S
T = T_P + T_S
SGU_W = 2 * D
SGU_G = 8
SGU_GD = SGU_W // SGU_G
CHUNK = 128
H = 8
DK = D // H
DV = 2 * DK
QD = H * DK
VD = H * DV
RET_IN = 2 * QD + 2 * VD
ROPE_BASE = 10000.0
NE = 32
TOPK = 4
FE = D
ALPHA = 1.702
LIMIT = 7.0
EPS = 1e-6

LANES = 128
ROW_TILES = D // LANES
TM = 256
NPT = T_P // TM
NST = T_S // TM
NT = NPT + NST
TPB = L_P // TM
BLK = 256
TK = T * TOPK
N_BLOCKS = -(-(TK + NE * (BLK - 1)) // BLK)
N_ROWS = N_BLOCKS * BLK
GROWS = 2048
SB = 16
VMEM_LIMIT = 56 * 1024 * 1024


def _cparams(n_axes):
    return pltpu.CompilerParams(dimension_semantics=("arbitrary",) * n_axes,
                                vmem_limit_bytes=VMEM_LIMIT)


def _rms(x, g):
    return (x * lax.rsqrt(jnp.mean(x * x, axis=-1, keepdims=True) + EPS)) * g


def _pick(is_p, p_ref, s_ref):
    return jnp.where(is_p, p_ref[0], s_ref[...])


def _mod_specs(j):
    return [
        pl.BlockSpec((1, 1, D), lambda t, j=j: (jnp.minimum(t // TPB, NB_P - 1), 0, j)),
        pl.BlockSpec((TM, D), lambda t, j=j: (jnp.maximum(t - NPT, 0), j)),
    ]


def _tile_spec(width):
    return pl.BlockSpec((TM, width), lambda t: (t, 0))


def _full_spec(shape):
    return pl.BlockSpec(shape, lambda *_: (0,) * len(shape))


def _mod_kernel(c_ref, w_ref, b_ref, o_ref):
    c = c_ref[...]
    cs = (c * jax.nn.sigmoid(c)).astype(BF16)
    o_ref[0] = jnp.dot(cs, w_ref[0].astype(BF16), preferred_element_type=F32) + b_ref[0]


def _modulation(c_all, w_mod, b_mod):
    tn = 1536
    nb = c_all.shape[0]
    return pl.pallas_call(
        _mod_kernel,
        grid=(DEPTH, 6 * D // tn),
        in_specs=[
            pl.BlockSpec((nb, D), lambda l, n: (0, 0)),
            pl.BlockSpec((1, D, tn), lambda l, n: (l, 0, n)),
            pl.BlockSpec((1, 1, tn), lambda l, n: (l, 0, n)),
        ],
        out_specs=pl.BlockSpec((1, nb, tn), lambda l, n: (l, 0, n)),
        out_shape=jax.ShapeDtypeStruct((DEPTH, nb, 6 * D), F32),
        compiler_params=_cparams(2),
        name="modulation",
    )(c_all, w_mod, b_mod.reshape(DEPTH, 1, 6 * D))


def _tail(is_p, x, y, g1p, g1s, sh2p, sh2s, sc2p, sc2s, n2g, wr, br, xo_ref, h2_ref, idx_ref, gate_ref):
    xn = x + _pick(is_p, g1p, g1s) * y
    xo_ref[...] = xn
    h2 = _rms(xn, n2g[...]) * (1.0 + _pick(is_p, sc2p, sc2s)) + _pick(is_p, sh2p, sh2s)
    for s in range(ROW_TILES):
        h2_ref[:, s, :] = h2[:, s * LANES:(s + 1) * LANES]
    logit =jnp.dot(h2.astype(BF16), wr[...], preferred_element_type=F32) + br[...]
    lane = lax.broadcasted_iota(I32, logit.shape, 1)
    vals, ids = [], []
    for _ in range(TOPK):
        m = jnp.max(logit, axis=-1, keepdims=True)
        sel = jnp.min(jnp.where(logit == m, lane, LANES), axis=-1, keepdims=True)
        vals.append(m)
        ids.append(sel)
        logit = jnp.where(lane == sel, -jnp.inf, logit)
    es = [jnp.exp(v - vals[0]) for v in vals]
    tot = (es[0] + es[1]) + (es[2] + es[3])
    idx_out = jnp.zeros(logit.shape, I32)
    gate_out = jnp.zeros(logit.shape, F32)
    for k in range(TOPK):
        idx_out = jnp.where(lane == k, ids[k], idx_out)
        gate_out = jnp.where(lane == k, es[k] / tot, gate_out)
    idx_ref[...] = idx_out
    gate_ref[...] = gate_out


def _tail_in_specs():
    return (_mod_specs(2) + _mod_specs(3) + _mod_specs(4)
            + [_full_spec((1, D)), _full_spec((D, LANES)), _full_spec((1, LANES))])


def _tile_spec_rows():
    return pl.BlockSpec((TM, ROW_TILES, LANES), lambda t: (t, 0, 0))


def _tail_out_specs():
    return [_tile_spec(D), _tile_spec_rows(), _tile_spec(LANES), _tile_spec(LANES)]


def _tail_out_shapes():
    return [jax.ShapeDtypeStruct((T, D), F32), jax.ShapeDtypeStruct((T, ROW_TILES, LANES), F32),
            jax.ShapeDtypeStruct((T, LANES), I32), jax.ShapeDtypeStruct((T, LANES), F32)]


def _sgu_kernel(x_ref, sh1p, sh1s, sc1p, sc1s, n1g, win, lng, lnb, mixw, mixb, wout,
                g1p, g1s, sh2p, sh2s, sc2p, sc2s, n2g, wr, br,
                xo_ref, h2_ref, idx_ref, gate_ref, v_ref, y_scr):
    t = pl.program_id(0)
    is_p = t < NPT
    x = x_ref[...]
    h = _rms(x, n1g[...]) * (1.0 + _pick(is_p, sc1p, sc1s)) + _pick(is_p, sh1p, sh1s)
    z = jnp.dot(h.astype(BF16), win[...], preferred_element_type=F32)
    z = 0.5 * z * (1.0 + lax.erf(z * (0.5 ** 0.5)))
    u = z[:, :SGU_W]
    v = z[:, SGU_W:]
    vc = v - jnp.mean(v, axis=-1, keepdims=True)
    vn = vc * lax.rsqrt(jnp.mean(vc * vc, axis=-1, keepdims=True) + EPS) * lng[...] + lnb[...]

    @pl.when(t >= NPT)
    def _():
        v_ref[...] = vn

    vb = vn.astype(BF16)
    ri = lax.broadcasted_iota(I32, (CHUNK, CHUNK), 0)
    ci = lax.broadcasted_iota(I32, (CHUNK, CHUNK), 1)
    causal = ci <= ri
    shift = jnp.broadcast_to(jnp.where(is_p, 7, 3), ri.shape)
    keep = causal & (lax.shift_right_logical(ri, shift) == lax.shift_right_logical(ci, shift))
    for g in range(SGU_G):
        wg = jnp.where(keep, mixw[0, g], 0.0).astype(BF16)
        for c in range(TM // CHUNK):
            rows = slice(c * CHUNK, (c + 1) * CHUNK)
            cols = slice(g * SGU_GD, (g + 1) * SGU_GD)
            mixed = jnp.dot(wg, vb[rows, cols], preferred_element_type=F32) + mixb[0, :, cols]
            y_scr[rows, cols] = (u[rows, cols] * mixed).astype(BF16)
    y = jnp.dot(y_scr[...], wout[...], preferred_element_type=F32)
    _tail(is_p, x, y, g1p, g1s, sh2p, sh2s, sc2p, sc2s, n2g, wr, br, xo_ref, h2_ref, idx_ref, gate_ref)


def _sgu_layer(x, mod_p, mod_s, n1g, n2g, win, lng, lnb, mixw, mixb, wout, wr, br):
    sel = lambda t: (jnp.where(t < NPT, 0, 1), 0, 0, 0)
    in_specs = ([_tile_spec(D)] + _mod_specs(0) + _mod_specs(1)
                + [_full_spec((1, D)), _full_spec((D, 2 * SGU_W)), _full_spec((1, SGU_W)),
                   _full_spec((1, SGU_W)),
                   pl.BlockSpec((1, SGU_G, CHUNK, CHUNK), sel),
                   pl.BlockSpec((1, CHUNK, SGU_W), lambda t: (jnp.where(t < NPT, 0, 1), 0, 0)),
                   _full_spec((SGU_W, D))]
                + _tail_in_specs())
    out_specs = _tail_out_specs() + [pl.BlockSpec((TM, SGU_W), lambda t: (jnp.maximum(t - NPT, 0), 0))]
    out_shape = _tail_out_shapes() + [jax.ShapeDtypeStruct((T_S, SGU_W), F32)]
    return pl.pallas_call(
        _sgu_kernel, grid=(NT,), in_specs=in_specs, out_specs=out_specs, out_shape=out_shape,
        scratch_shapes=[pltpu.VMEM((TM, SGU_W), BF16)],
        compiler_params=_cparams(1), name="sgu_layer",
    )(x, mod_p, mod_s, mod_p, mod_s, n1g, win, lng, lnb, mixw, mixb, wout,
      mod_p, mod_s, mod_p, mod_s, mod_p, mod_s, n2g, wr, br)


def _ret_proj_kernel(x_ref, sh1p, sh1s, sc1p, sc1s, n1g, win, cos_ref, sin_ref,
                     q_ref, k_ref, v_ref, sg_ref):
    t = pl.program_id(0)
    is_p = t < NPT
    x = x_ref[...]
    h = _rms(x, n1g[...]) * (1.0 + _pick(is_p, sc1p, sc1s)) + _pick(is_p, sh1p, sh1s)
    p = jnp.dot(h.astype(BF16), win[...], preferred_element_type=F32)
    cos = cos_ref[...]
    sin = sin_ref[...]
    for hd in range(H):
        cq = slice(hd * DK, (hd + 1) * DK)
        ck = slice(QD + hd * DK, QD + (hd + 1) * DK)
        qh = p[:, cq]
        kh = p[:, ck]
        q_ref[:, cq] = (qh * cos + pltpu.roll(qh, DK // 2, 1) * sin).astype(BF16)
        k_ref[:, cq] = ((kh * cos + pltpu.roll(kh, DK // 2, 1) * sin) * (DK ** -0.5)).astype(BF16)
    v_ref[...] = p[:, 2 * QD:2 * QD + VD].astype(BF16)
    g = p[:, 2 * QD + VD:]
    sg_ref[...] = (g * jax.nn.sigmoid(g)).astype(BF16)


def _ret_proj(x, mod_p, mod_s, n1g, win, cos_tab, sin_tab):
    in_specs = ([_tile_spec(D)] + _mod_specs(0) + _mod_specs(1)
                + [_full_spec((1, D)), _full_spec((D, RET_IN)), _tile_spec(DK), _tile_spec(DK)])
    out_specs = [_tile_spec(QD), _tile_spec(QD), _tile_spec(VD), _tile_spec(VD)]
    out_shape = [jax.ShapeDtypeStruct((T, QD), BF16), jax.ShapeDtypeStruct((T, QD), BF16),
                 jax.ShapeDtypeStruct((T, VD), BF16), jax.ShapeDtypeStruct((T, VD), BF16)]
    return pl.pallas_call(
        _ret_proj_kernel, grid=(NT,), in_specs=in_specs, out_specs=out_specs, out_shape=out_shape,
        compiler_params=_cparams(1), name="ret_proj",
    )(x, mod_p, mod_s, mod_p, mod_s, n1g, win, cos_tab, sin_tab)


def _ret_intra(q, k, v, decay, qdec, kdec):
    s = lax.dot_general(q, k, (((1,), (1,)), ((), ())), preferred_element_type=F32) * decay
    o = jnp.dot(s.astype(BF16), v, preferred_element_type=F32)
    return o, q.astype(F32) * qdec, k.astype(F32) * kdec


def _ret_finish(o, sg, ng):
    on = o * lax.rsqrt(jnp.mean(o * o, axis=-1, keepdims=True) + EPS)
    return (sg.astype(F32) * (on * ng)).astype(BF16)


def _ret_prompt_kernel(gam_ref, q_ref, k_ref, v_ref, sg_ref, dec_ref, qdec_ref, kdec_ref, ng_ref,
                       y_ref, s_ref):
    c = pl.program_id(1)

    @pl.when(c == 0)
    def _():
        s_ref[...] = jnp.zeros(s_ref.shape, F32)

    for hd in range(H):
        ck = slice(hd * DK, (hd + 1) * DK)
        cv = slice(hd * DV, (hd + 1) * DV)
        v = v_ref[:, cv]
        o, qd, kd = _ret_intra(q_ref[:, ck], k_ref[:, ck], v, dec_ref[hd], qdec_ref[:, ck], kdec_ref[:, ck])
        s_old = s_ref[0, hd]
        o = o + jnp.dot(qd.astype(BF16), s_old.astype(BF16), preferred_element_type=F32)
        s_ref[0, hd] = gam_ref[hd] * s_old + lax.dot_general(
            kd.astype(BF16), v, (((0,), (0,)), ((), ())), preferred_element_type=F32)
        y_ref[:, cv] = _ret_finish(o, sg_ref[:, cv], ng_ref[:, cv])


def _ret_prompt(q, k, v, sg, tabs, ng):
    nc = L_P // CHUNK
    row = lambda b, c: (b * nc + c, 0)
    smem = pl.BlockSpec(memory_space=pltpu.SMEM)
    in_specs = [smem,
                pl.BlockSpec((CHUNK, QD), row), pl.BlockSpec((CHUNK, QD), row),
                pl.BlockSpec((CHUNK, VD), row), pl.BlockSpec((CHUNK, VD), row),
                _full_spec((H, CHUNK, CHUNK)), _full_spec((CHUNK, QD)), _full_spec((CHUNK, QD)),
                _full_spec((1, VD))]
    out_specs = [pl.BlockSpec((CHUNK, VD), row),
                 pl.BlockSpec((1, H, DK, DV), lambda b, c: (b, 0, 0, 0))]
    out_shape = [jax.ShapeDtypeStruct((T, VD), BF16), jax.ShapeDtypeStruct((NB_P, H, DK, DV), F32)]
    return pl.pallas_call(
        _ret_prompt_kernel, grid=(NB_P, nc), in_specs=in_specs, out_specs=out_specs, out_shape=out_shape,
        compiler_params=_cparams(2), name="ret_prompt",
    )(tabs["gam"], q, k, v, sg, tabs["decay"], tabs["qdec"], tabs["kdec"], ng)


def _ret_sample_kernel(gam_ref, q_ref, k_ref, v_ref, sg_ref, dec_ref, qdec_ref, kdec_ref, ng_ref,
                       s_in_ref, y_in_ref, s_all_ref, y_ref, s_out_ref):
    del y_in_ref, s_all_ref
    s_in_ref = s_in_ref.at[0]
    s_out_ref = s_out_ref.at[0]
    hd = pl.program_id(1)
    v = v_ref[...]
    o, qd, kd = _ret_intra(q_ref[...], k_ref[...], v, dec_ref[0], qdec_ref[...], kdec_ref[...])
    gam = gam_ref[hd]
    row_batch = lax.broadcasted_iota(I32, (CHUNK, DK), 0) // L_S
    cross = []
    for b in range(SB):
        s_old = s_in_ref[b, 0]
        rows = slice(b * L_S, (b + 1) * L_S)
        cross.append(jnp.dot(qd[rows].astype(BF16), s_old.astype(BF16), preferred_element_type=F32))
        kb = jnp.where(row_batch == b, kd, 0.0).astype(BF16)
        s_out_ref[b, 0] = gam * s_old + lax.dot_general(
            kb, v, (((0,), (0,)), ((), ())), preferred_element_type=F32)
    o = o + jnp.concatenate(cross, axis=0)
    y_ref[...] = _ret_finish(o, sg_ref[...], ng_ref[...])


def _ret_sample(j, q, k, v, sg, tabs, ng, s_in, y_prev, s_all):
    base = T_P // CHUNK
    rq = lambda g, h: (base + g, h)
    st = pl.BlockSpec((1, SB, 1, DK, DV), lambda g, h: (j, g, h, 0, 0))
    smem = pl.BlockSpec(memory_space=pltpu.SMEM)
    in_specs = [smem,
                pl.BlockSpec((CHUNK, DK), rq), pl.BlockSpec((CHUNK, DK), rq),
                pl.BlockSpec((CHUNK, DV), rq), pl.BlockSpec((CHUNK, DV), rq),
                pl.BlockSpec((1, CHUNK, CHUNK), lambda g, h: (h, 0, 0)),
                pl.BlockSpec((CHUNK, DK), lambda g, h: (0, h)),
                pl.BlockSpec((CHUNK, DK), lambda g, h: (0, h)),
                pl.BlockSpec((1, DV), lambda g, h: (0, h)),
                st, pl.BlockSpec(memory_space=pl.ANY), pl.BlockSpec(memory_space=pl.ANY)]
    out_specs = [pl.BlockSpec((CHUNK, DV), rq), st]
    out_shape = [jax.ShapeDtypeStruct((T, VD), BF16), jax.ShapeDtypeStruct(s_all.shape, F32)]
    return pl.pallas_call(
        _ret_sample_kernel, grid=(NB_S // SB, H), in_specs=in_specs, out_specs=out_specs,
        out_shape=out_shape, input_output_aliases={10: 0, 11: 1},
        compiler_params=_cparams(2), name="ret_sample",
    )(tabs["gam"], q, k, v, sg, tabs["decay"], tabs["qdec"], tabs["kdec"], ng, s_in, y_prev, s_all)


def _ret_out_kernel(y_ref, wout, x_ref, g1p, g1s, sh2p, sh2s, sc2p, sc2s, n2g, wr, br,
                    xo_ref, h2_ref, idx_ref, gate_ref):
    is_p = pl.program_id(0) < NPT
    y = jnp.dot(y_ref[...], wout[...], preferred_element_type=F32)
    _tail(is_p, x_ref[...], y, g1p, g1s, sh2p, sh2s, sc2p, sc2s, n2g, wr, br,
          xo_ref, h2_ref, idx_ref, gate_ref)


def _ret_out(y, wout, x, mod_p, mod_s, n2g, wr, br):
    in_specs = [_tile_spec(VD), _full_spec((VD, D)), _tile_spec(D)] + _tail_in_specs()
    return pl.pallas_call(
        _ret_out_kernel, grid=(NT,), in_specs=in_specs, out_specs=_tail_out_specs(),
        out_shape=_tail_out_shapes(), compiler_params=_cparams(1), name="ret_out",
    )(y, wout, x, mod_p, mod_s, mod_p, mod_s, mod_p, mod_s, n2g, wr, br)


def _route_kernel(idx_ref, dest_ref, meta_ref, cnt_scr, base_scr):
    ph = pl.program_id(0)
    t = pl.program_id(1)
    idx = idx_ref[...]
    lane = lax.broadcasted_iota(I32, (TM, LANES), 1)
    hits = [lane == idx[:, k:k + 1] for k in range(TOPK)]
    chosen = jnp.zeros((TM, LANES), F32)
    for hk in hits:
        chosen = chosen + jnp.where(hk, 1.0, 0.0)
    colsum = jnp.sum(chosen, axis=0, keepdims=True)

    @pl.when((ph == 0) & (t == 0))
    def _():
        cnt_scr[...] = jnp.zeros(cnt_scr.shape, F32)

    @pl.when(ph == 0)
    def _():
        cnt_scr[...] += colsum

    @pl.when((ph == 1) & (t == 0))
    def _():
        cnt = cnt_scr[...]
        padded = (((cnt.astype(I32) + (BLK - 1)) // BLK) * BLK).astype(F32)
        lane1 = lax.broadcasted_iota(I32, (1, LANES), 1)
        end = padded
        s = 1
        while s < LANES:
            end = end + jnp.where(lane1 >= s, pltpu.roll(end, s, 1), 0.0)
            s *= 2
        base_scr[...] = end - padded
        row = lax.broadcasted_iota(I32, (8, LANES), 0)
        meta = jnp.where(row == 0, cnt, jnp.where(row == 1, end - padded, jnp.where(row == 2, end, 0.0)))
        meta_ref[...] = meta.astype(I32)

    @pl.when(ph == 1)
    def _():
        ri = lax.broadcasted_iota(I32, (TM, TM), 0)
        ci = lax.broadcasted_iota(I32, (TM, TM), 1)
        before = jnp.where(ci < ri, 1.0, 0.0).astype(BF16)
        pos = jnp.dot(before, chosen.astype(BF16), preferred_element_type=F32) + base_scr[...]
        dest = jnp.zeros((TM, LANES), I32)
        for k, hk in enumerate(hits):
            dk = jnp.sum(jnp.where(hk, pos, 0.0), axis=-1, keepdims=True)
            dest = jnp.where(lane == k, dk.astype(I32), dest)
        dest_ref[...] = dest
        base_scr[...] += colsum


def _route(idx):
    return pl.pallas_call(
        _route_kernel, grid=(2, NT),
        in_specs=[pl.BlockSpec((TM, LANES), lambda p, t: (t, 0))],
        out_specs=[pl.BlockSpec((TM, LANES), lambda p, t: (p * t, 0)),
                   pl.BlockSpec((8, LANES), lambda p, t: (0, 0))],
        out_shape=[jax.ShapeDtypeStruct((T, LANES), I32), jax.ShapeDtypeStruct((8, LANES), I32)],
        scratch_shapes=[pltpu.VMEM((1, LANES), F32), pltpu.VMEM((1, LANES), F32)],
        compiler_params=_cparams(2), name="moe_route",
    )(idx)


def _row_copy(src, dst, r_src, r_dst, sem):
    return pltpu.make_async_copy(src.at[r_src], dst.at[r_dst], sem)


def _dispatch_kernel(dest_ref, cnt_ref, first_ref, end_ref, src, dst, sem):
    n_rows = dst.shape[0]
    step = pl.program_id(0)
    base = step * GROWS

    def issue(i, carry):
        _row_copy(src, dst, (base + i) // TOPK, dest_ref[i], sem).start()
        return carry

    lax.fori_loop(0, GROWS, issue, 0, unroll=8)

    def drain(i, carry):
        _row_copy(src, dst, 0, 0, sem).wait()
        return carry

    lax.fori_loop(0, GROWS, drain, 0, unroll=8)

    @pl.when(step == 0)
    def _():
        def fill(lo, hi):
            def start(r, c):
                _row_copy(src, dst, 0, r, sem).start()
                return c

            lax.fori_loop(lo, hi, start, 0)

            def done(r, c):
                _row_copy(src, dst, 0, 0, sem).wait()
                return c

            lax.fori_loop(lo, hi, done, 0)

        def per_expert(e, carry):
            fill(first_ref[e] + cnt_ref[e], end_ref[e])
            return carry

        lax.fori_loop(0, NE, per_expert, 0)
        fill(end_ref[NE - 1], n_rows)


def _dispatch(h2, dest_flat, cnt, first, end, n_rows):
    n = dest_flat.shape[0]
    assert n % GROWS == 0
    smem = pl.BlockSpec(memory_space=pltpu.SMEM)
    return pl.pallas_call(
        _dispatch_kernel, grid=(n // GROWS,),
        in_specs=[pl.BlockSpec((GROWS,), lambda s: (s,), memory_space=pltpu.SMEM), smem, smem, smem,
                  pl.BlockSpec(memory_space=pl.ANY)],
        out_specs=pl.BlockSpec(memory_space=pl.ANY),
        out_shape=jax.ShapeDtypeStruct((n_rows, ROW_TILES, LANES), F32),
        scratch_shapes=[pltpu.SemaphoreType.DMA],
        compiler_params=_cparams(1), name="moe_dispatch",
    )(dest_flat, cnt, first, end, h2)


def _gather_kernel(idx_ref, src, dst, sem):
    base = pl.program_id(0) * GROWS

    def issue(i, carry):
        _row_copy(src, dst, idx_ref[i], base + i, sem).start()
        return carry

    lax.fori_loop(0, GROWS, issue, 0, unroll=8)

    def drain(i, carry):
        _row_copy(src, dst, 0, 0, sem).wait()
        return carry

    lax.fori_loop(0, GROWS, drain, 0, unroll=8)


def _gather_rows(src, idx):
    n = idx.shape[0]
    assert n % GROWS == 0
    return pl.pallas_call(
        _gather_kernel, grid=(n // GROWS,),
        in_specs=[pl.BlockSpec((GROWS,), lambda s: (s,), memory_space=pltpu.SMEM),
                  pl.BlockSpec(memory_space=pl.ANY)],
        out_specs=pl.BlockSpec(memory_space=pl.ANY),
        out_shape=jax.ShapeDtypeStruct((n,) + src.shape[1:], src.dtype),
        scratch_shapes=[pltpu.SemaphoreType.DMA],
        compiler_params=_cparams(1), name="gather_rows",
    )(idx, src)


def _load_rows(ref3):
    return jnp.concatenate([ref3[:, s, :] for s in range(ROW_TILES)], axis=1)


def _expert_kernel(be_ref, nu_ref, xb_ref, wu_ref, bu_ref, wd_ref, bd_ref, yb_ref, wub, wdb):
    i = pl.program_id(0)
    used = i < nu_ref[0]

    @pl.when(used)
    def _():
        fresh = (i == 0) | (be_ref[i] != be_ref[jnp.maximum(i - 1, 0)])

        @pl.when(fresh)
        def _():
            wub[...] = wu_ref[0, 0].astype(BF16)
            wdb[...] = wd_ref[0, 0].astype(BF16)

        z = jnp.dot(_load_rows(xb_ref).astype(BF16), wub[...], preferred_element_type=F32) + bu_ref[0, 0]
        glu = jnp.minimum(z[:, :FE], LIMIT)
        lin = jnp.clip(z[:, FE:], -LIMIT, LIMIT)
        act = glu * jax.nn.sigmoid(ALPHA * glu) * (lin + 1.0)
        y = jnp.dot(act.astype(BF16), wdb[...], preferred_element_type=F32) + bd_ref[0, 0]
        for s in range(ROW_TILES):
            yb_ref[:, s, :] = y[:, s * LANES:(s + 1) * LANES]

    @pl.when(jnp.logical_not(used))
    def _():
        yb_ref[...] = jnp.zeros(yb_ref.shape, F32)


def _experts(layer, xb, blk_e, n_used, wu, bu, wd, bd):
    grid_spec = pltpu.PrefetchScalarGridSpec(
        num_scalar_prefetch=2, grid=(N_BLOCKS,),
        in_specs=[pl.BlockSpec((BLK, ROW_TILES, LANES),
                               lambda i, be, nu: (jnp.minimum(i, nu[0] - 1), 0, 0)),
                  pl.BlockSpec((1, 1, D, 2 * FE), lambda i, be, nu: (layer, be[i], 0, 0)),
                  pl.BlockSpec((1, 1, 1, 2 * FE), lambda i, be, nu: (layer, be[i], 0, 0)),
                  pl.BlockSpec((1, 1, FE, D), lambda i, be, nu: (layer, be[i], 0, 0)),
                  pl.BlockSpec((1, 1, 1, D), lambda i, be, nu: (layer, be[i], 0, 0))],
        out_specs=pl.BlockSpec((BLK, ROW_TILES, LANES), lambda i, be, nu: (i, 0, 0)),
        scratch_shapes=[pltpu.VMEM((D, 2 * FE), BF16), pltpu.VMEM((FE, D), BF16)])
    return pl.pallas_call(
        _expert_kernel, grid_spec=grid_spec,
        out_shape=jax.ShapeDtypeStruct((N_ROWS, ROW_TILES, LANES), F32),
        compiler_params=_cparams(1), name="experts",
    )(blk_e, n_used, xb, wu, bu, wd, bd)


def _combine_kernel(x_ref, yk_ref, gate_ref, g2p, g2s, fg_ref, o_ref, *, final):
    is_p = pl.program_id(0) < NPT
    gates = gate_ref[...]
    acc = gates[:, 0:1] * _load_rows(yk_ref.at[0])
    for k in range(1, TOPK):
        acc = acc + gates[:, k:k + 1] * _load_rows(yk_ref.at[k])
    xn = x_ref[...] + _pick(is_p, g2p, g2s) * acc
    o_ref[...] = _rms(xn, fg_ref[...]) if final else xn


def _combine(x, yk, gates, mod_p, mod_s, final_g, final):
    in_specs = ([_tile_spec(D), pl.BlockSpec((TOPK, TM, ROW_TILES, LANES), lambda t: (0, t, 0, 0)),
                 _tile_spec(LANES)] + _mod_specs(5) + [_full_spec((1, D))])
    return pl.pallas_call(
        functools.partial(_combine_kernel, final=final), grid=(NT,), in_specs=in_specs,
        out_specs=_tile_spec(D), out_shape=jax.ShapeDtypeStruct((T, D), F32),
        compiler_params=_cparams(1), name="moe_combine",
    )(x, yk, gates, mod_p, mod_s, final_g)


def _moe(layer, x, h2, idx, gates, mod_p, mod_s, wu, bu, wd, bd, final_g, final):
    dest, meta = _route(idx)
    cnt, first, end = meta[0, :NE], meta[1, :NE], meta[2, :NE]
    blk_first = jnp.arange(N_BLOCKS, dtype=I32) * BLK
    blk_e = jnp.minimum(jnp.sum((end[None, :] <= blk_first[:, None]).astype(I32), axis=1), NE - 1)
    n_used = end[NE - 1:] // BLK
    dest = dest[:, :TOPK]
    xb = _dispatch(h2, dest.reshape(-1), cnt, first, end, N_ROWS)
    yb = _experts(layer, xb, blk_e, n_used, wu, bu, wd, bd)
    yk = _gather_rows(yb, dest.T.reshape(-1)).reshape(TOPK, T, ROW_TILES, LANES)
    return _combine(x, yk, gates, mod_p, mod_s, final_g, final)


def _rope_tables():
    half = DK // 2
    inv = 1.0 / (ROPE_BASE ** jnp.linspace(0.0, 1.0, half, dtype=F32))

    def tab(pos):
        ang = pos.astype(F32)[:, None] * inv[None, :]
        cos, sin = jnp.cos(ang), jnp.sin(ang)
        return jnp.concatenate([cos, cos], -1), jnp.concatenate([-sin, sin], -1)

    cp, sp = tab(jnp.arange(L_P, dtype=I32))
    cs, ss = tab(PAST + jnp.arange(L_S, dtype=I32))
    cos = jnp.concatenate([jnp.tile(cp, (NB_P, 1)), jnp.tile(cs, (NB_S, 1))], 0)
    sin = jnp.concatenate([jnp.tile(sp, (NB_P, 1)), jnp.tile(ss, (NB_S, 1))], 0)
    return cos, sin


def _decay_tables(cl):
    lg = jnp.log(1.0 - 2.0 ** (-5.0 - jnp.arange(H, dtype=F32)))
    r = jnp.arange(CHUNK)
    idx = (r % cl).astype(F32)
    diff = idx[:, None] - idx[None, :]
    same = (r[:, None] // cl) == (r[None, :] // cl)
    decay = jnp.where((same & (diff >= 0))[None],
                      jnp.exp(lg[:, None, None] * jnp.maximum(diff, 0.0)[None]), 0.0)
    qdec = jnp.exp(lg[None, :] * (idx[:, None] + 1.0))
    kdec = jnp.exp(lg[None, :] * (cl - 1.0 - idx[:, None]))
    wide = lambda a: jnp.repeat(a, DK, axis=1)
    return {"decay": decay, "qdec": wide(qdec), "kdec": wide(kdec), "gam": jnp.exp(lg * cl)}


def kernel(x_prompt, x_sample, c_prompt, c_sample, state_ret, w_mod, b_mod, norm1_g, norm2_g,
           sgu_w_in, sgu_ln_g, sgu_ln_b, sgu_w_s, sgu_b_s, sgu_w_out, ret_w_in, ret_norm_g, ret_w_out,
           moe_w_router, moe_b_router, moe_w_up, moe_b_up, moe_w_down, moe_b_down, final_g):
    x = jnp.concatenate([x_prompt.reshape(T_P, D), x_sample.reshape(T_S, D)], 0)
    mod = _modulation(jnp.concatenate([c_prompt, c_sample], 0), w_mod, b_mod)
    cos_tab, sin_tab = _rope_tables()
    tabs_p = _decay_tables(CHUNK)
    tabs_s = _decay_tables(L_S)
    wr_pad = jnp.pad(moe_w_router, ((0, 0), (0, 0), (0, LANES - NE))).astype(BF16)
    br_pad = jnp.pad(moe_b_router, ((0, 0), (0, LANES - NE)), constant_values=-1e30)
    fg = final_g.reshape(1, D)
    b_up = moe_b_up.reshape(DEPTH, NE, 1, 2 * FE)
    b_down = moe_b_down.reshape(DEPTH, NE, 1, D)

    ret_p, v_rows = [], []
    s_all = lax.empty(state_ret.shape, F32)
    for i in range(DEPTH):
        j = i // 2
        mod_p = mod[i, :NB_P].reshape(NB_P, 1, 6 * D)
        mod_s = jnp.repeat(mod[i, NB_P:], L_S, axis=0)
        n1g = norm1_g[i].reshape(1, D)
        n2g = norm2_g[i].reshape(1, D)
        wr = wr_pad[i]
        br = br_pad[i].reshape(1, LANES)
        if i % 2 == 0:
            mixw = jnp.stack([sgu_w_s[j], jnp.tile(sgu_w_s[j][:, :L_S, :L_S], (1, SB, SB))])
            bias_p = jnp.repeat(sgu_b_s[j].T, SGU_GD, axis=1)
            bias_s = jnp.tile(bias_p[:L_S], (SB, 1))
            x, h2, idx, gates, v = _sgu_layer(
                x, mod_p, mod_s, n1g, n2g, sgu_w_in[j].astype(BF16), sgu_ln_g[j].reshape(1, SGU_W),
                sgu_ln_b[j].reshape(1, SGU_W), mixw, jnp.stack([bias_p, bias_s]),
                sgu_w_out[j].astype(BF16), wr, br)
            v_rows.append(v.reshape(NB_S, L_S, SGU_W))
        else:
            q, k, v, sg = _ret_proj(x, mod_p, mod_s, n1g, ret_w_in[j].astype(BF16), cos_tab, sin_tab)
            ng = ret_norm_g[j].reshape(1, VD)
            y, s_p = _ret_prompt(q, k, v, sg, tabs_p, ng)
            y, s_all = _ret_sample(j, q, k, v, sg, tabs_s, ng, state_ret, y, s_all)
            ret_p.append(s_p)
            x, h2, idx, gates = _ret_out(y, ret_w_out[j].astype(BF16), x, mod_p, mod_s, n2g, wr, br)
        x = _moe(i, x, h2, idx, gates, mod_p, mod_s, moe_w_up, b_up, moe_w_down, b_down,
                 fg, final=(i == DEPTH - 1))
    y_prompt = x[:T_P].reshape(NB_P, L_P, D)
    y_sample = x[T_P:].reshape(NB_S, L_S, D)
    return (y_prompt, y_sample, jnp.stack(ret_p), s_all, jnp.stack(v_rows))
```

```python
import functools

import jax
import jax.numpy as jnp
from jax import lax
from jax.experimental import pallas as pl
from jax.experimental.pallas import tpu as pltpu

F32 = jnp.float32
BF16 = jnp.bfloat16
I32 = jnp.int32

D = 1024
NB_P, L_P = 8, 2048
NB_S, L_S = 128, 8
PAST = 16384
DEPTH = 4
T_P = NB_P * L_P
T_S = NB_S * L_S
T = T_P + T_S
SGU_W = 2 * D
SGU_G = 8
SGU_GD = SGU_W // SGU_G
CHUNK = 128
H = 8
DK = D // H
DV = 2 * DK
QD = H * DK
VD = H * DV
RET_IN = 2 * QD + 2 * VD
ROPE_BASE = 10000.0
NE = 32
TOPK = 4
FE = D
ALPHA = 1.702
LIMIT = 7.0
EPS = 1e-6

LANES = 128
ROW_TILES = D // LANES
TM = 256
NPT = T_P // TM
NST = T_S // TM
NT = NPT + NST
TPB = L_P // TM
BLK = 256
TK = T * TOPK
N_BLOCKS = -(-(TK + NE * (BLK - 1)) // BLK)
N_ROWS = N_BLOCKS * BLK
SB = 16
VMEM_LIMIT = 56 * 1024 * 1024


def _cparams(n_axes):
    return pltpu.CompilerParams(dimension_semantics=("arbitrary",) * n_axes,
                                vmem_limit_bytes=VMEM_LIMIT)


def _rms(x, g):
    return (x * lax.rsqrt(jnp.mean(x * x, axis=-1, keepdims=True) + EPS)) * g


def _pick(is_p, p_ref, s_ref):
    return jnp.where(is_p, p_ref[0], s_ref[...])


def _mod_specs(j):
    return [
        pl.BlockSpec((1, 1, D), lambda t, j=j: (jnp.minimum(t // TPB, NB_P - 1), 0, j)),
        pl.BlockSpec((TM, D), lambda t, j=j: (jnp.maximum(t - NPT, 0), j)),
    ]


def _tile_spec(width):
    return pl.BlockSpec((TM, width), lambda t: (t, 0))


def _full_spec(shape):
    return pl.BlockSpec(shape, lambda *_: (0,) * len(shape))


def _mod_kernel(c_ref, w_ref, b_ref, o_ref):
    c = c_ref[...]
    cs = (c * jax.nn.sigmoid(c)).astype(BF16)
    o_ref[0] = jnp.dot(cs, w_ref[0].astype(BF16), preferred_element_type=F32) + b_ref[0]


def _modulation(c_all, w_mod, b_mod):
    tn = 1536
    nb = c_all.shape[0]
    return pl.pallas_call(
        _mod_kernel,
        grid=(DEPTH, 6 * D // tn),
        in_specs=[
            pl.BlockSpec((nb, D), lambda l, n: (0, 0)),
            pl.BlockSpec((1, D, tn), lambda l, n: (l, 0, n)),
            pl.BlockSpec((1, 1, tn), lambda l, n: (l, 0, n)),
        ],
        out_specs=pl.BlockSpec((1, nb, tn), lambda l, n: (l, 0, n)),
        out_shape=jax.ShapeDtypeStruct((DEPTH, nb, 6 * D), F32),
        compiler_params=_cparams(2),
        name="modulation",
    )(c_all, w_mod, b_mod.reshape(DEPTH, 1, 6 * D))


def _tail(is_p, x, y, g1p, g1s, sh2p, sh2s, sc2p, sc2s, n2g, wr, br, xo_ref, h2_ref, idx_ref, gate_ref):
    xn = x + _pick(is_p, g1p, g1s) * y
    xo_ref[...] = xn
    h2 = _rms(xn, n2g[...]) * (1.0 + _pick(is_p, sc2p, sc2s)) + _pick(is_p, sh2p, sh2s)
    _store_rows(h2_ref, h2)
    logit =jnp.dot(h2.astype(BF16), wr[...], preferred_element_type=F32) + br[...]
    lane = lax.broadcasted_iota(I32, logit.shape, 1)
    vals, ids = [], []
    for _ in range(TOPK):
        m = jnp.max(logit, axis=-1, keepdims=True)
        sel = jnp.min(jnp.where(logit == m, lane, LANES), axis=-1, keepdims=True)
        vals.append(m)
        ids.append(sel)
        logit = jnp.where(lane == sel, -jnp.inf, logit)
    es = [jnp.exp(v - vals[0]) for v in vals]
    tot = (es[0] + es[1]) + (es[2] + es[3])
    idx_out = jnp.zeros(logit.shape, I32)
    gate_out = jnp.zeros(logit.shape, F32)
    for k in range(TOPK):
        idx_out = jnp.where(lane == k, ids[k], idx_out)
        gate_out = jnp.where(lane == k, es[k] / tot, gate_out)
    idx_ref[...] = idx_out
    gate_ref[...] = gate_out


def _tail_in_specs():
    return (_mod_specs(2) + _mod_specs(3) + _mod_specs(4)
            + [_full_spec((1, D)), _full_spec((D, LANES)), _full_spec((1, LANES))])


def _tile_spec_rows():
    return pl.BlockSpec((TM * ROW_TILES, LANES), lambda t: (t, 0))


def _tail_out_specs():
    return [_tile_spec(D), _tile_spec_rows(), _tile_spec(LANES), _tile_spec(LANES)]


def _tail_out_shapes():
    return [jax.ShapeDtypeStruct((T, D), F32), jax.ShapeDtypeStruct((T * ROW_TILES, LANES), F32),
            jax.ShapeDtypeStruct((T, LANES), I32), jax.ShapeDtypeStruct((T, LANES), F32)]


def _sgu_kernel(x_ref, sh1p, sh1s, sc1p, sc1s, n1g, win, lng, lnb, mixw, mixb, wout,
                g1p, g1s, sh2p, sh2s, sc2p, sc2s, n2g, wr, br,
                xo_ref, h2_ref, idx_ref, gate_ref, v_ref, y_scr):
    t = pl.program_id(0)
    is_p = t < NPT
    x = x_ref[...]
    h = _rms(x, n1g[...]) * (1.0 + _pick(is_p, sc1p, sc1s)) + _pick(is_p, sh1p, sh1s)
    z = jnp.dot(h.astype(BF16), win[...], preferred_element_type=F32)
    z = 0.5 * z * (1.0 + lax.erf(z * (0.5 ** 0.5)))
    u = z[:, :SGU_W]
    v = z[:, SGU_W:]
    vc = v - jnp.mean(v, axis=-1, keepdims=True)
    vn = vc * lax.rsqrt(jnp.mean(vc * vc, axis=-1, keepdims=True) + EPS) * lng[...] + lnb[...]

    @pl.when(t >= NPT)
    def _():
        v_ref[...] = vn

    vb = vn.astype(BF16)
    ri = lax.broadcasted_iota(I32, (CHUNK, CHUNK), 0)
    ci = lax.broadcasted_iota(I32, (CHUNK, CHUNK), 1)
    causal = ci <= ri
    shift = jnp.broadcast_to(jnp.where(is_p, 7, 3), ri.shape)
    keep = causal & (lax.shift_right_logical(ri, shift) == lax.shift_right_logical(ci, shift))
    for g in range(SGU_G):
        wg = jnp.where(keep, mixw[0, g], 0.0).astype(BF16)
        for c in range(TM // CHUNK):
            rows = slice(c * CHUNK, (c + 1) * CHUNK)
            cols = slice(g * SGU_GD, (g + 1) * SGU_GD)
            mixed = jnp.dot(wg, vb[rows, cols], preferred_element_type=F32) + mixb[0, :, cols]
            y_scr[rows, cols] = (u[rows, cols] * mixed).astype(BF16)
    y = jnp.dot(y_scr[...], wout[...], preferred_element_type=F32)
    _tail(is_p, x, y, g1p, g1s, sh2p, sh2s, sc2p, sc2s, n2g, wr, br, xo_ref, h2_ref, idx_ref, gate_ref)


def _sgu_layer(x, mod_p, mod_s, n1g, n2g, win, lng, lnb, mixw, mixb, wout, wr, br):
    sel = lambda t: (jnp.where(t < NPT, 0, 1), 0, 0, 0)
    in_specs = ([_tile_spec(D)] + _mod_specs(0) + _mod_specs(1)
                + [_full_spec((1, D)), _full_spec((D, 2 * SGU_W)), _full_spec((1, SGU_W)),
                   _full_spec((1, SGU_W)),
                   pl.BlockSpec((1, SGU_G, CHUNK, CHUNK), sel),
                   pl.BlockSpec((1, CHUNK, SGU_W), lambda t: (jnp.where(t < NPT, 0, 1), 0, 0)),
                   _full_spec((SGU_W, D))]
                + _tail_in_specs())
    out_specs = _tail_out_specs() + [pl.BlockSpec((TM, SGU_W), lambda t: (jnp.maximum(t - NPT, 0), 0))]
    out_shape = _tail_out_shapes() + [jax.ShapeDtypeStruct((T_S, SGU_W), F32)]
    return pl.pallas_call(
        _sgu_kernel, grid=(NT,), in_specs=in_specs, out_specs=out_specs, out_shape=out_shape,
        scratch_shapes=[pltpu.VMEM((TM, SGU_W), BF16)],
        compiler_params=_cparams(1), name="sgu_layer",
    )(x, mod_p, mod_s, mod_p, mod_s, n1g, win, lng, lnb, mixw, mixb, wout,
      mod_p, mod_s, mod_p, mod_s, mod_p, mod_s, n2g, wr, br)


def _ret_proj_kernel(x_ref, sh1p, sh1s, sc1p, sc1s, n1g, win, cos_ref, sin_ref,
                     q_ref, k_ref, v_ref, sg_ref):
    t = pl.program_id(0)
    is_p = t < NPT
    x = x_ref[...]
    h = _rms(x, n1g[...]) * (1.0 + _pick(is_p, sc1p, sc1s)) + _pick(is_p, sh1p, sh1s)
    p = jnp.dot(h.astype(BF16), win[...], preferred_element_type=F32)
    cos = cos_ref[...]
    sin = sin_ref[...]
    for hd in range(H):
        cq = slice(hd * DK, (hd + 1) * DK)
        ck = slice(QD + hd * DK, QD + (hd + 1) * DK)
        qh = p[:, cq]
        kh = p[:, ck]
        q_ref[:, cq] = (qh * cos + pltpu.roll(qh, DK // 2, 1) * sin).astype(BF16)
        k_ref[:, cq] = ((kh * cos + pltpu.roll(kh, DK // 2, 1) * sin) * (DK ** -0.5)).astype(BF16)
    v_ref[...] = p[:, 2 * QD:2 * QD + VD].astype(BF16)
    g = p[:, 2 * QD + VD:]
    sg_ref[...] = (g * jax.nn.sigmoid(g)).astype(BF16)


def _ret_proj(x, mod_p, mod_s, n1g, win, cos_tab, sin_tab):
    in_specs = ([_tile_spec(D)] + _mod_specs(0) + _mod_specs(1)
                + [_full_spec((1, D)), _full_spec((D, RET_IN)), _tile_spec(DK), _tile_spec(DK)])
    out_specs = [_tile_spec(QD), _tile_spec(QD), _tile_spec(VD), _tile_spec(VD)]
    out_shape = [jax.ShapeDtypeStruct((T, QD), BF16), jax.ShapeDtypeStruct((T, QD), BF16),
                 jax.ShapeDtypeStruct((T, VD), BF16), jax.ShapeDtypeStruct((T, VD), BF16)]
    return pl.pallas_call(
        _ret_proj_kernel, grid=(NT,), in_specs=in_specs, out_specs=out_specs, out_shape=out_shape,
        compiler_params=_cparams(1), name="ret_proj",
    )(x, mod_p, mod_s, mod_p, mod_s, n1g, win, cos_tab, sin_tab)


def _ret_intra(q, k, v, decay, qdec, kdec):
    s = lax.dot_general(q, k, (((1,), (1,)), ((), ())), preferred_element_type=F32) * decay
    o = jnp.dot(s.astype(BF16), v, preferred_element_type=F32)
    return o, q.astype(F32) * qdec, k.astype(F32) * kdec


def _ret_finish(o, sg, ng):
    on = o * lax.rsqrt(jnp.mean(o * o, axis=-1, keepdims=True) + EPS)
    return (sg.astype(F32) * (on * ng)).astype(BF16)


def _ret_prompt_kernel(gam_ref, q_ref, k_ref, v_ref, sg_ref, dec_ref, qdec_ref, kdec_ref, ng_ref,
                       y_ref, s_ref):
    c = pl.program_id(1)

    @pl.when(c == 0)
    def _():
        s_ref[...] = jnp.zeros(s_ref.shape, F32)

    for hd in range(H):
        ck = slice(hd * DK, (hd + 1) * DK)
        cv = slice(hd * DV, (hd + 1) * DV)
        v = v_ref[:, cv]
        o, qd, kd = _ret_intra(q_ref[:, ck], k_ref[:, ck], v, dec_ref[hd], qdec_ref[:, ck], kdec_ref[:, ck])
        s_old = s_ref[0, hd]
        o = o + jnp.dot(qd.astype(BF16), s_old.astype(BF16), preferred_element_type=F32)
        s_ref[0, hd] = gam_ref[hd] * s_old + lax.dot_general(
            kd.astype(BF16), v, (((0,), (0,)), ((), ())), preferred_element_type=F32)
        y_ref[:, cv] = _ret_finish(o, sg_ref[:, cv], ng_ref[:, cv])


def _ret_prompt(q, k, v, sg, tabs, ng):
    nc = L_P // CHUNK
    row = lambda b, c: (b * nc + c, 0)
    smem = pl.BlockSpec(memory_space=pltpu.SMEM)
    in_specs = [smem,
                pl.BlockSpec((CHUNK, QD), row), pl.BlockSpec((CHUNK, QD), row),
                pl.BlockSpec((CHUNK, VD), row), pl.BlockSpec((CHUNK, VD), row),
                _full_spec((H, CHUNK, CHUNK)), _full_spec((CHUNK, QD)), _full_spec((CHUNK, QD)),
                _full_spec((1, VD))]
    out_specs = [pl.BlockSpec((CHUNK, VD), row),
                 pl.BlockSpec((1, H, DK, DV), lambda b, c: (b, 0, 0, 0))]
    out_shape = [jax.ShapeDtypeStruct((T, VD), BF16), jax.ShapeDtypeStruct((NB_P, H, DK, DV), F32)]
    return pl.pallas_call(
        _ret_prompt_kernel, grid=(NB_P, nc), in_specs=in_specs, out_specs=out_specs, out_shape=out_shape,
        compiler_params=_cparams(2), name="ret_prompt",
    )(tabs["gam"], q, k, v, sg, tabs["decay"], tabs["qdec"], tabs["kdec"], ng)


def _ret_sample_kernel(gam_ref, q_ref, k_ref, v_ref, sg_ref, dec_ref, qdec_ref, kdec_ref, ng_ref,
                       s_in_ref, y_in_ref, s_all_ref, y_ref, s_out_ref):
    del y_in_ref, s_all_ref
    s_in_ref = s_in_ref.at[0]
    s_out_ref = s_out_ref.at[0]
    hd = pl.program_id(1)
    v = v_ref[...]
    o, qd, kd = _ret_intra(q_ref[...], k_ref[...], v, dec_ref[0], qdec_ref[...], kdec_ref[...])
    gam = gam_ref[hd]
    row_batch = lax.broadcasted_iota(I32, (CHUNK, DK), 0) // L_S
    cross = []
    for b in range(SB):
        s_old = s_in_ref[b, 0]
        rows = slice(b * L_S, (b + 1) * L_S)
        cross.append(jnp.dot(qd[rows].astype(BF16), s_old.astype(BF16), preferred_element_type=F32))
        kb = jnp.where(row_batch == b, kd, 0.0).astype(BF16)
        s_out_ref[b, 0] = gam * s_old + lax.dot_general(
            kb, v, (((0,), (0,)), ((), ())), preferred_element_type=F32)
    o = o + jnp.concatenate(cross, axis=0)
    y_ref[...] = _ret_finish(o, sg_ref[...], ng_ref[...])


def _ret_sample(j, q, k, v, sg, tabs, ng, s_in, y_prev, s_all):
    base = T_P // CHUNK
    rq = lambda g, h: (base + g, h)
    st = pl.BlockSpec((1, SB, 1, DK, DV), lambda g, h: (j, g, h, 0, 0))
    smem = pl.BlockSpec(memory_space=pltpu.SMEM)
    in_specs = [smem,
                pl.BlockSpec((CHUNK, DK), rq), pl.BlockSpec((CHUNK, DK), rq),
                pl.BlockSpec((CHUNK, DV), rq), pl.BlockSpec((CHUNK, DV), rq),
                pl.BlockSpec((1, CHUNK, CHUNK), lambda g, h: (h, 0, 0)),
                pl.BlockSpec((CHUNK, DK), lambda g, h: (0, h)),
                pl.BlockSpec((CHUNK, DK), lambda g, h: (0, h)),
                pl.BlockSpec((1, DV), lambda g, h: (0, h)),
                st, pl.BlockSpec(memory_space=pl.ANY), pl.BlockSpec(memory_space=pl.ANY)]
    out_specs = [pl.BlockSpec((CHUNK, DV), rq), st]
    out_shape = [jax.ShapeDtypeStruct((T, VD), BF16), jax.ShapeDtypeStruct(s_all.shape, F32)]
    return pl.pallas_call(
        _ret_sample_kernel, grid=(NB_S // SB, H), in_specs=in_specs, out_specs=out_specs,
        out_shape=out_shape, input_output_aliases={10: 0, 11: 1},
        compiler_params=_cparams(2), name="ret_sample",
    )(tabs["gam"], q, k, v, sg, tabs["decay"], tabs["qdec"], tabs["kdec"], ng, s_in, y_prev, s_all)


def _ret_out_kernel(y_ref, wout, x_ref, g1p, g1s, sh2p, sh2s, sc2p, sc2s, n2g, wr, br,
                    xo_ref, h2_ref, idx_ref, gate_ref):
    is_p = pl.program_id(0) < NPT
    y = jnp.dot(y_ref[...], wout[...], preferred_element_type=F32)
    _tail(is_p, x_ref[...], y, g1p, g1s, sh2p, sh2s, sc2p, sc2s, n2g, wr, br,
          xo_ref, h2_ref, idx_ref, gate_ref)


def _ret_out(y, wout, x, mod_p, mod_s, n2g, wr, br):
    in_specs = [_tile_spec(VD), _full_spec((VD, D)), _tile_spec(D)] + _tail_in_specs()
    return pl.pallas_call(
        _ret_out_kernel, grid=(NT,), in_specs=in_specs, out_specs=_tail_out_specs(),
        out_shape=_tail_out_shapes(), compiler_params=_cparams(1), name="ret_out",
    )(y, wout, x, mod_p, mod_s, mod_p, mod_s, mod_p, mod_s, n2g, wr, br)


def _route_kernel(idx_ref, dest_ref, meta_ref, cnt_scr, base_scr):
    ph = pl.program_id(0)
    t = pl.program_id(1)
    idx = idx_ref[...]
    lane = lax.broadcasted_iota(I32, (TM, LANES), 1)
    hits = [lane == idx[:, k:k + 1] for k in range(TOPK)]
    chosen = jnp.zeros((TM, LANES), F32)
    for hk in hits:
        chosen = chosen + jnp.where(hk, 1.0, 0.0)
    colsum = jnp.sum(chosen, axis=0, keepdims=True)

    @pl.when((ph == 0) & (t == 0))
    def _():
        cnt_scr[...] = jnp.zeros(cnt_scr.shape, F32)

    @pl.when(ph == 0)
    def _():
        cnt_scr[...] += colsum

    @pl.when((ph == 1) & (t == 0))
    def _():
        cnt = cnt_scr[...]
        padded = (((cnt.astype(I32) + (BLK - 1)) // BLK) * BLK).astype(F32)
        lane1 = lax.broadcasted_iota(I32, (1, LANES), 1)
        end = padded
        s = 1
        while s < LANES:
            end = end + jnp.where(lane1 >= s, pltpu.roll(end, s, 1), 0.0)
            s *= 2
        base_scr[...] = end - padded
        row = lax.broadcasted_iota(I32, (8, LANES), 0)
        meta = jnp.where(row == 0, cnt, jnp.where(row == 1, end - padded, jnp.where(row == 2, end, 0.0)))
        meta_ref[...] = meta.astype(I32)

    @pl.when(ph == 1)
    def _():
        ri = lax.broadcasted_iota(I32, (TM, TM), 0)
        ci = lax.broadcasted_iota(I32, (TM, TM), 1)
        before = jnp.where(ci < ri, 1.0, 0.0).astype(BF16)
        pos = jnp.dot(before, chosen.astype(BF16), preferred_element_type=F32) + base_scr[...]
        dest = jnp.zeros((TM, LANES), I32)
        for k, hk in enumerate(hits):
            dk = jnp.sum(jnp.where(hk, pos, 0.0), axis=-1, keepdims=True)
            dest = jnp.where(lane == k, dk.astype(I32), dest)
        dest_ref[...] = dest
        base_scr[...] += colsum


def _route(idx):
    return pl.pallas_call(
        _route_kernel, grid=(2, NT),
        in_specs=[pl.BlockSpec((TM, LANES), lambda p, t: (t, 0))],
        out_specs=[pl.BlockSpec((TM, LANES), lambda p, t: (p * t, 0)),
                   pl.BlockSpec((8, LANES), lambda p, t: (0, 0))],
        out_shape=[jax.ShapeDtypeStruct((T, LANES), I32), jax.ShapeDtypeStruct((8, LANES), I32)],
        scratch_shapes=[pltpu.VMEM((1, LANES), F32), pltpu.VMEM((1, LANES), F32)],
        compiler_params=_cparams(2), name="moe_route",
    )(idx)


def _tile_rows(ref, r):
    return ref.at[pl.ds(pl.multiple_of(r * ROW_TILES, ROW_TILES), ROW_TILES)]


def _row_copy(src, dst, r_src, r_dst, sem):
    return pltpu.make_async_copy(_tile_rows(src, r_src), _tile_rows(dst, r_dst), sem)


def _load_rows(ref2, rows):
    return jnp.concatenate([ref2[pl.ds(s, rows, stride=ROW_TILES), :] for s in range(ROW_TILES)], axis=1)


def _store_rows(ref2, val):
    rows = val.shape[0]
    for s in range(ROW_TILES):
        ref2[pl.ds(s, rows, stride=ROW_TILES), :] = val[:, s * LANES:(s + 1) * LANES]


def _dispatch_kernel(dest_ref, cnt_ref, first_ref, end_ref, src, dst, sem):
    n_rows = dst.shape[0] // ROW_TILES
    n_copies = dest_ref.shape[0]
    step = pl.program_id(0)

    def issue(i, carry):
        _row_copy(src, dst, i // TOPK, dest_ref[i], sem).start()
        return carry

    lax.fori_loop(0, n_copies, issue, 0, unroll=8)

    def drain(i, carry):
        _row_copy(src, dst, 0, 0, sem).wait()
        return carry

    lax.fori_loop(0, n_copies, drain, 0, unroll=8)

    @pl.when(step == 0)
    def _():
        def fill(lo, hi):
            def start(r, c):
                _row_copy(src, dst, 0, r, sem).start()
                return c

            lax.fori_loop(lo, hi, start, 0)

            def done(r, c):
                _row_copy(src, dst, 0, 0, sem).wait()
                return c

            lax.fori_loop(lo, hi, done, 0)

        def per_expert(e, carry):
            fill(first_ref[e] + cnt_ref[e], end_ref[e])
            return carry

        lax.fori_loop(0, NE, per_expert, 0)
        fill(end_ref[NE - 1], n_rows)


def _dispatch(h2, dest_flat, cnt, first, end, n_rows):
    n_tiles = dest_flat.shape[0] // (TM * TOPK)
    smem = pl.BlockSpec(memory_space=pltpu.SMEM)
    return pl.pallas_call(
        _dispatch_kernel, grid=(n_tiles,),
        in_specs=[pl.BlockSpec((TM * TOPK,), lambda t: (t,), memory_space=pltpu.SMEM), smem, smem, smem,
                  pl.BlockSpec((TM * ROW_TILES, LANES), lambda t: (t, 0))],
        out_specs=pl.BlockSpec(memory_space=pl.ANY),
        out_shape=jax.ShapeDtypeStruct((n_rows * ROW_TILES, LANES), F32),
        scratch_shapes=[pltpu.SemaphoreType.DMA],
        compiler_params=_cparams(1), name="moe_dispatch",
    )(dest_flat, cnt, first, end, h2)


def _expert_kernel(be_ref, nu_ref, xb_ref, wu_ref, bu_ref, wd_ref, bd_ref, yb_ref, wub, wdb):
    i = pl.program_id(0)
    used = i < nu_ref[0]

    @pl.when(used)
    def _():
        fresh = (i == 0) | (be_ref[i] != be_ref[jnp.maximum(i - 1, 0)])

        @pl.when(fresh)
        def _():
            wub[...] = wu_ref[0, 0].astype(BF16)
            wdb[...] = wd_ref[0, 0].astype(BF16)

        xb = _load_rows(xb_ref, BLK).astype(BF16)
        z = jnp.dot(xb, wub[...], preferred_element_type=F32) + bu_ref[0, 0]
        glu = jnp.minimum(z[:, :FE], LIMIT)
        lin = jnp.clip(z[:, FE:], -LIMIT, LIMIT)
        act = glu * jax.nn.sigmoid(ALPHA * glu) * (lin + 1.0)
        _store_rows(yb_ref, jnp.dot(act.astype(BF16), wdb[...], preferred_element_type=F32) + bd_ref[0, 0])

    @pl.when(jnp.logical_not(used))
    def _():
        yb_ref[...] = jnp.zeros(yb_ref.shape, F32)


def _experts(layer, xb, blk_e, n_used, wu, bu, wd, bd):
    grid_spec = pltpu.PrefetchScalarGridSpec(
        num_scalar_prefetch=2, grid=(N_BLOCKS,),
        in_specs=[pl.BlockSpec((BLK * ROW_TILES, LANES),
                               lambda i, be, nu: (jnp.minimum(i, nu[0] - 1), 0)),
                  pl.BlockSpec((1, 1, D, 2 * FE), lambda i, be, nu: (layer, be[i], 0, 0)),
                  pl.BlockSpec((1, 1, 1, 2 * FE), lambda i, be, nu: (layer, be[i], 0, 0)),
                  pl.BlockSpec((1, 1, FE, D), lambda i, be, nu: (layer, be[i], 0, 0)),
                  pl.BlockSpec((1, 1, 1, D), lambda i, be, nu: (layer, be[i], 0, 0))],
        out_specs=pl.BlockSpec((BLK * ROW_TILES, LANES), lambda i, be, nu: (i, 0)),
        scratch_shapes=[pltpu.VMEM((D, 2 * FE), BF16), pltpu.VMEM((FE, D), BF16)])
    return pl.pallas_call(
        _expert_kernel, grid_spec=grid_spec,
        out_shape=jax.ShapeDtypeStruct((N_ROWS * ROW_TILES, LANES), F32),
        compiler_params=_cparams(1), name="experts",
    )(blk_e, n_used, xb, wu, bu, wd, bd)


def _combine_kernel(dest_ref, yb, x_ref, gate_ref, g2p, g2s, fg_ref, o_ref, ybuf, sem, *, final):
    is_p = pl.program_id(0) < NPT
    n_copies = dest_ref.shape[0]

    def issue(i, carry):
        _row_copy(yb, ybuf.at[i % TOPK], dest_ref[i], i // TOPK, sem).start()
        return carry

    lax.fori_loop(0, n_copies, issue, 0, unroll=8)

    def drain(i, carry):
        _row_copy(yb, ybuf.at[0], 0, 0, sem).wait()
        return carry

    lax.fori_loop(0, n_copies, drain, 0, unroll=8)

    gates = gate_ref[...]
    rows = x_ref.shape[0]
    acc = gates[:, 0:1] * _load_rows(ybuf.at[0], rows)
    for k in range(1, TOPK):
        acc = acc + gates[:, k:k + 1] * _load_rows(ybuf.at[k], rows)
    xn = x_ref[...] + _pick(is_p, g2p, g2s) * acc
    o_ref[...] = _rms(xn, fg_ref[...]) if final else xn


def _combine(x, yb, dest_flat, gates, mod_p, mod_s, final_g, final):
    in_specs = ([pl.BlockSpec((TM * TOPK,), lambda t: (t,), memory_space=pltpu.SMEM),
                 pl.BlockSpec(memory_space=pl.ANY), _tile_spec(D), _tile_spec(LANES)]
                + _mod_specs(5) + [_full_spec((1, D))])
    return pl.pallas_call(
        functools.partial(_combine_kernel, final=final), grid=(NT,), in_specs=in_specs,
        out_specs=_tile_spec(D), out_shape=jax.ShapeDtypeStruct((T, D), F32),
        scratch_shapes=[pltpu.VMEM((TOPK, TM * ROW_TILES, LANES), F32), pltpu.SemaphoreType.DMA],
        compiler_params=_cparams(1), name="moe_combine",
    )(dest_flat, yb, x, gates, mod_p, mod_s, final_g)


def _moe(layer, x, h2, idx, gates, mod_p, mod_s, wu, bu, wd, bd, final_g, final):
    dest, meta = _route(idx)
    cnt, first, end = meta[0, :NE], meta[1, :NE], meta[2, :NE]
    blk_first = jnp.arange(N_BLOCKS, dtype=I32) * BLK
    blk_e = jnp.minimum(jnp.sum((end[None, :] <= blk_first[:, None]).astype(I32), axis=1), NE - 1)
    n_used = end[NE - 1:] // BLK
    dest_flat = dest[:, :TOPK].reshape(-1)
    xb = _dispatch(h2, dest_flat, cnt, first, end, N_ROWS)
    yb = _experts(layer, xb, blk_e, n_used, wu, bu, wd, bd)
    return _combine(x, yb, dest_flat, gates, mod_p, mod_s, final_g, final)


def _rope_tables():
    half = DK // 2
    inv = 1.0 / (ROPE_BASE ** jnp.linspace(0.0, 1.0, half, dtype=F32))

    def tab(pos):
        ang = pos.astype(F32)[:, None] * inv[None, :]
        cos, sin = jnp.cos(ang), jnp.sin(ang)
        return jnp.concatenate([cos, cos], -1), jnp.concatenate([-sin, sin], -1)

    cp, sp = tab(jnp.arange(L_P, dtype=I32))
    cs, ss = tab(PAST + jnp.arange(L_S, dtype=I32))
    cos = jnp.concatenate([jnp.tile(cp, (NB_P, 1)), jnp.tile(cs, (NB_S, 1))], 0)
    sin = jnp.concatenate([jnp.tile(sp, (NB_P, 1)), jnp.tile(ss, (NB_S, 1))], 0)
    return cos, sin


def _decay_tables(cl):
    lg = jnp.log(1.0 - 2.0 ** (-5.0 - jnp.arange(H, dtype=F32)))
    r = jnp.arange(CHUNK)
    idx = (r % cl).astype(F32)
    diff = idx[:, None] - idx[None, :]
    same = (r[:, None] // cl) == (r[None, :] // cl)
    decay = jnp.where((same & (diff >= 0))[None],
                      jnp.exp(lg[:, None, None] * jnp.maximum(diff, 0.0)[None]), 0.0)
    qdec = jnp.exp(lg[None, :] * (idx[:, None] + 1.0))
    kdec = jnp.exp(lg[None, :] * (cl - 1.0 - idx[:, None]))
    wide = lambda a: jnp.repeat(a, DK, axis=1)
    return {"decay": decay, "qdec": wide(qdec), "kdec": wide(kdec), "gam": jnp.exp(lg * cl)}


def kernel(x_prompt, x_sample, c_prompt, c_sample, state_ret, w_mod, b_mod, norm1_g, norm2_g,
           sgu_w_in, sgu_ln_g, sgu_ln_b, sgu_w_s, sgu_b_s, sgu_w_out, ret_w_in, ret_norm_g, ret_w_out,
           moe_w_router, moe_b_router, moe_w_up, moe_b_up, moe_w_down, moe_b_down, final_g):
    x = jnp.concatenate([x_prompt.reshape(T_P, D), x_sample.reshape(T_S, D)], 0)
    mod = _modulation(jnp.concatenate([c_prompt, c_sample], 0), w_mod, b_mod)
    cos_tab, sin_tab = _rope_tables()
    tabs_p = _decay_tables(CHUNK)
    tabs_s = _decay_tables(L_S)
    wr_pad = jnp.pad(moe_w_router, ((0, 0), (0, 0), (0, LANES - NE))).astype(BF16)
    br_pad = jnp.pad(moe_b_router, ((0, 0), (0, LANES - NE)), constant_values=-1e30)
    fg = final_g.reshape(1, D)
    b_up = moe_b_up.reshape(DEPTH, NE, 1, 2 * FE)
    b_down = moe_b_down.reshape(DEPTH, NE, 1, D)

    ret_p, v_rows = [], []
    s_all = lax.empty(state_ret.shape, F32)
    for i in range(DEPTH):
        j = i // 2
        mod_p = mod[i, :NB_P].reshape(NB_P, 1, 6 * D)
        mod_s = jnp.repeat(mod[i, NB_P:], L_S, axis=0)
        n1g = norm1_g[i].reshape(1, D)
        n2g = norm2_g[i].reshape(1, D)
        wr = wr_pad[i]
        br = br_pad[i].reshape(1, LANES)
        if i % 2 == 0:
            mixw = jnp.stack([sgu_w_s[j], jnp.tile(sgu_w_s[j][:, :L_S, :L_S], (1, SB, SB))])
            bias_p = jnp.repeat(sgu_b_s[j].T, SGU_GD, axis=1)
            bias_s = jnp.tile(bias_p[:L_S], (SB, 1))
            x, h2, idx, gates, v = _sgu_layer(
                x, mod_p, mod_s, n1g, n2g, sgu_w_in[j].astype(BF16), sgu_ln_g[j].reshape(1, SGU_W),
                sgu_ln_b[j].reshape(1, SGU_W), mixw, jnp.stack([bias_p, bias_s]),
                sgu_w_out[j].astype(BF16), wr, br)
            v_rows.append(v.reshape(NB_S, L_S, SGU_W))
        else:
            q, k, v, sg = _ret_proj(x, mod_p, mod_s, n1g, ret_w_in[j].astype(BF16), cos_tab, sin_tab)
            ng = ret_norm_g[j].reshape(1, VD)
            y, s_p = _ret_prompt(q, k, v, sg, tabs_p, ng)
            y, s_all = _ret_sample(j, q, k, v, sg, tabs_s, ng, state_ret, y, s_all)
            ret_p.append(s_p)
            x, h2, idx, gates = _ret_out(y, ret_w_out[j].astype(BF16), x, mod_p, mod_s, n2g, wr, br)
        x = _moe(i, x, h2, idx, gates, mod_p, mod_s, moe_w_up, b_up, moe_w_down, b_down,
                 fg, final=(i == DEPTH - 1))
    y_prompt = x[:T_P].reshape(NB_P, L_P, D)
    y_sample = x[T_P:].reshape(NB_S, L_S, D)
    return (y_prompt, y_sample, jnp.stack(ret_p), s_all, jnp.stack(v_rows))
```

```python
import functools

import jax
import jax.numpy as jnp
from jax import lax
from jax.experimental import pallas as pl
from jax.experimental.pallas import tpu as pltpu

F32 = jnp.float32
BF16 = jnp.bfloat16
I32 = jnp.int32

D = 1024
NB_P, L_P = 8, 2048
NB_S, L_S = 128, 8
PAST = 16384
DEPTH = 4
T_P = NB_P * L_P
T_S = NB_S * L_S
T = T_P + T_S
SGU_W = 2 * D
SGU_G = 8
SGU_GD = SGU_W // SGU_G
CHUNK = 128
H = 8
DK = D // H
DV = 2 * DK
QD = H * DK
VD = H * DV
RET_IN = 2 * QD + 2 * VD
ROPE_BASE = 10000.0
NE = 32
TOPK = 4
FE = D
ALPHA = 1.702
LIMIT = 7.0
EPS = 1e-6

LANES = 128
ROW_TILES = D // LANES
TM = 256
NPT = T_P // TM
NST = T_S // TM
NT = NPT + NST
TPB = L_P // TM
BLK = 256
TK = T * TOPK
N_BLOCKS = -(-(TK + NE * (BLK - 1)) // BLK)
N_ROWS = N_BLOCKS * BLK
SB = 16
VMEM_LIMIT = 56 * 1024 * 1024


def _cparams(n_axes):
    return pltpu.CompilerParams(dimension_semantics=("arbitrary",) * n_axes,
                                vmem_limit_bytes=VMEM_LIMIT)


def _rms(x, g):
    return (x * lax.rsqrt(jnp.mean(x * x, axis=-1, keepdims=True) + EPS)) * g


def _pick(is_p, p_ref, s_ref):
    return jnp.where(is_p, p_ref[0], s_ref[...])


def _mod_specs(j):
    return [
        pl.BlockSpec((1, 1, D), lambda t, j=j: (jnp.minimum(t // TPB, NB_P - 1), 0, j)),
        pl.BlockSpec((TM, D), lambda t, j=j: (jnp.maximum(t - NPT, 0), j)),
    ]


def _tile_spec(width):
    return pl.BlockSpec((TM, width), lambda t: (t, 0))


def _full_spec(shape):
    return pl.BlockSpec(shape, lambda *_: (0,) * len(shape))


def _mod_kernel(c_ref, w_ref, b_ref, o_ref):
    c = c_ref[...]
    cs = (c * jax.nn.sigmoid(c)).astype(BF16)
    o_ref[0] = jnp.dot(cs, w_ref[0].astype(BF16), preferred_element_type=F32) + b_ref[0]


def _modulation(c_all, w_mod, b_mod):
    tn = 1536
    nb = c_all.shape[0]
    return pl.pallas_call(
        _mod_kernel,
        grid=(DEPTH, 6 * D // tn),
        in_specs=[
            pl.BlockSpec((nb, D), lambda l, n: (0, 0)),
            pl.BlockSpec((1, D, tn), lambda l, n: (l, 0, n)),
            pl.BlockSpec((1, 1, tn), lambda l, n: (l, 0, n)),
        ],
        out_specs=pl.BlockSpec((1, nb, tn), lambda l, n: (l, 0, n)),
        out_shape=jax.ShapeDtypeStruct((DEPTH, nb, 6 * D), F32),
        compiler_params=_cparams(2),
        name="modulation",
    )(c_all, w_mod, b_mod.reshape(DEPTH, 1, 6 * D))


def _tail(is_p, x, y, g1p, g1s, sh2p, sh2s, sc2p, sc2s, n2g, wr, br, xo_ref, h2_ref, idx_ref, gate_ref):
    xn = x + _pick(is_p, g1p, g1s) * y
    xo_ref[...] = xn
    h2 = (_rms(xn, n2g[...]) * (1.0 + _pick(is_p, sc2p, sc2s)) + _pick(is_p, sh2p, sh2s)).astype(BF16)
    h2_ref[...] = h2
    logit = jnp.dot(h2, wr[...], preferred_element_type=F32) + br[...]
    lane = lax.broadcasted_iota(I32, logit.shape, 1)
    vals, ids = [], []
    for _ in range(TOPK):
        m = jnp.max(logit, axis=-1, keepdims=True)
        sel = jnp.min(jnp.where(logit == m, lane, LANES), axis=-1, keepdims=True)
        vals.append(m)
        ids.append(sel)
        logit = jnp.where(lane == sel, -jnp.inf, logit)
    es = [jnp.exp(v - vals[0]) for v in vals]
    tot = (es[0] + es[1]) + (es[2] + es[3])
    idx_out = jnp.zeros(logit.shape, I32)
    gate_out = jnp.zeros(logit.shape, F32)
    for k in range(TOPK):
        idx_out = jnp.where(lane == k, ids[k], idx_out)
        gate_out = jnp.where(lane == k, es[k] / tot, gate_out)
    idx_ref[...] = idx_out
    gate_ref[...] = gate_out


def _tail_in_specs():
    return (_mod_specs(2) + _mod_specs(3) + _mod_specs(4)
            + [_full_spec((1, D)), _full_spec((D, LANES)), _full_spec((1, LANES))])


def _tail_out_specs():
    return [_tile_spec(D), _tile_spec(D), _tile_spec(LANES), _tile_spec(LANES)]


def _tail_out_shapes():
    return [jax.ShapeDtypeStruct((T, D), F32), jax.ShapeDtypeStruct((T, D), BF16),
            jax.ShapeDtypeStruct((T, LANES), I32), jax.ShapeDtypeStruct((T, LANES), F32)]


def _sgu_kernel(x_ref, sh1p, sh1s, sc1p, sc1s, n1g, win, lng, lnb, mixw, mixb, wout,
                g1p, g1s, sh2p, sh2s, sc2p, sc2s, n2g, wr, br,
                xo_ref, h2_ref, idx_ref, gate_ref, v_ref, y_scr):
    t = pl.program_id(0)
    is_p = t < NPT
    x = x_ref[...]
    h = _rms(x, n1g[...]) * (1.0 + _pick(is_p, sc1p, sc1s)) + _pick(is_p, sh1p, sh1s)
    z = jnp.dot(h.astype(BF16), win[...], preferred_element_type=F32)
    z = 0.5 * z * (1.0 + lax.erf(z * (0.5 ** 0.5)))
    u = z[:, :SGU_W]
    v = z[:, SGU_W:]
    vc = v - jnp.mean(v, axis=-1, keepdims=True)
    vn = vc * lax.rsqrt(jnp.mean(vc * vc, axis=-1, keepdims=True) + EPS) * lng[...] + lnb[...]

    @pl.when(t >= NPT)
    def _():
        v_ref[...] = vn

    vb = vn.astype(BF16)
    ri = lax.broadcasted_iota(I32, (CHUNK, CHUNK), 0)
    ci = lax.broadcasted_iota(I32, (CHUNK, CHUNK), 1)
    causal = ci <= ri
    shift = jnp.broadcast_to(jnp.where(is_p, 7, 3), ri.shape)
    keep = causal & (lax.shift_right_logical(ri, shift) == lax.shift_right_logical(ci, shift))
    for g in range(SGU_G):
        wg = jnp.where(keep, mixw[0, g], 0.0).astype(BF16)
        for c in range(TM // CHUNK):
            rows = slice(c * CHUNK, (c + 1) * CHUNK)
            cols = slice(g * SGU_GD, (g + 1) * SGU_GD)
            mixed = jnp.dot(wg, vb[rows, cols], preferred_element_type=F32) + mixb[0, :, cols]
            y_scr[rows, cols] = (u[rows, cols] * mixed).astype(BF16)
    y = jnp.dot(y_scr[...], wout[...], preferred_element_type=F32)
    _tail(is_p, x, y, g1p, g1s, sh2p, sh2s, sc2p, sc2s, n2g, wr, br, xo_ref, h2_ref, idx_ref, gate_ref)


def _sgu_layer(x, mod_p, mod_s, n1g, n2g, win, lng, lnb, mixw, mixb, wout, wr, br):
    sel = lambda t: (jnp.where(t < NPT, 0, 1), 0, 0, 0)
    in_specs = ([_tile_spec(D)] + _mod_specs(0) + _mod_specs(1)
                + [_full_spec((1, D)), _full_spec((D, 2 * SGU_W)), _full_spec((1, SGU_W)),
                   _full_spec((1, SGU_W)),
                   pl.BlockSpec((1, SGU_G, CHUNK, CHUNK), sel),
                   pl.BlockSpec((1, CHUNK, SGU_W), lambda t: (jnp.where(t < NPT, 0, 1), 0, 0)),
                   _full_spec((SGU_W, D))]
                + _tail_in_specs())
    out_specs = _tail_out_specs() + [pl.BlockSpec((TM, SGU_W), lambda t: (jnp.maximum(t - NPT, 0), 0))]
    out_shape = _tail_out_shapes() + [jax.ShapeDtypeStruct((T_S, SGU_W), F32)]
    return pl.pallas_call(
        _sgu_kernel, grid=(NT,), in_specs=in_specs, out_specs=out_specs, out_shape=out_shape,
        scratch_shapes=[pltpu.VMEM((TM, SGU_W), BF16)],
        compiler_params=_cparams(1), name="sgu_layer",
    )(x, mod_p, mod_s, mod_p, mod_s, n1g, win, lng, lnb, mixw, mixb, wout,
      mod_p, mod_s, mod_p, mod_s, mod_p, mod_s, n2g, wr, br)


def _ret_proj_kernel(x_ref, sh1p, sh1s, sc1p, sc1s, n1g, win, cos_ref, sin_ref,
                     q_ref, k_ref, v_ref, sg_ref):
    t = pl.program_id(0)
    is_p = t < NPT
    x = x_ref[...]
    h = _rms(x, n1g[...]) * (1.0 + _pick(is_p, sc1p, sc1s)) + _pick(is_p, sh1p, sh1s)
    p = jnp.dot(h.astype(BF16), win[...], preferred_element_type=F32)
    cos = cos_ref[...]
    sin = sin_ref[...]
    for hd in range(H):
        cq = slice(hd * DK, (hd + 1) * DK)
        ck = slice(QD + hd * DK, QD + (hd + 1) * DK)
        qh = p[:, cq]
        kh = p[:, ck]
        q_ref[:, cq] = (qh * cos + pltpu.roll(qh, DK // 2, 1) * sin).astype(BF16)
        k_ref[:, cq] = ((kh * cos + pltpu.roll(kh, DK // 2, 1) * sin) * (DK ** -0.5)).astype(BF16)
    v_ref[...] = p[:, 2 * QD:2 * QD + VD].astype(BF16)
    g = p[:, 2 * QD + VD:]
    sg_ref[...] = (g * jax.nn.sigmoid(g)).astype(BF16)


def _ret_proj(x, mod_p, mod_s, n1g, win, cos_tab, sin_tab):
    in_specs = ([_tile_spec(D)] + _mod_specs(0) + _mod_specs(1)
                + [_full_spec((1, D)), _full_spec((D, RET_IN)), _tile_spec(DK), _tile_spec(DK)])
    out_specs = [_tile_spec(QD), _tile_spec(QD), _tile_spec(VD), _tile_spec(VD)]
    out_shape = [jax.ShapeDtypeStruct((T, QD), BF16), jax.ShapeDtypeStruct((T, QD), BF16),
                 jax.ShapeDtypeStruct((T, VD), BF16), jax.ShapeDtypeStruct((T, VD), BF16)]
    return pl.pallas_call(
        _ret_proj_kernel, grid=(NT,), in_specs=in_specs, out_specs=out_specs, out_shape=out_shape,
        compiler_params=_cparams(1), name="ret_proj",
    )(x, mod_p, mod_s, mod_p, mod_s, n1g, win, cos_tab, sin_tab)


def _ret_intra(q, k, v, decay, qdec, kdec):
    s = lax.dot_general(q, k, (((1,), (1,)), ((), ())), preferred_element_type=F32) * decay
    o = jnp.dot(s.astype(BF16), v, preferred_element_type=F32)
    return o, q.astype(F32) * qdec, k.astype(F32) * kdec


def _ret_finish(o, sg, ng):
    on = o * lax.rsqrt(jnp.mean(o * o, axis=-1, keepdims=True) + EPS)
    return (sg.astype(F32) * (on * ng)).astype(BF16)


def _ret_prompt_kernel(gam_ref, q_ref, k_ref, v_ref, sg_ref, dec_ref, qdec_ref, kdec_ref, ng_ref,
                       y_ref, s_ref):
    c = pl.program_id(1)

    @pl.when(c == 0)
    def _():
        s_ref[...] = jnp.zeros(s_ref.shape, F32)

    for hd in range(H):
        ck = slice(hd * DK, (hd + 1) * DK)
        cv = slice(hd * DV, (hd + 1) * DV)
        v = v_ref[:, cv]
        o, qd, kd = _ret_intra(q_ref[:, ck], k_ref[:, ck], v, dec_ref[hd], qdec_ref[:, ck], kdec_ref[:, ck])
        s_old = s_ref[0, hd]
        o = o + jnp.dot(qd.astype(BF16), s_old.astype(BF16), preferred_element_type=F32)
        s_ref[0, hd] = gam_ref[hd] * s_old + lax.dot_general(
            kd.astype(BF16), v, (((0,), (0,)), ((), ())), preferred_element_type=F32)
        y_ref[:, cv] = _ret_finish(o, sg_ref[:, cv], ng_ref[:, cv])


def _ret_prompt(q, k, v, sg, tabs, ng):
    nc = L_P // CHUNK
    row = lambda b, c: (b * nc + c, 0)
    smem = pl.BlockSpec(memory_space=pltpu.SMEM)
    in_specs = [smem,
                pl.BlockSpec((CHUNK, QD), row), pl.BlockSpec((CHUNK, QD), row),
                pl.BlockSpec((CHUNK, VD), row), pl.BlockSpec((CHUNK, VD), row),
                _full_spec((H, CHUNK, CHUNK)), _full_spec((CHUNK, QD)), _full_spec((CHUNK, QD)),
                _full_spec((1, VD))]
    out_specs = [pl.BlockSpec((CHUNK, VD), row),
                 pl.BlockSpec((1, H, DK, DV), lambda b, c: (b, 0, 0, 0))]
    out_shape = [jax.ShapeDtypeStruct((T, VD), BF16), jax.ShapeDtypeStruct((NB_P, H, DK, DV), F32)]
    return pl.pallas_call(
        _ret_prompt_kernel, grid=(NB_P, nc), in_specs=in_specs, out_specs=out_specs, out_shape=out_shape,
        compiler_params=_cparams(2), name="ret_prompt",
    )(tabs["gam"], q, k, v, sg, tabs["decay"], tabs["qdec"], tabs["kdec"], ng)


def _ret_sample_kernel(gam_ref, q_ref, k_ref, v_ref, sg_ref, dec_ref, qdec_ref, kdec_ref, ng_ref,
                       s_in_ref, y_in_ref, s_all_ref, y_ref, s_out_ref):
    del y_in_ref, s_all_ref
    s_in_ref = s_in_ref.at[0]
    s_out_ref = s_out_ref.at[0]
    hd = pl.program_id(1)
    v = v_ref[...]
    o, qd, kd = _ret_intra(q_ref[...], k_ref[...], v, dec_ref[0], qdec_ref[...], kdec_ref[...])
    gam = gam_ref[hd]
    row_batch = lax.broadcasted_iota(I32, (CHUNK, DK), 0) // L_S
    cross = []
    for b in range(SB):
        s_old = s_in_ref[b, 0]
        rows = slice(b * L_S, (b + 1) * L_S)
        cross.append(jnp.dot(qd[rows].astype(BF16), s_old.astype(BF16), preferred_element_type=F32))
        kb = jnp.where(row_batch == b, kd, 0.0).astype(BF16)
        s_out_ref[b, 0] = gam * s_old + lax.dot_general(
            kb, v, (((0,), (0,)), ((), ())), preferred_element_type=F32)
    o = o + jnp.concatenate(cross, axis=0)
    y_ref[...] = _ret_finish(o, sg_ref[...], ng_ref[...])


def _ret_sample(j, q, k, v, sg, tabs, ng, s_in, y_prev, s_all):
    base = T_P // CHUNK
    rq = lambda g, h: (base + g, h)
    st = pl.BlockSpec((1, SB, 1, DK, DV), lambda g, h: (j, g, h, 0, 0))
    smem = pl.BlockSpec(memory_space=pltpu.SMEM)
    in_specs = [smem,
                pl.BlockSpec((CHUNK, DK), rq), pl.BlockSpec((CHUNK, DK), rq),
                pl.BlockSpec((CHUNK, DV), rq), pl.BlockSpec((CHUNK, DV), rq),
                pl.BlockSpec((1, CHUNK, CHUNK), lambda g, h: (h, 0, 0)),
                pl.BlockSpec((CHUNK, DK), lambda g, h: (0, h)),
                pl.BlockSpec((CHUNK, DK), lambda g, h: (0, h)),
                pl.BlockSpec((1, DV), lambda g, h: (0, h)),
                st, pl.BlockSpec(memory_space=pl.ANY), pl.BlockSpec(memory_space=pl.ANY)]
    out_specs = [pl.BlockSpec((CHUNK, DV), rq), st]
    out_shape = [jax.ShapeDtypeStruct((T, VD), BF16), jax.ShapeDtypeStruct(s_all.shape, F32)]
    return pl.pallas_call(
        _ret_sample_kernel, grid=(NB_S // SB, H), in_specs=in_specs, out_specs=out_specs,
        out_shape=out_shape, input_output_aliases={10: 0, 11: 1},
        compiler_params=_cparams(2), name="ret_sample",
    )(tabs["gam"], q, k, v, sg, tabs["decay"], tabs["qdec"], tabs["kdec"], ng, s_in, y_prev, s_all)


def _ret_out_kernel(y_ref, wout, x_ref, g1p, g1s, sh2p, sh2s, sc2p, sc2s, n2g, wr, br,
                    xo_ref, h2_ref, idx_ref, gate_ref):
    is_p = pl.program_id(0) < NPT
    y = jnp.dot(y_ref[...], wout[...], preferred_element_type=F32)
    _tail(is_p, x_ref[...], y, g1p, g1s, sh2p, sh2s, sc2p, sc2s, n2g, wr, br,
          xo_ref, h2_ref, idx_ref, gate_ref)


def _ret_out(y, wout, x, mod_p, mod_s, n2g, wr, br):
    in_specs = [_tile_spec(VD), _full_spec((VD, D)), _tile_spec(D)] + _tail_in_specs()
    return pl.pallas_call(
        _ret_out_kernel, grid=(NT,), in_specs=in_specs, out_specs=_tail_out_specs(),
        out_shape=_tail_out_shapes(), compiler_params=_cparams(1), name="ret_out",
    )(y, wout, x, mod_p, mod_s, mod_p, mod_s, mod_p, mod_s, n2g, wr, br)


def _lane_prefix(v, lane1):
    s = 1
    while s < LANES:
        v = v + jnp.where(lane1 >= s, pltpu.roll(v, s, 1), 0.0)
        s *= 2
    return v


def _route_kernel(idx_ref, lp_ref, lpt_ref, runs_ref, meta_ref, cnt_scr, base_scr):
    ph = pl.program_id(0)
    t = pl.program_id(1)
    idx = idx_ref[...]
    lane = lax.broadcasted_iota(I32, (TM, LANES), 1)
    lane1 = lax.broadcasted_iota(I32, (1, LANES), 1)
    hits = [lane == idx[:, k:k + 1] for k in range(TOPK)]
    chosen = jnp.zeros((TM, LANES), F32)
    for hk in hits:
        chosen = chosen + jnp.where(hk, 1.0, 0.0)
    colsum = jnp.sum(chosen, axis=0, keepdims=True)

    @pl.when((ph == 0) & (t == 0))
    def _():
        cnt_scr[...] = jnp.zeros(cnt_scr.shape, F32)

    @pl.when(ph == 0)
    def _():
        cnt_scr[...] += colsum

    @pl.when((ph == 1) & (t == 0))
    def _():
        cnt = cnt_scr[...]
        padded = (((cnt.astype(I32) + (BLK - 1)) // BLK) * BLK).astype(F32)
        end = _lane_prefix(padded, lane1)
        base_scr[...] = end - padded
        row = lax.broadcasted_iota(I32, (8, LANES), 0)
        meta = jnp.where(row == 0, cnt, jnp.where(row == 1, end - padded, jnp.where(row == 2, end, 0.0)))
        meta_ref[...] = meta.astype(I32)

    @pl.when(ph == 1)
    def _():
        ri = lax.broadcasted_iota(I32, (TM, TM), 0)
        ci = lax.broadcasted_iota(I32, (TM, TM), 1)
        before = jnp.where(ci < ri, 1.0, 0.0).astype(BF16)
        loff = _lane_prefix(colsum, lane1) - colsum
        pos = jnp.dot(before, chosen.astype(BF16), preferred_element_type=F32) + loff
        lp = jnp.zeros((TM, LANES), F32)
        for k, hk in enumerate(hits):
            lp = jnp.where(lane == k, jnp.sum(jnp.where(hk, pos, 0.0), axis=-1, keepdims=True), lp)
        lp_ref[...] = lp.astype(I32)
        lpt_ref[...] = lp.T[:8].astype(I32)
        row = lax.broadcasted_iota(I32, (8, LANES), 0)
        runs = jnp.where(row == 0, colsum, jnp.where(row == 1, loff, jnp.where(row == 2, base_scr[...], 0.0)))
        runs_ref[0] = runs.astype(I32)
        base_scr[...] += colsum


def _route(idx):
    n_tiles = idx.shape[0] // TM
    return pl.pallas_call(
        _route_kernel, grid=(2, n_tiles),
        in_specs=[pl.BlockSpec((TM, LANES), lambda p, t: (t, 0))],
        out_specs=[pl.BlockSpec((TM, LANES), lambda p, t: (p * t, 0)),
                   pl.BlockSpec((8, TM), lambda p, t: (p * t, 0)),
                   pl.BlockSpec((1, 8, LANES), lambda p, t: (p * t, 0, 0)),
                   pl.BlockSpec((8, LANES), lambda p, t: (0, 0))],
        out_shape=[jax.ShapeDtypeStruct((n_tiles * TM, LANES), I32),
                   jax.ShapeDtypeStruct((n_tiles * 8, TM), I32),
                   jax.ShapeDtypeStruct((n_tiles, 8, LANES), I32),
                   jax.ShapeDtypeStruct((8, LANES), I32)],
        scratch_shapes=[pltpu.VMEM((1, LANES), F32), pltpu.VMEM((1, LANES), F32)],
        compiler_params=_cparams(2), name="moe_route",
    )(idx)


def _tile_rows(ref, r, n):
    return ref.at[pl.ds(pl.multiple_of(r * ROW_TILES, ROW_TILES), n * ROW_TILES)]


def _run_copy(src, dst, r_src, r_dst, n, sem, wait):
    for b in reversed(range(TM.bit_length())):
        size = 1 << b
        off = (n >> (b + 1)) << (b + 1)

        def piece(off=off, size=size):
            cp = pltpu.make_async_copy(_tile_rows(src, r_src + off, size),
                                       _tile_rows(dst, r_dst + off, size), sem)
            if wait:
                cp.wait()
            else:
                cp.start()

        if isinstance(n, int):
            if n & size:
                piece()
        else:
            pl.when((n & size) != 0)(piece)


def _load_rows(ref2, rows):
    return jnp.concatenate([ref2[pl.ds(s, rows, stride=ROW_TILES), :] for s in range(ROW_TILES)], axis=1)


def _store_rows(ref2, val):
    rows = val.shape[0]
    for s in range(ROW_TILES):
        ref2[pl.ds(s, rows, stride=ROW_TILES), :] = val[:, s * LANES:(s + 1) * LANES]


def _dispatch_kernel(len_ref, pos_ref, row_ref, cnt_ref, first_ref, end_ref, lpt_ref, h2_ref, dst,
                     buf, zbuf, sem):
    n_rows = dst.shape[0] // ROW_TILES
    step = pl.program_id(0)
    lpt = lpt_ref[...]
    p = lax.broadcasted_iota(I32, (TM * TOPK, TM), 0)
    pick = lpt[0:1, :] == p
    for k in range(1, TOPK):
        pick = pick | (lpt[k:k + 1, :] == p)
    perm = jnp.where(pick, 1.0, 0.0).astype(BF16)
    _store_rows(buf, jnp.dot(perm, h2_ref[...], preferred_element_type=F32))

    for wait in (False, True):
        def per_expert(e, carry, wait=wait):
            _run_copy(buf, dst, pos_ref[e], row_ref[e], len_ref[e], sem, wait)
            return carry

        lax.fori_loop(0, NE, per_expert, 0)

    @pl.when(step == 0)
    def _():
        zbuf[...] = jnp.zeros(zbuf.shape, F32)
        n_tail = (n_rows - end_ref[NE - 1]) // TM
        for wait in (False, True):
            def pad(e, carry, wait=wait):
                lo = first_ref[e] + cnt_ref[e]
                _run_copy(zbuf, dst, 0, lo, end_ref[e] - lo, sem, wait)
                return carry

            lax.fori_loop(0, NE, pad, 0)

            def tail(j, carry, wait=wait):
                _run_copy(zbuf, dst, 0, end_ref[NE - 1] + j * TM, TM, sem, wait)
                return carry

            lax.fori_loop(0, n_tail, tail, 0)


def _dispatch(h2, lpt, run_len, run_pos, run_row, cnt, first, end, n_rows):
    assert BLK == TM
    n_tiles = h2.shape[0] // TM
    smem = pl.BlockSpec(memory_space=pltpu.SMEM)
    per_tile = pl.BlockSpec((LANES,), lambda t: (t,), memory_space=pltpu.SMEM)
    return pl.pallas_call(
        _dispatch_kernel, grid=(n_tiles,),
        in_specs=[per_tile, per_tile, per_tile, smem, smem, smem,
                  pl.BlockSpec((8, TM), lambda t: (t, 0)), _tile_spec(D)],
        out_specs=pl.BlockSpec(memory_space=pl.ANY),
        out_shape=jax.ShapeDtypeStruct((n_rows * ROW_TILES, LANES), F32),
        scratch_shapes=[pltpu.VMEM((TM * TOPK * ROW_TILES, LANES), F32),
                        pltpu.VMEM((TM * ROW_TILES, LANES), F32), pltpu.SemaphoreType.DMA],
        compiler_params=_cparams(1), name="moe_dispatch",
    )(run_len, run_pos, run_row, cnt, first, end, lpt, h2)


def _expert_kernel(be_ref, nu_ref, xb_ref, wu_ref, bu_ref, wd_ref, bd_ref, yb_ref, wub, wdb):
    i = pl.program_id(0)
    used = i < nu_ref[0]

    @pl.when(used)
    def _():
        fresh = (i == 0) | (be_ref[i] != be_ref[jnp.maximum(i - 1, 0)])

        @pl.when(fresh)
        def _():
            wub[...] = wu_ref[0, 0].astype(BF16)
            wdb[...] = wd_ref[0, 0].astype(BF16)

        xb = _load_rows(xb_ref, BLK).astype(BF16)
        z = jnp.dot(xb, wub[...], preferred_element_type=F32) + bu_ref[0, 0]
        glu = jnp.minimum(z[:, :FE], LIMIT)
        lin = jnp.clip(z[:, FE:], -LIMIT, LIMIT)
        act = glu * jax.nn.sigmoid(ALPHA * glu) * (lin + 1.0)
        _store_rows(yb_ref, jnp.dot(act.astype(BF16), wdb[...], preferred_element_type=F32) + bd_ref[0, 0])

    @pl.when(jnp.logical_not(used))
    def _():
        yb_ref[...] = jnp.zeros(yb_ref.shape, F32)


def _experts(layer, xb, blk_e, n_used, wu, bu, wd, bd):
    grid_spec = pltpu.PrefetchScalarGridSpec(
        num_scalar_prefetch=2, grid=(N_BLOCKS,),
        in_specs=[pl.BlockSpec((BLK * ROW_TILES, LANES),
                               lambda i, be, nu: (jnp.minimum(i, nu[0] - 1), 0)),
                  pl.BlockSpec((1, 1, D, 2 * FE), lambda i, be, nu: (layer, be[i], 0, 0)),
                  pl.BlockSpec((1, 1, 1, 2 * FE), lambda i, be, nu: (layer, be[i], 0, 0)),
                  pl.BlockSpec((1, 1, FE, D), lambda i, be, nu: (layer, be[i], 0, 0)),
                  pl.BlockSpec((1, 1, 1, D), lambda i, be, nu: (layer, be[i], 0, 0))],
        out_specs=pl.BlockSpec((BLK * ROW_TILES, LANES), lambda i, be, nu: (i, 0)),
        scratch_shapes=[pltpu.VMEM((D, 2 * FE), BF16), pltpu.VMEM((FE, D), BF16)])
    return pl.pallas_call(
        _expert_kernel, grid_spec=grid_spec,
        out_shape=jax.ShapeDtypeStruct((N_ROWS * ROW_TILES, LANES), F32),
        compiler_params=_cparams(1), name="experts",
    )(blk_e, n_used, xb, wu, bu, wd, bd)


def _combine_kernel(len_ref, pos_ref, row_ref, yb, lp_ref, x_ref, gate_ref, g2p, g2s, fg_ref, o_ref,
                    ybuf, sem, *, final):
    is_p = pl.program_id(0) < NPT
    for wait in (False, True):
        def per_expert(e, carry, wait=wait):
            _run_copy(yb, ybuf, row_ref[e], pos_ref[e], len_ref[e], sem, wait)
            return carry

        lax.fori_loop(0, NE, per_expert, 0)

    rows = x_ref.shape[0]
    lp = lp_ref[...]
    gates = gate_ref[...]
    p = lax.broadcasted_iota(I32, (rows, rows * TOPK), 1)
    weights = jnp.zeros((rows, rows * TOPK), F32)
    for k in range(TOPK):
        weights = jnp.where(lp[:, k:k + 1] == p, gates[:, k:k + 1], weights)
    acc = jnp.dot(weights.astype(BF16), _load_rows(ybuf, rows * TOPK).astype(BF16),
                  preferred_element_type=F32)
    xn = x_ref[...] + _pick(is_p, g2p, g2s) * acc
    o_ref[...] = _rms(xn, fg_ref[...]) if final else xn


def _combine(x, yb, lp, run_len, run_pos, run_row, gates, mod_p, mod_s, final_g, final):
    n_tiles = x.shape[0] // TM
    per_tile = pl.BlockSpec((LANES,), lambda t: (t,), memory_space=pltpu.SMEM)
    in_specs = ([per_tile, per_tile, per_tile, pl.BlockSpec(memory_space=pl.ANY),
                 _tile_spec(LANES), _tile_spec(D), _tile_spec(LANES)]
                + _mod_specs(5) + [_full_spec((1, D))])
    return pl.pallas_call(
        functools.partial(_combine_kernel, final=final), grid=(n_tiles,), in_specs=in_specs,
        out_specs=_tile_spec(D), out_shape=jax.ShapeDtypeStruct(x.shape, F32),
        scratch_shapes=[pltpu.VMEM((TM * TOPK * ROW_TILES, LANES), F32), pltpu.SemaphoreType.DMA],
        compiler_params=_cparams(1), name="moe_combine",
    )(run_len, run_pos, run_row, yb, lp, x, gates, mod_p, mod_s, final_g)


def _moe(layer, x, h2, idx, gates, mod_p, mod_s, wu, bu, wd, bd, final_g, final):
    lp, lpt, runs, meta = _route(idx)
    cnt, first, end = meta[0, :NE], meta[1, :NE], meta[2, :NE]
    run_len, run_pos, run_row = (runs[:, r, :].reshape(-1) for r in range(3))
    blk_first = jnp.arange(N_BLOCKS, dtype=I32) * BLK
    blk_e = jnp.minimum(jnp.sum((end[None, :] <= blk_first[:, None]).astype(I32), axis=1), NE - 1)
    n_used = end[NE - 1:] // BLK
    xb = _dispatch(h2, lpt, run_len, run_pos, run_row, cnt, first, end, N_ROWS)
    yb = _experts(layer, xb, blk_e, n_used, wu, bu, wd, bd)
    return _combine(x, yb, lp, run_len, run_pos, run_row, gates, mod_p, mod_s, final_g, final)


def _rope_tables():
    half = DK // 2
    inv = 1.0 / (ROPE_BASE ** jnp.linspace(0.0, 1.0, half, dtype=F32))

    def tab(pos):
        ang = pos.astype(F32)[:, None] * inv[None, :]
        cos, sin = jnp.cos(ang), jnp.sin(ang)
        return jnp.concatenate([cos, cos], -1), jnp.concatenate([-sin, sin], -1)

    cp, sp = tab(jnp.arange(L_P, dtype=I32))
    cs, ss = tab(PAST + jnp.arange(L_S, dtype=I32))
    cos = jnp.concatenate([jnp.tile(cp, (NB_P, 1)), jnp.tile(cs, (NB_S, 1))], 0)
    sin = jnp.concatenate([jnp.tile(sp, (NB_P, 1)), jnp.tile(ss, (NB_S, 1))], 0)
    return cos, sin


def _decay_tables(cl):
    lg = jnp.log(1.0 - 2.0 ** (-5.0 - jnp.arange(H, dtype=F32)))
    r = jnp.arange(CHUNK)
    idx = (r % cl).astype(F32)
    diff = idx[:, None] - idx[None, :]
    same = (r[:, None] // cl) == (r[None, :] // cl)
    decay = jnp.where((same & (diff >= 0))[None],
                      jnp.exp(lg[:, None, None] * jnp.maximum(diff, 0.0)[None]), 0.0)
    qdec = jnp.exp(lg[None, :] * (idx[:, None] + 1.0))
    kdec = jnp.exp(lg[None, :] * (cl - 1.0 - idx[:, None]))
    wide = lambda a: jnp.repeat(a, DK, axis=1)
    return {"decay": decay, "qdec": wide(qdec), "kdec": wide(kdec), "gam": jnp.exp(lg * cl)}


def kernel(x_prompt, x_sample, c_prompt, c_sample, state_ret, w_mod, b_mod, norm1_g, norm2_g,
           sgu_w_in, sgu_ln_g, sgu_ln_b, sgu_w_s, sgu_b_s, sgu_w_out, ret_w_in, ret_norm_g, ret_w_out,
           moe_w_router, moe_b_router, moe_w_up, moe_b_up, moe_w_down, moe_b_down, final_g):
    x = jnp.concatenate([x_prompt.reshape(T_P, D), x_sample.reshape(T_S, D)], 0)
    mod = _modulation(jnp.concatenate([c_prompt, c_sample], 0), w_mod, b_mod)
    cos_tab, sin_tab = _rope_tables()
    tabs_p = _decay_tables(CHUNK)
    tabs_s = _decay_tables(L_S)
    wr_pad = jnp.pad(moe_w_router, ((0, 0), (0, 0), (0, LANES - NE))).astype(BF16)
    br_pad = jnp.pad(moe_b_router, ((0, 0), (0, LANES - NE)), constant_values=-1e30)
    fg = final_g.reshape(1, D)
    b_up = moe_b_up.reshape(DEPTH, NE, 1, 2 * FE)
    b_down = moe_b_down.reshape(DEPTH, NE, 1, D)

    ret_p, v_rows = [], []
    s_all = lax.empty(state_ret.shape, F32)
    for i in range(DEPTH):
        j = i // 2
        mod_p = mod[i, :NB_P].reshape(NB_P, 1, 6 * D)
        mod_s = jnp.repeat(mod[i, NB_P:], L_S, axis=0)
        n1g = norm1_g[i].reshape(1, D)
        n2g = norm2_g[i].reshape(1, D)
        wr = wr_pad[i]
        br = br_pad[i].reshape(1, LANES)
        if i % 2 == 0:
            mixw = jnp.stack([sgu_w_s[j], jnp.tile(sgu_w_s[j][:, :L_S, :L_S], (1, SB, SB))])
            bias_p = jnp.repeat(sgu_b_s[j].T, SGU_GD, axis=1)
            bias_s = jnp.tile(bias_p[:L_S], (SB, 1))
            x, h2, idx, gates, v = _sgu_layer(
                x, mod_p, mod_s, n1g, n2g, sgu_w_in[j].astype(BF16), sgu_ln_g[j].reshape(1, SGU_W),
                sgu_ln_b[j].reshape(1, SGU_W), mixw, jnp.stack([bias_p, bias_s]),
                sgu_w_out[j].astype(BF16), wr, br)
            v_rows.append(v.reshape(NB_S, L_S, SGU_W))
        else:
            q, k, v, sg = _ret_proj(x, mod_p, mod_s, n1g, ret_w_in[j].astype(BF16), cos_tab, sin_tab)
            ng = ret_norm_g[j].reshape(1, VD)
            y, s_p = _ret_prompt(q, k, v, sg, tabs_p, ng)
            y, s_all = _ret_sample(j, q, k, v, sg, tabs_s, ng, state_ret, y, s_all)
            ret_p.append(s_p)
            x, h2, idx, gates = _ret_out(y, ret_w_out[j].astype(BF16), x, mod_p, mod_s, n2g, wr, br)
        x = _moe(i, x, h2, idx, gates, mod_p, mod_s, moe_w_up, b_up, moe_w_down, b_down,
                 fg, final=(i == DEPTH - 1))
    y_prompt = x[:T_P].reshape(NB_P, L_P, D)
    y_sample = x[T_P:].reshape(NB_S, L_S, D)
    return (y_prompt, y_sample, jnp.stack(ret_p), s_all, jnp.stack(v_rows))
```

```python
import functools

import jax
import jax.numpy as jnp
from jax import lax
from jax.experimental import pallas as pl
from jax.experimental.pallas import tpu as pltpu

F32 = jnp.float32
BF16 = jnp.bfloat16
I32 = jnp.int32

D = 1024
NB_P, L_P = 8, 2048
NB_S, L_S = 128, 8
PAST = 16384
DEPTH = 4
T_P = NB_P * L_P
T_S = NB_S * L_S
T = T_P + T_S
SGU_W = 2 * D
SGU_G = 8
SGU_GD = SGU_W // SGU_G
CHUNK = 128
H = 8
DK = D // H
DV = 2 * DK
QD = H * DK
VD = H * DV
RET_IN = 2 * QD + 2 * VD
ROPE_BASE = 10000.0
NE = 32
TOPK = 4
FE = D
ALPHA = 1.702
LIMIT = 7.0
EPS = 1e-6

LANES = 128
ROW_TILES = D // LANES
TM = 256
NPT = T_P // TM
NST = T_S // TM
NT = NPT + NST
TPB = L_P // TM
BLK = 256
TK = T * TOPK
N_BLOCKS = -(-(TK + NE * (BLK - 1)) // BLK)
N_ROWS = N_BLOCKS * BLK
SB = 16
VMEM_LIMIT = 56 * 1024 * 1024


def _cparams(n_axes):
    return pltpu.CompilerParams(dimension_semantics=("arbitrary",) * n_axes,
                                vmem_limit_bytes=VMEM_LIMIT)


def _rms(x, g):
    return (x * lax.rsqrt(jnp.mean(x * x, axis=-1, keepdims=True) + EPS)) * g


def _pick(is_p, p_ref, s_ref):
    return jnp.where(is_p, p_ref[0], s_ref[...])


def _mod_specs(j):
    return [
        pl.BlockSpec((1, 1, D), lambda t, j=j: (jnp.minimum(t // TPB, NB_P - 1), 0, j)),
        pl.BlockSpec((TM, D), lambda t, j=j: (jnp.maximum(t - NPT, 0), j)),
    ]


def _tile_spec(width):
    return pl.BlockSpec((TM, width), lambda t: (t, 0))


def _full_spec(shape):
    return pl.BlockSpec(shape, lambda *_: (0,) * len(shape))


def _mod_kernel(c_ref, w_ref, b_ref, o_ref):
    c = c_ref[...]
    cs = (c * jax.nn.sigmoid(c)).astype(BF16)
    o_ref[0] = jnp.dot(cs, w_ref[0].astype(BF16), preferred_element_type=F32) + b_ref[0]


def _modulation(c_all, w_mod, b_mod):
    tn = 1536
    nb = c_all.shape[0]
    return pl.pallas_call(
        _mod_kernel,
        grid=(DEPTH, 6 * D // tn),
        in_specs=[
            pl.BlockSpec((nb, D), lambda l, n: (0, 0)),
            pl.BlockSpec((1, D, tn), lambda l, n: (l, 0, n)),
            pl.BlockSpec((1, 1, tn), lambda l, n: (l, 0, n)),
        ],
        out_specs=pl.BlockSpec((1, nb, tn), lambda l, n: (l, 0, n)),
        out_shape=jax.ShapeDtypeStruct((DEPTH, nb, 6 * D), F32),
        compiler_params=_cparams(2),
        name="modulation",
    )(c_all, w_mod, b_mod.reshape(DEPTH, 1, 6 * D))


def _tail(is_p, x, y, g1p, g1s, sh2p, sh2s, sc2p, sc2s, n2g, wr, br, xo_ref, h2_ref, idx_ref, gate_ref):
    xn = x + _pick(is_p, g1p, g1s) * y
    xo_ref[...] = xn
    h2 = (_rms(xn, n2g[...]) * (1.0 + _pick(is_p, sc2p, sc2s)) + _pick(is_p, sh2p, sh2s)).astype(BF16)
    h2_ref[...] = h2
    logit = jnp.dot(h2, wr[...], preferred_element_type=F32) + br[...]
    lane = lax.broadcasted_iota(I32, logit.shape, 1)
    vals, ids = [], []
    for _ in range(TOPK):
        m = jnp.max(logit, axis=-1, keepdims=True)
        sel = jnp.min(jnp.where(logit == m, lane, LANES), axis=-1, keepdims=True)
        vals.append(m)
        ids.append(sel)
        logit = jnp.where(lane == sel, -jnp.inf, logit)
    es = [jnp.exp(v - vals[0]) for v in vals]
    tot = (es[0] + es[1]) + (es[2] + es[3])
    idx_out = jnp.zeros(logit.shape, I32)
    gate_out = jnp.zeros(logit.shape, F32)
    for k in range(TOPK):
        idx_out = jnp.where(lane == k, ids[k], idx_out)
        gate_out = jnp.where(lane == k, es[k] / tot, gate_out)
    idx_ref[...] = idx_out
    gate_ref[...] = gate_out


def _tail_in_specs():
    return (_mod_specs(2) + _mod_specs(3) + _mod_specs(4)
            + [_full_spec((1, D)), _full_spec((D, LANES)), _full_spec((1, LANES))])


def _tail_out_specs():
    return [_tile_spec(D), _tile_spec(D), _tile_spec(LANES), _tile_spec(LANES)]


def _tail_out_shapes():
    return [jax.ShapeDtypeStruct((T, D), F32), jax.ShapeDtypeStruct((T, D), BF16),
            jax.ShapeDtypeStruct((T, LANES), I32), jax.ShapeDtypeStruct((T, LANES), F32)]


def _sgu_kernel(x_ref, sh1p, sh1s, sc1p, sc1s, n1g, win, lng, lnb, mixw, mixb, wout,
                g1p, g1s, sh2p, sh2s, sc2p, sc2s, n2g, wr, br,
                xo_ref, h2_ref, idx_ref, gate_ref, v_ref, y_scr):
    t = pl.program_id(0)
    is_p = t < NPT
    x = x_ref[...]
    h = _rms(x, n1g[...]) * (1.0 + _pick(is_p, sc1p, sc1s)) + _pick(is_p, sh1p, sh1s)
    z = jnp.dot(h.astype(BF16), win[...], preferred_element_type=F32)
    z = 0.5 * z * (1.0 + lax.erf(z * (0.5 ** 0.5)))
    u = z[:, :SGU_W]
    v = z[:, SGU_W:]
    vc = v - jnp.mean(v, axis=-1, keepdims=True)
    vn = vc * lax.rsqrt(jnp.mean(vc * vc, axis=-1, keepdims=True) + EPS) * lng[...] + lnb[...]

    @pl.when(t >= NPT)
    def _():
        v_ref[...] = vn

    vb = vn.astype(BF16)
    ri = lax.broadcasted_iota(I32, (CHUNK, CHUNK), 0)
    ci = lax.broadcasted_iota(I32, (CHUNK, CHUNK), 1)
    causal = ci <= ri
    shift = jnp.broadcast_to(jnp.where(is_p, 7, 3), ri.shape)
    keep = causal & (lax.shift_right_logical(ri, shift) == lax.shift_right_logical(ci, shift))
    for g in range(SGU_G):
        wg = jnp.where(keep, mixw[0, g], 0.0).astype(BF16)
        for c in range(TM // CHUNK):
            rows = slice(c * CHUNK, (c + 1) * CHUNK)
            cols = slice(g * SGU_GD, (g + 1) * SGU_GD)
            mixed = jnp.dot(wg, vb[rows, cols], preferred_element_type=F32) + mixb[0, :, cols]
            y_scr[rows, cols] = (u[rows, cols] * mixed).astype(BF16)
    y = jnp.dot(y_scr[...], wout[...], preferred_element_type=F32)
    _tail(is_p, x, y, g1p, g1s, sh2p, sh2s, sc2p, sc2s, n2g, wr, br, xo_ref, h2_ref, idx_ref, gate_ref)


def _sgu_layer(x, mod_p, mod_s, n1g, n2g, win, lng, lnb, mixw, mixb, wout, wr, br):
    sel = lambda t: (jnp.where(t < NPT, 0, 1), 0, 0, 0)
    in_specs = ([_tile_spec(D)] + _mod_specs(0) + _mod_specs(1)
                + [_full_spec((1, D)), _full_spec((D, 2 * SGU_W)), _full_spec((1, SGU_W)),
                   _full_spec((1, SGU_W)),
                   pl.BlockSpec((1, SGU_G, CHUNK, CHUNK), sel),
                   pl.BlockSpec((1, CHUNK, SGU_W), lambda t: (jnp.where(t < NPT, 0, 1), 0, 0)),
                   _full_spec((SGU_W, D))]
                + _tail_in_specs())
    out_specs = _tail_out_specs() + [pl.BlockSpec((TM, SGU_W), lambda t: (jnp.maximum(t - NPT, 0), 0))]
    out_shape = _tail_out_shapes() + [jax.ShapeDtypeStruct((T_S, SGU_W), F32)]
    return pl.pallas_call(
        _sgu_kernel, grid=(NT,), in_specs=in_specs, out_specs=out_specs, out_shape=out_shape,
        scratch_shapes=[pltpu.VMEM((TM, SGU_W), BF16)],
        compiler_params=_cparams(1), name="sgu_layer",
    )(x, mod_p, mod_s, mod_p, mod_s, n1g, win, lng, lnb, mixw, mixb, wout,
      mod_p, mod_s, mod_p, mod_s, mod_p, mod_s, n2g, wr, br)


def _ret_proj_kernel(x_ref, sh1p, sh1s, sc1p, sc1s, n1g, win, cos_ref, sin_ref,
                     q_ref, k_ref, v_ref, sg_ref):
    t = pl.program_id(0)
    is_p = t < NPT
    x = x_ref[...]
    h = _rms(x, n1g[...]) * (1.0 + _pick(is_p, sc1p, sc1s)) + _pick(is_p, sh1p, sh1s)
    p = jnp.dot(h.astype(BF16), win[...], preferred_element_type=F32)
    cos = cos_ref[...]
    sin = sin_ref[...]
    for hd in range(H):
        cq = slice(hd * DK, (hd + 1) * DK)
        ck = slice(QD + hd * DK, QD + (hd + 1) * DK)
        qh = p[:, cq]
        kh = p[:, ck]
        q_ref[:, cq] = (qh * cos + pltpu.roll(qh, DK // 2, 1) * sin).astype(BF16)
        k_ref[:, cq] = ((kh * cos + pltpu.roll(kh, DK // 2, 1) * sin) * (DK ** -0.5)).astype(BF16)
    v_ref[...] = p[:, 2 * QD:2 * QD + VD].astype(BF16)
    g = p[:, 2 * QD + VD:]
    sg_ref[...] = (g * jax.nn.sigmoid(g)).astype(BF16)


def _ret_proj(x, mod_p, mod_s, n1g, win, cos_tab, sin_tab):
    in_specs = ([_tile_spec(D)] + _mod_specs(0) + _mod_specs(1)
                + [_full_spec((1, D)), _full_spec((D, RET_IN)), _tile_spec(DK), _tile_spec(DK)])
    out_specs = [_tile_spec(QD), _tile_spec(QD), _tile_spec(VD), _tile_spec(VD)]
    out_shape = [jax.ShapeDtypeStruct((T, QD), BF16), jax.ShapeDtypeStruct((T, QD), BF16),
                 jax.ShapeDtypeStruct((T, VD), BF16), jax.ShapeDtypeStruct((T, VD), BF16)]
    return pl.pallas_call(
        _ret_proj_kernel, grid=(NT,), in_specs=in_specs, out_specs=out_specs, out_shape=out_shape,
        compiler_params=_cparams(1), name="ret_proj",
    )(x, mod_p, mod_s, mod_p, mod_s, n1g, win, cos_tab, sin_tab)


def _ret_intra(q, k, v, decay, qdec, kdec):
    s = lax.dot_general(q, k, (((1,), (1,)), ((), ())), preferred_element_type=F32) * decay
    o = jnp.dot(s.astype(BF16), v, preferred_element_type=F32)
    return o, q.astype(F32) * qdec, k.astype(F32) * kdec


def _ret_finish(o, sg, ng):
    on = o * lax.rsqrt(jnp.mean(o * o, axis=-1, keepdims=True) + EPS)
    return (sg.astype(F32) * (on * ng)).astype(BF16)


def _ret_prompt_kernel(gam_ref, q_ref, k_ref, v_ref, sg_ref, dec_ref, qdec_ref, kdec_ref, ng_ref,
                       y_ref, s_ref):
    c = pl.program_id(1)

    @pl.when(c == 0)
    def _():
        s_ref[...] = jnp.zeros(s_ref.shape, F32)

    for hd in range(H):
        ck = slice(hd * DK, (hd + 1) * DK)
        cv = slice(hd * DV, (hd + 1) * DV)
        v = v_ref[:, cv]
        o, qd, kd = _ret_intra(q_ref[:, ck], k_ref[:, ck], v, dec_ref[hd], qdec_ref[:, ck], kdec_ref[:, ck])
        s_old = s_ref[0, hd]
        o = o + jnp.dot(qd.astype(BF16), s_old.astype(BF16), preferred_element_type=F32)
        s_ref[0, hd] = gam_ref[hd] * s_old + lax.dot_general(
            kd.astype(BF16), v, (((0,), (0,)), ((), ())), preferred_element_type=F32)
        y_ref[:, cv] = _ret_finish(o, sg_ref[:, cv], ng_ref[:, cv])


def _ret_prompt(q, k, v, sg, tabs, ng):
    nc = L_P // CHUNK
    row = lambda b, c: (b * nc + c, 0)
    smem = pl.BlockSpec(memory_space=pltpu.SMEM)
    in_specs = [smem,
                pl.BlockSpec((CHUNK, QD), row), pl.BlockSpec((CHUNK, QD), row),
                pl.BlockSpec((CHUNK, VD), row), pl.BlockSpec((CHUNK, VD), row),
                _full_spec((H, CHUNK, CHUNK)), _full_spec((CHUNK, QD)), _full_spec((CHUNK, QD)),
                _full_spec((1, VD))]
    out_specs = [pl.BlockSpec((CHUNK, VD), row),
                 pl.BlockSpec((1, H, DK, DV), lambda b, c: (b, 0, 0, 0))]
    out_shape = [jax.ShapeDtypeStruct((T, VD), BF16), jax.ShapeDtypeStruct((NB_P, H, DK, DV), F32)]
    return pl.pallas_call(
        _ret_prompt_kernel, grid=(NB_P, nc), in_specs=in_specs, out_specs=out_specs, out_shape=out_shape,
        compiler_params=_cparams(2), name="ret_prompt",
    )(tabs["gam"], q, k, v, sg, tabs["decay"], tabs["qdec"], tabs["kdec"], ng)


def _ret_sample_kernel(gam_ref, q_ref, k_ref, v_ref, sg_ref, dec_ref, qdec_ref, kdec_ref, ng_ref,
                       s_in_ref, y_in_ref, s_all_ref, y_ref, s_out_ref):
    del y_in_ref, s_all_ref
    s_in_ref = s_in_ref.at[0]
    s_out_ref = s_out_ref.at[0]
    hd = pl.program_id(1)
    v = v_ref[...]
    o, qd, kd = _ret_intra(q_ref[...], k_ref[...], v, dec_ref[0], qdec_ref[...], kdec_ref[...])
    gam = gam_ref[hd]
    row_batch = lax.broadcasted_iota(I32, (CHUNK, DK), 0) // L_S
    cross = []
    for b in range(SB):
        s_old = s_in_ref[b, 0]
        rows = slice(b * L_S, (b + 1) * L_S)
        cross.append(jnp.dot(qd[rows].astype(BF16), s_old.astype(BF16), preferred_element_type=F32))
        kb = jnp.where(row_batch == b, kd, 0.0).astype(BF16)
        s_out_ref[b, 0] = gam * s_old + lax.dot_general(
            kb, v, (((0,), (0,)), ((), ())), preferred_element_type=F32)
    o = o + jnp.concatenate(cross, axis=0)
    y_ref[...] = _ret_finish(o, sg_ref[...], ng_ref[...])


def _ret_sample(j, q, k, v, sg, tabs, ng, s_in, y_prev, s_all):
    base = T_P // CHUNK
    rq = lambda g, h: (base + g, h)
    st = pl.BlockSpec((1, SB, 1, DK, DV), lambda g, h: (j, g, h, 0, 0))
    smem = pl.BlockSpec(memory_space=pltpu.SMEM)
    in_specs = [smem,
                pl.BlockSpec((CHUNK, DK), rq), pl.BlockSpec((CHUNK, DK), rq),
                pl.BlockSpec((CHUNK, DV), rq), pl.BlockSpec((CHUNK, DV), rq),
                pl.BlockSpec((1, CHUNK, CHUNK), lambda g, h: (h, 0, 0)),
                pl.BlockSpec((CHUNK, DK), lambda g, h: (0, h)),
                pl.BlockSpec((CHUNK, DK), lambda g, h: (0, h)),
                pl.BlockSpec((1, DV), lambda g, h: (0, h)),
                st, pl.BlockSpec(memory_space=pl.ANY), pl.BlockSpec(memory_space=pl.ANY)]
    out_specs = [pl.BlockSpec((CHUNK, DV), rq), st]
    out_shape = [jax.ShapeDtypeStruct((T, VD), BF16), jax.ShapeDtypeStruct(s_all.shape, F32)]
    return pl.pallas_call(
        _ret_sample_kernel, grid=(NB_S // SB, H), in_specs=in_specs, out_specs=out_specs,
        out_shape=out_shape, input_output_aliases={10: 0, 11: 1},
        compiler_params=_cparams(2), name="ret_sample",
    )(tabs["gam"], q, k, v, sg, tabs["decay"], tabs["qdec"], tabs["kdec"], ng, s_in, y_prev, s_all)


def _ret_out_kernel(y_ref, wout, x_ref, g1p, g1s, sh2p, sh2s, sc2p, sc2s, n2g, wr, br,
                    xo_ref, h2_ref, idx_ref, gate_ref):
    is_p = pl.program_id(0) < NPT
    y = jnp.dot(y_ref[...], wout[...], preferred_element_type=F32)
    _tail(is_p, x_ref[...], y, g1p, g1s, sh2p, sh2s, sc2p, sc2s, n2g, wr, br,
          xo_ref, h2_ref, idx_ref, gate_ref)


def _ret_out(y, wout, x, mod_p, mod_s, n2g, wr, br):
    in_specs = [_tile_spec(VD), _full_spec((VD, D)), _tile_spec(D)] + _tail_in_specs()
    return pl.pallas_call(
        _ret_out_kernel, grid=(NT,), in_specs=in_specs, out_specs=_tail_out_specs(),
        out_shape=_tail_out_shapes(), compiler_params=_cparams(1), name="ret_out",
    )(y, wout, x, mod_p, mod_s, mod_p, mod_s, mod_p, mod_s, n2g, wr, br)


def _lane_prefix(v, lane1):
    s = 1
    while s < LANES:
        v = v + jnp.where(lane1 >= s, pltpu.roll(v, s, 1), 0.0)
        s *= 2
    return v


def _route_kernel(idx_ref, lp_ref, lpt_ref, runs_ref, meta_ref, cnt_scr, base_scr):
    ph = pl.program_id(0)
    t = pl.program_id(1)
    idx = idx_ref[...]
    lane = lax.broadcasted_iota(I32, (TM, LANES), 1)
    lane1 = lax.broadcasted_iota(I32, (1, LANES), 1)
    hits = [lane == idx[:, k:k + 1] for k in range(TOPK)]
    chosen = jnp.zeros((TM, LANES), F32)
    for hk in hits:
        chosen = chosen + jnp.where(hk, 1.0, 0.0)
    colsum = jnp.sum(chosen, axis=0, keepdims=True)

    @pl.when((ph == 0) & (t == 0))
    def _():
        cnt_scr[...] = jnp.zeros(cnt_scr.shape, F32)

    @pl.when(ph == 0)
    def _():
        cnt_scr[...] += colsum

    @pl.when((ph == 1) & (t == 0))
    def _():
        cnt = cnt_scr[...]
        padded = (((cnt.astype(I32) + (BLK - 1)) // BLK) * BLK).astype(F32)
        end = _lane_prefix(padded, lane1)
        base_scr[...] = end - padded
        row = lax.broadcasted_iota(I32, (8, LANES), 0)
        meta = jnp.where(row == 0, cnt, jnp.where(row == 1, end - padded, jnp.where(row == 2, end, 0.0)))
        meta_ref[...] = meta.astype(I32)

    @pl.when(ph == 1)
    def _():
        ri = lax.broadcasted_iota(I32, (TM, TM), 0)
        ci = lax.broadcasted_iota(I32, (TM, TM), 1)
        before = jnp.where(ci < ri, 1.0, 0.0).astype(BF16)
        loff = _lane_prefix(colsum, lane1) - colsum
        pos = jnp.dot(before, chosen.astype(BF16), preferred_element_type=F32) + loff
        lp = jnp.zeros((TM, LANES), F32)
        for k, hk in enumerate(hits):
            lp = jnp.where(lane == k, jnp.sum(jnp.where(hk, pos, 0.0), axis=-1, keepdims=True), lp)
        lp_ref[...] = lp.astype(I32)
        lpt_ref[...] = lp.T[:8].astype(I32)
        row = lax.broadcasted_iota(I32, (8, LANES), 0)
        runs = jnp.where(row == 0, colsum, jnp.where(row == 1, loff, jnp.where(row == 2, base_scr[...], 0.0)))
        runs_ref[0] = runs.astype(I32)
        base_scr[...] += colsum


def _route(idx):
    n_tiles = idx.shape[0] // TM
    return pl.pallas_call(
        _route_kernel, grid=(2, n_tiles),
        in_specs=[pl.BlockSpec((TM, LANES), lambda p, t: (t, 0))],
        out_specs=[pl.BlockSpec((TM, LANES), lambda p, t: (p * t, 0)),
                   pl.BlockSpec((8, TM), lambda p, t: (p * t, 0)),
                   pl.BlockSpec((1, 8, LANES), lambda p, t: (p * t, 0, 0)),
                   pl.BlockSpec((8, LANES), lambda p, t: (0, 0))],
        out_shape=[jax.ShapeDtypeStruct((n_tiles * TM, LANES), I32),
                   jax.ShapeDtypeStruct((n_tiles * 8, TM), I32),
                   jax.ShapeDtypeStruct((n_tiles, 8, LANES), I32),
                   jax.ShapeDtypeStruct((8, LANES), I32)],
        scratch_shapes=[pltpu.VMEM((1, LANES), F32), pltpu.VMEM((1, LANES), F32)],
        compiler_params=_cparams(2), name="moe_route",
    )(idx)


def _tile_rows(ref, r, n):
    return ref.at[:, pl.ds(r, n), :]


def _run_copy(src, dst, r_src, r_dst, n, sem, wait):
    for b in reversed(range(TM.bit_length())):
        size = 1 << b
        off = (n >> (b + 1)) << (b + 1)

        def piece(off=off, size=size):
            cp = pltpu.make_async_copy(_tile_rows(src, r_src + off, size),
                                       _tile_rows(dst, r_dst + off, size), sem)
            if wait:
                cp.wait()
            else:
                cp.start()

        if isinstance(n, int):
            if n & size:
                piece()
        else:
            pl.when((n & size) != 0)(piece)


def _load_rows(ref3):
    return jnp.concatenate([ref3[s] for s in range(ROW_TILES)], axis=1)


def _store_rows(ref3, val):
    for s in range(ROW_TILES):
        ref3[s] = val[:, s * LANES:(s + 1) * LANES]


def _dispatch_kernel(len_ref, pos_ref, row_ref, cnt_ref, first_ref, end_ref, lpt_ref, h2_ref, dst,
                     buf, zbuf, sem):
    n_rows = dst.shape[1]
    step = pl.program_id(0)
    lpt = lpt_ref[...]
    p = lax.broadcasted_iota(I32, (TM * TOPK, TM), 0)
    pick = lpt[0:1, :] == p
    for k in range(1, TOPK):
        pick = pick | (lpt[k:k + 1, :] == p)
    perm = jnp.where(pick, 1.0, 0.0).astype(BF16)
    _store_rows(buf, jnp.dot(perm, h2_ref[...], preferred_element_type=F32))

    def per_expert(e, carry):
        _run_copy(buf, dst, pos_ref[e], row_ref[e], len_ref[e], sem, False)
        return carry

    lax.fori_loop(0, NE, per_expert, 0)
    pltpu.make_async_copy(buf, _tile_rows(dst, 0, TM * TOPK), sem).wait()

    @pl.when(step == 0)
    def _():
        zbuf[...] = jnp.zeros(zbuf.shape, F32)
        n_tail = (n_rows - end_ref[NE - 1]) // TM
        for wait in (False, True):
            def pad(e, carry, wait=wait):
                lo = first_ref[e] + cnt_ref[e]
                _run_copy(zbuf, dst, 0, lo, end_ref[e] - lo, sem, wait)
                return carry

            lax.fori_loop(0, NE, pad, 0)

            def tail(j, carry, wait=wait):
                _run_copy(zbuf, dst, 0, end_ref[NE - 1] + j * TM, TM, sem, wait)
                return carry

            lax.fori_loop(0, n_tail, tail, 0)


def _dispatch(h2, lpt, run_len, run_pos, run_row, cnt, first, end, n_rows):
    assert BLK == TM
    n_tiles = h2.shape[0] // TM
    smem = pl.BlockSpec(memory_space=pltpu.SMEM)
    per_tile = pl.BlockSpec((LANES,), lambda t: (t,), memory_space=pltpu.SMEM)
    return pl.pallas_call(
        _dispatch_kernel, grid=(n_tiles,),
        in_specs=[per_tile, per_tile, per_tile, smem, smem, smem,
                  pl.BlockSpec((8, TM), lambda t: (t, 0)), _tile_spec(D)],
        out_specs=pl.BlockSpec(memory_space=pl.ANY),
        out_shape=jax.ShapeDtypeStruct((ROW_TILES, n_rows, LANES), F32),
        scratch_shapes=[pltpu.VMEM((ROW_TILES, TM * TOPK, LANES), F32),
                        pltpu.VMEM((ROW_TILES, TM, LANES), F32), pltpu.SemaphoreType.DMA],
        compiler_params=_cparams(1), name="moe_dispatch",
    )(run_len, run_pos, run_row, cnt, first, end, lpt, h2)


def _expert_kernel(be_ref, nu_ref, xb_ref, wu_ref, bu_ref, wd_ref, bd_ref, yb_ref, wub, wdb):
    i = pl.program_id(0)
    used = i < nu_ref[0]

    @pl.when(used)
    def _():
        fresh = (i == 0) | (be_ref[i] != be_ref[jnp.maximum(i - 1, 0)])

        @pl.when(fresh)
        def _():
            wub[...] = wu_ref[0, 0].astype(BF16)
            wdb[...] = wd_ref[0, 0].astype(BF16)

        xb = _load_rows(xb_ref).astype(BF16)
        z = jnp.dot(xb, wub[...], preferred_element_type=F32) + bu_ref[0, 0]
        glu = jnp.minimum(z[:, :FE], LIMIT)
        lin = jnp.clip(z[:, FE:], -LIMIT, LIMIT)
        act = glu * jax.nn.sigmoid(ALPHA * glu) * (lin + 1.0)
        _store_rows(yb_ref, jnp.dot(act.astype(BF16), wdb[...], preferred_element_type=F32) + bd_ref[0, 0])

    @pl.when(jnp.logical_not(used))
    def _():
        yb_ref[...] = jnp.zeros(yb_ref.shape, F32)


def _experts(layer, xb, blk_e, n_used, wu, bu, wd, bd):
    grid_spec = pltpu.PrefetchScalarGridSpec(
        num_scalar_prefetch=2, grid=(N_BLOCKS,),
        in_specs=[pl.BlockSpec((ROW_TILES, BLK, LANES),
                               lambda i, be, nu: (0, jnp.minimum(i, nu[0] - 1), 0)),
                  pl.BlockSpec((1, 1, D, 2 * FE), lambda i, be, nu: (layer, be[i], 0, 0)),
                  pl.BlockSpec((1, 1, 1, 2 * FE), lambda i, be, nu: (layer, be[i], 0, 0)),
                  pl.BlockSpec((1, 1, FE, D), lambda i, be, nu: (layer, be[i], 0, 0)),
                  pl.BlockSpec((1, 1, 1, D), lambda i, be, nu: (layer, be[i], 0, 0))],
        out_specs=pl.BlockSpec((ROW_TILES, BLK, LANES), lambda i, be, nu: (0, i, 0)),
        scratch_shapes=[pltpu.VMEM((D, 2 * FE), BF16), pltpu.VMEM((FE, D), BF16)])
    return pl.pallas_call(
        _expert_kernel, grid_spec=grid_spec,
        out_shape=jax.ShapeDtypeStruct((ROW_TILES, N_ROWS, LANES), F32),
        compiler_params=_cparams(1), name="experts",
    )(blk_e, n_used, xb, wu, bu, wd, bd)


def _combine_kernel(len_ref, pos_ref, row_ref, yb, lp_ref, x_ref, gate_ref, g2p, g2s, fg_ref, o_ref,
                    ybuf, sem, *, final):
    is_p = pl.program_id(0) < NPT
    def per_expert(e, carry):
        _run_copy(yb, ybuf, row_ref[e], pos_ref[e], len_ref[e], sem, False)
        return carry

    lax.fori_loop(0, NE, per_expert, 0)
    pltpu.make_async_copy(_tile_rows(yb, 0, TM * TOPK), ybuf, sem).wait()

    rows = x_ref.shape[0]
    lp = lp_ref[...]
    gates = gate_ref[...]
    p = lax.broadcasted_iota(I32, (rows, rows * TOPK), 1)
    weights = jnp.zeros((rows, rows * TOPK), F32)
    for k in range(TOPK):
        weights = jnp.where(lp[:, k:k + 1] == p, gates[:, k:k + 1], weights)
    acc = jnp.dot(weights.astype(BF16), _load_rows(ybuf).astype(BF16), preferred_element_type=F32)
    xn = x_ref[...] + _pick(is_p, g2p, g2s) * acc
    o_ref[...] = _rms(xn, fg_ref[...]) if final else xn


def _combine(x, yb, lp, run_len, run_pos, run_row, gates, mod_p, mod_s, final_g, final):
    n_tiles = x.shape[0] // TM
    per_tile = pl.BlockSpec((LANES,), lambda t: (t,), memory_space=pltpu.SMEM)
    in_specs = ([per_tile, per_tile, per_tile, pl.BlockSpec(memory_space=pl.ANY),
                 _tile_spec(LANES), _tile_spec(D), _tile_spec(LANES)]
                + _mod_specs(5) + [_full_spec((1, D))])
    return pl.pallas_call(
        functools.partial(_combine_kernel, final=final), grid=(n_tiles,), in_specs=in_specs,
        out_specs=_tile_spec(D), out_shape=jax.ShapeDtypeStruct(x.shape, F32),
        scratch_shapes=[pltpu.VMEM((ROW_TILES, TM * TOPK, LANES), F32), pltpu.SemaphoreType.DMA],
        compiler_params=_cparams(1), name="moe_combine",
    )(run_len, run_pos, run_row, yb, lp, x, gates, mod_p, mod_s, final_g)


def _moe(layer, x, h2, idx, gates, mod_p, mod_s, wu, bu, wd, bd, final_g, final):
    lp, lpt, runs, meta = _route(idx)
    cnt, first, end = meta[0, :NE], meta[1, :NE], meta[2, :NE]
    run_len, run_pos, run_row = (runs[:, r, :].reshape(-1) for r in range(3))
    blk_first = jnp.arange(N_BLOCKS, dtype=I32) * BLK
    blk_e = jnp.minimum(jnp.sum((end[None, :] <= blk_first[:, None]).astype(I32), axis=1), NE - 1)
    n_used = end[NE - 1:] // BLK
    xb = _dispatch(h2, lpt, run_len, run_pos, run_row, cnt, first, end, N_ROWS)
    yb = _experts(layer, xb, blk_e, n_used, wu, bu, wd, bd)
    return _combine(x, yb, lp, run_len, run_pos, run_row, gates, mod_p, mod_s, final_g, final)


def _rope_tables():
    half = DK // 2
    inv = 1.0 / (ROPE_BASE ** jnp.linspace(0.0, 1.0, half, dtype=F32))

    def tab(pos):
        ang = pos.astype(F32)[:, None] * inv[None, :]
        cos, sin = jnp.cos(ang), jnp.sin(ang)
        return jnp.concatenate([cos, cos], -1), jnp.concatenate([-sin, sin], -1)

    cp, sp = tab(jnp.arange(L_P, dtype=I32))
    cs, ss = tab(PAST + jnp.arange(L_S, dtype=I32))
    cos = jnp.concatenate([jnp.tile(cp, (NB_P, 1)), jnp.tile(cs, (NB_S, 1))], 0)
    sin = jnp.concatenate([jnp.tile(sp, (NB_P, 1)), jnp.tile(ss, (NB_S, 1))], 0)
    return cos, sin


def _decay_tables(cl):
    lg = jnp.log(1.0 - 2.0 ** (-5.0 - jnp.arange(H, dtype=F32)))
    r = jnp.arange(CHUNK)
    idx = (r % cl).astype(F32)
    diff = idx[:, None] - idx[None, :]
    same = (r[:, None] // cl) == (r[None, :] // cl)
    decay = jnp.where((same & (diff >= 0))[None],
                      jnp.exp(lg[:, None, None] * jnp.maximum(diff, 0.0)[None]), 0.0)
    qdec = jnp.exp(lg[None, :] * (idx[:, None] + 1.0))
    kdec = jnp.exp(lg[None, :] * (cl - 1.0 - idx[:, None]))
    wide = lambda a: jnp.repeat(a, DK, axis=1)
    return {"decay": decay, "qdec": wide(qdec), "kdec": wide(kdec), "gam": jnp.exp(lg * cl)}


def kernel(x_prompt, x_sample, c_prompt, c_sample, state_ret, w_mod, b_mod, norm1_g, norm2_g,
           sgu_w_in, sgu_ln_g, sgu_ln_b, sgu_w_s, sgu_b_s, sgu_w_out, ret_w_in, ret_norm_g, ret_w_out,
           moe_w_router, moe_b_router, moe_w_up, moe_b_up, moe_w_down, moe_b_down, final_g):
    x = jnp.concatenate([x_prompt.reshape(T_P, D), x_sample.reshape(T_S, D)], 0)
    mod = _modulation(jnp.concatenate([c_prompt, c_sample], 0), w_mod, b_mod)
    cos_tab, sin_tab = _rope_tables()
    tabs_p = _decay_tables(CHUNK)
    tabs_s = _decay_tables(L_S)
    wr_pad = jnp.pad(moe_w_router, ((0, 0), (0, 0), (0, LANES - NE))).astype(BF16)
    br_pad = jnp.pad(moe_b_router, ((0, 0), (0, LANES - NE)), constant_values=-1e30)
    fg = final_g.reshape(1, D)
    b_up = moe_b_up.reshape(DEPTH, NE, 1, 2 * FE)
    b_down = moe_b_down.reshape(DEPTH, NE, 1, D)

    ret_p, v_rows = [], []
    s_all = lax.empty(state_ret.shape, F32)
    for i in range(DEPTH):
        j = i // 2
        mod_p = mod[i, :NB_P].reshape(NB_P, 1, 6 * D)
        mod_s = jnp.repeat(mod[i, NB_P:], L_S, axis=0)
        n1g = norm1_g[i].reshape(1, D)
        n2g = norm2_g[i].reshape(1, D)
        wr = wr_pad[i]
        br = br_pad[i].reshape(1, LANES)
        if i % 2 == 0:
            mixw = jnp.stack([sgu_w_s[j], jnp.tile(sgu_w_s[j][:, :L_S, :L_S], (1, SB, SB))])
            bias_p = jnp.repeat(sgu_b_s[j].T, SGU_GD, axis=1)
            bias_s = jnp.tile(bias_p[:L_S], (SB, 1))
            x, h2, idx, gates, v = _sgu_layer(
                x, mod_p, mod_s, n1g, n2g, sgu_w_in[j].astype(BF16), sgu_ln_g[j].reshape(1, SGU_W),
                sgu_ln_b[j].reshape(1, SGU_W), mixw, jnp.stack([bias_p, bias_s]),
                sgu_w_out[j].astype(BF16), wr, br)
            v_rows.append(v.reshape(NB_S, L_S, SGU_W))
        else:
            q, k, v, sg = _ret_proj(x, mod_p, mod_s, n1g, ret_w_in[j].astype(BF16), cos_tab, sin_tab)
            ng = ret_norm_g[j].reshape(1, VD)
            y, s_p = _ret_prompt(q, k, v, sg, tabs_p, ng)
            y, s_all = _ret_sample(j, q, k, v, sg, tabs_s, ng, state_ret, y, s_all)
            ret_p.append(s_p)
            x, h2, idx, gates = _ret_out(y, ret_w_out[j].astype(BF16), x, mod_p, mod_s, n2g, wr, br)
        x = _moe(i, x, h2, idx, gates, mod_p, mod_s, moe_w_up, b_up, moe_w_down, b_down,
                 fg, final=(i == DEPTH - 1))
    y_prompt = x[:T_P].reshape(NB_P, L_P, D)
    y_sample = x[T_P:].reshape(NB_S, L_S, D)
    return (y_prompt, y_sample, jnp.stack(ret_p), s_all, jnp.stack(v_rows))
```

```python
import functools

import jax
import jax.numpy as jnp
from jax import lax
from jax.experimental import pallas as pl
from jax.experimental.pallas import tpu as pltpu

F32 = jnp.float32
BF16 = jnp.bfloat16
I32 = jnp.int32

D = 1024
NB_P, L_P = 8, 2048
NB_S, L_S = 128, 8
PAST = 16384
DEPTH = 4
T_P = NB_P * L_P
T_S = NB_S * L_S
T = T_P + T_S
SGU_W = 2 * D
SGU_G = 8
SGU_GD = SGU_W // SGU_G
CHUNK = 128
H = 8
DK = D // H
DV = 2 * DK
QD = H * DK
VD = H * DV
RET_IN = 2 * QD + 2 * VD
ROPE_BASE = 10000.0
NE = 32
TOPK = 4
FE = D
ALPHA = 1.702
LIMIT = 7.0
EPS = 1e-6

LANES = 128
ROW_TILES = D // LANES
TD = 512
NPT = T_P // TD
NT = T // TD
TM = 256
BLK = 512
TK = T * TOPK
N_BLOCKS = -(-(TK + NE * (BLK - 1)) // BLK)
N_ROWS = N_BLOCKS * BLK
SB = 16
VMEM_LIMIT = 56 * 1024 * 1024


def _cparams(n_axes):
    return pltpu.CompilerParams(dimension_semantics=("arbitrary",) * n_axes,
                                vmem_limit_bytes=VMEM_LIMIT)


def _rms(x, g):
    return (x * lax.rsqrt(jnp.mean(x * x, axis=-1, keepdims=True) + EPS)) * g


def _pick(is_p, p_ref, s_ref):
    return jnp.where(is_p, p_ref[0], s_ref[...])


def _mod_specs(j, tm=TD):
    per_batch, n_prompt = L_P // tm, T_P // tm
    return [
        pl.BlockSpec((1, 1, D), lambda t: (jnp.minimum(t // per_batch, NB_P - 1), 0, j)),
        pl.BlockSpec((tm, D), lambda t: (jnp.maximum(t - n_prompt, 0), j), pipeline_mode=pl.Buffered(1)),
    ]


def _tile_spec(width, tm=TD):
    return pl.BlockSpec((tm, width), lambda t: (t, 0))


def _full_spec(shape):
    return pl.BlockSpec(shape, lambda *_: (0,) * len(shape), pipeline_mode=pl.Buffered(1))


def _mod_kernel(c_ref, w_ref, b_ref, o_ref):
    c = c_ref[...]
    cs = (c * jax.nn.sigmoid(c)).astype(BF16)
    o_ref[0] = jnp.dot(cs, w_ref[0].astype(BF16), preferred_element_type=F32) + b_ref[0]


def _modulation(c_all, w_mod, b_mod):
    tn = 1536
    nb = c_all.shape[0]
    return pl.pallas_call(
        _mod_kernel,
        grid=(DEPTH, 6 * D // tn),
        in_specs=[
            pl.BlockSpec((nb, D), lambda l, n: (0, 0)),
            pl.BlockSpec((1, D, tn), lambda l, n: (l, 0, n)),
            pl.BlockSpec((1, 1, tn), lambda l, n: (l, 0, n)),
        ],
        out_specs=pl.BlockSpec((1, nb, tn), lambda l, n: (l, 0, n)),
        out_shape=jax.ShapeDtypeStruct((DEPTH, nb, 6 * D), F32),
        compiler_params=_cparams(2),
        name="modulation",
    )(c_all, w_mod, b_mod.reshape(DEPTH, 1, 6 * D))


def _tail(is_p, x, y, g1p, g1s, sh2p, sh2s, sc2p, sc2s, n2g, wr, br, xo_ref, h2_ref, idx_ref, gate_ref,
          cnt_ref):
    xn = x + _pick(is_p, g1p, g1s) * y
    xo_ref[...] = xn
    h2 = (_rms(xn, n2g[...]) * (1.0 + _pick(is_p, sc2p, sc2s)) + _pick(is_p, sh2p, sh2s)).astype(BF16)
    h2_ref[...] = h2
    logit = jnp.dot(h2, wr[...], preferred_element_type=F32) + br[...]
    lane = lax.broadcasted_iota(I32, logit.shape, 1)
    vals, ids = [], []
    for _ in range(TOPK):
        m = jnp.max(logit, axis=-1, keepdims=True)
        sel = jnp.min(jnp.where(logit == m, lane, LANES), axis=-1, keepdims=True)
        vals.append(m)
        ids.append(sel)
        logit = jnp.where(lane == sel, -jnp.inf, logit)
    es = [jnp.exp(v - vals[0]) for v in vals]
    tot = (es[0] + es[1]) + (es[2] + es[3])
    idx_out = jnp.zeros(logit.shape, I32)
    gate_out = jnp.zeros(logit.shape, F32)
    chosen = jnp.zeros(logit.shape, F32)
    for k in range(TOPK):
        idx_out = jnp.where(lane == k, ids[k], idx_out)
        gate_out = jnp.where(lane == k, es[k] / tot, gate_out)
        chosen = chosen + jnp.where(lane == ids[k], 1.0, 0.0)
    idx_ref[...] = idx_out
    gate_ref[...] = gate_out

    @pl.when(pl.program_id(0) == 0)
    def _():
        cnt_ref[...] = jnp.zeros(cnt_ref.shape, F32)

    cnt_ref[...] += jnp.sum(chosen, axis=0, keepdims=True)


def _tail_in_specs():
    return (_mod_specs(2) + _mod_specs(3) + _mod_specs(4)
            + [_full_spec((1, D)), _full_spec((D, LANES)), _full_spec((1, LANES))])


def _tail_out_specs():
    return [_tile_spec(D), _tile_spec(D), _tile_spec(LANES), _tile_spec(LANES),
            pl.BlockSpec((1, LANES), lambda t: (0, 0))]


def _tail_out_shapes():
    return [jax.ShapeDtypeStruct((T, D), F32), jax.ShapeDtypeStruct((T, D), BF16),
            jax.ShapeDtypeStruct((T, LANES), I32), jax.ShapeDtypeStruct((T, LANES), F32),
            jax.ShapeDtypeStruct((1, LANES), F32)]


def _sgu_kernel(x_ref, sh1p, sh1s, sc1p, sc1s, n1g, win, lng, lnb, mixw, mixb, wout,
                g1p, g1s, sh2p, sh2s, sc2p, sc2s, n2g, wr, br,
                xo_ref, h2_ref, idx_ref, gate_ref, cnt_ref, v_ref, y_scr):
    t = pl.program_id(0)
    is_p = t < NPT
    x = x_ref[...]
    h = _rms(x, n1g[...]) * (1.0 + _pick(is_p, sc1p, sc1s)) + _pick(is_p, sh1p, sh1s)
    z = jnp.dot(h.astype(BF16), win[...], preferred_element_type=F32)
    z = 0.5 * z * (1.0 + lax.erf(z * (0.5 ** 0.5)))
    u = z[:, :SGU_W]
    v = z[:, SGU_W:]
    vc = v - jnp.mean(v, axis=-1, keepdims=True)
    vn = vc * lax.rsqrt(jnp.mean(vc * vc, axis=-1, keepdims=True) + EPS) * lng[...] + lnb[...]

    @pl.when(t >= NPT)
    def _():
        v_ref[...] = vn

    vb = vn.astype(BF16)
    ri = lax.broadcasted_iota(I32, (CHUNK, CHUNK), 0)
    ci = lax.broadcasted_iota(I32, (CHUNK, CHUNK), 1)
    causal = ci <= ri
    shift = jnp.broadcast_to(jnp.where(is_p, 7, 3), ri.shape)
    keep = causal & (lax.shift_right_logical(ri, shift) == lax.shift_right_logical(ci, shift))
    for g in range(SGU_G):
        wg = jnp.where(keep, mixw[0, g], 0.0).astype(BF16)
        for c in range(TD // CHUNK):
            rows = slice(c * CHUNK, (c + 1) * CHUNK)
            cols = slice(g * SGU_GD, (g + 1) * SGU_GD)
            mixed = jnp.dot(wg, vb[rows, cols], preferred_element_type=F32) + mixb[0, :, cols]
            y_scr[rows, cols] = (u[rows, cols] * mixed).astype(BF16)
    y = jnp.dot(y_scr[...], wout[...], preferred_element_type=F32)
    _tail(is_p, x, y, g1p, g1s, sh2p, sh2s, sc2p, sc2s, n2g, wr, br, xo_ref, h2_ref, idx_ref, gate_ref,
          cnt_ref)


def _sgu_layer(x, mod_p, mod_s, n1g, n2g, win, lng, lnb, mixw, mixb, wout, wr, br):
    sel = lambda t: (jnp.where(t < NPT, 0, 1), 0, 0, 0)
    in_specs = ([_tile_spec(D)] + _mod_specs(0) + _mod_specs(1)
                + [_full_spec((1, D)), _full_spec((D, 2 * SGU_W)), _full_spec((1, SGU_W)),
                   _full_spec((1, SGU_W)),
                   pl.BlockSpec((1, SGU_G, CHUNK, CHUNK), sel),
                   pl.BlockSpec((1, CHUNK, SGU_W), lambda t: (jnp.where(t < NPT, 0, 1), 0, 0)),
                   _full_spec((SGU_W, D))]
                + _tail_in_specs())
    out_specs = _tail_out_specs() + [pl.BlockSpec((TD, SGU_W), lambda t: (jnp.maximum(t - NPT, 0), 0))]
    out_shape = _tail_out_shapes() + [jax.ShapeDtypeStruct((T_S, SGU_W), F32)]
    return pl.pallas_call(
        _sgu_kernel, grid=(NT,), in_specs=in_specs, out_specs=out_specs, out_shape=out_shape,
        scratch_shapes=[pltpu.VMEM((TD, SGU_W), BF16)],
        compiler_params=_cparams(1), name="sgu_layer",
    )(x, mod_p, mod_s, mod_p, mod_s, n1g, win, lng, lnb, mixw, mixb, wout,
      mod_p, mod_s, mod_p, mod_s, mod_p, mod_s, n2g, wr, br)


def _ret_proj_kernel(x_ref, sh1p, sh1s, sc1p, sc1s, n1g, win, cos_ref, sin_ref,
                     q_ref, k_ref, v_ref, sg_ref):
    t = pl.program_id(0)
    is_p = t < NPT
    x = x_ref[...]
    h = _rms(x, n1g[...]) * (1.0 + _pick(is_p, sc1p, sc1s)) + _pick(is_p, sh1p, sh1s)
    p = jnp.dot(h.astype(BF16), win[...], preferred_element_type=F32)
    cos = cos_ref[...]
    sin = sin_ref[...]
    for hd in range(H):
        cq = slice(hd * DK, (hd + 1) * DK)
        ck = slice(QD + hd * DK, QD + (hd + 1) * DK)
        qh = p[:, cq]
        kh = p[:, ck]
        q_ref[:, cq] = (qh * cos + pltpu.roll(qh, DK // 2, 1) * sin).astype(BF16)
        k_ref[:, cq] = ((kh * cos + pltpu.roll(kh, DK // 2, 1) * sin) * (DK ** -0.5)).astype(BF16)
    v_ref[...] = p[:, 2 * QD:2 * QD + VD].astype(BF16)
    g = p[:, 2 * QD + VD:]
    sg_ref[...] = (g * jax.nn.sigmoid(g)).astype(BF16)


def _ret_proj(x, mod_p, mod_s, n1g, win, cos_tab, sin_tab):
    in_specs = ([_tile_spec(D)] + _mod_specs(0) + _mod_specs(1)
                + [_full_spec((1, D)), _full_spec((D, RET_IN)), _tile_spec(DK), _tile_spec(DK)])
    out_specs = [_tile_spec(QD), _tile_spec(QD), _tile_spec(VD), _tile_spec(VD)]
    out_shape = [jax.ShapeDtypeStruct((T, QD), BF16), jax.ShapeDtypeStruct((T, QD), BF16),
                 jax.ShapeDtypeStruct((T, VD), BF16), jax.ShapeDtypeStruct((T, VD), BF16)]
    return pl.pallas_call(
        _ret_proj_kernel, grid=(NT,), in_specs=in_specs, out_specs=out_specs, out_shape=out_shape,
        compiler_params=_cparams(1), name="ret_proj",
    )(x, mod_p, mod_s, mod_p, mod_s, n1g, win, cos_tab, sin_tab)


def _ret_intra(q, k, v, decay, qdec, kdec):
    s = lax.dot_general(q, k, (((1,), (1,)), ((), ())), preferred_element_type=F32) * decay
    o = jnp.dot(s.astype(BF16), v, preferred_element_type=F32)
    return o, q.astype(F32) * qdec, k.astype(F32) * kdec


def _ret_finish(o, sg, ng):
    on = o * lax.rsqrt(jnp.mean(o * o, axis=-1, keepdims=True) + EPS)
    return (sg.astype(F32) * (on * ng)).astype(BF16)


def _ret_prompt_kernel(gam_ref, q_ref, k_ref, v_ref, sg_ref, dec_ref, qdec_ref, kdec_ref, ng_ref,
                       y_ref, s_ref):
    c = pl.program_id(1)

    @pl.when(c == 0)
    def _():
        s_ref[...] = jnp.zeros(s_ref.shape, F32)

    for hd in range(H):
        ck = slice(hd * DK, (hd + 1) * DK)
        cv = slice(hd * DV, (hd + 1) * DV)
        v = v_ref[:, cv]
        o, qd, kd = _ret_intra(q_ref[:, ck], k_ref[:, ck], v, dec_ref[hd], qdec_ref[:, ck], kdec_ref[:, ck])
        s_old = s_ref[0, hd]
        o = o + jnp.dot(qd.astype(BF16), s_old.astype(BF16), preferred_element_type=F32)
        s_ref[0, hd] = gam_ref[hd] * s_old + lax.dot_general(
            kd.astype(BF16), v, (((0,), (0,)), ((), ())), preferred_element_type=F32)
        y_ref[:, cv] = _ret_finish(o, sg_ref[:, cv], ng_ref[:, cv])


def _ret_prompt(q, k, v, sg, tabs, ng):
    nc = L_P // CHUNK
    row = lambda b, c: (b * nc + c, 0)
    smem = pl.BlockSpec(memory_space=pltpu.SMEM)
    in_specs = [smem,
                pl.BlockSpec((CHUNK, QD), row), pl.BlockSpec((CHUNK, QD), row),
                pl.BlockSpec((CHUNK, VD), row), pl.BlockSpec((CHUNK, VD), row),
                _full_spec((H, CHUNK, CHUNK)), _full_spec((CHUNK, QD)), _full_spec((CHUNK, QD)),
                _full_spec((1, VD))]
    out_specs = [pl.BlockSpec((CHUNK, VD), row),
                 pl.BlockSpec((1, H, DK, DV), lambda b, c: (b, 0, 0, 0))]
    out_shape = [jax.ShapeDtypeStruct((T, VD), BF16), jax.ShapeDtypeStruct((NB_P, H, DK, DV), F32)]
    return pl.pallas_call(
        _ret_prompt_kernel, grid=(NB_P, nc), in_specs=in_specs, out_specs=out_specs, out_shape=out_shape,
        compiler_params=_cparams(2), name="ret_prompt",
    )(tabs["gam"], q, k, v, sg, tabs["decay"], tabs["qdec"], tabs["kdec"], ng)


def _ret_sample_kernel(gam_ref, q_ref, k_ref, v_ref, sg_ref, dec_ref, qdec_ref, kdec_ref, ng_ref,
                       s_in_ref, y_in_ref, s_all_ref, y_ref, s_out_ref):
    del y_in_ref, s_all_ref
    s_in_ref = s_in_ref.at[0]
    s_out_ref = s_out_ref.at[0]
    hd = pl.program_id(1)
    v = v_ref[...]
    o, qd, kd = _ret_intra(q_ref[...], k_ref[...], v, dec_ref[0], qdec_ref[...], kdec_ref[...])
    gam = gam_ref[hd]
    row_batch = lax.broadcasted_iota(I32, (CHUNK, DK), 0) // L_S
    cross = []
    for b in range(SB):
        s_old = s_in_ref[b, 0]
        rows = slice(b * L_S, (b + 1) * L_S)
        cross.append(jnp.dot(qd[rows].astype(BF16), s_old.astype(BF16), preferred_element_type=F32))
        kb = jnp.where(row_batch == b, kd, 0.0).astype(BF16)
        s_out_ref[b, 0] = gam * s_old + lax.dot_general(
            kb, v, (((0,), (0,)), ((), ())), preferred_element_type=F32)
    o = o + jnp.concatenate(cross, axis=0)
    y_ref[...] = _ret_finish(o, sg_ref[...], ng_ref[...])


def _ret_sample(j, q, k, v, sg, tabs, ng, s_in, y_prev, s_all):
    base = T_P // CHUNK
    rq = lambda g, h: (base + g, h)
    st = pl.BlockSpec((1, SB, 1, DK, DV), lambda g, h: (j, g, h, 0, 0))
    smem = pl.BlockSpec(memory_space=pltpu.SMEM)
    in_specs = [smem,
                pl.BlockSpec((CHUNK, DK), rq), pl.BlockSpec((CHUNK, DK), rq),
                pl.BlockSpec((CHUNK, DV), rq), pl.BlockSpec((CHUNK, DV), rq),
                pl.BlockSpec((1, CHUNK, CHUNK), lambda g, h: (h, 0, 0)),
                pl.BlockSpec((CHUNK, DK), lambda g, h: (0, h)),
                pl.BlockSpec((CHUNK, DK), lambda g, h: (0, h)),
                pl.BlockSpec((1, DV), lambda g, h: (0, h)),
                st, pl.BlockSpec(memory_space=pl.ANY), pl.BlockSpec(memory_space=pl.ANY)]
    out_specs = [pl.BlockSpec((CHUNK, DV), rq), st]
    out_shape = [jax.ShapeDtypeStruct((T, VD), BF16), jax.ShapeDtypeStruct(s_all.shape, F32)]
    return pl.pallas_call(
        _ret_sample_kernel, grid=(NB_S // SB, H), in_specs=in_specs, out_specs=out_specs,
        out_shape=out_shape, input_output_aliases={10: 0, 11: 1},
        compiler_params=_cparams(2), name="ret_sample",
    )(tabs["gam"], q, k, v, sg, tabs["decay"], tabs["qdec"], tabs["kdec"], ng, s_in, y_prev, s_all)


def _ret_out_kernel(y_ref, wout, x_ref, g1p, g1s, sh2p, sh2s, sc2p, sc2s, n2g, wr, br,
                    xo_ref, h2_ref, idx_ref, gate_ref, cnt_ref):
    is_p = pl.program_id(0) < NPT
    y = jnp.dot(y_ref[...], wout[...], preferred_element_type=F32)
    _tail(is_p, x_ref[...], y, g1p, g1s, sh2p, sh2s, sc2p, sc2s, n2g, wr, br,
          xo_ref, h2_ref, idx_ref, gate_ref, cnt_ref)


def _ret_out(y, wout, x, mod_p, mod_s, n2g, wr, br):
    in_specs = [_tile_spec(VD), _full_spec((VD, D)), _tile_spec(D)] + _tail_in_specs()
    return pl.pallas_call(
        _ret_out_kernel, grid=(NT,), in_specs=in_specs, out_specs=_tail_out_specs(),
        out_shape=_tail_out_shapes(), compiler_params=_cparams(1), name="ret_out",
    )(y, wout, x, mod_p, mod_s, mod_p, mod_s, mod_p, mod_s, n2g, wr, br)


def _lane_prefix(v, lane1):
    s = 1
    while s < LANES:
        v = v + jnp.where(lane1 >= s, pltpu.roll(v, s, 1), 0.0)
        s *= 2
    return v


def _route_kernel(cnt_ref, idx_ref, lp_ref, lpt_ref, runs_ref, meta_ref, base_scr):
    t = pl.program_id(0)
    idx = idx_ref[...]
    lane = lax.broadcasted_iota(I32, (TM, LANES), 1)
    lane1 = lax.broadcasted_iota(I32, (1, LANES), 1)
    hits = [lane == idx[:, k:k + 1] for k in range(TOPK)]
    chosen = jnp.zeros((TM, LANES), F32)
    for hk in hits:
        chosen = chosen + jnp.where(hk, 1.0, 0.0)
    colsum = jnp.sum(chosen, axis=0, keepdims=True)

    @pl.when(t == 0)
    def _():
        cnt = cnt_ref[...]
        padded = (((cnt.astype(I32) + (BLK - 1)) // BLK) * BLK).astype(F32)
        end = _lane_prefix(padded, lane1)
        base_scr[...] = end - padded
        row = lax.broadcasted_iota(I32, (8, LANES), 0)
        meta = jnp.where(row == 0, cnt, jnp.where(row == 1, end - padded, jnp.where(row == 2, end, 0.0)))
        meta_ref[...] = meta.astype(I32)

    ri = lax.broadcasted_iota(I32, (TM, TM), 0)
    ci = lax.broadcasted_iota(I32, (TM, TM), 1)
    before = jnp.where(ci < ri, 1.0, 0.0).astype(BF16)
    loff = _lane_prefix(colsum, lane1) - colsum
    pos = jnp.dot(before, chosen.astype(BF16), preferred_element_type=F32) + loff
    lp = jnp.zeros((TM, LANES), F32)
    for k, hk in enumerate(hits):
        lp = jnp.where(lane == k, jnp.sum(jnp.where(hk, pos, 0.0), axis=-1, keepdims=True), lp)
    lp_ref[...] = lp.astype(I32)
    lpt_ref[...] = lp.T[:8].astype(I32)
    row = lax.broadcasted_iota(I32, (8, LANES), 0)
    runs = jnp.where(row == 0, colsum, jnp.where(row == 1, loff, jnp.where(row == 2, base_scr[...], 0.0)))
    runs_ref[0] = runs.astype(I32)
    base_scr[...] += colsum


def _route(cnt, idx):
    n_tiles = idx.shape[0] // TM
    return pl.pallas_call(
        _route_kernel, grid=(n_tiles,),
        in_specs=[pl.BlockSpec((1, LANES), lambda t: (0, 0)), pl.BlockSpec((TM, LANES), lambda t: (t, 0))],
        out_specs=[pl.BlockSpec((TM, LANES), lambda t: (t, 0)),
                   pl.BlockSpec((8, TM), lambda t: (t, 0)),
                   pl.BlockSpec((1, 8, LANES), lambda t: (t, 0, 0)),
                   pl.BlockSpec((8, LANES), lambda t: (0, 0))],
        out_shape=[jax.ShapeDtypeStruct((n_tiles * TM, LANES), I32),
                   jax.ShapeDtypeStruct((n_tiles * 8, TM), I32),
                   jax.ShapeDtypeStruct((n_tiles, 8, LANES), I32),
                   jax.ShapeDtypeStruct((8, LANES), I32)],
        scratch_shapes=[pltpu.VMEM((1, LANES), F32)],
        compiler_params=_cparams(1), name="moe_route",
    )(cnt, idx)


def _tile_rows(ref, r, n):
    return ref.at[:, pl.ds(r, n), :]


def _run_copy(src, dst, r_src, r_dst, n, sem, wait):
    for b in reversed(range(TM.bit_length())):
        size = 1 << b
        off = (n >> (b + 1)) << (b + 1)

        def piece(off=off, size=size):
            cp = pltpu.make_async_copy(_tile_rows(src, r_src + off, size),
                                       _tile_rows(dst, r_dst + off, size), sem)
            if wait:
                cp.wait()
            else:
                cp.start()

        if isinstance(n, int):
            if n & size:
                piece()
        else:
            pl.when((n & size) != 0)(piece)


def _load_rows(ref3):
    return jnp.concatenate([ref3[s] for s in range(ROW_TILES)], axis=1)


def _store_rows(ref3, val):
    for s in range(ROW_TILES):
        ref3[s] = val[:, s * LANES:(s + 1) * LANES]


def _dispatch_kernel(len_ref, pos_ref, row_ref, cnt_ref, first_ref, end_ref, lpt_ref, h2_ref, dst,
                     buf, zbuf, sem, zsem):
    n_rows = dst.shape[1]
    step = pl.program_id(0)
    slot = step % 2
    cur = buf.at[slot]
    lpt = lpt_ref[...]
    p = lax.broadcasted_iota(I32, (TM * TOPK, TM), 0)
    pick = lpt[0:1, :] == p
    for k in range(1, TOPK):
        pick = pick | (lpt[k:k + 1, :] == p)
    perm = jnp.where(pick, 1.0, 0.0).astype(BF16)
    _store_rows(cur, jnp.dot(perm, h2_ref[...], preferred_element_type=F32))

    def per_expert(e, carry):
        _run_copy(cur, dst, pos_ref[e], row_ref[e], len_ref[e], sem.at[slot], False)
        return carry

    lax.fori_loop(0, NE, per_expert, 0)

    def drain(s):
        pltpu.make_async_copy(buf.at[s], _tile_rows(dst, 0, TM * TOPK), sem.at[s]).wait()

    @pl.when(step > 0)
    def _():
        drain(1 - slot)

    @pl.when(step == pl.num_programs(0) - 1)
    def _():
        drain(slot)

    @pl.when(step == 0)
    def _():
        zbuf[...] = jnp.zeros(zbuf.shape, F32)
        n_tail = (n_rows - end_ref[NE - 1]) // TM
        for wait in (False, True):
            def pad(e, carry, wait=wait):
                lo = first_ref[e] + cnt_ref[e]
                _run_copy(zbuf, dst, 0, lo, end_ref[e] - lo, zsem, wait)
                return carry

            lax.fori_loop(0, NE, pad, 0)

            def tail(j, carry, wait=wait):
                _run_copy(zbuf, dst, 0, end_ref[NE - 1] + j * TM, TM, zsem, wait)
                return carry

            lax.fori_loop(0, n_tail, tail, 0)


def _dispatch(h2, lpt, run_len, run_pos, run_row, cnt, first, end, n_rows):
    assert BLK % TM == 0 and BLK < 2 * TM + 1
    n_tiles = h2.shape[0] // TM
    smem = pl.BlockSpec(memory_space=pltpu.SMEM)
    per_tile = pl.BlockSpec((LANES,), lambda t: (t,), memory_space=pltpu.SMEM)
    return pl.pallas_call(
        _dispatch_kernel, grid=(n_tiles,),
        in_specs=[per_tile, per_tile, per_tile, smem, smem, smem,
                  pl.BlockSpec((8, TM), lambda t: (t, 0)), _tile_spec(D, TM)],
        out_specs=pl.BlockSpec(memory_space=pl.ANY),
        out_shape=jax.ShapeDtypeStruct((ROW_TILES, n_rows, LANES), F32),
        scratch_shapes=[pltpu.VMEM((2, ROW_TILES, TM * TOPK, LANES), F32),
                        pltpu.VMEM((ROW_TILES, BLK, LANES), F32),
                        pltpu.SemaphoreType.DMA((2,)), pltpu.SemaphoreType.DMA],
        compiler_params=_cparams(1), name="moe_dispatch",
    )(run_len, run_pos, run_row, cnt, first, end, lpt, h2)


def _expert_kernel(be_ref, nu_ref, xb_ref, wu_ref, bu_ref, wd_ref, bd_ref, yb_ref, wub, wdb):
    i = pl.program_id(0)
    used = i < nu_ref[0]

    @pl.when(used)
    def _():
        fresh = (i == 0) | (be_ref[i] != be_ref[jnp.maximum(i - 1, 0)])

        @pl.when(fresh)
        def _():
            wub[...] = wu_ref[0, 0].astype(BF16)
            wdb[...] = wd_ref[0, 0].astype(BF16)

        xb = _load_rows(xb_ref).astype(BF16)
        z = jnp.dot(xb, wub[...], preferred_element_type=F32) + bu_ref[0, 0]
        glu = jnp.minimum(z[:, :FE], LIMIT)
        lin = jnp.clip(z[:, FE:], -LIMIT, LIMIT)
        act = glu * jax.nn.sigmoid(ALPHA * glu) * (lin + 1.0)
        _store_rows(yb_ref, jnp.dot(act.astype(BF16), wdb[...], preferred_element_type=F32) + bd_ref[0, 0])

    @pl.when(jnp.logical_not(used))
    def _():
        yb_ref[...] = jnp.zeros(yb_ref.shape, F32)


def _experts(layer, xb, blk_e, n_used, wu, bu, wd, bd):
    grid_spec = pltpu.PrefetchScalarGridSpec(
        num_scalar_prefetch=2, grid=(N_BLOCKS,),
        in_specs=[pl.BlockSpec((ROW_TILES, BLK, LANES),
                               lambda i, be, nu: (0, jnp.minimum(i, nu[0] - 1), 0)),
                  pl.BlockSpec((1, 1, D, 2 * FE), lambda i, be, nu: (layer, be[i], 0, 0)),
                  pl.BlockSpec((1, 1, 1, 2 * FE), lambda i, be, nu: (layer, be[i], 0, 0)),
                  pl.BlockSpec((1, 1, FE, D), lambda i, be, nu: (layer, be[i], 0, 0)),
                  pl.BlockSpec((1, 1, 1, D), lambda i, be, nu: (layer, be[i], 0, 0))],
        out_specs=pl.BlockSpec((ROW_TILES, BLK, LANES), lambda i, be, nu: (0, i, 0)),
        scratch_shapes=[pltpu.VMEM((D, 2 * FE), BF16), pltpu.VMEM((FE, D), BF16)])
    return pl.pallas_call(
        _expert_kernel, grid_spec=grid_spec,
        out_shape=jax.ShapeDtypeStruct((ROW_TILES, N_ROWS, LANES), F32),
        compiler_params=_cparams(1), name="experts",
    )(blk_e, n_used, xb, wu, bu, wd, bd)


def _combine_kernel(len_ref, pos_ref, row_ref, len_nx, pos_nx, row_nx, yb, lp_ref, x_ref, gate_ref,
                    g2p, g2s, fg_ref, o_ref, ybuf, sem, *, final):
    step = pl.program_id(0)
    slot = step % 2
    is_p = step < T_P // TM

    def fetch(len_r, pos_r, row_r, s):
        def per_expert(e, carry):
            _run_copy(yb, ybuf.at[s], row_r[e], pos_r[e], len_r[e], sem.at[s], False)
            return carry

        lax.fori_loop(0, NE, per_expert, 0)

    @pl.when(step == 0)
    def _():
        fetch(len_ref, pos_ref, row_ref, slot)

    @pl.when(step + 1 < pl.num_programs(0))
    def _():
        fetch(len_nx, pos_nx, row_nx, 1 - slot)

    cur = ybuf.at[slot]
    pltpu.make_async_copy(_tile_rows(yb, 0, TM * TOPK), cur, sem.at[slot]).wait()

    rows = x_ref.shape[0]
    lp = lp_ref[...]
    gates = gate_ref[...]
    p = lax.broadcasted_iota(I32, (rows, rows * TOPK), 1)
    weights = jnp.zeros((rows, rows * TOPK), F32)
    for k in range(TOPK):
        weights = jnp.where(lp[:, k:k + 1] == p, gates[:, k:k + 1], weights)
    acc = jnp.dot(weights.astype(BF16), _load_rows(cur).astype(BF16), preferred_element_type=F32)
    xn = x_ref[...] + _pick(is_p, g2p, g2s) * acc
    o_ref[...] = _rms(xn, fg_ref[...]) if final else xn


def _combine(x, yb, lp, run_len, run_pos, run_row, gates, mod_p, mod_s, final_g, final):
    n_tiles = x.shape[0] // TM
    per_tile = pl.BlockSpec((LANES,), lambda t: (t,), memory_space=pltpu.SMEM)
    next_tile = pl.BlockSpec((LANES,), lambda t: (jnp.minimum(t + 1, n_tiles - 1),),
                             memory_space=pltpu.SMEM)
    in_specs = ([per_tile, per_tile, per_tile, next_tile, next_tile, next_tile,
                 pl.BlockSpec(memory_space=pl.ANY),
                 _tile_spec(LANES, TM), _tile_spec(D, TM), _tile_spec(LANES, TM)]
                + _mod_specs(5, TM) + [_full_spec((1, D))])
    return pl.pallas_call(
        functools.partial(_combine_kernel, final=final), grid=(n_tiles,), in_specs=in_specs,
        out_specs=_tile_spec(D, TM), out_shape=jax.ShapeDtypeStruct(x.shape, F32),
        scratch_shapes=[pltpu.VMEM((2, ROW_TILES, TM * TOPK, LANES), F32), pltpu.SemaphoreType.DMA((2,))],
        compiler_params=_cparams(1), name="moe_combine",
    )(run_len, run_pos, run_row, run_len, run_pos, run_row, yb, lp, x, gates, mod_p, mod_s, final_g)


def _moe(layer, x, h2, idx, gates, cnt_all, mod_p, mod_s, wu, bu, wd, bd, final_g, final):
    lp, lpt, runs, meta = _route(cnt_all, idx)
    cnt, first, end = meta[0, :NE], meta[1, :NE], meta[2, :NE]
    run_len, run_pos, run_row = (runs[:, r, :].reshape(-1) for r in range(3))
    blk_first = jnp.arange(N_BLOCKS, dtype=I32) * BLK
    blk_e = jnp.minimum(jnp.sum((end[None, :] <= blk_first[:, None]).astype(I32), axis=1), NE - 1)
    n_used = end[NE - 1:] // BLK
    xb = _dispatch(h2, lpt, run_len, run_pos, run_row, cnt, first, end, N_ROWS)
    yb = _experts(layer, xb, blk_e, n_used, wu, bu, wd, bd)
    return _combine(x, yb, lp, run_len, run_pos, run_row, gates, mod_p, mod_s, final_g, final)


def _rope_tables():
    half = DK // 2
    inv = 1.0 / (ROPE_BASE ** jnp.linspace(0.0, 1.0, half, dtype=F32))

    def tab(pos):
        ang = pos.astype(F32)[:, None] * inv[None, :]
        cos, sin = jnp.cos(ang), jnp.sin(ang)
        return jnp.concatenate([cos, cos], -1), jnp.concatenate([-sin, sin], -1)

    cp, sp = tab(jnp.arange(L_P, dtype=I32))
    cs, ss = tab(PAST + jnp.arange(L_S, dtype=I32))
    cos = jnp.concatenate([jnp.tile(cp, (NB_P, 1)), jnp.tile(cs, (NB_S, 1))], 0)
    sin = jnp.concatenate([jnp.tile(sp, (NB_P, 1)), jnp.tile(ss, (NB_S, 1))], 0)
    return cos, sin


def _decay_tables(cl):
    lg = jnp.log(1.0 - 2.0 ** (-5.0 - jnp.arange(H, dtype=F32)))
    r = jnp.arange(CHUNK)
    idx = (r % cl).astype(F32)
    diff = idx[:, None] - idx[None, :]
    same = (r[:, None] // cl) == (r[None, :] // cl)
    decay = jnp.where((same & (diff >= 0))[None],
                      jnp.exp(lg[:, None, None] * jnp.maximum(diff, 0.0)[None]), 0.0)
    qdec = jnp.exp(lg[None, :] * (idx[:, None] + 1.0))
    kdec = jnp.exp(lg[None, :] * (cl - 1.0 - idx[:, None]))
    wide = lambda a: jnp.repeat(a, DK, axis=1)
    return {"decay": decay, "qdec": wide(qdec), "kdec": wide(kdec), "gam": jnp.exp(lg * cl)}


def kernel(x_prompt, x_sample, c_prompt, c_sample, state_ret, w_mod, b_mod, norm1_g, norm2_g,
           sgu_w_in, sgu_ln_g, sgu_ln_b, sgu_w_s, sgu_b_s, sgu_w_out, ret_w_in, ret_norm_g, ret_w_out,
           moe_w_router, moe_b_router, moe_w_up, moe_b_up, moe_w_down, moe_b_down, final_g):
    x = jnp.concatenate([x_prompt.reshape(T_P, D), x_sample.reshape(T_S, D)], 0)
    mod = _modulation(jnp.concatenate([c_prompt, c_sample], 0), w_mod, b_mod)
    cos_tab, sin_tab = _rope_tables()
    tabs_p = _decay_tables(CHUNK)
    tabs_s = _decay_tables(L_S)
    wr_pad = jnp.pad(moe_w_router, ((0, 0), (0, 0), (0, LANES - NE))).astype(BF16)
    br_pad = jnp.pad(moe_b_router, ((0, 0), (0, LANES - NE)), constant_values=-1e30)
    fg = final_g.reshape(1, D)
    b_up = moe_b_up.reshape(DEPTH, NE, 1, 2 * FE)
    b_down = moe_b_down.reshape(DEPTH, NE, 1, D)

    ret_p, v_rows = [], []
    s_all = lax.empty(state_ret.shape, F32)
    for i in range(DEPTH):
        j = i // 2
        mod_p = mod[i, :NB_P].reshape(NB_P, 1, 6 * D)
        mod_s = jnp.repeat(mod[i, NB_P:], L_S, axis=0)
        n1g = norm1_g[i].reshape(1, D)
        n2g = norm2_g[i].reshape(1, D)
        wr = wr_pad[i]
        br = br_pad[i].reshape(1, LANES)
        if i % 2 == 0:
            mixw = jnp.stack([sgu_w_s[j], jnp.tile(sgu_w_s[j][:, :L_S, :L_S], (1, SB, SB))])
            bias_p = jnp.repeat(sgu_b_s[j].T, SGU_GD, axis=1)
            bias_s = jnp.tile(bias_p[:L_S], (SB, 1))
            x, h2, idx, gates, cnt, v = _sgu_layer(
                x, mod_p, mod_s, n1g, n2g, sgu_w_in[j].astype(BF16), sgu_ln_g[j].reshape(1, SGU_W),
                sgu_ln_b[j].reshape(1, SGU_W), mixw, jnp.stack([bias_p, bias_s]),
                sgu_w_out[j].astype(BF16), wr, br)
            v_rows.append(v.reshape(NB_S, L_S, SGU_W))
        else:
            q, k, v, sg = _ret_proj(x, mod_p, mod_s, n1g, ret_w_in[j].astype(BF16), cos_tab, sin_tab)
            ng = ret_norm_g[j].reshape(1, VD)
            y, s_p = _ret_prompt(q, k, v, sg, tabs_p, ng)
            y, s_all = _ret_sample(j, q, k, v, sg, tabs_s, ng, state_ret, y, s_all)
            ret_p.append(s_p)
            x, h2, idx, gates, cnt = _ret_out(y, ret_w_out[j].astype(BF16), x, mod_p, mod_s, n2g, wr, br)
        x = _moe(i, x, h2, idx, gates, cnt, mod_p, mod_s, moe_w_up, b_up, moe_w_down, b_down,
                 fg, final=(i == DEPTH - 1))
    y_prompt = x[:T_P].reshape(NB_P, L_P, D)
    y_sample = x[T_P:].reshape(NB_S, L_S, D)
    return (y_prompt, y_sample, jnp.stack(ret_p), s_all, jnp.stack(v_rows))
```

```python
import functools

import jax
import jax.numpy as jnp
from jax import lax
from jax.experimental import pallas as pl
from jax.experimental.pallas import tpu as pltpu

F32 = jnp.float32
BF16 = jnp.bfloat16
I32 = jnp.int32

D = 1024
NB_P, L_P = 8, 2048
NB_S, L_S = 128, 8
PAST = 16384
DEPTH = 4
T_P = NB_P * L_P
T_S = NB_S * L_S
T = T_P + T_S
SGU_W = 2 * D
SGU_G = 8
SGU_GD = SGU_W // SGU_G
CHUNK = 128
H = 8
DK = D // H
DV = 2 * DK
QD = H * DK
VD = H * DV
RET_IN = 2 * QD + 2 * VD
ROPE_BASE = 10000.0
NE = 32
TOPK = 4
FE = D
ALPHA = 1.702
LIMIT = 7.0
EPS = 1e-6

LANES = 128
SLABS = D // (2 * LANES)
HIGH_HALF = -65536
TD = 512
NPT = T_P // TD
NT = T // TD
TM = 256
BLK = 512
TK = T * TOPK
N_BLOCKS = -(-(TK + NE * (BLK - 1)) // BLK)
N_ROWS = N_BLOCKS * BLK
SB = 16
VMEM_LIMIT = 56 * 1024 * 1024


def _cparams(n_axes):
    return pltpu.CompilerParams(dimension_semantics=("arbitrary",) * n_axes,
                                vmem_limit_bytes=VMEM_LIMIT)


def _rms(x, g):
    return (x * lax.rsqrt(jnp.mean(x * x, axis=-1, keepdims=True) + EPS)) * g


def _pick(is_p, p_ref, s_ref):
    return jnp.where(is_p, p_ref[0], s_ref[...])


def _mod_specs(j, tm=TD):
    per_batch, n_prompt = L_P // tm, T_P // tm
    return [
        pl.BlockSpec((1, 1, D), lambda t: (jnp.minimum(t // per_batch, NB_P - 1), 0, j)),
        pl.BlockSpec((tm, D), lambda t: (jnp.maximum(t - n_prompt, 0), j), pipeline_mode=pl.Buffered(1)),
    ]


def _tile_spec(width, tm=TD):
    return pl.BlockSpec((tm, width), lambda t: (t, 0))


def _full_spec(shape):
    return pl.BlockSpec(shape, lambda *_: (0,) * len(shape), pipeline_mode=pl.Buffered(1))


def _mod_kernel(c_ref, w_ref, b_ref, o_ref):
    c = c_ref[...]
    cs = (c * jax.nn.sigmoid(c)).astype(BF16)
    o_ref[0] = jnp.dot(cs, w_ref[0].astype(BF16), preferred_element_type=F32) + b_ref[0]


def _modulation(c_all, w_mod, b_mod):
    tn = 1536
    nb = c_all.shape[0]
    return pl.pallas_call(
        _mod_kernel,
        grid=(DEPTH, 6 * D // tn),
        in_specs=[
            pl.BlockSpec((nb, D), lambda l, n: (0, 0)),
            pl.BlockSpec((1, D, tn), lambda l, n: (l, 0, n)),
            pl.BlockSpec((1, 1, tn), lambda l, n: (l, 0, n)),
        ],
        out_specs=pl.BlockSpec((1, nb, tn), lambda l, n: (l, 0, n)),
        out_shape=jax.ShapeDtypeStruct((DEPTH, nb, 6 * D), F32),
        compiler_params=_cparams(2),
        name="modulation",
    )(c_all, w_mod, b_mod.reshape(DEPTH, 1, 6 * D))


def _tail(is_p, x, y, g1p, g1s, sh2p, sh2s, sc2p, sc2s, n2g, wr, br, xo_ref, h2_ref, idx_ref, gate_ref,
          cnt_ref):
    xn = x + _pick(is_p, g1p, g1s) * y
    xo_ref[...] = xn
    h2 = (_rms(xn, n2g[...]) * (1.0 + _pick(is_p, sc2p, sc2s)) + _pick(is_p, sh2p, sh2s)).astype(BF16)
    h2_ref[...] = h2
    logit = jnp.dot(h2, wr[...], preferred_element_type=F32) + br[...]
    lane = lax.broadcasted_iota(I32, logit.shape, 1)
    vals, ids = [], []
    for _ in range(TOPK):
        m = jnp.max(logit, axis=-1, keepdims=True)
        sel = jnp.min(jnp.where(logit == m, lane, LANES), axis=-1, keepdims=True)
        vals.append(m)
        ids.append(sel)
        logit = jnp.where(lane == sel, -jnp.inf, logit)
    es = [jnp.exp(v - vals[0]) for v in vals]
    tot = (es[0] + es[1]) + (es[2] + es[3])
    idx_out = jnp.zeros(logit.shape, I32)
    gate_out = jnp.zeros(logit.shape, F32)
    chosen = jnp.zeros(logit.shape, F32)
    for k in range(TOPK):
        idx_out = jnp.where(lane == k, ids[k], idx_out)
        gate_out = jnp.where(lane == k, es[k] / tot, gate_out)
        chosen = chosen + jnp.where(lane == ids[k], 1.0, 0.0)
    idx_ref[...] = idx_out
    gate_ref[...] = gate_out

    @pl.when(pl.program_id(0) == 0)
    def _():
        cnt_ref[...] = jnp.zeros(cnt_ref.shape, F32)

    cnt_ref[...] += jnp.sum(chosen, axis=0, keepdims=True)


def _tail_in_specs():
    return (_mod_specs(2) + _mod_specs(3) + _mod_specs(4)
            + [_full_spec((1, D)), _full_spec((D, LANES)), _full_spec((1, LANES))])


def _tail_out_specs():
    return [_tile_spec(D), _tile_spec(D), _tile_spec(LANES), _tile_spec(LANES),
            pl.BlockSpec((1, LANES), lambda t: (0, 0))]


def _tail_out_shapes():
    return [jax.ShapeDtypeStruct((T, D), F32), jax.ShapeDtypeStruct((T, D), BF16),
            jax.ShapeDtypeStruct((T, LANES), I32), jax.ShapeDtypeStruct((T, LANES), F32),
            jax.ShapeDtypeStruct((1, LANES), F32)]


def _sgu_kernel(x_ref, sh1p, sh1s, sc1p, sc1s, n1g, win, lng, lnb, mixw, mixb, wout,
                g1p, g1s, sh2p, sh2s, sc2p, sc2s, n2g, wr, br,
                xo_ref, h2_ref, idx_ref, gate_ref, cnt_ref, v_ref, y_scr):
    t = pl.program_id(0)
    is_p = t < NPT
    x = x_ref[...]
    h = _rms(x, n1g[...]) * (1.0 + _pick(is_p, sc1p, sc1s)) + _pick(is_p, sh1p, sh1s)
    z = jnp.dot(h.astype(BF16), win[...], preferred_element_type=F32)
    z = 0.5 * z * (1.0 + lax.erf(z * (0.5 ** 0.5)))
    u = z[:, :SGU_W]
    v = z[:, SGU_W:]
    vc = v - jnp.mean(v, axis=-1, keepdims=True)
    vn = vc * lax.rsqrt(jnp.mean(vc * vc, axis=-1, keepdims=True) + EPS) * lng[...] + lnb[...]

    @pl.when(t >= NPT)
    def _():
        v_ref[...] = vn

    vb = vn.astype(BF16)
    ri = lax.broadcasted_iota(I32, (CHUNK, CHUNK), 0)
    ci = lax.broadcasted_iota(I32, (CHUNK, CHUNK), 1)
    causal = ci <= ri
    shift = jnp.broadcast_to(jnp.where(is_p, 7, 3), ri.shape)
    keep = causal & (lax.shift_right_logical(ri, shift) == lax.shift_right_logical(ci, shift))
    for g in range(SGU_G):
        wg = jnp.where(keep, mixw[0, g], 0.0).astype(BF16)
        for c in range(TD // CHUNK):
            rows = slice(c * CHUNK, (c + 1) * CHUNK)
            cols = slice(g * SGU_GD, (g + 1) * SGU_GD)
            mixed = jnp.dot(wg, vb[rows, cols], preferred_element_type=F32) + mixb[0, :, cols]
            y_scr[rows, cols] = (u[rows, cols] * mixed).astype(BF16)
    y = jnp.dot(y_scr[...], wout[...], preferred_element_type=F32)
    _tail(is_p, x, y, g1p, g1s, sh2p, sh2s, sc2p, sc2s, n2g, wr, br, xo_ref, h2_ref, idx_ref, gate_ref,
          cnt_ref)


def _sgu_layer(x, mod_p, mod_s, n1g, n2g, win, lng, lnb, mixw, mixb, wout, wr, br):
    sel = lambda t: (jnp.where(t < NPT, 0, 1), 0, 0, 0)
    in_specs = ([_tile_spec(D)] + _mod_specs(0) + _mod_specs(1)
                + [_full_spec((1, D)), _full_spec((D, 2 * SGU_W)), _full_spec((1, SGU_W)),
                   _full_spec((1, SGU_W)),
                   pl.BlockSpec((1, SGU_G, CHUNK, CHUNK), sel),
                   pl.BlockSpec((1, CHUNK, SGU_W), lambda t: (jnp.where(t < NPT, 0, 1), 0, 0)),
                   _full_spec((SGU_W, D))]
                + _tail_in_specs())
    out_specs = _tail_out_specs() + [pl.BlockSpec((TD, SGU_W), lambda t: (jnp.maximum(t - NPT, 0), 0))]
    out_shape = _tail_out_shapes() + [jax.ShapeDtypeStruct((T_S, SGU_W), F32)]
    return pl.pallas_call(
        _sgu_kernel, grid=(NT,), in_specs=in_specs, out_specs=out_specs, out_shape=out_shape,
        scratch_shapes=[pltpu.VMEM((TD, SGU_W), BF16)],
        compiler_params=_cparams(1), name="sgu_layer",
    )(x, mod_p, mod_s, mod_p, mod_s, n1g, win, lng, lnb, mixw, mixb, wout,
      mod_p, mod_s, mod_p, mod_s, mod_p, mod_s, n2g, wr, br)


def _ret_proj_kernel(x_ref, sh1p, sh1s, sc1p, sc1s, n1g, win, cos_ref, sin_ref,
                     q_ref, k_ref, v_ref, sg_ref):
    t = pl.program_id(0)
    is_p = t < NPT
    x = x_ref[...]
    h = _rms(x, n1g[...]) * (1.0 + _pick(is_p, sc1p, sc1s)) + _pick(is_p, sh1p, sh1s)
    p = jnp.dot(h.astype(BF16), win[...], preferred_element_type=F32)
    cos = cos_ref[...]
    sin = sin_ref[...]
    for hd in range(H):
        cq = slice(hd * DK, (hd + 1) * DK)
        ck = slice(QD + hd * DK, QD + (hd + 1) * DK)
        qh = p[:, cq]
        kh = p[:, ck]
        q_ref[:, cq] = (qh * cos + pltpu.roll(qh, DK // 2, 1) * sin).astype(BF16)
        k_ref[:, cq] = ((kh * cos + pltpu.roll(kh, DK // 2, 1) * sin) * (DK ** -0.5)).astype(BF16)
    v_ref[...] = p[:, 2 * QD:2 * QD + VD].astype(BF16)
    g = p[:, 2 * QD + VD:]
    sg_ref[...] = (g * jax.nn.sigmoid(g)).astype(BF16)


def _ret_proj(x, mod_p, mod_s, n1g, win, cos_tab, sin_tab):
    in_specs = ([_tile_spec(D)] + _mod_specs(0) + _mod_specs(1)
                + [_full_spec((1, D)), _full_spec((D, RET_IN)), _tile_spec(DK), _tile_spec(DK)])
    out_specs = [_tile_spec(QD), _tile_spec(QD), _tile_spec(VD), _tile_spec(VD)]
    out_shape = [jax.ShapeDtypeStruct((T, QD), BF16), jax.ShapeDtypeStruct((T, QD), BF16),
                 jax.ShapeDtypeStruct((T, VD), BF16), jax.ShapeDtypeStruct((T, VD), BF16)]
    return pl.pallas_call(
        _ret_proj_kernel, grid=(NT,), in_specs=in_specs, out_specs=out_specs, out_shape=out_shape,
        compiler_params=_cparams(1), name="ret_proj",
    )(x, mod_p, mod_s, mod_p, mod_s, n1g, win, cos_tab, sin_tab)


def _ret_intra(q, k, v, decay, qdec, kdec):
    s = lax.dot_general(q, k, (((1,), (1,)), ((), ())), preferred_element_type=F32) * decay
    o = jnp.dot(s.astype(BF16), v, preferred_element_type=F32)
    return o, q.astype(F32) * qdec, k.astype(F32) * kdec


def _ret_finish(o, sg, ng):
    on = o * lax.rsqrt(jnp.mean(o * o, axis=-1, keepdims=True) + EPS)
    return (sg.astype(F32) * (on * ng)).astype(BF16)


def _ret_prompt_kernel(gam_ref, q_ref, k_ref, v_ref, sg_ref, dec_ref, qdec_ref, kdec_ref, ng_ref,
                       y_ref, s_ref):
    c = pl.program_id(1)

    @pl.when(c == 0)
    def _():
        s_ref[...] = jnp.zeros(s_ref.shape, F32)

    for hd in range(H):
        ck = slice(hd * DK, (hd + 1) * DK)
        cv = slice(hd * DV, (hd + 1) * DV)
        v = v_ref[:, cv]
        o, qd, kd = _ret_intra(q_ref[:, ck], k_ref[:, ck], v, dec_ref[hd], qdec_ref[:, ck], kdec_ref[:, ck])
        s_old = s_ref[0, hd]
        o = o + jnp.dot(qd.astype(BF16), s_old.astype(BF16), preferred_element_type=F32)
        s_ref[0, hd] = gam_ref[hd] * s_old + lax.dot_general(
            kd.astype(BF16), v, (((0,), (0,)), ((), ())), preferred_element_type=F32)
        y_ref[:, cv] = _ret_finish(o, sg_ref[:, cv], ng_ref[:, cv])


def _ret_prompt(q, k, v, sg, tabs, ng):
    nc = L_P // CHUNK
    row = lambda b, c: (b * nc + c, 0)
    smem = pl.BlockSpec(memory_space=pltpu.SMEM)
    in_specs = [smem,
                pl.BlockSpec((CHUNK, QD), row), pl.BlockSpec((CHUNK, QD), row),
                pl.BlockSpec((CHUNK, VD), row), pl.BlockSpec((CHUNK, VD), row),
                _full_spec((H, CHUNK, CHUNK)), _full_spec((CHUNK, QD)), _full_spec((CHUNK, QD)),
                _full_spec((1, VD))]
    out_specs = [pl.BlockSpec((CHUNK, VD), row),
                 pl.BlockSpec((1, H, DK, DV), lambda b, c: (b, 0, 0, 0))]
    out_shape = [jax.ShapeDtypeStruct((T, VD), BF16), jax.ShapeDtypeStruct((NB_P, H, DK, DV), F32)]
    return pl.pallas_call(
        _ret_prompt_kernel, grid=(NB_P, nc), in_specs=in_specs, out_specs=out_specs, out_shape=out_shape,
        compiler_params=_cparams(2), name="ret_prompt",
    )(tabs["gam"], q, k, v, sg, tabs["decay"], tabs["qdec"], tabs["kdec"], ng)


def _ret_sample_kernel(gam_ref, q_ref, k_ref, v_ref, sg_ref, dec_ref, qdec_ref, kdec_ref, ng_ref,
                       s_in_ref, y_in_ref, s_all_ref, y_ref, s_out_ref):
    del y_in_ref, s_all_ref
    s_in_ref = s_in_ref.at[0]
    s_out_ref = s_out_ref.at[0]
    hd = pl.program_id(1)
    v = v_ref[...]
    o, qd, kd = _ret_intra(q_ref[...], k_ref[...], v, dec_ref[0], qdec_ref[...], kdec_ref[...])
    gam = gam_ref[hd]
    row_batch = lax.broadcasted_iota(I32, (CHUNK, DK), 0) // L_S
    cross = []
    for b in range(SB):
        s_old = s_in_ref[b, 0]
        rows = slice(b * L_S, (b + 1) * L_S)
        cross.append(jnp.dot(qd[rows].astype(BF16), s_old.astype(BF16), preferred_element_type=F32))
        kb = jnp.where(row_batch == b, kd, 0.0).astype(BF16)
        s_out_ref[b, 0] = gam * s_old + lax.dot_general(
            kb, v, (((0,), (0,)), ((), ())), preferred_element_type=F32)
    o = o + jnp.concatenate(cross, axis=0)
    y_ref[...] = _ret_finish(o, sg_ref[...], ng_ref[...])


def _ret_sample(j, q, k, v, sg, tabs, ng, s_in, y_prev, s_all):
    base = T_P // CHUNK
    rq = lambda g, h: (base + g, h)
    st = pl.BlockSpec((1, SB, 1, DK, DV), lambda g, h: (j, g, h, 0, 0))
    smem = pl.BlockSpec(memory_space=pltpu.SMEM)
    in_specs = [smem,
                pl.BlockSpec((CHUNK, DK), rq), pl.BlockSpec((CHUNK, DK), rq),
                pl.BlockSpec((CHUNK, DV), rq), pl.BlockSpec((CHUNK, DV), rq),
                pl.BlockSpec((1, CHUNK, CHUNK), lambda g, h: (h, 0, 0)),
                pl.BlockSpec((CHUNK, DK), lambda g, h: (0, h)),
                pl.BlockSpec((CHUNK, DK), lambda g, h: (0, h)),
                pl.BlockSpec((1, DV), lambda g, h: (0, h)),
                st, pl.BlockSpec(memory_space=pl.ANY), pl.BlockSpec(memory_space=pl.ANY)]
    out_specs = [pl.BlockSpec((CHUNK, DV), rq), st]
    out_shape = [jax.ShapeDtypeStruct((T, VD), BF16), jax.ShapeDtypeStruct(s_all.shape, F32)]
    return pl.pallas_call(
        _ret_sample_kernel, grid=(NB_S // SB, H), in_specs=in_specs, out_specs=out_specs,
        out_shape=out_shape, input_output_aliases={10: 0, 11: 1},
        compiler_params=_cparams(2), name="ret_sample",
    )(tabs["gam"], q, k, v, sg, tabs["decay"], tabs["qdec"], tabs["kdec"], ng, s_in, y_prev, s_all)


def _ret_out_kernel(y_ref, wout, x_ref, g1p, g1s, sh2p, sh2s, sc2p, sc2s, n2g, wr, br,
                    xo_ref, h2_ref, idx_ref, gate_ref, cnt_ref):
    is_p = pl.program_id(0) < NPT
    y = jnp.dot(y_ref[...], wout[...], preferred_element_type=F32)
    _tail(is_p, x_ref[...], y, g1p, g1s, sh2p, sh2s, sc2p, sc2s, n2g, wr, br,
          xo_ref, h2_ref, idx_ref, gate_ref, cnt_ref)


def _ret_out(y, wout, x, mod_p, mod_s, n2g, wr, br):
    in_specs = [_tile_spec(VD), _full_spec((VD, D)), _tile_spec(D)] + _tail_in_specs()
    return pl.pallas_call(
        _ret_out_kernel, grid=(NT,), in_specs=in_specs, out_specs=_tail_out_specs(),
        out_shape=_tail_out_shapes(), compiler_params=_cparams(1), name="ret_out",
    )(y, wout, x, mod_p, mod_s, mod_p, mod_s, mod_p, mod_s, n2g, wr, br)


def _lane_prefix(v, lane1):
    s = 1
    while s < LANES:
        v = v + jnp.where(lane1 >= s, pltpu.roll(v, s, 1), 0.0)
        s *= 2
    return v


def _route_kernel(cnt_ref, idx_ref, lp_ref, lpt_ref, runs_ref, meta_ref, base_scr):
    t = pl.program_id(0)
    idx = idx_ref[...]
    lane = lax.broadcasted_iota(I32, (TM, LANES), 1)
    lane1 = lax.broadcasted_iota(I32, (1, LANES), 1)
    hits = [lane == idx[:, k:k + 1] for k in range(TOPK)]
    chosen = jnp.zeros((TM, LANES), F32)
    for hk in hits:
        chosen = chosen + jnp.where(hk, 1.0, 0.0)
    colsum = jnp.sum(chosen, axis=0, keepdims=True)

    @pl.when(t == 0)
    def _():
        cnt = cnt_ref[...]
        padded = (((cnt.astype(I32) + (BLK - 1)) // BLK) * BLK).astype(F32)
        end = _lane_prefix(padded, lane1)
        base_scr[...] = end - padded
        row = lax.broadcasted_iota(I32, (8, LANES), 0)
        meta = jnp.where(row == 0, cnt, jnp.where(row == 1, end - padded, jnp.where(row == 2, end, 0.0)))
        meta_ref[...] = meta.astype(I32)

    ri = lax.broadcasted_iota(I32, (TM, TM), 0)
    ci = lax.broadcasted_iota(I32, (TM, TM), 1)
    before = jnp.where(ci < ri, 1.0, 0.0).astype(BF16)
    loff = _lane_prefix(colsum, lane1) - colsum
    pos = jnp.dot(before, chosen.astype(BF16), preferred_element_type=F32) + loff
    lp = jnp.zeros((TM, LANES), F32)
    for k, hk in enumerate(hits):
        lp = jnp.where(lane == k, jnp.sum(jnp.where(hk, pos, 0.0), axis=-1, keepdims=True), lp)
    lp_ref[...] = lp.astype(I32)
    lpt_ref[...] = lp.T[:8].astype(I32)
    row = lax.broadcasted_iota(I32, (8, LANES), 0)
    runs = jnp.where(row == 0, colsum, jnp.where(row == 1, loff, jnp.where(row == 2, base_scr[...], 0.0)))
    runs_ref[0] = runs.astype(I32)
    base_scr[...] += colsum


def _route(cnt, idx):
    n_tiles = idx.shape[0] // TM
    return pl.pallas_call(
        _route_kernel, grid=(n_tiles,),
        in_specs=[pl.BlockSpec((1, LANES), lambda t: (0, 0)), pl.BlockSpec((TM, LANES), lambda t: (t, 0))],
        out_specs=[pl.BlockSpec((TM, LANES), lambda t: (t, 0)),
                   pl.BlockSpec((8, TM), lambda t: (t, 0)),
                   pl.BlockSpec((1, 8, LANES), lambda t: (t, 0, 0)),
                   pl.BlockSpec((8, LANES), lambda t: (0, 0))],
        out_shape=[jax.ShapeDtypeStruct((n_tiles * TM, LANES), I32),
                   jax.ShapeDtypeStruct((n_tiles * 8, TM), I32),
                   jax.ShapeDtypeStruct((n_tiles, 8, LANES), I32),
                   jax.ShapeDtypeStruct((8, LANES), I32)],
        scratch_shapes=[pltpu.VMEM((1, LANES), F32)],
        compiler_params=_cparams(1), name="moe_route",
    )(cnt, idx)


def _tile_rows(ref, r, n):
    return ref.at[:, pl.ds(r, n), :]


def _run_copy(src, dst, r_src, r_dst, n, sem, wait):
    for b in reversed(range(TM.bit_length())):
        size = 1 << b
        off = (n >> (b + 1)) << (b + 1)

        def piece(off=off, size=size):
            cp = pltpu.make_async_copy(_tile_rows(src, r_src + off, size),
                                       _tile_rows(dst, r_dst + off, size), sem)
            if wait:
                cp.wait()
            else:
                cp.start()

        if isinstance(n, int):
            if n & size:
                piece()
        else:
            pl.when((n & size) != 0)(piece)


def _load_rows(ref3):
    parts = []
    for s in range(SLABS):
        word = ref3[s]
        parts.append(lax.bitcast_convert_type(lax.shift_left(word, 16), F32))
        parts.append(lax.bitcast_convert_type(word & HIGH_HALF, F32))
    return jnp.concatenate(parts, axis=1).astype(BF16)


def _store_rows(ref3, val):
    bits = lax.bitcast_convert_type(val, I32)
    for s in range(SLABS):
        low = lax.shift_right_logical(bits[:, (2 * s) * LANES:(2 * s + 1) * LANES], 16)
        ref3[s] = bits[:, (2 * s + 1) * LANES:(2 * s + 2) * LANES] | low


def _dispatch_kernel(len_ref, pos_ref, row_ref, cnt_ref, first_ref, end_ref, lpt_ref, h2_ref, dst,
                     buf, zbuf, sem, zsem):
    n_rows = dst.shape[1]
    step = pl.program_id(0)
    slot = step % 2
    cur = buf.at[slot]
    lpt = lpt_ref[...]
    p = lax.broadcasted_iota(I32, (TM * TOPK, TM), 0)
    pick = lpt[0:1, :] == p
    for k in range(1, TOPK):
        pick = pick | (lpt[k:k + 1, :] == p)
    perm = jnp.where(pick, 1.0, 0.0).astype(BF16)
    _store_rows(cur, jnp.dot(perm, h2_ref[...], preferred_element_type=F32))

    def per_expert(e, carry):
        _run_copy(cur, dst, pos_ref[e], row_ref[e], len_ref[e], sem.at[slot], False)
        return carry

    lax.fori_loop(0, NE, per_expert, 0)

    def drain(s):
        pltpu.make_async_copy(buf.at[s], _tile_rows(dst, 0, TM * TOPK), sem.at[s]).wait()

    @pl.when(step > 0)
    def _():
        drain(1 - slot)

    @pl.when(step == pl.num_programs(0) - 1)
    def _():
        drain(slot)

    @pl.when(step == 0)
    def _():
        zbuf[...] = jnp.zeros(zbuf.shape, I32)
        n_tail = (n_rows - end_ref[NE - 1]) // TM
        for wait in (False, True):
            def pad(e, carry, wait=wait):
                lo = first_ref[e] + cnt_ref[e]
                _run_copy(zbuf, dst, 0, lo, end_ref[e] - lo, zsem, wait)
                return carry

            lax.fori_loop(0, NE, pad, 0)

            def tail(j, carry, wait=wait):
                _run_copy(zbuf, dst, 0, end_ref[NE - 1] + j * TM, TM, zsem, wait)
                return carry

            lax.fori_loop(0, n_tail, tail, 0)


def _dispatch(h2, lpt, run_len, run_pos, run_row, cnt, first, end, n_rows):
    assert BLK % TM == 0 and BLK < 2 * TM + 1
    n_tiles = h2.shape[0] // TM
    smem = pl.BlockSpec(memory_space=pltpu.SMEM)
    per_tile = pl.BlockSpec((LANES,), lambda t: (t,), memory_space=pltpu.SMEM)
    return pl.pallas_call(
        _dispatch_kernel, grid=(n_tiles,),
        in_specs=[per_tile, per_tile, per_tile, smem, smem, smem,
                  pl.BlockSpec((8, TM), lambda t: (t, 0)), _tile_spec(D, TM)],
        out_specs=pl.BlockSpec(memory_space=pl.ANY),
        out_shape=jax.ShapeDtypeStruct((SLABS, n_rows, LANES), I32),
        scratch_shapes=[pltpu.VMEM((2, SLABS, TM * TOPK, LANES), I32),
                        pltpu.VMEM((SLABS, BLK, LANES), I32),
                        pltpu.SemaphoreType.DMA((2,)), pltpu.SemaphoreType.DMA],
        compiler_params=_cparams(1), name="moe_dispatch",
    )(run_len, run_pos, run_row, cnt, first, end, lpt, h2)


def _expert_kernel(be_ref, nu_ref, xb_ref, wu_ref, bu_ref, wd_ref, bd_ref, yb_ref, wub, wdb):
    i = pl.program_id(0)
    used = i < nu_ref[0]

    @pl.when(used)
    def _():
        fresh = (i == 0) | (be_ref[i] != be_ref[jnp.maximum(i - 1, 0)])

        @pl.when(fresh)
        def _():
            wub[...] = wu_ref[0, 0].astype(BF16)
            wdb[...] = wd_ref[0, 0].astype(BF16)

        z = jnp.dot(_load_rows(xb_ref), wub[...], preferred_element_type=F32) + bu_ref[0, 0]
        glu = jnp.minimum(z[:, :FE], LIMIT)
        lin = jnp.clip(z[:, FE:], -LIMIT, LIMIT)
        act = glu * jax.nn.sigmoid(ALPHA * glu) * (lin + 1.0)
        y = jnp.dot(act.astype(BF16), wdb[...], preferred_element_type=F32) + bd_ref[0, 0]
        _store_rows(yb_ref, y.astype(BF16).astype(F32))

    @pl.when(jnp.logical_not(used))
    def _():
        yb_ref[...] = jnp.zeros(yb_ref.shape, I32)


def _experts(layer, xb, blk_e, n_used, wu, bu, wd, bd):
    grid_spec = pltpu.PrefetchScalarGridSpec(
        num_scalar_prefetch=2, grid=(N_BLOCKS,),
        in_specs=[pl.BlockSpec((SLABS, BLK, LANES),
                               lambda i, be, nu: (0, jnp.minimum(i, nu[0] - 1), 0)),
                  pl.BlockSpec((1, 1, D, 2 * FE), lambda i, be, nu: (layer, be[i], 0, 0)),
                  pl.BlockSpec((1, 1, 1, 2 * FE), lambda i, be, nu: (layer, be[i], 0, 0)),
                  pl.BlockSpec((1, 1, FE, D), lambda i, be, nu: (layer, be[i], 0, 0)),
                  pl.BlockSpec((1, 1, 1, D), lambda i, be, nu: (layer, be[i], 0, 0))],
        out_specs=pl.BlockSpec((SLABS, BLK, LANES), lambda i, be, nu: (0, i, 0)),
        scratch_shapes=[pltpu.VMEM((D, 2 * FE), BF16), pltpu.VMEM((FE, D), BF16)])
    return pl.pallas_call(
        _expert_kernel, grid_spec=grid_spec,
        out_shape=jax.ShapeDtypeStruct((SLABS, N_ROWS, LANES), I32),
        compiler_params=_cparams(1), name="experts",
    )(blk_e, n_used, xb, wu, bu, wd, bd)


def _combine_kernel(len_ref, pos_ref, row_ref, len_nx, pos_nx, row_nx, yb, lp_ref, x_ref, gate_ref,
                    g2p, g2s, fg_ref, o_ref, ybuf, sem, *, final):
    step = pl.program_id(0)
    slot = step % 2
    is_p = step < T_P // TM

    def fetch(len_r, pos_r, row_r, s):
        def per_expert(e, carry):
            _run_copy(yb, ybuf.at[s], row_r[e], pos_r[e], len_r[e], sem.at[s], False)
            return carry

        lax.fori_loop(0, NE, per_expert, 0)

    @pl.when(step == 0)
    def _():
        fetch(len_ref, pos_ref, row_ref, slot)

    @pl.when(step + 1 < pl.num_programs(0))
    def _():
        fetch(len_nx, pos_nx, row_nx, 1 - slot)

    cur = ybuf.at[slot]
    pltpu.make_async_copy(_tile_rows(yb, 0, TM * TOPK), cur, sem.at[slot]).wait()

    rows = x_ref.shape[0]
    lp = lp_ref[...]
    gates = gate_ref[...]
    p = lax.broadcasted_iota(I32, (rows, rows * TOPK), 1)
    weights = jnp.zeros((rows, rows * TOPK), F32)
    for k in range(TOPK):
        weights = jnp.where(lp[:, k:k + 1] == p, gates[:, k:k + 1], weights)
    acc = jnp.dot(weights.astype(BF16), _load_rows(cur), preferred_element_type=F32)
    xn = x_ref[...] + _pick(is_p, g2p, g2s) * acc
    o_ref[...] = _rms(xn, fg_ref[...]) if final else xn


def _combine(x, yb, lp, run_len, run_pos, run_row, gates, mod_p, mod_s, final_g, final):
    n_tiles = x.shape[0] // TM
    per_tile = pl.BlockSpec((LANES,), lambda t: (t,), memory_space=pltpu.SMEM)
    next_tile = pl.BlockSpec((LANES,), lambda t: (jnp.minimum(t + 1, n_tiles - 1),),
                             memory_space=pltpu.SMEM)
    in_specs = ([per_tile, per_tile, per_tile, next_tile, next_tile, next_tile,
                 pl.BlockSpec(memory_space=pl.ANY),
                 _tile_spec(LANES, TM), _tile_spec(D, TM), _tile_spec(LANES, TM)]
                + _mod_specs(5, TM) + [_full_spec((1, D))])
    return pl.pallas_call(
        functools.partial(_combine_kernel, final=final), grid=(n_tiles,), in_specs=in_specs,
        out_specs=_tile_spec(D, TM), out_shape=jax.ShapeDtypeStruct(x.shape, F32),
        scratch_shapes=[pltpu.VMEM((2, SLABS, TM * TOPK, LANES), I32), pltpu.SemaphoreType.DMA((2,))],
        compiler_params=_cparams(1), name="moe_combine",
    )(run_len, run_pos, run_row, run_len, run_pos, run_row, yb, lp, x, gates, mod_p, mod_s, final_g)


def _moe(layer, x, h2, idx, gates, cnt_all, mod_p, mod_s, wu, bu, wd, bd, final_g, final):
    lp, lpt, runs, meta = _route(cnt_all, idx)
    cnt, first, end = meta[0, :NE], meta[1, :NE], meta[2, :NE]
    run_len, run_pos, run_row = (runs[:, r, :].reshape(-1) for r in range(3))
    blk_first = jnp.arange(N_BLOCKS, dtype=I32) * BLK
    blk_e = jnp.minimum(jnp.sum((end[None, :] <= blk_first[:, None]).astype(I32), axis=1), NE - 1)
    n_used = end[NE - 1:] // BLK
    xb = _dispatch(h2, lpt, run_len, run_pos, run_row, cnt, first, end, N_ROWS)
    yb = _experts(layer, xb, blk_e, n_used, wu, bu, wd, bd)
    return _combine(x, yb, lp, run_len, run_pos, run_row, gates, mod_p, mod_s, final_g, final)


def _rope_tables():
    half = DK // 2
    inv = 1.0 / (ROPE_BASE ** jnp.linspace(0.0, 1.0, half, dtype=F32))

    def tab(pos):
        ang = pos.astype(F32)[:, None] * inv[None, :]
        cos, sin = jnp.cos(ang), jnp.sin(ang)
        return jnp.concatenate([cos, cos], -1), jnp.concatenate([-sin, sin], -1)

    cp, sp = tab(jnp.arange(L_P, dtype=I32))
    cs, ss = tab(PAST + jnp.arange(L_S, dtype=I32))
    cos = jnp.concatenate([jnp.tile(cp, (NB_P, 1)), jnp.tile(cs, (NB_S, 1))], 0)
    sin = jnp.concatenate([jnp.tile(sp, (NB_P, 1)), jnp.tile(ss, (NB_S, 1))], 0)
    return cos, sin


def _decay_tables(cl):
    lg = jnp.log(1.0 - 2.0 ** (-5.0 - jnp.arange(H, dtype=F32)))
    r = jnp.arange(CHUNK)
    idx = (r % cl).astype(F32)
    diff = idx[:, None] - idx[None, :]
    same = (r[:, None] // cl) == (r[None, :] // cl)
    decay = jnp.where((same & (diff >= 0))[None],
                      jnp.exp(lg[:, None, None] * jnp.maximum(diff, 0.0)[None]), 0.0)
    qdec = jnp.exp(lg[None, :] * (idx[:, None] + 1.0))
    kdec = jnp.exp(lg[None, :] * (cl - 1.0 - idx[:, None]))
    wide = lambda a: jnp.repeat(a, DK, axis=1)
    return {"decay": decay, "qdec": wide(qdec), "kdec": wide(kdec), "gam": jnp.exp(lg * cl)}


def kernel(x_prompt, x_sample, c_prompt, c_sample, state_ret, w_mod, b_mod, norm1_g, norm2_g,
           sgu_w_in, sgu_ln_g, sgu_ln_b, sgu_w_s, sgu_b_s, sgu_w_out, ret_w_in, ret_norm_g, ret_w_out,
           moe_w_router, moe_b_router, moe_w_up, moe_b_up, moe_w_down, moe_b_down, final_g):
    x = jnp.concatenate([x_prompt.reshape(T_P, D), x_sample.reshape(T_S, D)], 0)
    mod = _modulation(jnp.concatenate([c_prompt, c_sample], 0), w_mod, b_mod)
    cos_tab, sin_tab = _rope_tables()
    tabs_p = _decay_tables(CHUNK)
    tabs_s = _decay_tables(L_S)
    wr_pad = jnp.pad(moe_w_router, ((0, 0), (0, 0), (0, LANES - NE))).astype(BF16)
    br_pad = jnp.pad(moe_b_router, ((0, 0), (0, LANES - NE)), constant_values=-1e30)
    fg = final_g.reshape(1, D)
    b_up = moe_b_up.reshape(DEPTH, NE, 1, 2 * FE)
    b_down = moe_b_down.reshape(DEPTH, NE, 1, D)

    ret_p, v_rows = [], []
    s_all = lax.empty(state_ret.shape, F32)
    for i in range(DEPTH):
        j = i // 2
        mod_p = mod[i, :NB_P].reshape(NB_P, 1, 6 * D)
        mod_s = jnp.repeat(mod[i, NB_P:], L_S, axis=0)
        n1g = norm1_g[i].reshape(1, D)
        n2g = norm2_g[i].reshape(1, D)
        wr = wr_pad[i]
        br = br_pad[i].reshape(1, LANES)
        if i % 2 == 0:
            mixw = jnp.stack([sgu_w_s[j], jnp.tile(sgu_w_s[j][:, :L_S, :L_S], (1, SB, SB))])
            bias_p = jnp.repeat(sgu_b_s[j].T, SGU_GD, axis=1)
            bias_s = jnp.tile(bias_p[:L_S], (SB, 1))
            x, h2, idx, gates, cnt, v = _sgu_layer(
                x, mod_p, mod_s, n1g, n2g, sgu_w_in[j].astype(BF16), sgu_ln_g[j].reshape(1, SGU_W),
                sgu_ln_b[j].reshape(1, SGU_W), mixw, jnp.stack([bias_p, bias_s]),
                sgu_w_out[j].astype(BF16), wr, br)
            v_rows.append(v.reshape(NB_S, L_S, SGU_W))
        else:
            q, k, v, sg = _ret_proj(x, mod_p, mod_s, n1g, ret_w_in[j].astype(BF16), cos_tab, sin_tab)
            ng = ret_norm_g[j].reshape(1, VD)
            y, s_p = _ret_prompt(q, k, v, sg, tabs_p, ng)
            y, s_all = _ret_sample(j, q, k, v, sg, tabs_s, ng, state_ret, y, s_all)
            ret_p.append(s_p)
            x, h2, idx, gates, cnt = _ret_out(y, ret_w_out[j].astype(BF16), x, mod_p, mod_s, n2g, wr, br)
        x = _moe(i, x, h2, idx, gates, cnt, mod_p, mod_s, moe_w_up, b_up, moe_w_down, b_down,
                 fg, final=(i == DEPTH - 1))
    y_prompt = x[:T_P].reshape(NB_P, L_P, D)
    y_sample = x[T_P:].reshape(NB_S, L_S, D)
    return (y_prompt, y_sample, jnp.stack(ret_p), s_all, jnp.stack(v_rows))
```

```python
import functools

import jax
import jax.numpy as jnp
from jax import lax
from jax.experimental import pallas as pl
from jax.experimental.pallas import tpu as pltpu

F32 = jnp.float32
BF16 = jnp.bfloat16
I32 = jnp.int32

D = 1024
NB_P, L_P = 8, 2048
NB_S, L_S = 128, 8
PAST = 16384
DEPTH = 4
T_P = NB_P * L_P
T_S = NB_S * L_S
T = T_P + T_S
SGU_W = 2 * D
SGU_G = 8
SGU_GD = SGU_W // SGU_G
CHUNK = 128
H = 8
DK = D // H
DV = 2 * DK
QD = H * DK
VD = H * DV
RET_IN = 2 * QD + 2 * VD
ROPE_BASE = 10000.0
NE = 32
TOPK = 4
FE = D
ALPHA = 1.702
LIMIT = 7.0
EPS = 1e-6

LANES = 128
SLABS = D // (2 * LANES)
HIGH_HALF = -65536
TD = 512
NPT = T_P // TD
NT = T // TD
TM = 256
BLK = 512
TK = T * TOPK
N_BLOCKS = -(-(TK + NE * (BLK - 1)) // BLK)
N_ROWS = N_BLOCKS * BLK
SB = 16
VMEM_LIMIT = 56 * 1024 * 1024


def _cparams(n_axes):
    return pltpu.CompilerParams(dimension_semantics=("arbitrary",) * n_axes,
                                vmem_limit_bytes=VMEM_LIMIT)


def _rms(x, g):
    return (x * lax.rsqrt(jnp.mean(x * x, axis=-1, keepdims=True) + EPS)) * g


def _pick(is_p, p_ref, s_ref):
    return jnp.where(is_p, p_ref[0], s_ref[...])


def _mod_specs(j, tm=TD):
    per_batch, n_prompt = L_P // tm, T_P // tm
    return [
        pl.BlockSpec((1, 1, D), lambda t: (jnp.minimum(t // per_batch, NB_P - 1), 0, j)),
        pl.BlockSpec((tm, D), lambda t: (jnp.maximum(t - n_prompt, 0), j), pipeline_mode=pl.Buffered(1)),
    ]


def _tile_spec(width, tm=TD):
    return pl.BlockSpec((tm, width), lambda t: (t, 0))


def _full_spec(shape):
    return pl.BlockSpec(shape, lambda *_: (0,) * len(shape), pipeline_mode=pl.Buffered(1))


def _mod_kernel(c_ref, w_ref, b_ref, o_ref):
    c = c_ref[...]
    cs = (c * jax.nn.sigmoid(c)).astype(BF16)
    o_ref[0] = jnp.dot(cs, w_ref[0].astype(BF16), preferred_element_type=F32) + b_ref[0]


def _modulation(c_all, w_mod, b_mod):
    tn = 1536
    nb = c_all.shape[0]
    return pl.pallas_call(
        _mod_kernel,
        grid=(DEPTH, 6 * D // tn),
        in_specs=[
            pl.BlockSpec((nb, D), lambda l, n: (0, 0)),
            pl.BlockSpec((1, D, tn), lambda l, n: (l, 0, n)),
            pl.BlockSpec((1, 1, tn), lambda l, n: (l, 0, n)),
        ],
        out_specs=pl.BlockSpec((1, nb, tn), lambda l, n: (l, 0, n)),
        out_shape=jax.ShapeDtypeStruct((DEPTH, nb, 6 * D), F32),
        compiler_params=_cparams(2),
        name="modulation",
    )(c_all, w_mod, b_mod.reshape(DEPTH, 1, 6 * D))


def _tail(is_p, x, y, g1p, g1s, sh2p, sh2s, sc2p, sc2s, n2g, wr, br, xo_ref, h2_ref, idx_ref, gate_ref,
          cnt_ref):
    xn = x + _pick(is_p, g1p, g1s) * y
    xo_ref[...] = xn
    h2 = (_rms(xn, n2g[...]) * (1.0 + _pick(is_p, sc2p, sc2s)) + _pick(is_p, sh2p, sh2s)).astype(BF16)
    h2_ref[...] = h2
    logit = jnp.dot(h2, wr[...], preferred_element_type=F32) + br[...]
    lane = lax.broadcasted_iota(I32, logit.shape, 1)
    vals, ids = [], []
    for _ in range(TOPK):
        m = jnp.max(logit, axis=-1, keepdims=True)
        sel = jnp.min(jnp.where(logit == m, lane, LANES), axis=-1, keepdims=True)
        vals.append(m)
        ids.append(sel)
        logit = jnp.where(lane == sel, -jnp.inf, logit)
    es = [jnp.exp(v - vals[0]) for v in vals]
    tot = (es[0] + es[1]) + (es[2] + es[3])
    idx_out = jnp.zeros(logit.shape, I32)
    gate_out = jnp.zeros(logit.shape, F32)
    chosen = jnp.zeros(logit.shape, F32)
    for k in range(TOPK):
        idx_out = jnp.where(lane == k, ids[k], idx_out)
        gate_out = jnp.where(lane == k, es[k] / tot, gate_out)
        chosen = chosen + jnp.where(lane == ids[k], 1.0, 0.0)
    idx_ref[...] = idx_out
    gate_ref[...] = gate_out

    @pl.when(pl.program_id(0) == 0)
    def _():
        cnt_ref[...] = jnp.zeros(cnt_ref.shape, F32)

    cnt_ref[...] += jnp.sum(chosen, axis=0, keepdims=True)


def _tail_in_specs():
    return (_mod_specs(2) + _mod_specs(3) + _mod_specs(4)
            + [_full_spec((1, D)), _full_spec((D, LANES)), _full_spec((1, LANES))])


def _tail_out_specs():
    return [_tile_spec(D), _tile_spec(D), _tile_spec(LANES), _tile_spec(LANES),
            pl.BlockSpec((1, LANES), lambda t: (0, 0))]


def _tail_out_shapes():
    return [jax.ShapeDtypeStruct((T, D), F32), jax.ShapeDtypeStruct((T, D), BF16),
            jax.ShapeDtypeStruct((T, LANES), I32), jax.ShapeDtypeStruct((T, LANES), F32),
            jax.ShapeDtypeStruct((1, LANES), F32)]


def _sgu_kernel(x_ref, sh1p, sh1s, sc1p, sc1s, n1g, win, lng, lnb, mixw, mixb, wout,
                g1p, g1s, sh2p, sh2s, sc2p, sc2s, n2g, wr, br,
                xo_ref, h2_ref, idx_ref, gate_ref, cnt_ref, v_ref, y_scr):
    t = pl.program_id(0)
    is_p = t < NPT
    x = x_ref[...]
    h = _rms(x, n1g[...]) * (1.0 + _pick(is_p, sc1p, sc1s)) + _pick(is_p, sh1p, sh1s)
    z = jnp.dot(h.astype(BF16), win[...], preferred_element_type=F32)
    z = 0.5 * z * (1.0 + lax.erf(z * (0.5 ** 0.5)))
    u = z[:, :SGU_W]
    v = z[:, SGU_W:]
    vc = v - jnp.mean(v, axis=-1, keepdims=True)
    vn = vc * lax.rsqrt(jnp.mean(vc * vc, axis=-1, keepdims=True) + EPS) * lng[...] + lnb[...]

    @pl.when(t >= NPT)
    def _():
        v_ref[...] = vn

    vb = vn.astype(BF16)
    ri = lax.broadcasted_iota(I32, (CHUNK, CHUNK), 0)
    ci = lax.broadcasted_iota(I32, (CHUNK, CHUNK), 1)
    causal = ci <= ri
    shift = jnp.broadcast_to(jnp.where(is_p, 7, 3), ri.shape)
    keep = causal & (lax.shift_right_logical(ri, shift) == lax.shift_right_logical(ci, shift))
    for g in range(SGU_G):
        wg = jnp.where(keep, mixw[0, g], 0.0).astype(BF16)
        for c in range(TD // CHUNK):
            rows = slice(c * CHUNK, (c + 1) * CHUNK)
            cols = slice(g * SGU_GD, (g + 1) * SGU_GD)
            mixed = jnp.dot(wg, vb[rows, cols], preferred_element_type=F32) + mixb[0, :, cols]
            y_scr[rows, cols] = (u[rows, cols] * mixed).astype(BF16)
    y = jnp.dot(y_scr[...], wout[...], preferred_element_type=F32)
    _tail(is_p, x, y, g1p, g1s, sh2p, sh2s, sc2p, sc2s, n2g, wr, br, xo_ref, h2_ref, idx_ref, gate_ref,
          cnt_ref)


def _sgu_layer(x, mod_p, mod_s, n1g, n2g, win, lng, lnb, mixw, mixb, wout, wr, br):
    sel = lambda t: (jnp.where(t < NPT, 0, 1), 0, 0, 0)
    in_specs = ([_tile_spec(D)] + _mod_specs(0) + _mod_specs(1)
                + [_full_spec((1, D)), _full_spec((D, 2 * SGU_W)), _full_spec((1, SGU_W)),
                   _full_spec((1, SGU_W)),
                   pl.BlockSpec((1, SGU_G, CHUNK, CHUNK), sel),
                   pl.BlockSpec((1, CHUNK, SGU_W), lambda t: (jnp.where(t < NPT, 0, 1), 0, 0)),
                   _full_spec((SGU_W, D))]
                + _tail_in_specs())
    out_specs = _tail_out_specs() + [pl.BlockSpec((TD, SGU_W), lambda t: (jnp.maximum(t - NPT, 0), 0))]
    out_shape = _tail_out_shapes() + [jax.ShapeDtypeStruct((T_S, SGU_W), F32)]
    return pl.pallas_call(
        _sgu_kernel, grid=(NT,), in_specs=in_specs, out_specs=out_specs, out_shape=out_shape,
        scratch_shapes=[pltpu.VMEM((TD, SGU_W), BF16)],
        compiler_params=_cparams(1), name="sgu_layer",
    )(x, mod_p, mod_s, mod_p, mod_s, n1g, win, lng, lnb, mixw, mixb, wout,
      mod_p, mod_s, mod_p, mod_s, mod_p, mod_s, n2g, wr, br)


def _ret_proj_kernel(x_ref, sh1p, sh1s, sc1p, sc1s, n1g, win, cos_ref, sin_ref,
                     q_ref, k_ref, v_ref, sg_ref):
    t = pl.program_id(0)
    is_p = t < NPT
    x = x_ref[...]
    h = _rms(x, n1g[...]) * (1.0 + _pick(is_p, sc1p, sc1s)) + _pick(is_p, sh1p, sh1s)
    p = jnp.dot(h.astype(BF16), win[...], preferred_element_type=F32)
    cos = cos_ref[...]
    sin = sin_ref[...]
    for hd in range(H):
        cq = slice(hd * DK, (hd + 1) * DK)
        ck = slice(QD + hd * DK, QD + (hd + 1) * DK)
        qh = p[:, cq]
        kh = p[:, ck]
        q_ref[:, cq] = (qh * cos + pltpu.roll(qh, DK // 2, 1) * sin).astype(BF16)
        k_ref[:, cq] = ((kh * cos + pltpu.roll(kh, DK // 2, 1) * sin) * (DK ** -0.5)).astype(BF16)
    v_ref[...] = p[:, 2 * QD:2 * QD + VD].astype(BF16)
    g = p[:, 2 * QD + VD:]
    sg_ref[...] = (g * jax.nn.sigmoid(g)).astype(BF16)


def _ret_proj(x, mod_p, mod_s, n1g, win, cos_tab, sin_tab):
    in_specs = ([_tile_spec(D)] + _mod_specs(0) + _mod_specs(1)
                + [_full_spec((1, D)), _full_spec((D, RET_IN)), _tile_spec(DK), _tile_spec(DK)])
    out_specs = [_tile_spec(QD), _tile_spec(QD), _tile_spec(VD), _tile_spec(VD)]
    out_shape = [jax.ShapeDtypeStruct((T, QD), BF16), jax.ShapeDtypeStruct((T, QD), BF16),
                 jax.ShapeDtypeStruct((T, VD), BF16), jax.ShapeDtypeStruct((T, VD), BF16)]
    return pl.pallas_call(
        _ret_proj_kernel, grid=(NT,), in_specs=in_specs, out_specs=out_specs, out_shape=out_shape,
        compiler_params=_cparams(1), name="ret_proj",
    )(x, mod_p, mod_s, mod_p, mod_s, n1g, win, cos_tab, sin_tab)


def _ret_intra(q, k, v, decay, qdec, kdec):
    s = lax.dot_general(q, k, (((1,), (1,)), ((), ())), preferred_element_type=F32) * decay
    o = jnp.dot(s.astype(BF16), v, preferred_element_type=F32)
    return o, q.astype(F32) * qdec, k.astype(F32) * kdec


def _ret_finish(o, sg, ng):
    on = o * lax.rsqrt(jnp.mean(o * o, axis=-1, keepdims=True) + EPS)
    return (sg.astype(F32) * (on * ng)).astype(BF16)


def _ret_prompt_kernel(gam_ref, q_ref, k_ref, v_ref, sg_ref, dec_ref, qdec_ref, kdec_ref, ng_ref,
                       y_ref, s_ref):
    c = pl.program_id(1)

    @pl.when(c == 0)
    def _():
        s_ref[...] = jnp.zeros(s_ref.shape, F32)

    cks = [slice(hd * DK, (hd + 1) * DK) for hd in range(H)]
    cvs = [slice(hd * DV, (hd + 1) * DV) for hd in range(H)]
    vs = [v_ref[:, cv] for cv in cvs]
    intra = [_ret_intra(q_ref[:, ck], k_ref[:, ck], v, dec_ref[hd], qdec_ref[:, ck], kdec_ref[:, ck])
             for hd, (ck, v) in enumerate(zip(cks, vs))]
    olds = [s_ref[0, hd] for hd in range(H)]
    outs = [o + jnp.dot(qd.astype(BF16), s_old.astype(BF16), preferred_element_type=F32)
            for (o, qd, _), s_old in zip(intra, olds)]
    for hd in range(H):
        s_ref[0, hd] = gam_ref[hd] * olds[hd] + lax.dot_general(
            intra[hd][2].astype(BF16), vs[hd], (((0,), (0,)), ((), ())), preferred_element_type=F32)
    for hd, cv in enumerate(cvs):
        y_ref[:, cv] = _ret_finish(outs[hd], sg_ref[:, cv], ng_ref[:, cv])


def _ret_prompt(q, k, v, sg, tabs, ng):
    nc = L_P // CHUNK
    row = lambda b, c: (b * nc + c, 0)
    smem = pl.BlockSpec(memory_space=pltpu.SMEM)
    in_specs = [smem,
                pl.BlockSpec((CHUNK, QD), row), pl.BlockSpec((CHUNK, QD), row),
                pl.BlockSpec((CHUNK, VD), row), pl.BlockSpec((CHUNK, VD), row),
                _full_spec((H, CHUNK, CHUNK)), _full_spec((CHUNK, QD)), _full_spec((CHUNK, QD)),
                _full_spec((1, VD))]
    out_specs = [pl.BlockSpec((CHUNK, VD), row),
                 pl.BlockSpec((1, H, DK, DV), lambda b, c: (b, 0, 0, 0))]
    out_shape = [jax.ShapeDtypeStruct((T, VD), BF16), jax.ShapeDtypeStruct((NB_P, H, DK, DV), F32)]
    return pl.pallas_call(
        _ret_prompt_kernel, grid=(NB_P, nc), in_specs=in_specs, out_specs=out_specs, out_shape=out_shape,
        compiler_params=_cparams(2), name="ret_prompt",
    )(tabs["gam"], q, k, v, sg, tabs["decay"], tabs["qdec"], tabs["kdec"], ng)


def _ret_sample_kernel(gam_ref, q_ref, k_ref, v_ref, sg_ref, dec_ref, qdec_ref, kdec_ref, ng_ref,
                       s_in_ref, y_in_ref, s_all_ref, y_ref, s_out_ref):
    del y_in_ref, s_all_ref
    s_in_ref = s_in_ref.at[0]
    s_out_ref = s_out_ref.at[0]
    hd = pl.program_id(1)
    v = v_ref[...]
    o, qd, kd = _ret_intra(q_ref[...], k_ref[...], v, dec_ref[0], qdec_ref[...], kdec_ref[...])
    gam = gam_ref[hd]
    row_batch = lax.broadcasted_iota(I32, (CHUNK, DK), 0) // L_S
    cross = [jnp.dot(qd[b * L_S:(b + 1) * L_S].astype(BF16), s_in_ref[b, 0].astype(BF16),
                     preferred_element_type=F32) for b in range(SB)]
    for b in range(SB):
        kb = jnp.where(row_batch == b, kd, 0.0).astype(BF16)
        s_out_ref[b, 0] = gam * s_in_ref[b, 0] + lax.dot_general(
            kb, v, (((0,), (0,)), ((), ())), preferred_element_type=F32)
    o = o + jnp.concatenate(cross, axis=0)
    y_ref[...] = _ret_finish(o, sg_ref[...], ng_ref[...])


def _ret_sample(j, q, k, v, sg, tabs, ng, s_in, y_prev, s_all):
    base = T_P // CHUNK
    rq = lambda g, h: (base + g, h)
    st = pl.BlockSpec((1, SB, 1, DK, DV), lambda g, h: (j, g, h, 0, 0))
    smem = pl.BlockSpec(memory_space=pltpu.SMEM)
    in_specs = [smem,
                pl.BlockSpec((CHUNK, DK), rq), pl.BlockSpec((CHUNK, DK), rq),
                pl.BlockSpec((CHUNK, DV), rq), pl.BlockSpec((CHUNK, DV), rq),
                pl.BlockSpec((1, CHUNK, CHUNK), lambda g, h: (h, 0, 0)),
                pl.BlockSpec((CHUNK, DK), lambda g, h: (0, h)),
                pl.BlockSpec((CHUNK, DK), lambda g, h: (0, h)),
                pl.BlockSpec((1, DV), lambda g, h: (0, h)),
                st, pl.BlockSpec(memory_space=pl.ANY), pl.BlockSpec(memory_space=pl.ANY)]
    out_specs = [pl.BlockSpec((CHUNK, DV), rq), st]
    out_shape = [jax.ShapeDtypeStruct((T, VD), BF16), jax.ShapeDtypeStruct(s_all.shape, F32)]
    return pl.pallas_call(
        _ret_sample_kernel, grid=(NB_S // SB, H), in_specs=in_specs, out_specs=out_specs,
        out_shape=out_shape, input_output_aliases={10: 0, 11: 1},
        compiler_params=_cparams(2), name="ret_sample",
    )(tabs["gam"], q, k, v, sg, tabs["decay"], tabs["qdec"], tabs["kdec"], ng, s_in, y_prev, s_all)


def _ret_out_kernel(y_ref, wout, x_ref, g1p, g1s, sh2p, sh2s, sc2p, sc2s, n2g, wr, br,
                    xo_ref, h2_ref, idx_ref, gate_ref, cnt_ref):
    is_p = pl.program_id(0) < NPT
    y = jnp.dot(y_ref[...], wout[...], preferred_element_type=F32)
    _tail(is_p, x_ref[...], y, g1p, g1s, sh2p, sh2s, sc2p, sc2s, n2g, wr, br,
          xo_ref, h2_ref, idx_ref, gate_ref, cnt_ref)


def _ret_out(y, wout, x, mod_p, mod_s, n2g, wr, br):
    in_specs = [_tile_spec(VD), _full_spec((VD, D)), _tile_spec(D)] + _tail_in_specs()
    return pl.pallas_call(
        _ret_out_kernel, grid=(NT,), in_specs=in_specs, out_specs=_tail_out_specs(),
        out_shape=_tail_out_shapes(), compiler_params=_cparams(1), name="ret_out",
    )(y, wout, x, mod_p, mod_s, mod_p, mod_s, mod_p, mod_s, n2g, wr, br)


def _lane_prefix(v, lane1):
    s = 1
    while s < LANES:
        v = v + jnp.where(lane1 >= s, pltpu.roll(v, s, 1), 0.0)
        s *= 2
    return v


def _route_kernel(cnt_ref, idx_ref, lp_ref, lpt_ref, runs_ref, meta_ref, base_scr):
    t = pl.program_id(0)
    idx = idx_ref[...]
    lane = lax.broadcasted_iota(I32, (TM, LANES), 1)
    lane1 = lax.broadcasted_iota(I32, (1, LANES), 1)
    hits = [lane == idx[:, k:k + 1] for k in range(TOPK)]
    chosen = jnp.zeros((TM, LANES), F32)
    for hk in hits:
        chosen = chosen + jnp.where(hk, 1.0, 0.0)
    colsum = jnp.sum(chosen, axis=0, keepdims=True)

    @pl.when(t == 0)
    def _():
        cnt = cnt_ref[...]
        padded = (((cnt.astype(I32) + (BLK - 1)) // BLK) * BLK).astype(F32)
        end = _lane_prefix(padded, lane1)
        base_scr[...] = end - padded
        row = lax.broadcasted_iota(I32, (8, LANES), 0)
        meta = jnp.where(row == 0, cnt, jnp.where(row == 1, end - padded, jnp.where(row == 2, end, 0.0)))
        meta_ref[...] = meta.astype(I32)

    ri = lax.broadcasted_iota(I32, (TM, TM), 0)
    ci = lax.broadcasted_iota(I32, (TM, TM), 1)
    before = jnp.where(ci < ri, 1.0, 0.0).astype(BF16)
    loff = _lane_prefix(colsum, lane1) - colsum
    pos = jnp.dot(before, chosen.astype(BF16), preferred_element_type=F32) + loff
    lp = jnp.zeros((TM, LANES), F32)
    for k, hk in enumerate(hits):
        lp = jnp.where(lane == k, jnp.sum(jnp.where(hk, pos, 0.0), axis=-1, keepdims=True), lp)
    lp_ref[...] = lp.astype(I32)
    lpt_ref[...] = lp.T[:8].astype(I32)
    row = lax.broadcasted_iota(I32, (8, LANES), 0)
    runs = jnp.where(row == 0, colsum, jnp.where(row == 1, loff, jnp.where(row == 2, base_scr[...], 0.0)))
    runs_ref[0] = runs.astype(I32)
    base_scr[...] += colsum


def _route(cnt, idx):
    n_tiles = idx.shape[0] // TM
    return pl.pallas_call(
        _route_kernel, grid=(n_tiles,),
        in_specs=[pl.BlockSpec((1, LANES), lambda t: (0, 0)), pl.BlockSpec((TM, LANES), lambda t: (t, 0))],
        out_specs=[pl.BlockSpec((TM, LANES), lambda t: (t, 0)),
                   pl.BlockSpec((8, TM), lambda t: (t, 0)),
                   pl.BlockSpec((1, 8, LANES), lambda t: (t, 0, 0)),
                   pl.BlockSpec((8, LANES), lambda t: (0, 0))],
        out_shape=[jax.ShapeDtypeStruct((n_tiles * TM, LANES), I32),
                   jax.ShapeDtypeStruct((n_tiles * 8, TM), I32),
                   jax.ShapeDtypeStruct((n_tiles, 8, LANES), I32),
                   jax.ShapeDtypeStruct((8, LANES), I32)],
        scratch_shapes=[pltpu.VMEM((1, LANES), F32)],
        compiler_params=_cparams(1), name="moe_route",
    )(cnt, idx)


def _tile_rows(ref, r, n):
    return ref.at[:, pl.ds(r, n), :]


def _run_copy(src, dst, r_src, r_dst, n, sem, wait):
    for b in reversed(range(TM.bit_length())):
        size = 1 << b
        off = (n >> (b + 1)) << (b + 1)

        def piece(off=off, size=size):
            cp = pltpu.make_async_copy(_tile_rows(src, r_src + off, size),
                                       _tile_rows(dst, r_dst + off, size), sem)
            if wait:
                cp.wait()
            else:
                cp.start()

        if isinstance(n, int):
            if n & size:
                piece()
        else:
            pl.when((n & size) != 0)(piece)


def _load_rows(ref3, rows=None):
    rows = ref3.shape[1] if rows is None else rows
    parts = []
    for s in range(SLABS):
        word = ref3[s, :rows, :]
        parts.append(lax.bitcast_convert_type(lax.shift_left(word, 16), F32))
        parts.append(lax.bitcast_convert_type(word & HIGH_HALF, F32))
    return jnp.concatenate(parts, axis=1).astype(BF16)


def _store_rows(ref3, val):
    rows = val.shape[0]
    bits = lax.bitcast_convert_type(val, I32)
    for s in range(SLABS):
        low = lax.shift_right_logical(bits[:, (2 * s) * LANES:(2 * s + 1) * LANES], 16)
        ref3[s, :rows, :] = bits[:, (2 * s + 1) * LANES:(2 * s + 2) * LANES] | low


def _dispatch_kernel(len_ref, pos_ref, row_ref, cnt_ref, first_ref, end_ref, lpt_ref, h2_ref, dst,
                     buf, zbuf, sem, zsem):
    n_rows = dst.shape[1]
    step = pl.program_id(0)
    slot = step % 2
    cur = buf.at[slot]
    lpt = lpt_ref[...]
    p = lax.broadcasted_iota(I32, (TM * TOPK, TM), 0)
    pick = lpt[0:1, :] == p
    for k in range(1, TOPK):
        pick = pick | (lpt[k:k + 1, :] == p)
    perm = jnp.where(pick, 1.0, 0.0).astype(BF16)
    _store_rows(cur, jnp.dot(perm, h2_ref[...], preferred_element_type=F32))

    def per_expert(e, carry):
        _run_copy(cur, dst, pos_ref[e], row_ref[e], len_ref[e], sem.at[slot], False)
        return carry

    lax.fori_loop(0, NE, per_expert, 0)

    def drain(s):
        pltpu.make_async_copy(buf.at[s], _tile_rows(dst, 0, TM * TOPK), sem.at[s]).wait()

    @pl.when(step > 0)
    def _():
        drain(1 - slot)

    @pl.when(step == pl.num_programs(0) - 1)
    def _():
        drain(slot)

    @pl.when(step == 0)
    def _():
        zbuf[...] = jnp.zeros(zbuf.shape, I32)
        n_tail = (n_rows - end_ref[NE - 1]) // TM
        for wait in (False, True):
            def pad(e, carry, wait=wait):
                lo = first_ref[e] + cnt_ref[e]
                _run_copy(zbuf, dst, 0, lo, end_ref[e] - lo, zsem, wait)
                return carry

            lax.fori_loop(0, NE, pad, 0)

            def tail(j, carry, wait=wait):
                _run_copy(zbuf, dst, 0, end_ref[NE - 1] + j * TM, TM, zsem, wait)
                return carry

            lax.fori_loop(0, n_tail, tail, 0)


def _dispatch(h2, lpt, run_len, run_pos, run_row, cnt, first, end, n_rows):
    assert BLK % TM == 0 and BLK < 2 * TM + 1
    n_tiles = h2.shape[0] // TM
    smem = pl.BlockSpec(memory_space=pltpu.SMEM)
    per_tile = pl.BlockSpec((LANES,), lambda t: (t,), memory_space=pltpu.SMEM)
    return pl.pallas_call(
        _dispatch_kernel, grid=(n_tiles,),
        in_specs=[per_tile, per_tile, per_tile, smem, smem, smem,
                  pl.BlockSpec((8, TM), lambda t: (t, 0)), _tile_spec(D, TM)],
        out_specs=pl.BlockSpec(memory_space=pl.ANY),
        out_shape=jax.ShapeDtypeStruct((SLABS, n_rows, LANES), I32),
        scratch_shapes=[pltpu.VMEM((2, SLABS, TM * TOPK, LANES), I32),
                        pltpu.VMEM((SLABS, BLK, LANES), I32),
                        pltpu.SemaphoreType.DMA((2,)), pltpu.SemaphoreType.DMA],
        compiler_params=_cparams(1), name="moe_dispatch",
    )(run_len, run_pos, run_row, cnt, first, end, lpt, h2)


def _expert_kernel(be_ref, nu_ref, nr_ref, xb_ref, wu_ref, bu_ref, wd_ref, bd_ref, yb_ref, wub, wdb):
    i = pl.program_id(0)
    used = i < nu_ref[0]
    half = nr_ref[i] <= BLK // 2

    @pl.when(used & ((i == 0) | (be_ref[i] != be_ref[jnp.maximum(i - 1, 0)])))
    def _():
        wub[...] = wu_ref[0, 0].astype(BF16)
        wdb[...] = wd_ref[0, 0].astype(BF16)

    def swiglu_rows(rows):
        z = jnp.dot(_load_rows(xb_ref, rows), wub[...], preferred_element_type=F32) + bu_ref[0, 0]
        glu = jnp.minimum(z[:, :FE], LIMIT)
        lin = jnp.clip(z[:, FE:], -LIMIT, LIMIT)
        act = glu * jax.nn.sigmoid(ALPHA * glu) * (lin + 1.0)
        y = jnp.dot(act.astype(BF16), wdb[...], preferred_element_type=F32) + bd_ref[0, 0]
        _store_rows(yb_ref, y.astype(BF16).astype(F32))

    @pl.when(used & jnp.logical_not(half))
    def _():
        swiglu_rows(BLK)

    @pl.when(used & half)
    def _():
        swiglu_rows(BLK // 2)
        yb_ref[:, BLK // 2:, :] = jnp.zeros((SLABS, BLK // 2, LANES), I32)

    @pl.when(jnp.logical_not(used))
    def _():
        yb_ref[...] = jnp.zeros(yb_ref.shape, I32)


def _experts(layer, xb, blk_e, n_used, blk_rows, wu, bu, wd, bd):
    grid_spec = pltpu.PrefetchScalarGridSpec(
        num_scalar_prefetch=3, grid=(N_BLOCKS,),
        in_specs=[pl.BlockSpec((SLABS, BLK, LANES),
                               lambda i, be, nu, nr: (0, jnp.minimum(i, nu[0] - 1), 0)),
                  pl.BlockSpec((1, 1, D, 2 * FE), lambda i, be, nu, nr: (layer, be[i], 0, 0)),
                  pl.BlockSpec((1, 1, 1, 2 * FE), lambda i, be, nu, nr: (layer, be[i], 0, 0)),
                  pl.BlockSpec((1, 1, FE, D), lambda i, be, nu, nr: (layer, be[i], 0, 0)),
                  pl.BlockSpec((1, 1, 1, D), lambda i, be, nu, nr: (layer, be[i], 0, 0))],
        out_specs=pl.BlockSpec((SLABS, BLK, LANES), lambda i, be, nu, nr: (0, i, 0)),
        scratch_shapes=[pltpu.VMEM((D, 2 * FE), BF16), pltpu.VMEM((FE, D), BF16)])
    return pl.pallas_call(
        _expert_kernel, grid_spec=grid_spec,
        out_shape=jax.ShapeDtypeStruct((SLABS, N_ROWS, LANES), I32),
        compiler_params=_cparams(1), name="experts",
    )(blk_e, n_used, blk_rows, xb, wu, bu, wd, bd)


def _combine_kernel(len_ref, pos_ref, row_ref, len_nx, pos_nx, row_nx, yb, lp_ref, x_ref, gate_ref,
                    g2p, g2s, fg_ref, o_ref, ybuf, sem, *, final):
    step = pl.program_id(0)
    slot = step % 2
    is_p = step < T_P // TM

    def fetch(len_r, pos_r, row_r, s):
        def per_expert(e, carry):
            _run_copy(yb, ybuf.at[s], row_r[e], pos_r[e], len_r[e], sem.at[s], False)
            return carry

        lax.fori_loop(0, NE, per_expert, 0)

    @pl.when(step == 0)
    def _():
        fetch(len_ref, pos_ref, row_ref, slot)

    @pl.when(step + 1 < pl.num_programs(0))
    def _():
        fetch(len_nx, pos_nx, row_nx, 1 - slot)

    cur = ybuf.at[slot]
    pltpu.make_async_copy(_tile_rows(yb, 0, TM * TOPK), cur, sem.at[slot]).wait()

    rows = x_ref.shape[0]
    lp = lp_ref[...]
    gates = gate_ref[...]
    p = lax.broadcasted_iota(I32, (rows, rows * TOPK), 1)
    weights = jnp.zeros((rows, rows * TOPK), F32)
    for k in range(TOPK):
        weights = jnp.where(lp[:, k:k + 1] == p, gates[:, k:k + 1], weights)
    acc = jnp.dot(weights.astype(BF16), _load_rows(cur), preferred_element_type=F32)
    xn = x_ref[...] + _pick(is_p, g2p, g2s) * acc
    o_ref[...] = _rms(xn, fg_ref[...]) if final else xn


def _combine(x, yb, lp, run_len, run_pos, run_row, gates, mod_p, mod_s, final_g, final):
    n_tiles = x.shape[0] // TM
    per_tile = pl.BlockSpec((LANES,), lambda t: (t,), memory_space=pltpu.SMEM)
    next_tile = pl.BlockSpec((LANES,), lambda t: (jnp.minimum(t + 1, n_tiles - 1),),
                             memory_space=pltpu.SMEM)
    in_specs = ([per_tile, per_tile, per_tile, next_tile, next_tile, next_tile,
                 pl.BlockSpec(memory_space=pl.ANY),
                 _tile_spec(LANES, TM), _tile_spec(D, TM), _tile_spec(LANES, TM)]
                + _mod_specs(5, TM) + [_full_spec((1, D))])
    return pl.pallas_call(
        functools.partial(_combine_kernel, final=final), grid=(n_tiles,), in_specs=in_specs,
        out_specs=_tile_spec(D, TM), out_shape=jax.ShapeDtypeStruct(x.shape, F32),
        scratch_shapes=[pltpu.VMEM((2, SLABS, TM * TOPK, LANES), I32), pltpu.SemaphoreType.DMA((2,))],
        compiler_params=_cparams(1), name="moe_combine",
    )(run_len, run_pos, run_row, run_len, run_pos, run_row, yb, lp, x, gates, mod_p, mod_s, final_g)


def _moe(layer, x, h2, idx, gates, cnt_all, mod_p, mod_s, wu, bu, wd, bd, final_g, final):
    lp, lpt, runs, meta = _route(cnt_all, idx)
    cnt, first, end = meta[0, :NE], meta[1, :NE], meta[2, :NE]
    run_len, run_pos, run_row = (runs[:, r, :].reshape(-1) for r in range(3))
    blk_first = jnp.arange(N_BLOCKS, dtype=I32) * BLK
    blk_e = jnp.minimum(jnp.sum((end[None, :] <= blk_first[:, None]).astype(I32), axis=1), NE - 1)
    n_used = end[NE - 1:] // BLK
    blk_rows = jnp.clip((first + cnt)[blk_e] - blk_first, 0, BLK)
    xb = _dispatch(h2, lpt, run_len, run_pos, run_row, cnt, first, end, N_ROWS)
    yb = _experts(layer, xb, blk_e, n_used, blk_rows, wu, bu, wd, bd)
    return _combine(x, yb, lp, run_len, run_pos, run_row, gates, mod_p, mod_s, final_g, final)


def _rope_tables():
    half = DK // 2
    inv = 1.0 / (ROPE_BASE ** jnp.linspace(0.0, 1.0, half, dtype=F32))

    def tab(pos):
        ang = pos.astype(F32)[:, None] * inv[None, :]
        cos, sin = jnp.cos(ang), jnp.sin(ang)
        return jnp.concatenate([cos, cos], -1), jnp.concatenate([-sin, sin], -1)

    cp, sp = tab(jnp.arange(L_P, dtype=I32))
    cs, ss = tab(PAST + jnp.arange(L_S, dtype=I32))
    cos = jnp.concatenate([jnp.tile(cp, (NB_P, 1)), jnp.tile(cs, (NB_S, 1))], 0)
    sin = jnp.concatenate([jnp.tile(sp, (NB_P, 1)), jnp.tile(ss, (NB_S, 1))], 0)
    return cos, sin


def _decay_tables(cl):
    lg = jnp.log(1.0 - 2.0 ** (-5.0 - jnp.arange(H, dtype=F32)))
    r = jnp.arange(CHUNK)
    idx = (r % cl).astype(F32)
    diff = idx[:, None] - idx[None, :]
    same = (r[:, None] // cl) == (r[None, :] // cl)
    decay = jnp.where((same & (diff >= 0))[None],
                      jnp.exp(lg[:, None, None] * jnp.maximum(diff, 0.0)[None]), 0.0)
    qdec = jnp.exp(lg[None, :] * (idx[:, None] + 1.0))
    kdec = jnp.exp(lg[None, :] * (cl - 1.0 - idx[:, None]))
    wide = lambda a: jnp.repeat(a, DK, axis=1)
    return {"decay": decay, "qdec": wide(qdec), "kdec": wide(kdec), "gam": jnp.exp(lg * cl)}


def kernel(x_prompt, x_sample, c_prompt, c_sample, state_ret, w_mod, b_mod, norm1_g, norm2_g,
           sgu_w_in, sgu_ln_g, sgu_ln_b, sgu_w_s, sgu_b_s, sgu_w_out, ret_w_in, ret_norm_g, ret_w_out,
           moe_w_router, moe_b_router, moe_w_up, moe_b_up, moe_w_down, moe_b_down, final_g):
    x = jnp.concatenate([x_prompt.reshape(T_P, D), x_sample.reshape(T_S, D)], 0)
    mod = _modulation(jnp.concatenate([c_prompt, c_sample], 0), w_mod, b_mod)
    cos_tab, sin_tab = _rope_tables()
    tabs_p = _decay_tables(CHUNK)
    tabs_s = _decay_tables(L_S)
    wr_pad = jnp.pad(moe_w_router, ((0, 0), (0, 0), (0, LANES - NE))).astype(BF16)
    br_pad = jnp.pad(moe_b_router, ((0, 0), (0, LANES - NE)), constant_values=-1e30)
    fg = final_g.reshape(1, D)
    b_up = moe_b_up.reshape(DEPTH, NE, 1, 2 * FE)
    b_down = moe_b_down.reshape(DEPTH, NE, 1, D)

    ret_p, v_rows = [], []
    s_all = lax.empty(state_ret.shape, F32)
    for i in range(DEPTH):
        j = i // 2
        mod_p = mod[i, :NB_P].reshape(NB_P, 1, 6 * D)
        mod_s = jnp.repeat(mod[i, NB_P:], L_S, axis=0)
        n1g = norm1_g[i].reshape(1, D)
        n2g = norm2_g[i].reshape(1, D)
        wr = wr_pad[i]
        br = br_pad[i].reshape(1, LANES)
        if i % 2 == 0:
            mixw = jnp.stack([sgu_w_s[j], jnp.tile(sgu_w_s[j][:, :L_S, :L_S], (1, SB, SB))])
            bias_p = jnp.repeat(sgu_b_s[j].T, SGU_GD, axis=1)
            bias_s = jnp.tile(bias_p[:L_S], (SB, 1))
            x, h2, idx, gates, cnt, v = _sgu_layer(
                x, mod_p, mod_s, n1g, n2g, sgu_w_in[j].astype(BF16), sgu_ln_g[j].reshape(1, SGU_W),
                sgu_ln_b[j].reshape(1, SGU_W), mixw, jnp.stack([bias_p, bias_s]),
                sgu_w_out[j].astype(BF16), wr, br)
            v_rows.append(v.reshape(NB_S, L_S, SGU_W))
        else:
            q, k, v, sg = _ret_proj(x, mod_p, mod_s, n1g, ret_w_in[j].astype(BF16), cos_tab, sin_tab)
            ng = ret_norm_g[j].reshape(1, VD)
            y, s_p = _ret_prompt(q, k, v, sg, tabs_p, ng)
            y, s_all = _ret_sample(j, q, k, v, sg, tabs_s, ng, state_ret, y, s_all)
            ret_p.append(s_p)
            x, h2, idx, gates, cnt = _ret_out(y, ret_w_out[j].astype(BF16), x, mod_p, mod_s, n2g, wr, br)
        x = _moe(i, x, h2, idx, gates, cnt, mod_p, mod_s, moe_w_up, b_up, moe_w_down, b_down,
                 fg, final=(i == DEPTH - 1))
    y_prompt = x[:T_P].reshape(NB_P, L_P, D)
    y_sample = x[T_P:].reshape(NB_S, L_S, D)
    return (y_prompt, y_sample, jnp.stack(ret_p), s_all, jnp.stack(v_rows))
```

```python
import functools

import jax
import jax.numpy as jnp
from jax import lax
from jax.experimental import pallas as pl
from jax.experimental.pallas import tpu as pltpu

F32 = jnp.float32
BF16 = jnp.bfloat16
I32 = jnp.int32

D = 1024
NB_P, L_P = 8, 2048
NB_S, L_S = 128, 8
PAST = 16384
DEPTH = 4
T_P = NB_P * L_P
T_S = NB_S * L_S
T = T_P + T_S
SGU_W = 2 * D
SGU_G = 8
SGU_GD = SGU_W // SGU_G
CHUNK = 128
H = 8
DK = D // H
DV = 2 * DK
QD = H * DK
VD = H * DV
RET_IN = 2 * QD + 2 * VD
ROPE_BASE = 10000.0
NE = 32
TOPK = 4
FE = D
ALPHA = 1.702
LIMIT = 7.0
EPS = 1e-6

LANES = 128
SLABS = D // (2 * LANES)
HIGH_HALF = -65536
TD = 512
NPT = T_P // TD
NT = T // TD
TM = 256
BLK = 512
TK = T * TOPK
N_BLOCKS = -(-(TK + NE * (BLK - 1)) // BLK)
N_ROWS = N_BLOCKS * BLK
SB = 16
VMEM_LIMIT = 56 * 1024 * 1024


def _cparams(n_axes):
    return pltpu.CompilerParams(dimension_semantics=("arbitrary",) * n_axes,
                                vmem_limit_bytes=VMEM_LIMIT)


def _rms(x, g):
    return (x * lax.rsqrt(jnp.mean(x * x, axis=-1, keepdims=True) + EPS)) * g


def _pick(is_p, p_ref, s_ref):
    return jnp.where(is_p, p_ref[0], s_ref[...])


def _mod_specs(j, tm=TD):
    per_batch, n_prompt = L_P // tm, T_P // tm
    return [
        pl.BlockSpec((1, 1, D), lambda t: (jnp.minimum(t // per_batch, NB_P - 1), 0, j)),
        pl.BlockSpec((tm, D), lambda t: (jnp.maximum(t - n_prompt, 0), j), pipeline_mode=pl.Buffered(1)),
    ]


def _tile_spec(width, tm=TD):
    return pl.BlockSpec((tm, width), lambda t: (t, 0))


def _full_spec(shape):
    return pl.BlockSpec(shape, lambda *_: (0,) * len(shape), pipeline_mode=pl.Buffered(1))


def _mod_kernel(c_ref, w_ref, b_ref, o_ref):
    c = c_ref[...]
    cs = (c * jax.nn.sigmoid(c)).astype(BF16)
    o_ref[0] = jnp.dot(cs, w_ref[0].astype(BF16), preferred_element_type=F32) + b_ref[0]


def _modulation(c_all, w_mod, b_mod):
    tn = 1536
    nb = c_all.shape[0]
    return pl.pallas_call(
        _mod_kernel,
        grid=(DEPTH, 6 * D // tn),
        in_specs=[
            pl.BlockSpec((nb, D), lambda l, n: (0, 0)),
            pl.BlockSpec((1, D, tn), lambda l, n: (l, 0, n)),
            pl.BlockSpec((1, 1, tn), lambda l, n: (l, 0, n)),
        ],
        out_specs=pl.BlockSpec((1, nb, tn), lambda l, n: (l, 0, n)),
        out_shape=jax.ShapeDtypeStruct((DEPTH, nb, 6 * D), F32),
        compiler_params=_cparams(2),
        name="modulation",
    )(c_all, w_mod, b_mod.reshape(DEPTH, 1, 6 * D))


def _tail(is_p, x, y, g1p, g1s, sh2p, sh2s, sc2p, sc2s, n2g, wr, br, xo_ref, h2_ref, idx_ref, gate_ref,
          cnt_ref):
    xn = x + _pick(is_p, g1p, g1s) * y
    xo_ref[...] = xn
    h2 = (_rms(xn, n2g[...]) * (1.0 + _pick(is_p, sc2p, sc2s)) + _pick(is_p, sh2p, sh2s)).astype(BF16)
    h2_ref[...] = h2
    logit = jnp.dot(h2, wr[...], preferred_element_type=F32) + br[...]
    lane = lax.broadcasted_iota(I32, logit.shape, 1)
    vals, ids = [], []
    for _ in range(TOPK):
        m = jnp.max(logit, axis=-1, keepdims=True)
        sel = jnp.min(jnp.where(logit == m, lane, LANES), axis=-1, keepdims=True)
        vals.append(m)
        ids.append(sel)
        logit = jnp.where(lane == sel, -jnp.inf, logit)
    es = [jnp.exp(v - vals[0]) for v in vals]
    tot = (es[0] + es[1]) + (es[2] + es[3])
    idx_out = jnp.zeros(logit.shape, I32)
    gate_out = jnp.zeros(logit.shape, F32)
    chosen = jnp.zeros(logit.shape, F32)
    for k in range(TOPK):
        idx_out = jnp.where(lane == k, ids[k], idx_out)
        gate_out = jnp.where(lane == k, es[k] / tot, gate_out)
        chosen = chosen + jnp.where(lane == ids[k], 1.0, 0.0)
    idx_ref[...] = idx_out
    gate_ref[...] = gate_out

    @pl.when(pl.program_id(0) == 0)
    def _():
        cnt_ref[...] = jnp.zeros(cnt_ref.shape, F32)

    cnt_ref[...] += jnp.sum(chosen, axis=0, keepdims=True)


def _tail_in_specs():
    return (_mod_specs(2) + _mod_specs(3) + _mod_specs(4)
            + [_full_spec((1, D)), _full_spec((D, LANES)), _full_spec((1, LANES))])


def _tail_out_specs():
    return [_tile_spec(D), _tile_spec(D), _tile_spec(LANES), _tile_spec(LANES),
            pl.BlockSpec((1, LANES), lambda t: (0, 0))]


def _tail_out_shapes():
    return [jax.ShapeDtypeStruct((T, D), F32), jax.ShapeDtypeStruct((T, D), BF16),
            jax.ShapeDtypeStruct((T, LANES), I32), jax.ShapeDtypeStruct((T, LANES), F32),
            jax.ShapeDtypeStruct((1, LANES), F32)]


def _sgu_kernel(x_ref, sh1p, sh1s, sc1p, sc1s, n1g, win, lng, lnb, mixw, mixb, wout,
                g1p, g1s, sh2p, sh2s, sc2p, sc2s, n2g, wr, br,
                xo_ref, h2_ref, idx_ref, gate_ref, cnt_ref, v_ref, y_scr):
    t = pl.program_id(0)
    is_p = t < NPT
    x = x_ref[...]
    h = _rms(x, n1g[...]) * (1.0 + _pick(is_p, sc1p, sc1s)) + _pick(is_p, sh1p, sh1s)
    z = jnp.dot(h.astype(BF16), win[...], preferred_element_type=F32)
    z = 0.5 * z * (1.0 + lax.erf(z * (0.5 ** 0.5)))
    u = z[:, :SGU_W]
    v = z[:, SGU_W:]
    vc = v - jnp.mean(v, axis=-1, keepdims=True)
    vn = vc * lax.rsqrt(jnp.mean(vc * vc, axis=-1, keepdims=True) + EPS) * lng[...] + lnb[...]

    @pl.when(t >= NPT)
    def _():
        v_ref[...] = vn

    vb = vn.astype(BF16)
    ri = lax.broadcasted_iota(I32, (CHUNK, CHUNK), 0)
    ci = lax.broadcasted_iota(I32, (CHUNK, CHUNK), 1)
    causal = ci <= ri
    shift = jnp.broadcast_to(jnp.where(is_p, 7, 3), ri.shape)
    keep = causal & (lax.shift_right_logical(ri, shift) == lax.shift_right_logical(ci, shift))
    for g in range(SGU_G):
        wg = jnp.where(keep, mixw[0, g], 0.0).astype(BF16)
        for c in range(TD // CHUNK):
            rows = slice(c * CHUNK, (c + 1) * CHUNK)
            cols = slice(g * SGU_GD, (g + 1) * SGU_GD)
            mixed = jnp.dot(wg, vb[rows, cols], preferred_element_type=F32) + mixb[0, :, cols]
            y_scr[rows, cols] = (u[rows, cols] * mixed).astype(BF16)
    y = jnp.dot(y_scr[...], wout[...], preferred_element_type=F32)
    _tail(is_p, x, y, g1p, g1s, sh2p, sh2s, sc2p, sc2s, n2g, wr, br, xo_ref, h2_ref, idx_ref, gate_ref,
          cnt_ref)


def _sgu_layer(x, mod_p, mod_s, n1g, n2g, win, lng, lnb, mixw, mixb, wout, wr, br):
    sel = lambda t: (jnp.where(t < NPT, 0, 1), 0, 0, 0)
    in_specs = ([_tile_spec(D)] + _mod_specs(0) + _mod_specs(1)
                + [_full_spec((1, D)), _full_spec((D, 2 * SGU_W)), _full_spec((1, SGU_W)),
                   _full_spec((1, SGU_W)),
                   pl.BlockSpec((1, SGU_G, CHUNK, CHUNK), sel),
                   pl.BlockSpec((1, CHUNK, SGU_W), lambda t: (jnp.where(t < NPT, 0, 1), 0, 0)),
                   _full_spec((SGU_W, D))]
                + _tail_in_specs())
    out_specs = _tail_out_specs() + [pl.BlockSpec((TD, SGU_W), lambda t: (jnp.maximum(t - NPT, 0), 0))]
    out_shape = _tail_out_shapes() + [jax.ShapeDtypeStruct((T_S, SGU_W), F32)]
    return pl.pallas_call(
        _sgu_kernel, grid=(NT,), in_specs=in_specs, out_specs=out_specs, out_shape=out_shape,
        scratch_shapes=[pltpu.VMEM((TD, SGU_W), BF16)],
        compiler_params=_cparams(1), name="sgu_layer",
    )(x, mod_p, mod_s, mod_p, mod_s, n1g, win, lng, lnb, mixw, mixb, wout,
      mod_p, mod_s, mod_p, mod_s, mod_p, mod_s, n2g, wr, br)


def _ret_proj_kernel(x_ref, sh1p, sh1s, sc1p, sc1s, n1g, win, cos_ref, sin_ref,
                     q_ref, k_ref, v_ref, sg_ref):
    t = pl.program_id(0)
    is_p = t < NPT
    x = x_ref[...]
    h = _rms(x, n1g[...]) * (1.0 + _pick(is_p, sc1p, sc1s)) + _pick(is_p, sh1p, sh1s)
    p = jnp.dot(h.astype(BF16), win[...], preferred_element_type=F32)
    cos = cos_ref[...]
    sin = sin_ref[...]
    for hd in range(H):
        cq = slice(hd * DK, (hd + 1) * DK)
        ck = slice(QD + hd * DK, QD + (hd + 1) * DK)
        qh = p[:, cq]
        kh = p[:, ck]
        q_ref[:, cq] = (qh * cos + pltpu.roll(qh, DK // 2, 1) * sin).astype(BF16)
        k_ref[:, cq] = ((kh * cos + pltpu.roll(kh, DK // 2, 1) * sin) * (DK ** -0.5)).astype(BF16)
    v_ref[...] = p[:, 2 * QD:2 * QD + VD].astype(BF16)
    g = p[:, 2 * QD + VD:]
    sg_ref[...] = (g * jax.nn.sigmoid(g)).astype(BF16)


def _ret_proj(x, mod_p, mod_s, n1g, win, cos_tab, sin_tab):
    in_specs = ([_tile_spec(D)] + _mod_specs(0) + _mod_specs(1)
                + [_full_spec((1, D)), _full_spec((D, RET_IN)), _tile_spec(DK), _tile_spec(DK)])
    out_specs = [_tile_spec(QD), _tile_spec(QD), _tile_spec(VD), _tile_spec(VD)]
    out_shape = [jax.ShapeDtypeStruct((T, QD), BF16), jax.ShapeDtypeStruct((T, QD), BF16),
                 jax.ShapeDtypeStruct((T, VD), BF16), jax.ShapeDtypeStruct((T, VD), BF16)]
    return pl.pallas_call(
        _ret_proj_kernel, grid=(NT,), in_specs=in_specs, out_specs=out_specs, out_shape=out_shape,
        compiler_params=_cparams(1), name="ret_proj",
    )(x, mod_p, mod_s, mod_p, mod_s, n1g, win, cos_tab, sin_tab)


def _ret_intra(q, k, v, decay, qdec, kdec):
    s = lax.dot_general(q, k, (((1,), (1,)), ((), ())), preferred_element_type=F32) * decay
    o = jnp.dot(s.astype(BF16), v, preferred_element_type=F32)
    return o, q.astype(F32) * qdec, k.astype(F32) * kdec


def _ret_finish(o, sg, ng):
    on = o * lax.rsqrt(jnp.mean(o * o, axis=-1, keepdims=True) + EPS)
    return (sg.astype(F32) * (on * ng)).astype(BF16)


def _ret_prompt_kernel(gam_ref, q_ref, k_ref, v_ref, sg_ref, dec_ref, qdec_ref, kdec_ref, ng_ref,
                       y_ref, s_ref):
    c = pl.program_id(1)

    @pl.when(c == 0)
    def _():
        s_ref[...] = jnp.zeros(s_ref.shape, F32)

    cks = [slice(hd * DK, (hd + 1) * DK) for hd in range(H)]
    cvs = [slice(hd * DV, (hd + 1) * DV) for hd in range(H)]
    vs = [v_ref[:, cv] for cv in cvs]
    intra = [_ret_intra(q_ref[:, ck], k_ref[:, ck], v, dec_ref[hd], qdec_ref[:, ck], kdec_ref[:, ck])
             for hd, (ck, v) in enumerate(zip(cks, vs))]
    olds = [s_ref[0, hd] for hd in range(H)]
    outs = [o + jnp.dot(qd.astype(BF16), s_old.astype(BF16), preferred_element_type=F32)
            for (o, qd, _), s_old in zip(intra, olds)]
    for hd in range(H):
        s_ref[0, hd] = gam_ref[hd] * olds[hd] + lax.dot_general(
            intra[hd][2].astype(BF16), vs[hd], (((0,), (0,)), ((), ())), preferred_element_type=F32)
    for hd, cv in enumerate(cvs):
        y_ref[:, cv] = _ret_finish(outs[hd], sg_ref[:, cv], ng_ref[:, cv])


def _ret_prompt(q, k, v, sg, tabs, ng):
    nc = L_P // CHUNK
    row = lambda b, c: (b * nc + c, 0)
    smem = pl.BlockSpec(memory_space=pltpu.SMEM)
    in_specs = [smem,
                pl.BlockSpec((CHUNK, QD), row), pl.BlockSpec((CHUNK, QD), row),
                pl.BlockSpec((CHUNK, VD), row), pl.BlockSpec((CHUNK, VD), row),
                _full_spec((H, CHUNK, CHUNK)), _full_spec((CHUNK, QD)), _full_spec((CHUNK, QD)),
                _full_spec((1, VD))]
    out_specs = [pl.BlockSpec((CHUNK, VD), row),
                 pl.BlockSpec((1, H, DK, DV), lambda b, c: (b, 0, 0, 0))]
    out_shape = [jax.ShapeDtypeStruct((T, VD), BF16), jax.ShapeDtypeStruct((NB_P, H, DK, DV), F32)]
    return pl.pallas_call(
        _ret_prompt_kernel, grid=(NB_P, nc), in_specs=in_specs, out_specs=out_specs, out_shape=out_shape,
        compiler_params=_cparams(2), name="ret_prompt",
    )(tabs["gam"], q, k, v, sg, tabs["decay"], tabs["qdec"], tabs["kdec"], ng)


def _ret_sample_kernel(gam_ref, q_ref, k_ref, v_ref, sg_ref, dec_ref, qdec_ref, kdec_ref, ng_ref,
                       s_in_ref, y_in_ref, s_all_ref, y_ref, s_out_ref):
    del y_in_ref, s_all_ref
    s_in_ref = s_in_ref.at[0]
    s_out_ref = s_out_ref.at[0]
    hd = pl.program_id(1)
    v = v_ref[...]
    o, qd, kd = _ret_intra(q_ref[...], k_ref[...], v, dec_ref[0], qdec_ref[...], kdec_ref[...])
    gam = gam_ref[hd]
    row_batch = lax.broadcasted_iota(I32, (CHUNK, DK), 0) // L_S
    cross = [jnp.dot(qd[b * L_S:(b + 1) * L_S].astype(BF16), s_in_ref[b, 0].astype(BF16),
                     preferred_element_type=F32) for b in range(SB)]
    for b in range(SB):
        kb = jnp.where(row_batch == b, kd, 0.0).astype(BF16)
        s_out_ref[b, 0] = gam * s_in_ref[b, 0] + lax.dot_general(
            kb, v, (((0,), (0,)), ((), ())), preferred_element_type=F32)
    o = o + jnp.concatenate(cross, axis=0)
    y_ref[...] = _ret_finish(o, sg_ref[...], ng_ref[...])


def _ret_sample(j, q, k, v, sg, tabs, ng, s_in, y_prev, s_all):
    base = T_P // CHUNK
    rq = lambda g, h: (base + g, h)
    st = pl.BlockSpec((1, SB, 1, DK, DV), lambda g, h: (j, g, h, 0, 0))
    smem = pl.BlockSpec(memory_space=pltpu.SMEM)
    in_specs = [smem,
                pl.BlockSpec((CHUNK, DK), rq), pl.BlockSpec((CHUNK, DK), rq),
                pl.BlockSpec((CHUNK, DV), rq), pl.BlockSpec((CHUNK, DV), rq),
                pl.BlockSpec((1, CHUNK, CHUNK), lambda g, h: (h, 0, 0)),
                pl.BlockSpec((CHUNK, DK), lambda g, h: (0, h)),
                pl.BlockSpec((CHUNK, DK), lambda g, h: (0, h)),
                pl.BlockSpec((1, DV), lambda g, h: (0, h)),
                st, pl.BlockSpec(memory_space=pl.ANY), pl.BlockSpec(memory_space=pl.ANY)]
    out_specs = [pl.BlockSpec((CHUNK, DV), rq), st]
    out_shape = [jax.ShapeDtypeStruct((T, VD), BF16), jax.ShapeDtypeStruct(s_all.shape, F32)]
    return pl.pallas_call(
        _ret_sample_kernel, grid=(NB_S // SB, H), in_specs=in_specs, out_specs=out_specs,
        out_shape=out_shape, input_output_aliases={10: 0, 11: 1},
        compiler_params=_cparams(2), name="ret_sample",
    )(tabs["gam"], q, k, v, sg, tabs["decay"], tabs["qdec"], tabs["kdec"], ng, s_in, y_prev, s_all)


def _ret_out_kernel(y_ref, wout, x_ref, g1p, g1s, sh2p, sh2s, sc2p, sc2s, n2g, wr, br,
                    xo_ref, h2_ref, idx_ref, gate_ref, cnt_ref):
    is_p = pl.program_id(0) < NPT
    y = jnp.dot(y_ref[...], wout[...], preferred_element_type=F32)
    _tail(is_p, x_ref[...], y, g1p, g1s, sh2p, sh2s, sc2p, sc2s, n2g, wr, br,
          xo_ref, h2_ref, idx_ref, gate_ref, cnt_ref)


def _ret_out(y, wout, x, mod_p, mod_s, n2g, wr, br):
    in_specs = [_tile_spec(VD), _full_spec((VD, D)), _tile_spec(D)] + _tail_in_specs()
    return pl.pallas_call(
        _ret_out_kernel, grid=(NT,), in_specs=in_specs, out_specs=_tail_out_specs(),
        out_shape=_tail_out_shapes(), compiler_params=_cparams(1), name="ret_out",
    )(y, wout, x, mod_p, mod_s, mod_p, mod_s, mod_p, mod_s, n2g, wr, br)


def _lane_prefix(v, lane1):
    s = 1
    while s < LANES:
        v = v + jnp.where(lane1 >= s, pltpu.roll(v, s, 1), 0.0)
        s *= 2
    return v


def _route_kernel(cnt_ref, idx_ref, lp_ref, lpt_ref, runs_ref, meta_ref, base_scr):
    t = pl.program_id(0)
    idx = idx_ref[...]
    lane = lax.broadcasted_iota(I32, (TM, LANES), 1)
    lane1 = lax.broadcasted_iota(I32, (1, LANES), 1)
    hits = [lane == idx[:, k:k + 1] for k in range(TOPK)]
    chosen = jnp.zeros((TM, LANES), F32)
    for hk in hits:
        chosen = chosen + jnp.where(hk, 1.0, 0.0)
    colsum = jnp.sum(chosen, axis=0, keepdims=True)

    @pl.when(t == 0)
    def _():
        cnt = cnt_ref[...]
        padded = (((cnt.astype(I32) + (BLK - 1)) // BLK) * BLK).astype(F32)
        end = _lane_prefix(padded, lane1)
        base_scr[...] = end - padded
        row = lax.broadcasted_iota(I32, (8, LANES), 0)
        meta = jnp.where(row == 0, cnt, jnp.where(row == 1, end - padded, jnp.where(row == 2, end, 0.0)))
        meta_ref[...] = meta.astype(I32)

    ri = lax.broadcasted_iota(I32, (TM, TM), 0)
    ci = lax.broadcasted_iota(I32, (TM, TM), 1)
    before = jnp.where(ci < ri, 1.0, 0.0).astype(BF16)
    loff = _lane_prefix(colsum, lane1) - colsum
    pos = jnp.dot(before, chosen.astype(BF16), preferred_element_type=F32) + loff
    lp = jnp.zeros((TM, LANES), F32)
    for k, hk in enumerate(hits):
        lp = jnp.where(lane == k, jnp.sum(jnp.where(hk, pos, 0.0), axis=-1, keepdims=True), lp)
    lp_ref[...] = lp.astype(I32)
    lpt_ref[...] = lp.T[:8].astype(I32)
    row = lax.broadcasted_iota(I32, (8, LANES), 0)
    runs = jnp.where(row == 0, colsum, jnp.where(row == 1, loff, jnp.where(row == 2, base_scr[...], 0.0)))
    runs_ref[0] = runs.astype(I32)
    base_scr[...] += colsum


def _route(cnt, idx):
    n_tiles = idx.shape[0] // TM
    return pl.pallas_call(
        _route_kernel, grid=(n_tiles,),
        in_specs=[pl.BlockSpec((1, LANES), lambda t: (0, 0)), pl.BlockSpec((TM, LANES), lambda t: (t, 0))],
        out_specs=[pl.BlockSpec((TM, LANES), lambda t: (t, 0)),
                   pl.BlockSpec((8, TM), lambda t: (t, 0)),
                   pl.BlockSpec((1, 8, LANES), lambda t: (t, 0, 0)),
                   pl.BlockSpec((8, LANES), lambda t: (0, 0))],
        out_shape=[jax.ShapeDtypeStruct((n_tiles * TM, LANES), I32),
                   jax.ShapeDtypeStruct((n_tiles * 8, TM), I32),
                   jax.ShapeDtypeStruct((n_tiles, 8, LANES), I32),
                   jax.ShapeDtypeStruct((8, LANES), I32)],
        scratch_shapes=[pltpu.VMEM((1, LANES), F32)],
        compiler_params=_cparams(1), name="moe_route",
    )(cnt, idx)


def _tile_rows(ref, r, n):
    return ref.at[:, pl.ds(r, n), :]


def _run_copy(src, dst, r_src, r_dst, n, sem, wait):
    for b in reversed(range(TM.bit_length())):
        size = 1 << b
        off = (n >> (b + 1)) << (b + 1)

        def piece(off=off, size=size):
            cp = pltpu.make_async_copy(_tile_rows(src, r_src + off, size),
                                       _tile_rows(dst, r_dst + off, size), sem)
            if wait:
                cp.wait()
            else:
                cp.start()

        if isinstance(n, int):
            if n & size:
                piece()
        else:
            pl.when((n & size) != 0)(piece)


def _load_rows(ref3, rows=None):
    rows = ref3.shape[1] if rows is None else rows
    parts = []
    for s in range(SLABS):
        word = ref3[s, :rows, :]
        parts.append(lax.bitcast_convert_type(lax.shift_left(word, 16), F32))
        parts.append(lax.bitcast_convert_type(word & HIGH_HALF, F32))
    return jnp.concatenate(parts, axis=1).astype(BF16)


def _store_rows(ref3, val):
    rows = val.shape[0]
    bits = lax.bitcast_convert_type(val, I32)
    for s in range(SLABS):
        low = lax.shift_right_logical(bits[:, (2 * s) * LANES:(2 * s + 1) * LANES], 16)
        ref3[s, :rows, :] = bits[:, (2 * s + 1) * LANES:(2 * s + 2) * LANES] | low


def _dispatch_kernel(len_ref, pos_ref, row_ref, cnt_ref, first_ref, end_ref, lpt_ref, h2_ref, dst,
                     buf, zbuf, sem, zsem):
    n_rows = dst.shape[1]
    step = pl.program_id(0)
    slot = step % 2
    cur = buf.at[slot]
    lpt = lpt_ref[...]
    p = lax.broadcasted_iota(I32, (TM * TOPK, TM), 0)
    pick = lpt[0:1, :] == p
    for k in range(1, TOPK):
        pick = pick | (lpt[k:k + 1, :] == p)
    perm = jnp.where(pick, 1.0, 0.0).astype(BF16)
    _store_rows(cur, jnp.dot(perm, h2_ref[...], preferred_element_type=F32))

    def per_expert(e, carry):
        _run_copy(cur, dst, pos_ref[e], row_ref[e], len_ref[e], sem.at[slot], False)
        return carry

    lax.fori_loop(0, NE, per_expert, 0)

    def drain(s):
        pltpu.make_async_copy(buf.at[s], _tile_rows(dst, 0, TM * TOPK), sem.at[s]).wait()

    @pl.when(step > 0)
    def _():
        drain(1 - slot)

    @pl.when(step == pl.num_programs(0) - 1)
    def _():
        drain(slot)

    @pl.when(step == 0)
    def _():
        zbuf[...] = jnp.zeros(zbuf.shape, I32)
        n_tail = (n_rows - end_ref[NE - 1]) // TM
        for wait in (False, True):
            def pad(e, carry, wait=wait):
                lo = first_ref[e] + cnt_ref[e]
                _run_copy(zbuf, dst, 0, lo, end_ref[e] - lo, zsem, wait)
                return carry

            lax.fori_loop(0, NE, pad, 0)

            def tail(j, carry, wait=wait):
                _run_copy(zbuf, dst, 0, end_ref[NE - 1] + j * TM, TM, zsem, wait)
                return carry

            lax.fori_loop(0, n_tail, tail, 0)


def _dispatch(h2, lpt, run_len, run_pos, run_row, cnt, first, end, n_rows):
    assert BLK % TM == 0 and BLK < 2 * TM + 1
    n_tiles = h2.shape[0] // TM
    smem = pl.BlockSpec(memory_space=pltpu.SMEM)
    per_tile = pl.BlockSpec((LANES,), lambda t: (t,), memory_space=pltpu.SMEM)
    return pl.pallas_call(
        _dispatch_kernel, grid=(n_tiles,),
        in_specs=[per_tile, per_tile, per_tile, smem, smem, smem,
                  pl.BlockSpec((8, TM), lambda t: (t, 0)), _tile_spec(D, TM)],
        out_specs=pl.BlockSpec(memory_space=pl.ANY),
        out_shape=jax.ShapeDtypeStruct((SLABS, n_rows, LANES), I32),
        scratch_shapes=[pltpu.VMEM((2, SLABS, TM * TOPK, LANES), I32),
                        pltpu.VMEM((SLABS, BLK, LANES), I32),
                        pltpu.SemaphoreType.DMA((2,)), pltpu.SemaphoreType.DMA],
        compiler_params=_cparams(1), name="moe_dispatch",
    )(run_len, run_pos, run_row, cnt, first, end, lpt, h2)


def _expert_kernel(be_ref, nu_ref, nr_ref, nxt_ref, par_ref, xb_ref, wu_hbm, bu_ref, wd_hbm, bd_ref, yb_ref,
                   wuf, wdf, wub, wdb, sem, *, layer):
    i = pl.program_id(0)
    used = i < nu_ref[0]
    half = nr_ref[i] <= BLK // 2
    slot = par_ref[i]

    def weights(e, s):
        return (pltpu.make_async_copy(wu_hbm.at[layer, e], wuf.at[s], sem.at[s, 0]),
                pltpu.make_async_copy(wd_hbm.at[layer, e], wdf.at[s], sem.at[s, 1]))

    @pl.when(i == 0)
    def _():
        for cp in weights(be_ref[0], slot):
            cp.start()

    @pl.when(used & ((i == 0) | (be_ref[i] != be_ref[jnp.maximum(i - 1, 0)])))
    def _():
        for cp in weights(be_ref[i], slot):
            cp.wait()
        wub[...] = wuf[slot].astype(BF16)
        wdb[...] = wdf[slot].astype(BF16)

        @pl.when(nxt_ref[i] >= 0)
        def _():
            for cp in weights(nxt_ref[i], 1 - slot):
                cp.start()

    def swiglu_rows(rows):
        z = jnp.dot(_load_rows(xb_ref, rows), wub[...], preferred_element_type=F32) + bu_ref[0, 0]
        glu = jnp.minimum(z[:, :FE], LIMIT)
        lin = jnp.clip(z[:, FE:], -LIMIT, LIMIT)
        act = glu * jax.nn.sigmoid(ALPHA * glu) * (lin + 1.0)
        y = jnp.dot(act.astype(BF16), wdb[...], preferred_element_type=F32) + bd_ref[0, 0]
        _store_rows(yb_ref, y.astype(BF16).astype(F32))

    @pl.when(used & jnp.logical_not(half))
    def _():
        swiglu_rows(BLK)

    @pl.when(used & half)
    def _():
        swiglu_rows(BLK // 2)
        yb_ref[:, BLK // 2:, :] = jnp.zeros((SLABS, BLK // 2, LANES), I32)

    @pl.when(jnp.logical_not(used))
    def _():
        yb_ref[...] = jnp.zeros(yb_ref.shape, I32)


def _experts(layer, xb, blk_e, n_used, blk_rows, blk_next, blk_par, wu, bu, wd, bd):
    n_blocks = blk_e.shape[0]
    grid_spec = pltpu.PrefetchScalarGridSpec(
        num_scalar_prefetch=5, grid=(n_blocks,),
        in_specs=[pl.BlockSpec((SLABS, BLK, LANES), lambda i, be, nu, *_: (0, jnp.minimum(i, nu[0] - 1), 0)),
                  pl.BlockSpec(memory_space=pl.ANY),
                  pl.BlockSpec((1, 1, 1, 2 * FE), lambda i, be, *_: (layer, be[i], 0, 0)),
                  pl.BlockSpec(memory_space=pl.ANY),
                  pl.BlockSpec((1, 1, 1, D), lambda i, be, *_: (layer, be[i], 0, 0))],
        out_specs=pl.BlockSpec((SLABS, BLK, LANES), lambda i, *_: (0, i, 0)),
        scratch_shapes=[pltpu.VMEM((2, D, 2 * FE), F32), pltpu.VMEM((2, FE, D), F32),
                        pltpu.VMEM((D, 2 * FE), BF16), pltpu.VMEM((FE, D), BF16),
                        pltpu.SemaphoreType.DMA((2, 2))])
    return pl.pallas_call(
        functools.partial(_expert_kernel, layer=layer), grid_spec=grid_spec,
        out_shape=jax.ShapeDtypeStruct((SLABS, n_blocks * BLK, LANES), I32),
        compiler_params=_cparams(1), name="experts",
    )(blk_e, n_used, blk_rows, blk_next, blk_par, xb, wu, bu, wd, bd)


def _combine_kernel(len_ref, pos_ref, row_ref, len_nx, pos_nx, row_nx, yb, lp_ref, x_ref, gate_ref,
                    g2p, g2s, fg_ref, o_ref, ybuf, sem, *, final):
    step = pl.program_id(0)
    slot = step % 2
    is_p = step < T_P // TM

    def fetch(len_r, pos_r, row_r, s):
        def per_expert(e, carry):
            _run_copy(yb, ybuf.at[s], row_r[e], pos_r[e], len_r[e], sem.at[s], False)
            return carry

        lax.fori_loop(0, NE, per_expert, 0)

    @pl.when(step == 0)
    def _():
        fetch(len_ref, pos_ref, row_ref, slot)

    @pl.when(step + 1 < pl.num_programs(0))
    def _():
        fetch(len_nx, pos_nx, row_nx, 1 - slot)

    cur = ybuf.at[slot]
    pltpu.make_async_copy(_tile_rows(yb, 0, TM * TOPK), cur, sem.at[slot]).wait()

    rows = x_ref.shape[0]
    lp = lp_ref[...]
    gates = gate_ref[...]
    p = lax.broadcasted_iota(I32, (rows, rows * TOPK), 1)
    weights = jnp.zeros((rows, rows * TOPK), F32)
    for k in range(TOPK):
        weights = jnp.where(lp[:, k:k + 1] == p, gates[:, k:k + 1], weights)
    acc = jnp.dot(weights.astype(BF16), _load_rows(cur), preferred_element_type=F32)
    xn = x_ref[...] + _pick(is_p, g2p, g2s) * acc
    o_ref[...] = _rms(xn, fg_ref[...]) if final else xn


def _combine(x, yb, lp, run_len, run_pos, run_row, gates, mod_p, mod_s, final_g, final):
    n_tiles = x.shape[0] // TM
    per_tile = pl.BlockSpec((LANES,), lambda t: (t,), memory_space=pltpu.SMEM)
    next_tile = pl.BlockSpec((LANES,), lambda t: (jnp.minimum(t + 1, n_tiles - 1),),
                             memory_space=pltpu.SMEM)
    in_specs = ([per_tile, per_tile, per_tile, next_tile, next_tile, next_tile,
                 pl.BlockSpec(memory_space=pl.ANY),
                 _tile_spec(LANES, TM), _tile_spec(D, TM), _tile_spec(LANES, TM)]
                + _mod_specs(5, TM) + [_full_spec((1, D))])
    return pl.pallas_call(
        functools.partial(_combine_kernel, final=final), grid=(n_tiles,), in_specs=in_specs,
        out_specs=_tile_spec(D, TM), out_shape=jax.ShapeDtypeStruct(x.shape, F32),
        scratch_shapes=[pltpu.VMEM((2, SLABS, TM * TOPK, LANES), I32), pltpu.SemaphoreType.DMA((2,))],
        compiler_params=_cparams(1), name="moe_combine",
    )(run_len, run_pos, run_row, run_len, run_pos, run_row, yb, lp, x, gates, mod_p, mod_s, final_g)


def _block_tables(cnt, first, end, n_blocks):
    experts = jnp.arange(NE, dtype=I32)
    blk_first = jnp.arange(n_blocks, dtype=I32) * BLK
    blk_e = jnp.minimum(jnp.sum((end[None, :] <= blk_first[:, None]).astype(I32), axis=1), NE - 1)
    n_used = end[NE - 1:] // BLK
    has_rows = cnt > 0
    later = has_rows[None, :] & (experts[None, :] > experts[:, None])
    next_e = jnp.min(jnp.where(later, experts[None, :], NE), axis=1)
    next_e = jnp.where(next_e == NE, -1, next_e)
    parity = (jnp.cumsum(has_rows.astype(I32)) - 1) % 2
    own = blk_e[:, None] == experts[None, :]
    of_block = lambda per_expert: jnp.sum(jnp.where(own, per_expert[None, :], 0), axis=1)
    blk_rows = jnp.clip(of_block(first + cnt) - blk_first, 0, BLK)
    return blk_e, n_used, blk_rows, of_block(next_e), of_block(parity)


def _moe(layer, x, h2, idx, gates, cnt_all, mod_p, mod_s, wu, bu, wd, bd, final_g, final):
    lp, lpt, runs, meta = _route(cnt_all, idx)
    cnt, first, end = meta[0, :NE], meta[1, :NE], meta[2, :NE]
    run_len, run_pos, run_row = (runs[:, r, :].reshape(-1) for r in range(3))
    xb = _dispatch(h2, lpt, run_len, run_pos, run_row, cnt, first, end, N_ROWS)
    yb = _experts(layer, xb, *_block_tables(cnt, first, end, N_BLOCKS), wu, bu, wd, bd)
    return _combine(x, yb, lp, run_len, run_pos, run_row, gates, mod_p, mod_s, final_g, final)


def _rope_tables():
    half = DK // 2
    inv = 1.0 / (ROPE_BASE ** jnp.linspace(0.0, 1.0, half, dtype=F32))

    def tab(pos):
        ang = pos.astype(F32)[:, None] * inv[None, :]
        cos, sin = jnp.cos(ang), jnp.sin(ang)
        return jnp.concatenate([cos, cos], -1), jnp.concatenate([-sin, sin], -1)

    cp, sp = tab(jnp.arange(L_P, dtype=I32))
    cs, ss = tab(PAST + jnp.arange(L_S, dtype=I32))
    cos = jnp.concatenate([jnp.tile(cp, (NB_P, 1)), jnp.tile(cs, (NB_S, 1))], 0)
    sin = jnp.concatenate([jnp.tile(sp, (NB_P, 1)), jnp.tile(ss, (NB_S, 1))], 0)
    return cos, sin


def _decay_tables(cl):
    lg = jnp.log(1.0 - 2.0 ** (-5.0 - jnp.arange(H, dtype=F32)))
    r = jnp.arange(CHUNK)
    idx = (r % cl).astype(F32)
    diff = idx[:, None] - idx[None, :]
    same = (r[:, None] // cl) == (r[None, :] // cl)
    decay = jnp.where((same & (diff >= 0))[None],
                      jnp.exp(lg[:, None, None] * jnp.maximum(diff, 0.0)[None]), 0.0)
    qdec = jnp.exp(lg[None, :] * (idx[:, None] + 1.0))
    kdec = jnp.exp(lg[None, :] * (cl - 1.0 - idx[:, None]))
    wide = lambda a: jnp.repeat(a, DK, axis=1)
    return {"decay": decay, "qdec": wide(qdec), "kdec": wide(kdec), "gam": jnp.exp(lg * cl)}


def kernel(x_prompt, x_sample, c_prompt, c_sample, state_ret, w_mod, b_mod, norm1_g, norm2_g,
           sgu_w_in, sgu_ln_g, sgu_ln_b, sgu_w_s, sgu_b_s, sgu_w_out, ret_w_in, ret_norm_g, ret_w_out,
           moe_w_router, moe_b_router, moe_w_up, moe_b_up, moe_w_down, moe_b_down, final_g):
    x = jnp.concatenate([x_prompt.reshape(T_P, D), x_sample.reshape(T_S, D)], 0)
    mod = _modulation(jnp.concatenate([c_prompt, c_sample], 0), w_mod, b_mod)
    cos_tab, sin_tab = _rope_tables()
    tabs_p = _decay_tables(CHUNK)
    tabs_s = _decay_tables(L_S)
    wr_pad = jnp.pad(moe_w_router, ((0, 0), (0, 0), (0, LANES - NE))).astype(BF16)
    br_pad = jnp.pad(moe_b_router, ((0, 0), (0, LANES - NE)), constant_values=-1e30)
    fg = final_g.reshape(1, D)
    b_up = moe_b_up.reshape(DEPTH, NE, 1, 2 * FE)
    b_down = moe_b_down.reshape(DEPTH, NE, 1, D)

    ret_p, v_rows = [], []
    s_all = lax.empty(state_ret.shape, F32)
    for i in range(DEPTH):
        j = i // 2
        mod_p = mod[i, :NB_P].reshape(NB_P, 1, 6 * D)
        mod_s = jnp.repeat(mod[i, NB_P:], L_S, axis=0)
        n1g = norm1_g[i].reshape(1, D)
        n2g = norm2_g[i].reshape(1, D)
        wr = wr_pad[i]
        br = br_pad[i].reshape(1, LANES)
        if i % 2 == 0:
            mixw = jnp.stack([sgu_w_s[j], jnp.tile(sgu_w_s[j][:, :L_S, :L_S], (1, SB, SB))])
            bias_p = jnp.repeat(sgu_b_s[j].T, SGU_GD, axis=1)
            bias_s = jnp.tile(bias_p[:L_S], (SB, 1))
            x, h2, idx, gates, cnt, v = _sgu_layer(
                x, mod_p, mod_s, n1g, n2g, sgu_w_in[j].astype(BF16), sgu_ln_g[j].reshape(1, SGU_W),
                sgu_ln_b[j].reshape(1, SGU_W), mixw, jnp.stack([bias_p, bias_s]),
                sgu_w_out[j].astype(BF16), wr, br)
            v_rows.append(v.reshape(NB_S, L_S, SGU_W))
        else:
            q, k, v, sg = _ret_proj(x, mod_p, mod_s, n1g, ret_w_in[j].astype(BF16), cos_tab, sin_tab)
            ng = ret_norm_g[j].reshape(1, VD)
            y, s_p = _ret_prompt(q, k, v, sg, tabs_p, ng)
            y, s_all = _ret_sample(j, q, k, v, sg, tabs_s, ng, state_ret, y, s_all)
            ret_p.append(s_p)
            x, h2, idx, gates, cnt = _ret_out(y, ret_w_out[j].astype(BF16), x, mod_p, mod_s, n2g, wr, br)
        x = _moe(i, x, h2, idx, gates, cnt, mod_p, mod_s, moe_w_up, b_up, moe_w_down, b_down,
                 fg, final=(i == DEPTH - 1))
    y_prompt = x[:T_P].reshape(NB_P, L_P, D)
    y_sample = x[T_P:].reshape(NB_S, L_S, D)
    return (y_prompt, y_sample, jnp.stack(ret_p), s_all, jnp.stack(v_rows))
```

```python
import functools

import jax
import jax.numpy as jnp
from jax import lax
from jax.experimental import pallas as pl
from jax.experimental.pallas import tpu as pltpu

F32 = jnp.float32
BF16 = jnp.bfloat16
I32 = jnp.int32

D = 1024
NB_P, L_P = 8, 2048
NB_S, L_S = 128, 8
PAST = 16384
DEPTH = 4
T_P = NB_P * L_P
T_S = NB_S * L_S
T = T_P + T_S
SGU_W = 2 * D
SGU_G = 8
SGU_GD = SGU_W // SGU_G
CHUNK = 128
H = 8
DK = D // H
DV = 2 * DK
QD = H * DK
VD = H * DV
RET_IN = 2 * QD + 2 * VD
ROPE_BASE = 10000.0
NE = 32
TOPK = 4
FE = D
ALPHA = 1.702
LIMIT = 7.0
EPS = 1e-6

LANES = 128
SLABS = D // (2 * LANES)
HIGH_HALF = -65536
TD = 512
NPT = T_P // TD
NT = T // TD
TM = 256
BLK = 512
TK = T * TOPK
N_BLOCKS = -(-(TK + NE * (BLK - 1)) // BLK)
N_ROWS = N_BLOCKS * BLK
SB = 16
RET_CHUNKS = 4
VMEM_LIMIT = 56 * 1024 * 1024


def _cparams(n_axes):
    return pltpu.CompilerParams(dimension_semantics=("arbitrary",) * n_axes,
                                vmem_limit_bytes=VMEM_LIMIT)


def _rms(x, g):
    return (x * lax.rsqrt(jnp.mean(x * x, axis=-1, keepdims=True) + EPS)) * g


def _pick(is_p, p_ref, s_ref):
    return jnp.where(is_p, p_ref[0], s_ref[...])


def _mod_specs(j, tm=TD):
    per_batch, n_prompt = L_P // tm, T_P // tm
    return [
        pl.BlockSpec((1, 1, D), lambda t: (jnp.minimum(t // per_batch, NB_P - 1), 0, j)),
        pl.BlockSpec((tm, D), lambda t: (jnp.maximum(t - n_prompt, 0), j), pipeline_mode=pl.Buffered(1)),
    ]


def _tile_spec(width, tm=TD):
    return pl.BlockSpec((tm, width), lambda t: (t, 0))


def _full_spec(shape):
    return pl.BlockSpec(shape, lambda *_: (0,) * len(shape), pipeline_mode=pl.Buffered(1))


def _mod_kernel(c_ref, w_ref, b_ref, o_ref):
    c = c_ref[...]
    cs = (c * jax.nn.sigmoid(c)).astype(BF16)
    o_ref[0] = jnp.dot(cs, w_ref[0].astype(BF16), preferred_element_type=F32) + b_ref[0]


def _modulation(c_all, w_mod, b_mod):
    tn = 1536
    nb = c_all.shape[0]
    return pl.pallas_call(
        _mod_kernel,
        grid=(DEPTH, 6 * D // tn),
        in_specs=[
            pl.BlockSpec((nb, D), lambda l, n: (0, 0)),
            pl.BlockSpec((1, D, tn), lambda l, n: (l, 0, n)),
            pl.BlockSpec((1, 1, tn), lambda l, n: (l, 0, n)),
        ],
        out_specs=pl.BlockSpec((1, nb, tn), lambda l, n: (l, 0, n)),
        out_shape=jax.ShapeDtypeStruct((DEPTH, nb, 6 * D), F32),
        compiler_params=_cparams(2),
        name="modulation",
    )(c_all, w_mod, b_mod.reshape(DEPTH, 1, 6 * D))


def _tail(is_p, x, y, g1p, g1s, sh2p, sh2s, sc2p, sc2s, n2g, wr, br, xo_ref, h2_ref, idx_ref, gate_ref,
          cnt_ref):
    xn = x + _pick(is_p, g1p, g1s) * y
    xo_ref[...] = xn
    h2 = (_rms(xn, n2g[...]) * (1.0 + _pick(is_p, sc2p, sc2s)) + _pick(is_p, sh2p, sh2s)).astype(BF16)
    h2_ref[...] = h2
    logit = jnp.dot(h2, wr[...], preferred_element_type=F32) + br[...]
    lane = lax.broadcasted_iota(I32, logit.shape, 1)
    vals, ids = [], []
    for _ in range(TOPK):
        m = jnp.max(logit, axis=-1, keepdims=True)
        sel = jnp.min(jnp.where(logit == m, lane, LANES), axis=-1, keepdims=True)
        vals.append(m)
        ids.append(sel)
        logit = jnp.where(lane == sel, -jnp.inf, logit)
    es = [jnp.exp(v - vals[0]) for v in vals]
    tot = (es[0] + es[1]) + (es[2] + es[3])
    idx_out = jnp.zeros(logit.shape, I32)
    gate_out = jnp.zeros(logit.shape, F32)
    chosen = jnp.zeros(logit.shape, F32)
    for k in range(TOPK):
        idx_out = jnp.where(lane == k, ids[k], idx_out)
        gate_out = jnp.where(lane == k, es[k] / tot, gate_out)
        chosen = chosen + jnp.where(lane == ids[k], 1.0, 0.0)
    idx_ref[...] = idx_out
    gate_ref[...] = gate_out

    @pl.when(pl.program_id(0) == 0)
    def _():
        cnt_ref[...] = jnp.zeros(cnt_ref.shape, F32)

    cnt_ref[...] += jnp.sum(chosen, axis=0, keepdims=True)


def _tail_in_specs():
    return (_mod_specs(2) + _mod_specs(3) + _mod_specs(4)
            + [_full_spec((1, D)), _full_spec((D, LANES)), _full_spec((1, LANES))])


def _tail_out_specs():
    return [_tile_spec(D), _tile_spec(D), _tile_spec(LANES), _tile_spec(LANES),
            pl.BlockSpec((1, LANES), lambda t: (0, 0))]


def _tail_out_shapes():
    return [jax.ShapeDtypeStruct((T, D), F32), jax.ShapeDtypeStruct((T, D), BF16),
            jax.ShapeDtypeStruct((T, LANES), I32), jax.ShapeDtypeStruct((T, LANES), F32),
            jax.ShapeDtypeStruct((1, LANES), F32)]


def _sgu_kernel(x_ref, sh1p, sh1s, sc1p, sc1s, n1g, win, lng, lnb, mixw, mixb, wout,
                g1p, g1s, sh2p, sh2s, sc2p, sc2s, n2g, wr, br,
                xo_ref, h2_ref, idx_ref, gate_ref, cnt_ref, v_ref, y_scr):
    t = pl.program_id(0)
    is_p = t < NPT
    x = x_ref[...]
    h = _rms(x, n1g[...]) * (1.0 + _pick(is_p, sc1p, sc1s)) + _pick(is_p, sh1p, sh1s)
    z = jnp.dot(h.astype(BF16), win[...], preferred_element_type=F32)
    z = 0.5 * z * (1.0 + lax.erf(z * (0.5 ** 0.5)))
    u = z[:, :SGU_W]
    v = z[:, SGU_W:]
    vc = v - jnp.mean(v, axis=-1, keepdims=True)
    vn = vc * lax.rsqrt(jnp.mean(vc * vc, axis=-1, keepdims=True) + EPS) * lng[...] + lnb[...]

    @pl.when(t >= NPT)
    def _():
        v_ref[...] = vn

    vb = vn.astype(BF16)
    ri = lax.broadcasted_iota(I32, (CHUNK, CHUNK), 0)
    ci = lax.broadcasted_iota(I32, (CHUNK, CHUNK), 1)
    causal = ci <= ri
    shift = jnp.broadcast_to(jnp.where(is_p, 7, 3), ri.shape)
    keep = causal & (lax.shift_right_logical(ri, shift) == lax.shift_right_logical(ci, shift))
    for g in range(SGU_G):
        wg = jnp.where(keep, mixw[0, g], 0.0).astype(BF16)
        for c in range(TD // CHUNK):
            rows = slice(c * CHUNK, (c + 1) * CHUNK)
            cols = slice(g * SGU_GD, (g + 1) * SGU_GD)
            mixed = jnp.dot(wg, vb[rows, cols], preferred_element_type=F32) + mixb[0, :, cols]
            y_scr[rows, cols] = (u[rows, cols] * mixed).astype(BF16)
    y = jnp.dot(y_scr[...], wout[...], preferred_element_type=F32)
    _tail(is_p, x, y, g1p, g1s, sh2p, sh2s, sc2p, sc2s, n2g, wr, br, xo_ref, h2_ref, idx_ref, gate_ref,
          cnt_ref)


def _sgu_layer(x, mod_p, mod_s, n1g, n2g, win, lng, lnb, mixw, mixb, wout, wr, br):
    sel = lambda t: (jnp.where(t < NPT, 0, 1), 0, 0, 0)
    in_specs = ([_tile_spec(D)] + _mod_specs(0) + _mod_specs(1)
                + [_full_spec((1, D)), _full_spec((D, 2 * SGU_W)), _full_spec((1, SGU_W)),
                   _full_spec((1, SGU_W)),
                   pl.BlockSpec((1, SGU_G, CHUNK, CHUNK), sel),
                   pl.BlockSpec((1, CHUNK, SGU_W), lambda t: (jnp.where(t < NPT, 0, 1), 0, 0)),
                   _full_spec((SGU_W, D))]
                + _tail_in_specs())
    out_specs = _tail_out_specs() + [pl.BlockSpec((TD, SGU_W), lambda t: (jnp.maximum(t - NPT, 0), 0))]
    out_shape = _tail_out_shapes() + [jax.ShapeDtypeStruct((T_S, SGU_W), F32)]
    return pl.pallas_call(
        _sgu_kernel, grid=(NT,), in_specs=in_specs, out_specs=out_specs, out_shape=out_shape,
        scratch_shapes=[pltpu.VMEM((TD, SGU_W), BF16)],
        compiler_params=_cparams(1), name="sgu_layer",
    )(x, mod_p, mod_s, mod_p, mod_s, n1g, win, lng, lnb, mixw, mixb, wout,
      mod_p, mod_s, mod_p, mod_s, mod_p, mod_s, n2g, wr, br)


def _ret_proj_kernel(x_ref, sh1p, sh1s, sc1p, sc1s, n1g, win, cos_ref, sin_ref,
                     q_ref, k_ref, v_ref, sg_ref):
    t = pl.program_id(0)
    is_p = t < NPT
    x = x_ref[...]
    h = _rms(x, n1g[...]) * (1.0 + _pick(is_p, sc1p, sc1s)) + _pick(is_p, sh1p, sh1s)
    p = jnp.dot(h.astype(BF16), win[...], preferred_element_type=F32)
    cos = cos_ref[...]
    sin = sin_ref[...]
    for hd in range(H):
        cq = slice(hd * DK, (hd + 1) * DK)
        ck = slice(QD + hd * DK, QD + (hd + 1) * DK)
        qh = p[:, cq]
        kh = p[:, ck]
        q_ref[:, cq] = (qh * cos + pltpu.roll(qh, DK // 2, 1) * sin).astype(BF16)
        k_ref[:, cq] = ((kh * cos + pltpu.roll(kh, DK // 2, 1) * sin) * (DK ** -0.5)).astype(BF16)
    v_ref[...] = p[:, 2 * QD:2 * QD + VD].astype(BF16)
    g = p[:, 2 * QD + VD:]
    sg_ref[...] = (g * jax.nn.sigmoid(g)).astype(BF16)


def _ret_proj(x, mod_p, mod_s, n1g, win, cos_tab, sin_tab):
    in_specs = ([_tile_spec(D)] + _mod_specs(0) + _mod_specs(1)
                + [_full_spec((1, D)), _full_spec((D, RET_IN)), _tile_spec(DK), _tile_spec(DK)])
    out_specs = [_tile_spec(QD), _tile_spec(QD), _tile_spec(VD), _tile_spec(VD)]
    out_shape = [jax.ShapeDtypeStruct((T, QD), BF16), jax.ShapeDtypeStruct((T, QD), BF16),
                 jax.ShapeDtypeStruct((T, VD), BF16), jax.ShapeDtypeStruct((T, VD), BF16)]
    return pl.pallas_call(
        _ret_proj_kernel, grid=(NT,), in_specs=in_specs, out_specs=out_specs, out_shape=out_shape,
        compiler_params=_cparams(1), name="ret_proj",
    )(x, mod_p, mod_s, mod_p, mod_s, n1g, win, cos_tab, sin_tab)


def _ret_intra(q, k, v, decay, qdec, kdec):
    s = lax.dot_general(q, k, (((1,), (1,)), ((), ())), preferred_element_type=F32) * decay
    o = jnp.dot(s.astype(BF16), v, preferred_element_type=F32)
    return o, q.astype(F32) * qdec, k.astype(F32) * kdec


def _ret_finish(o, sg, ng):
    on = o * lax.rsqrt(jnp.mean(o * o, axis=-1, keepdims=True) + EPS)
    return (sg.astype(F32) * (on * ng)).astype(BF16)


def _ret_prompt_kernel(gam_ref, q_ref, k_ref, v_ref, sg_ref, dec_ref, qdec_ref, kdec_ref, ng_ref,
                       y_ref, s_ref):
    c = pl.program_id(1)

    @pl.when(c == 0)
    def _():
        s_ref[...] = jnp.zeros(s_ref.shape, F32)

    cks = [slice(hd * DK, (hd + 1) * DK) for hd in range(H)]
    cvs = [slice(hd * DV, (hd + 1) * DV) for hd in range(H)]
    for j in range(RET_CHUNKS):
        rows = slice(j * CHUNK, (j + 1) * CHUNK)
        vs = [v_ref[rows, cv] for cv in cvs]
        intra = [_ret_intra(q_ref[rows, ck], k_ref[rows, ck], v, dec_ref[hd], qdec_ref[:, ck], kdec_ref[:, ck])
                 for hd, (ck, v) in enumerate(zip(cks, vs))]
        olds = [s_ref[0, hd] for hd in range(H)]
        outs = [o + jnp.dot(qd.astype(BF16), s_old.astype(BF16), preferred_element_type=F32)
                for (o, qd, _), s_old in zip(intra, olds)]
        for hd in range(H):
            s_ref[0, hd] = gam_ref[hd] * olds[hd] + lax.dot_general(
                intra[hd][2].astype(BF16), vs[hd], (((0,), (0,)), ((), ())), preferred_element_type=F32)
        for hd, cv in enumerate(cvs):
            y_ref[rows, cv] = _ret_finish(outs[hd], sg_ref[rows, cv], ng_ref[:, cv])


def _ret_prompt(q, k, v, sg, tabs, ng):
    rows = RET_CHUNKS * CHUNK
    nc = L_P // rows
    row = lambda b, c: (b * nc + c, 0)
    smem = pl.BlockSpec(memory_space=pltpu.SMEM)
    in_specs = [smem,
                pl.BlockSpec((rows, QD), row), pl.BlockSpec((rows, QD), row),
                pl.BlockSpec((rows, VD), row), pl.BlockSpec((rows, VD), row),
                _full_spec((H, CHUNK, CHUNK)), _full_spec((CHUNK, QD)), _full_spec((CHUNK, QD)),
                _full_spec((1, VD))]
    out_specs = [pl.BlockSpec((rows, VD), row),
                 pl.BlockSpec((1, H, DK, DV), lambda b, c: (b, 0, 0, 0))]
    out_shape = [jax.ShapeDtypeStruct((T, VD), BF16), jax.ShapeDtypeStruct((NB_P, H, DK, DV), F32)]
    return pl.pallas_call(
        _ret_prompt_kernel, grid=(NB_P, nc), in_specs=in_specs, out_specs=out_specs, out_shape=out_shape,
        compiler_params=_cparams(2), name="ret_prompt",
    )(tabs["gam"], q, k, v, sg, tabs["decay"], tabs["qdec"], tabs["kdec"], ng)


def _ret_sample_kernel(gam_ref, q_ref, k_ref, v_ref, sg_ref, dec_ref, qdec_ref, kdec_ref, ng_ref,
                       s_in_ref, y_in_ref, s_all_ref, y_ref, s_out_ref):
    del y_in_ref, s_all_ref
    s_in_ref = s_in_ref.at[0]
    s_out_ref = s_out_ref.at[0]
    hd = pl.program_id(1)
    v = v_ref[...]
    o, qd, kd = _ret_intra(q_ref[...], k_ref[...], v, dec_ref[0], qdec_ref[...], kdec_ref[...])
    gam = gam_ref[hd]
    row_batch = lax.broadcasted_iota(I32, (CHUNK, DK), 0) // L_S
    cross = [jnp.dot(qd[b * L_S:(b + 1) * L_S].astype(BF16), s_in_ref[b, 0].astype(BF16),
                     preferred_element_type=F32) for b in range(SB)]
    for b in range(SB):
        kb = jnp.where(row_batch == b, kd, 0.0).astype(BF16)
        s_out_ref[b, 0] = gam * s_in_ref[b, 0] + lax.dot_general(
            kb, v, (((0,), (0,)), ((), ())), preferred_element_type=F32)
    o = o + jnp.concatenate(cross, axis=0)
    y_ref[...] = _ret_finish(o, sg_ref[...], ng_ref[...])


def _ret_sample(j, q, k, v, sg, tabs, ng, s_in, y_prev, s_all):
    base = T_P // CHUNK
    rq = lambda g, h: (base + g, h)
    st = pl.BlockSpec((1, SB, 1, DK, DV), lambda g, h: (j, g, h, 0, 0))
    smem = pl.BlockSpec(memory_space=pltpu.SMEM)
    in_specs = [smem,
                pl.BlockSpec((CHUNK, DK), rq), pl.BlockSpec((CHUNK, DK), rq),
                pl.BlockSpec((CHUNK, DV), rq), pl.BlockSpec((CHUNK, DV), rq),
                pl.BlockSpec((1, CHUNK, CHUNK), lambda g, h: (h, 0, 0)),
                pl.BlockSpec((CHUNK, DK), lambda g, h: (0, h)),
                pl.BlockSpec((CHUNK, DK), lambda g, h: (0, h)),
                pl.BlockSpec((1, DV), lambda g, h: (0, h)),
                st, pl.BlockSpec(memory_space=pl.ANY), pl.BlockSpec(memory_space=pl.ANY)]
    out_specs = [pl.BlockSpec((CHUNK, DV), rq), st]
    out_shape = [jax.ShapeDtypeStruct((T, VD), BF16), jax.ShapeDtypeStruct(s_all.shape, F32)]
    return pl.pallas_call(
        _ret_sample_kernel, grid=(NB_S // SB, H), in_specs=in_specs, out_specs=out_specs,
        out_shape=out_shape, input_output_aliases={10: 0, 11: 1},
        compiler_params=_cparams(2), name="ret_sample",
    )(tabs["gam"], q, k, v, sg, tabs["decay"], tabs["qdec"], tabs["kdec"], ng, s_in, y_prev, s_all)


def _ret_out_kernel(y_ref, wout, x_ref, g1p, g1s, sh2p, sh2s, sc2p, sc2s, n2g, wr, br,
                    xo_ref, h2_ref, idx_ref, gate_ref, cnt_ref):
    is_p = pl.program_id(0) < NPT
    y = jnp.dot(y_ref[...], wout[...], preferred_element_type=F32)
    _tail(is_p, x_ref[...], y, g1p, g1s, sh2p, sh2s, sc2p, sc2s, n2g, wr, br,
          xo_ref, h2_ref, idx_ref, gate_ref, cnt_ref)


def _ret_out(y, wout, x, mod_p, mod_s, n2g, wr, br):
    in_specs = [_tile_spec(VD), _full_spec((VD, D)), _tile_spec(D)] + _tail_in_specs()
    return pl.pallas_call(
        _ret_out_kernel, grid=(NT,), in_specs=in_specs, out_specs=_tail_out_specs(),
        out_shape=_tail_out_shapes(), compiler_params=_cparams(1), name="ret_out",
    )(y, wout, x, mod_p, mod_s, mod_p, mod_s, mod_p, mod_s, n2g, wr, br)


def _lane_prefix(v, lane1):
    s = 1
    while s < LANES:
        v = v + jnp.where(lane1 >= s, pltpu.roll(v, s, 1), 0.0)
        s *= 2
    return v


def _route_kernel(cnt_ref, idx_ref, lp_ref, lpt_ref, runs_ref, meta_ref, base_scr):
    t = pl.program_id(0)
    idx = idx_ref[...]
    lane = lax.broadcasted_iota(I32, (TM, LANES), 1)
    lane1 = lax.broadcasted_iota(I32, (1, LANES), 1)
    hits = [lane == idx[:, k:k + 1] for k in range(TOPK)]
    chosen = jnp.zeros((TM, LANES), F32)
    for hk in hits:
        chosen = chosen + jnp.where(hk, 1.0, 0.0)
    colsum = jnp.sum(chosen, axis=0, keepdims=True)

    @pl.when(t == 0)
    def _():
        cnt = cnt_ref[...]
        padded = (((cnt.astype(I32) + (BLK - 1)) // BLK) * BLK).astype(F32)
        end = _lane_prefix(padded, lane1)
        base_scr[...] = end - padded
        row = lax.broadcasted_iota(I32, (8, LANES), 0)
        meta = jnp.where(row == 0, cnt, jnp.where(row == 1, end - padded, jnp.where(row == 2, end, 0.0)))
        meta_ref[...] = meta.astype(I32)

    ri = lax.broadcasted_iota(I32, (TM, TM), 0)
    ci = lax.broadcasted_iota(I32, (TM, TM), 1)
    before = jnp.where(ci < ri, 1.0, 0.0).astype(BF16)
    loff = _lane_prefix(colsum, lane1) - colsum
    pos = jnp.dot(before, chosen.astype(BF16), preferred_element_type=F32) + loff
    lp = jnp.zeros((TM, LANES), F32)
    for k, hk in enumerate(hits):
        lp = jnp.where(lane == k, jnp.sum(jnp.where(hk, pos, 0.0), axis=-1, keepdims=True), lp)
    lp_ref[...] = lp.astype(I32)
    lpt_ref[...] = lp.T[:8].astype(I32)
    row = lax.broadcasted_iota(I32, (8, LANES), 0)
    runs = jnp.where(row == 0, colsum, jnp.where(row == 1, loff, jnp.where(row == 2, base_scr[...], 0.0)))
    runs_ref[0] = runs.astype(I32)
    base_scr[...] += colsum


def _route(cnt, idx):
    n_tiles = idx.shape[0] // TM
    return pl.pallas_call(
        _route_kernel, grid=(n_tiles,),
        in_specs=[pl.BlockSpec((1, LANES), lambda t: (0, 0)), pl.BlockSpec((TM, LANES), lambda t: (t, 0))],
        out_specs=[pl.BlockSpec((TM, LANES), lambda t: (t, 0)),
                   pl.BlockSpec((8, TM), lambda t: (t, 0)),
                   pl.BlockSpec((1, 8, LANES), lambda t: (t, 0, 0)),
                   pl.BlockSpec((8, LANES), lambda t: (0, 0))],
        out_shape=[jax.ShapeDtypeStruct((n_tiles * TM, LANES), I32),
                   jax.ShapeDtypeStruct((n_tiles * 8, TM), I32),
                   jax.ShapeDtypeStruct((n_tiles, 8, LANES), I32),
                   jax.ShapeDtypeStruct((8, LANES), I32)],
        scratch_shapes=[pltpu.VMEM((1, LANES), F32)],
        compiler_params=_cparams(1), name="moe_route",
    )(cnt, idx)


def _tile_rows(ref, r, n):
    return ref.at[:, pl.ds(r, n), :]


def _run_copy(src, dst, r_src, r_dst, n, sem, wait):
    for b in reversed(range(TM.bit_length())):
        size = 1 << b
        off = (n >> (b + 1)) << (b + 1)

        def piece(off=off, size=size):
            cp = pltpu.make_async_copy(_tile_rows(src, r_src + off, size),
                                       _tile_rows(dst, r_dst + off, size), sem)
            if wait:
                cp.wait()
            else:
                cp.start()

        if isinstance(n, int):
            if n & size:
                piece()
        else:
            pl.when((n & size) != 0)(piece)


def _load_rows(ref3, rows=None):
    rows = ref3.shape[1] if rows is None else rows
    parts = []
    for s in range(SLABS):
        word = ref3[s, :rows, :]
        parts.append(lax.bitcast_convert_type(lax.shift_left(word, 16), F32))
        parts.append(lax.bitcast_convert_type(word & HIGH_HALF, F32))
    return jnp.concatenate(parts, axis=1).astype(BF16)


def _store_rows(ref3, val):
    rows = val.shape[0]
    bits = lax.bitcast_convert_type(val, I32)
    for s in range(SLABS):
        low = lax.shift_right_logical(bits[:, (2 * s) * LANES:(2 * s + 1) * LANES], 16)
        ref3[s, :rows, :] = bits[:, (2 * s + 1) * LANES:(2 * s + 2) * LANES] | low


def _dispatch_kernel(len_ref, pos_ref, row_ref, cnt_ref, first_ref, end_ref, lpt_ref, h2_ref, dst,
                     buf, zbuf, sem, zsem):
    n_rows = dst.shape[1]
    step = pl.program_id(0)
    slot = step % 2
    cur = buf.at[slot]
    lpt = lpt_ref[...]
    p = lax.broadcasted_iota(I32, (TM * TOPK, TM), 0)
    pick = lpt[0:1, :] == p
    for k in range(1, TOPK):
        pick = pick | (lpt[k:k + 1, :] == p)
    perm = jnp.where(pick, 1.0, 0.0).astype(BF16)
    _store_rows(cur, jnp.dot(perm, h2_ref[...], preferred_element_type=F32))

    def per_expert(e, carry):
        _run_copy(cur, dst, pos_ref[e], row_ref[e], len_ref[e], sem.at[slot], False)
        return carry

    lax.fori_loop(0, NE, per_expert, 0)

    def drain(s):
        pltpu.make_async_copy(buf.at[s], _tile_rows(dst, 0, TM * TOPK), sem.at[s]).wait()

    @pl.when(step > 0)
    def _():
        drain(1 - slot)

    @pl.when(step == pl.num_programs(0) - 1)
    def _():
        drain(slot)

    @pl.when(step == 0)
    def _():
        zbuf[...] = jnp.zeros(zbuf.shape, I32)
        n_tail = (n_rows - end_ref[NE - 1]) // TM
        for wait in (False, True):
            def pad(e, carry, wait=wait):
                lo = first_ref[e] + cnt_ref[e]
                _run_copy(zbuf, dst, 0, lo, end_ref[e] - lo, zsem, wait)
                return carry

            lax.fori_loop(0, NE, pad, 0)

            def tail(j, carry, wait=wait):
                _run_copy(zbuf, dst, 0, end_ref[NE - 1] + j * TM, TM, zsem, wait)
                return carry

            lax.fori_loop(0, n_tail, tail, 0)


def _dispatch(h2, lpt, run_len, run_pos, run_row, cnt, first, end, n_rows):
    assert BLK % TM == 0 and BLK < 2 * TM + 1
    n_tiles = h2.shape[0] // TM
    smem = pl.BlockSpec(memory_space=pltpu.SMEM)
    per_tile = pl.BlockSpec((LANES,), lambda t: (t,), memory_space=pltpu.SMEM)
    return pl.pallas_call(
        _dispatch_kernel, grid=(n_tiles,),
        in_specs=[per_tile, per_tile, per_tile, smem, smem, smem,
                  pl.BlockSpec((8, TM), lambda t: (t, 0)), _tile_spec(D, TM)],
        out_specs=pl.BlockSpec(memory_space=pl.ANY),
        out_shape=jax.ShapeDtypeStruct((SLABS, n_rows, LANES), I32),
        scratch_shapes=[pltpu.VMEM((2, SLABS, TM * TOPK, LANES), I32),
                        pltpu.VMEM((SLABS, BLK, LANES), I32),
                        pltpu.SemaphoreType.DMA((2,)), pltpu.SemaphoreType.DMA],
        compiler_params=_cparams(1), name="moe_dispatch",
    )(run_len, run_pos, run_row, cnt, first, end, lpt, h2)


def _expert_kernel(be_ref, nu_ref, nr_ref, nxt_ref, par_ref, xb_ref, wu_hbm, bu_ref, wd_hbm, bd_ref, yb_ref,
                   wuf, wdf, wub, wdb, sem, *, layer):
    i = pl.program_id(0)
    used = i < nu_ref[0]
    half = nr_ref[i] <= BLK // 2
    slot = par_ref[i]

    def weights(e, s):
        return (pltpu.make_async_copy(wu_hbm.at[layer, e], wuf.at[s], sem.at[s, 0]),
                pltpu.make_async_copy(wd_hbm.at[layer, e], wdf.at[s], sem.at[s, 1]))

    @pl.when(i == 0)
    def _():
        for cp in weights(be_ref[0], slot):
            cp.start()

    @pl.when(used & ((i == 0) | (be_ref[i] != be_ref[jnp.maximum(i - 1, 0)])))
    def _():
        for cp in weights(be_ref[i], slot):
            cp.wait()
        wub[...] = wuf[slot].astype(BF16)
        wdb[...] = wdf[slot].astype(BF16)

        @pl.when(nxt_ref[i] >= 0)
        def _():
            for cp in weights(nxt_ref[i], 1 - slot):
                cp.start()

    def swiglu_rows(rows):
        z = jnp.dot(_load_rows(xb_ref, rows), wub[...], preferred_element_type=F32) + bu_ref[0, 0]
        glu = jnp.minimum(z[:, :FE], LIMIT)
        lin = jnp.clip(z[:, FE:], -LIMIT, LIMIT)
        act = glu * jax.nn.sigmoid(ALPHA * glu) * (lin + 1.0)
        y = jnp.dot(act.astype(BF16), wdb[...], preferred_element_type=F32) + bd_ref[0, 0]
        _store_rows(yb_ref, y.astype(BF16).astype(F32))

    @pl.when(used & jnp.logical_not(half))
    def _():
        swiglu_rows(BLK)

    @pl.when(used & half)
    def _():
        swiglu_rows(BLK // 2)
        yb_ref[:, BLK // 2:, :] = jnp.zeros((SLABS, BLK // 2, LANES), I32)

    @pl.when(jnp.logical_not(used))
    def _():
        yb_ref[...] = jnp.zeros(yb_ref.shape, I32)


def _experts(layer, xb, blk_e, n_used, blk_rows, blk_next, blk_par, wu, bu, wd, bd):
    n_blocks = blk_e.shape[0]
    grid_spec = pltpu.PrefetchScalarGridSpec(
        num_scalar_prefetch=5, grid=(n_blocks,),
        in_specs=[pl.BlockSpec((SLABS, BLK, LANES), lambda i, be, nu, *_: (0, jnp.minimum(i, nu[0] - 1), 0)),
                  pl.BlockSpec(memory_space=pl.ANY),
                  pl.BlockSpec((1, 1, 1, 2 * FE), lambda i, be, *_: (layer, be[i], 0, 0)),
                  pl.BlockSpec(memory_space=pl.ANY),
                  pl.BlockSpec((1, 1, 1, D), lambda i, be, *_: (layer, be[i], 0, 0))],
        out_specs=pl.BlockSpec((SLABS, BLK, LANES), lambda i, *_: (0, i, 0)),
        scratch_shapes=[pltpu.VMEM((2, D, 2 * FE), F32), pltpu.VMEM((2, FE, D), F32),
                        pltpu.VMEM((D, 2 * FE), BF16), pltpu.VMEM((FE, D), BF16),
                        pltpu.SemaphoreType.DMA((2, 2))])
    return pl.pallas_call(
        functools.partial(_expert_kernel, layer=layer), grid_spec=grid_spec,
        out_shape=jax.ShapeDtypeStruct((SLABS, n_blocks * BLK, LANES), I32),
        compiler_params=_cparams(1), name="experts",
    )(blk_e, n_used, blk_rows, blk_next, blk_par, xb, wu, bu, wd, bd)


def _combine_kernel(len_ref, pos_ref, row_ref, len_nx, pos_nx, row_nx, yb, lp_ref, x_ref, gate_ref,
                    g2p, g2s, fg_ref, *outs_and_scratch, final):
    *o_ref, ybuf, sem = outs_and_scratch
    step = pl.program_id(0)
    slot = step % 2
    is_p = step < T_P // TM

    def fetch(len_r, pos_r, row_r, s):
        def per_expert(e, carry):
            _run_copy(yb, ybuf.at[s], row_r[e], pos_r[e], len_r[e], sem.at[s], False)
            return carry

        lax.fori_loop(0, NE, per_expert, 0)

    @pl.when(step == 0)
    def _():
        fetch(len_ref, pos_ref, row_ref, slot)

    @pl.when(step + 1 < pl.num_programs(0))
    def _():
        fetch(len_nx, pos_nx, row_nx, 1 - slot)

    cur = ybuf.at[slot]
    pltpu.make_async_copy(_tile_rows(yb, 0, TM * TOPK), cur, sem.at[slot]).wait()

    rows = x_ref.shape[0]
    lp = lp_ref[...]
    gates = gate_ref[...]
    p = lax.broadcasted_iota(I32, (rows, rows * TOPK), 1)
    weights = jnp.zeros((rows, rows * TOPK), F32)
    for k in range(TOPK):
        weights = jnp.where(lp[:, k:k + 1] == p, gates[:, k:k + 1], weights)
    acc = jnp.dot(weights.astype(BF16), _load_rows(cur), preferred_element_type=F32)
    xn = x_ref[...] + _pick(is_p, g2p, g2s) * acc
    if not final:
        o_ref[0][...] = xn
        return
    y = _rms(xn, fg_ref[...])
    op_ref, os_ref = o_ref

    @pl.when(is_p)
    def _():
        op_ref[...] = y

    @pl.when(jnp.logical_not(is_p))
    def _():
        os_ref[...] = y


def _combine(x, yb, lp, run_len, run_pos, run_row, gates, mod_p, mod_s, final_g, final):
    n_tiles = x.shape[0] // TM
    n_prompt = T_P // TM
    if final:
        out_specs = [pl.BlockSpec((TM, D), lambda t: (jnp.minimum(t, n_prompt - 1), 0)),
                     pl.BlockSpec((TM, D), lambda t: (jnp.maximum(t - n_prompt, 0), 0))]
        out_shape = [jax.ShapeDtypeStruct((T_P, D), F32), jax.ShapeDtypeStruct((x.shape[0] - T_P, D), F32)]
    else:
        out_specs, out_shape = _tile_spec(D, TM), jax.ShapeDtypeStruct(x.shape, F32)
    per_tile = pl.BlockSpec((LANES,), lambda t: (t,), memory_space=pltpu.SMEM)
    next_tile = pl.BlockSpec((LANES,), lambda t: (jnp.minimum(t + 1, n_tiles - 1),),
                             memory_space=pltpu.SMEM)
    in_specs = ([per_tile, per_tile, per_tile, next_tile, next_tile, next_tile,
                 pl.BlockSpec(memory_space=pl.ANY),
                 _tile_spec(LANES, TM), _tile_spec(D, TM), _tile_spec(LANES, TM)]
                + _mod_specs(5, TM) + [_full_spec((1, D))])
    return pl.pallas_call(
        functools.partial(_combine_kernel, final=final), grid=(n_tiles,), in_specs=in_specs,
        out_specs=out_specs, out_shape=out_shape,
        scratch_shapes=[pltpu.VMEM((2, SLABS, TM * TOPK, LANES), I32), pltpu.SemaphoreType.DMA((2,))],
        compiler_params=_cparams(1), name="moe_combine",
    )(run_len, run_pos, run_row, run_len, run_pos, run_row, yb, lp, x, gates, mod_p, mod_s, final_g)


def _block_tables(cnt, first, end, n_blocks):
    experts = jnp.arange(NE, dtype=I32)
    blk_first = jnp.arange(n_blocks, dtype=I32) * BLK
    blk_e = jnp.minimum(jnp.sum((end[None, :] <= blk_first[:, None]).astype(I32), axis=1), NE - 1)
    n_used = end[NE - 1:] // BLK
    has_rows = cnt > 0
    later = has_rows[None, :] & (experts[None, :] > experts[:, None])
    next_e = jnp.min(jnp.where(later, experts[None, :], NE), axis=1)
    next_e = jnp.where(next_e == NE, -1, next_e)
    parity = (jnp.cumsum(has_rows.astype(I32)) - 1) % 2
    own = blk_e[:, None] == experts[None, :]
    of_block = lambda per_expert: jnp.sum(jnp.where(own, per_expert[None, :], 0), axis=1)
    blk_rows = jnp.clip(of_block(first + cnt) - blk_first, 0, BLK)
    return blk_e, n_used, blk_rows, of_block(next_e), of_block(parity)


def _moe(layer, x, h2, idx, gates, cnt_all, mod_p, mod_s, wu, bu, wd, bd, final_g, final):
    lp, lpt, runs, meta = _route(cnt_all, idx)
    cnt, first, end = meta[0, :NE], meta[1, :NE], meta[2, :NE]
    run_len, run_pos, run_row = (runs[:, r, :].reshape(-1) for r in range(3))
    xb = _dispatch(h2, lpt, run_len, run_pos, run_row, cnt, first, end, N_ROWS)
    yb = _experts(layer, xb, *_block_tables(cnt, first, end, N_BLOCKS), wu, bu, wd, bd)
    return _combine(x, yb, lp, run_len, run_pos, run_row, gates, mod_p, mod_s, final_g, final)


def _rope_tables():
    half = DK // 2
    inv = 1.0 / (ROPE_BASE ** jnp.linspace(0.0, 1.0, half, dtype=F32))

    def tab(pos):
        ang = pos.astype(F32)[:, None] * inv[None, :]
        cos, sin = jnp.cos(ang), jnp.sin(ang)
        return jnp.concatenate([cos, cos], -1), jnp.concatenate([-sin, sin], -1)

    cp, sp = tab(jnp.arange(L_P, dtype=I32))
    cs, ss = tab(PAST + jnp.arange(L_S, dtype=I32))
    cos = jnp.concatenate([jnp.tile(cp, (NB_P, 1)), jnp.tile(cs, (NB_S, 1))], 0)
    sin = jnp.concatenate([jnp.tile(sp, (NB_P, 1)), jnp.tile(ss, (NB_S, 1))], 0)
    return cos, sin


def _decay_tables(cl):
    lg = jnp.log(1.0 - 2.0 ** (-5.0 - jnp.arange(H, dtype=F32)))
    r = jnp.arange(CHUNK)
    idx = (r % cl).astype(F32)
    diff = idx[:, None] - idx[None, :]
    same = (r[:, None] // cl) == (r[None, :] // cl)
    decay = jnp.where((same & (diff >= 0))[None],
                      jnp.exp(lg[:, None, None] * jnp.maximum(diff, 0.0)[None]), 0.0)
    qdec = jnp.exp(lg[None, :] * (idx[:, None] + 1.0))
    kdec = jnp.exp(lg[None, :] * (cl - 1.0 - idx[:, None]))
    wide = lambda a: jnp.repeat(a, DK, axis=1)
    return {"decay": decay, "qdec": wide(qdec), "kdec": wide(kdec), "gam": jnp.exp(lg * cl)}


def kernel(x_prompt, x_sample, c_prompt, c_sample, state_ret, w_mod, b_mod, norm1_g, norm2_g,
           sgu_w_in, sgu_ln_g, sgu_ln_b, sgu_w_s, sgu_b_s, sgu_w_out, ret_w_in, ret_norm_g, ret_w_out,
           moe_w_router, moe_b_router, moe_w_up, moe_b_up, moe_w_down, moe_b_down, final_g):
    x = jnp.concatenate([x_prompt.reshape(T_P, D), x_sample.reshape(T_S, D)], 0)
    mod = _modulation(jnp.concatenate([c_prompt, c_sample], 0), w_mod, b_mod)
    cos_tab, sin_tab = _rope_tables()
    tabs_p = _decay_tables(CHUNK)
    tabs_s = _decay_tables(L_S)
    wr_pad = jnp.pad(moe_w_router, ((0, 0), (0, 0), (0, LANES - NE))).astype(BF16)
    br_pad = jnp.pad(moe_b_router, ((0, 0), (0, LANES - NE)), constant_values=-1e30)
    fg = final_g.reshape(1, D)
    b_up = moe_b_up.reshape(DEPTH, NE, 1, 2 * FE)
    b_down = moe_b_down.reshape(DEPTH, NE, 1, D)

    ret_p, v_rows = [], []
    s_all = lax.empty(state_ret.shape, F32)
    for i in range(DEPTH):
        j = i // 2
        mod_p = mod[i, :NB_P].reshape(NB_P, 1, 6 * D)
        mod_s = jnp.repeat(mod[i, NB_P:], L_S, axis=0)
        n1g = norm1_g[i].reshape(1, D)
        n2g = norm2_g[i].reshape(1, D)
        wr = wr_pad[i]
        br = br_pad[i].reshape(1, LANES)
        if i % 2 == 0:
            mixw = jnp.stack([sgu_w_s[j], jnp.tile(sgu_w_s[j][:, :L_S, :L_S], (1, SB, SB))])
            bias_p = jnp.repeat(sgu_b_s[j].T, SGU_GD, axis=1)
            bias_s = jnp.tile(bias_p[:L_S], (SB, 1))
            x, h2, idx, gates, cnt, v = _sgu_layer(
                x, mod_p, mod_s, n1g, n2g, sgu_w_in[j].astype(BF16), sgu_ln_g[j].reshape(1, SGU_W),
                sgu_ln_b[j].reshape(1, SGU_W), mixw, jnp.stack([bias_p, bias_s]),
                sgu_w_out[j].astype(BF16), wr, br)
            v_rows.append(v.reshape(NB_S, L_S, SGU_W))
        else:
            q, k, v, sg = _ret_proj(x, mod_p, mod_s, n1g, ret_w_in[j].astype(BF16), cos_tab, sin_tab)
            ng = ret_norm_g[j].reshape(1, VD)
            y, s_p = _ret_prompt(q, k, v, sg, tabs_p, ng)
            y, s_all = _ret_sample(j, q, k, v, sg, tabs_s, ng, state_ret, y, s_all)
            ret_p.append(s_p)
            x, h2, idx, gates, cnt = _ret_out(y, ret_w_out[j].astype(BF16), x, mod_p, mod_s, n2g, wr, br)
        x = _moe(i, x, h2, idx, gates, cnt, mod_p, mod_s, moe_w_up, b_up, moe_w_down, b_down,
                 fg, final=(i == DEPTH - 1))
    y_prompt, y_sample = x
    return (y_prompt.reshape(NB_P, L_P, D), y_sample.reshape(NB_S, L_S, D), jnp.stack(ret_p), s_all,
            jnp.stack(v_rows))
```

```python
import functools

import jax
import jax.numpy as jnp
from jax import lax
from jax.experimental import pallas as pl
from jax.experimental.pallas import tpu as pltpu

F32 = jnp.float32
BF16 = jnp.bfloat16
I32 = jnp.int32

D = 1024
NB_P, L_P = 8, 2048
NB_S, L_S = 128, 8
PAST = 16384
DEPTH = 4
T_P = NB_P * L_P
T_S = NB_S * L_S
T = T_P + T_S
SGU_W = 2 * D
SGU_G = 8
SGU_GD = SGU_W // SGU_G
CHUNK = 128
H = 8
DK = D // H
DV = 2 * DK
QD = H * DK
VD = H * DV
RET_IN = 2 * QD + 2 * VD
ROPE_BASE = 10000.0
NE = 32
TOPK = 4
FE = D
ALPHA = 1.702
LIMIT = 7.0
EPS = 1e-6

LANES = 128
SLABS = D // (2 * LANES)
HIGH_HALF = -65536
TD = 512
NPT = T_P // TD
NT = T // TD
TM = 256
BLK = 512
TK = T * TOPK
N_BLOCKS = -(-(TK + NE * (BLK - 1)) // BLK)
N_ROWS = N_BLOCKS * BLK
SB = 16
ROUTE_TILES = 4
SH = 2
RET_CHUNKS = 4
VMEM_LIMIT = 56 * 1024 * 1024


def _cparams(n_axes):
    return pltpu.CompilerParams(dimension_semantics=("arbitrary",) * n_axes,
                                vmem_limit_bytes=VMEM_LIMIT)


def _rms(x, g):
    return (x * lax.rsqrt(jnp.mean(x * x, axis=-1, keepdims=True) + EPS)) * g


def _pick(is_p, p_ref, s_ref):
    return jnp.where(is_p, p_ref[0], s_ref[...])


def _mod_specs(j, tm=TD):
    per_batch, n_prompt = L_P // tm, T_P // tm
    return [
        pl.BlockSpec((1, 1, D), lambda t: (jnp.minimum(t // per_batch, NB_P - 1), 0, j)),
        pl.BlockSpec((tm, D), lambda t: (jnp.maximum(t - n_prompt, 0), j), pipeline_mode=pl.Buffered(1)),
    ]


def _tile_spec(width, tm=TD):
    return pl.BlockSpec((tm, width), lambda t: (t, 0))


def _full_spec(shape):
    return pl.BlockSpec(shape, lambda *_: (0,) * len(shape), pipeline_mode=pl.Buffered(1))


def _mod_kernel(c_ref, w_ref, b_ref, o_ref):
    c = c_ref[...]
    cs = (c * jax.nn.sigmoid(c)).astype(BF16)
    o_ref[0] = jnp.dot(cs, w_ref[0].astype(BF16), preferred_element_type=F32) + b_ref[0]


def _modulation(c_all, w_mod, b_mod):
    tn = 1536
    nb = c_all.shape[0]
    return pl.pallas_call(
        _mod_kernel,
        grid=(DEPTH, 6 * D // tn),
        in_specs=[
            pl.BlockSpec((nb, D), lambda l, n: (0, 0)),
            pl.BlockSpec((1, D, tn), lambda l, n: (l, 0, n)),
            pl.BlockSpec((1, 1, tn), lambda l, n: (l, 0, n)),
        ],
        out_specs=pl.BlockSpec((1, nb, tn), lambda l, n: (l, 0, n)),
        out_shape=jax.ShapeDtypeStruct((DEPTH, nb, 6 * D), F32),
        compiler_params=_cparams(2),
        name="modulation",
    )(c_all, w_mod, b_mod.reshape(DEPTH, 1, 6 * D))


def _tail(is_p, x, y, g1p, g1s, sh2p, sh2s, sc2p, sc2s, n2g, wr, br, xo_ref, h2_ref, idx_ref, gate_ref,
          cnt_ref):
    xn = x + _pick(is_p, g1p, g1s) * y
    xo_ref[...] = xn
    h2 = (_rms(xn, n2g[...]) * (1.0 + _pick(is_p, sc2p, sc2s)) + _pick(is_p, sh2p, sh2s)).astype(BF16)
    h2_ref[...] = h2
    logit = jnp.dot(h2, wr[...], preferred_element_type=F32) + br[...]
    lane = lax.broadcasted_iota(I32, logit.shape, 1)
    vals, ids = [], []
    for _ in range(TOPK):
        m = jnp.max(logit, axis=-1, keepdims=True)
        sel = jnp.min(jnp.where(logit == m, lane, LANES), axis=-1, keepdims=True)
        vals.append(m)
        ids.append(sel)
        logit = jnp.where(lane == sel, -jnp.inf, logit)
    es = [jnp.exp(v - vals[0]) for v in vals]
    tot = (es[0] + es[1]) + (es[2] + es[3])
    idx_out = jnp.zeros(logit.shape, I32)
    gate_out = jnp.zeros(logit.shape, F32)
    chosen = jnp.zeros(logit.shape, F32)
    for k in range(TOPK):
        idx_out = jnp.where(lane == k, ids[k], idx_out)
        gate_out = jnp.where(lane == k, es[k] / tot, gate_out)
        chosen = chosen + jnp.where(lane == ids[k], 1.0, 0.0)
    idx_ref[...] = idx_out
    gate_ref[...] = gate_out

    @pl.when(pl.program_id(0) == 0)
    def _():
        cnt_ref[...] = jnp.zeros(cnt_ref.shape, F32)

    cnt_ref[...] += jnp.sum(chosen, axis=0, keepdims=True)


def _tail_in_specs():
    return (_mod_specs(2) + _mod_specs(3) + _mod_specs(4)
            + [_full_spec((1, D)), _full_spec((D, LANES)), _full_spec((1, LANES))])


def _tail_out_specs():
    return [_tile_spec(D), _tile_spec(D), _tile_spec(LANES), _tile_spec(LANES),
            pl.BlockSpec((1, LANES), lambda t: (0, 0))]


def _tail_out_shapes():
    return [jax.ShapeDtypeStruct((T, D), F32), jax.ShapeDtypeStruct((T, D), BF16),
            jax.ShapeDtypeStruct((T, LANES), I32), jax.ShapeDtypeStruct((T, LANES), F32),
            jax.ShapeDtypeStruct((1, LANES), F32)]


def _sgu_kernel(x_ref, sh1p, sh1s, sc1p, sc1s, n1g, win, lng, lnb, mixw, mixb, wout,
                g1p, g1s, sh2p, sh2s, sc2p, sc2s, n2g, wr, br,
                xo_ref, h2_ref, idx_ref, gate_ref, cnt_ref, v_ref, y_scr):
    t = pl.program_id(0)
    is_p = t < NPT
    x = x_ref[...]
    h = _rms(x, n1g[...]) * (1.0 + _pick(is_p, sc1p, sc1s)) + _pick(is_p, sh1p, sh1s)
    z = jnp.dot(h.astype(BF16), win[...], preferred_element_type=F32)
    z = 0.5 * z * (1.0 + lax.erf(z * (0.5 ** 0.5)))
    u = z[:, :SGU_W]
    v = z[:, SGU_W:]
    vc = v - jnp.mean(v, axis=-1, keepdims=True)
    vn = vc * lax.rsqrt(jnp.mean(vc * vc, axis=-1, keepdims=True) + EPS) * lng[...] + lnb[...]

    @pl.when(t >= NPT)
    def _():
        v_ref[...] = vn

    vb = vn.astype(BF16)
    ri = lax.broadcasted_iota(I32, (CHUNK, CHUNK), 0)
    ci = lax.broadcasted_iota(I32, (CHUNK, CHUNK), 1)
    causal = ci <= ri
    shift = jnp.broadcast_to(jnp.where(is_p, 7, 3), ri.shape)
    keep = causal & (lax.shift_right_logical(ri, shift) == lax.shift_right_logical(ci, shift))
    for g in range(SGU_G):
        wg = jnp.where(keep, mixw[0, g], 0.0).astype(BF16)
        for c in range(TD // CHUNK):
            rows = slice(c * CHUNK, (c + 1) * CHUNK)
            cols = slice(g * SGU_GD, (g + 1) * SGU_GD)
            mixed = jnp.dot(wg, vb[rows, cols], preferred_element_type=F32) + mixb[0, :, cols]
            y_scr[rows, cols] = (u[rows, cols] * mixed).astype(BF16)
    y = jnp.dot(y_scr[...], wout[...], preferred_element_type=F32)
    _tail(is_p, x, y, g1p, g1s, sh2p, sh2s, sc2p, sc2s, n2g, wr, br, xo_ref, h2_ref, idx_ref, gate_ref,
          cnt_ref)


def _sgu_layer(x, mod_p, mod_s, n1g, n2g, win, lng, lnb, mixw, mixb, wout, wr, br):
    sel = lambda t: (jnp.where(t < NPT, 0, 1), 0, 0, 0)
    in_specs = ([_tile_spec(D)] + _mod_specs(0) + _mod_specs(1)
                + [_full_spec((1, D)), _full_spec((D, 2 * SGU_W)), _full_spec((1, SGU_W)),
                   _full_spec((1, SGU_W)),
                   pl.BlockSpec((1, SGU_G, CHUNK, CHUNK), sel),
                   pl.BlockSpec((1, CHUNK, SGU_W), lambda t: (jnp.where(t < NPT, 0, 1), 0, 0)),
                   _full_spec((SGU_W, D))]
                + _tail_in_specs())
    out_specs = _tail_out_specs() + [pl.BlockSpec((TD, SGU_W), lambda t: (jnp.maximum(t - NPT, 0), 0))]
    out_shape = _tail_out_shapes() + [jax.ShapeDtypeStruct((T_S, SGU_W), F32)]
    return pl.pallas_call(
        _sgu_kernel, grid=(NT,), in_specs=in_specs, out_specs=out_specs, out_shape=out_shape,
        scratch_shapes=[pltpu.VMEM((TD, SGU_W), BF16)],
        compiler_params=_cparams(1), name="sgu_layer",
    )(x, mod_p, mod_s, mod_p, mod_s, n1g, win, lng, lnb, mixw, mixb, wout,
      mod_p, mod_s, mod_p, mod_s, mod_p, mod_s, n2g, wr, br)


def _ret_proj_kernel(x_ref, sh1p, sh1s, sc1p, sc1s, n1g, win, cos_ref, sin_ref,
                     q_ref, k_ref, v_ref, sg_ref):
    t = pl.program_id(0)
    is_p = t < NPT
    x = x_ref[...]
    h = _rms(x, n1g[...]) * (1.0 + _pick(is_p, sc1p, sc1s)) + _pick(is_p, sh1p, sh1s)
    p = jnp.dot(h.astype(BF16), win[...], preferred_element_type=F32)
    cos = cos_ref[...]
    sin = sin_ref[...]
    for hd in range(H):
        cq = slice(hd * DK, (hd + 1) * DK)
        ck = slice(QD + hd * DK, QD + (hd + 1) * DK)
        qh = p[:, cq]
        kh = p[:, ck]
        q_ref[:, cq] = (qh * cos + pltpu.roll(qh, DK // 2, 1) * sin).astype(BF16)
        k_ref[:, cq] = ((kh * cos + pltpu.roll(kh, DK // 2, 1) * sin) * (DK ** -0.5)).astype(BF16)
    v_ref[...] = p[:, 2 * QD:2 * QD + VD].astype(BF16)
    g = p[:, 2 * QD + VD:]
    sg_ref[...] = (g * jax.nn.sigmoid(g)).astype(BF16)


def _ret_proj(x, mod_p, mod_s, n1g, win, cos_tab, sin_tab):
    in_specs = ([_tile_spec(D)] + _mod_specs(0) + _mod_specs(1)
                + [_full_spec((1, D)), _full_spec((D, RET_IN)), _tile_spec(DK), _tile_spec(DK)])
    out_specs = [_tile_spec(QD), _tile_spec(QD), _tile_spec(VD), _tile_spec(VD)]
    out_shape = [jax.ShapeDtypeStruct((T, QD), BF16), jax.ShapeDtypeStruct((T, QD), BF16),
                 jax.ShapeDtypeStruct((T, VD), BF16), jax.ShapeDtypeStruct((T, VD), BF16)]
    return pl.pallas_call(
        _ret_proj_kernel, grid=(NT,), in_specs=in_specs, out_specs=out_specs, out_shape=out_shape,
        compiler_params=_cparams(1), name="ret_proj",
    )(x, mod_p, mod_s, mod_p, mod_s, n1g, win, cos_tab, sin_tab)


def _ret_intra(q, k, v, decay, qdec, kdec):
    s = lax.dot_general(q, k, (((1,), (1,)), ((), ())), preferred_element_type=F32) * decay
    o = jnp.dot(s.astype(BF16), v, preferred_element_type=F32)
    return o, q.astype(F32) * qdec, k.astype(F32) * kdec


def _ret_finish(o, sg, ng):
    on = o * lax.rsqrt(jnp.mean(o * o, axis=-1, keepdims=True) + EPS)
    return (sg.astype(F32) * (on * ng)).astype(BF16)


def _ret_prompt_kernel(gam_ref, q_ref, k_ref, v_ref, sg_ref, dec_ref, qdec_ref, kdec_ref, ng_ref,
                       y_ref, s_ref):
    c = pl.program_id(1)

    @pl.when(c == 0)
    def _():
        s_ref[...] = jnp.zeros(s_ref.shape, F32)

    cks = [slice(hd * DK, (hd + 1) * DK) for hd in range(H)]
    cvs = [slice(hd * DV, (hd + 1) * DV) for hd in range(H)]
    for j in range(RET_CHUNKS):
        rows = slice(j * CHUNK, (j + 1) * CHUNK)
        vs = [v_ref[rows, cv] for cv in cvs]
        intra = [_ret_intra(q_ref[rows, ck], k_ref[rows, ck], v, dec_ref[hd], qdec_ref[:, ck], kdec_ref[:, ck])
                 for hd, (ck, v) in enumerate(zip(cks, vs))]
        olds = [s_ref[0, hd] for hd in range(H)]
        outs = [o + jnp.dot(qd.astype(BF16), s_old.astype(BF16), preferred_element_type=F32)
                for (o, qd, _), s_old in zip(intra, olds)]
        for hd in range(H):
            s_ref[0, hd] = gam_ref[hd] * olds[hd] + lax.dot_general(
                intra[hd][2].astype(BF16), vs[hd], (((0,), (0,)), ((), ())), preferred_element_type=F32)
        for hd, cv in enumerate(cvs):
            y_ref[rows, cv] = _ret_finish(outs[hd], sg_ref[rows, cv], ng_ref[:, cv])


def _ret_prompt(q, k, v, sg, tabs, ng):
    rows = RET_CHUNKS * CHUNK
    nc = L_P // rows
    row = lambda b, c: (b * nc + c, 0)
    smem = pl.BlockSpec(memory_space=pltpu.SMEM)
    in_specs = [smem,
                pl.BlockSpec((rows, QD), row), pl.BlockSpec((rows, QD), row),
                pl.BlockSpec((rows, VD), row), pl.BlockSpec((rows, VD), row),
                _full_spec((H, CHUNK, CHUNK)), _full_spec((CHUNK, QD)), _full_spec((CHUNK, QD)),
                _full_spec((1, VD))]
    out_specs = [pl.BlockSpec((rows, VD), row),
                 pl.BlockSpec((1, H, DK, DV), lambda b, c: (b, 0, 0, 0))]
    out_shape = [jax.ShapeDtypeStruct((T, VD), BF16), jax.ShapeDtypeStruct((NB_P, H, DK, DV), F32)]
    return pl.pallas_call(
        _ret_prompt_kernel, grid=(NB_P, nc), in_specs=in_specs, out_specs=out_specs, out_shape=out_shape,
        compiler_params=_cparams(2), name="ret_prompt",
    )(tabs["gam"], q, k, v, sg, tabs["decay"], tabs["qdec"], tabs["kdec"], ng)


def _ret_sample_kernel(gam_ref, q_ref, k_ref, v_ref, sg_ref, dec_ref, qdec_ref, kdec_ref, ng_ref,
                       s_in_ref, y_in_ref, s_all_ref, y_ref, s_out_ref):
    del y_in_ref, s_all_ref
    s_in_ref = s_in_ref.at[0]
    s_out_ref = s_out_ref.at[0]
    row_batch = lax.broadcasted_iota(I32, (CHUNK, DK), 0) // L_S
    for i in range(SH):
        ck = slice(i * DK, (i + 1) * DK)
        cv = slice(i * DV, (i + 1) * DV)
        v = v_ref[:, cv]
        o, qd, kd = _ret_intra(q_ref[:, ck], k_ref[:, ck], v, dec_ref[i], qdec_ref[:, ck], kdec_ref[:, ck])
        gam = gam_ref[pl.program_id(1) * SH + i]
        cross = [jnp.dot(qd[b * L_S:(b + 1) * L_S].astype(BF16), s_in_ref[b, i].astype(BF16),
                         preferred_element_type=F32) for b in range(SB)]
        for b in range(SB):
            kb = jnp.where(row_batch == b, kd, 0.0).astype(BF16)
            s_out_ref[b, i] = gam * s_in_ref[b, i] + lax.dot_general(
                kb, v, (((0,), (0,)), ((), ())), preferred_element_type=F32)
        o = o + jnp.concatenate(cross, axis=0)
        y_ref[:, cv] = _ret_finish(o, sg_ref[:, cv], ng_ref[:, cv])


def _ret_sample(j, q, k, v, sg, tabs, ng, s_in, y_prev, s_all):
    base = T_P // CHUNK
    rq = lambda g, h: (base + g, h)
    st = pl.BlockSpec((1, SB, SH, DK, DV), lambda g, h: (j, g, h, 0, 0))
    smem = pl.BlockSpec(memory_space=pltpu.SMEM)
    in_specs = [smem,
                pl.BlockSpec((CHUNK, SH * DK), rq), pl.BlockSpec((CHUNK, SH * DK), rq),
                pl.BlockSpec((CHUNK, SH * DV), rq), pl.BlockSpec((CHUNK, SH * DV), rq),
                pl.BlockSpec((SH, CHUNK, CHUNK), lambda g, h: (h, 0, 0)),
                pl.BlockSpec((CHUNK, SH * DK), lambda g, h: (0, h)),
                pl.BlockSpec((CHUNK, SH * DK), lambda g, h: (0, h)),
                pl.BlockSpec((1, SH * DV), lambda g, h: (0, h)),
                st, pl.BlockSpec(memory_space=pl.ANY), pl.BlockSpec(memory_space=pl.ANY)]
    out_specs = [pl.BlockSpec((CHUNK, SH * DV), rq), st]
    out_shape = [jax.ShapeDtypeStruct((T, VD), BF16), jax.ShapeDtypeStruct(s_all.shape, F32)]
    return pl.pallas_call(
        _ret_sample_kernel, grid=(NB_S // SB, H // SH), in_specs=in_specs, out_specs=out_specs,
        out_shape=out_shape, input_output_aliases={10: 0, 11: 1},
        compiler_params=_cparams(2), name="ret_sample",
    )(tabs["gam"], q, k, v, sg, tabs["decay"], tabs["qdec"], tabs["kdec"], ng, s_in, y_prev, s_all)


def _ret_out_kernel(y_ref, wout, x_ref, g1p, g1s, sh2p, sh2s, sc2p, sc2s, n2g, wr, br,
                    xo_ref, h2_ref, idx_ref, gate_ref, cnt_ref):
    is_p = pl.program_id(0) < NPT
    y = jnp.dot(y_ref[...], wout[...], preferred_element_type=F32)
    _tail(is_p, x_ref[...], y, g1p, g1s, sh2p, sh2s, sc2p, sc2s, n2g, wr, br,
          xo_ref, h2_ref, idx_ref, gate_ref, cnt_ref)


def _ret_out(y, wout, x, mod_p, mod_s, n2g, wr, br):
    in_specs = [_tile_spec(VD), _full_spec((VD, D)), _tile_spec(D)] + _tail_in_specs()
    return pl.pallas_call(
        _ret_out_kernel, grid=(NT,), in_specs=in_specs, out_specs=_tail_out_specs(),
        out_shape=_tail_out_shapes(), compiler_params=_cparams(1), name="ret_out",
    )(y, wout, x, mod_p, mod_s, mod_p, mod_s, mod_p, mod_s, n2g, wr, br)


def _lane_prefix(v, lane1):
    s = 1
    while s < LANES:
        v = v + jnp.where(lane1 >= s, pltpu.roll(v, s, 1), 0.0)
        s *= 2
    return v


def _route_kernel(cnt_ref, idx_ref, lp_ref, lpt_ref, runs_ref, meta_ref, base_scr):
    lane = lax.broadcasted_iota(I32, (TM, LANES), 1)
    lane1 = lax.broadcasted_iota(I32, (1, LANES), 1)
    row = lax.broadcasted_iota(I32, (8, LANES), 0)

    @pl.when(pl.program_id(0) == 0)
    def _():
        cnt = cnt_ref[...]
        padded = (((cnt.astype(I32) + (BLK - 1)) // BLK) * BLK).astype(F32)
        end = _lane_prefix(padded, lane1)
        base_scr[...] = end - padded
        meta = jnp.where(row == 0, cnt, jnp.where(row == 1, end - padded, jnp.where(row == 2, end, 0.0)))
        meta_ref[...] = meta.astype(I32)

    ri = lax.broadcasted_iota(I32, (TM, TM), 0)
    ci = lax.broadcasted_iota(I32, (TM, TM), 1)
    before = jnp.where(ci < ri, 1.0, 0.0).astype(BF16)
    base = base_scr[...]
    for j in range(ROUTE_TILES):
        idx = idx_ref[j * TM:(j + 1) * TM, :]
        hits = [lane == idx[:, k:k + 1] for k in range(TOPK)]
        chosen = jnp.zeros((TM, LANES), F32)
        for hk in hits:
            chosen = chosen + jnp.where(hk, 1.0, 0.0)
        colsum = jnp.sum(chosen, axis=0, keepdims=True)
        loff = _lane_prefix(colsum, lane1) - colsum
        pos = jnp.dot(before, chosen.astype(BF16), preferred_element_type=F32) + loff
        lp = jnp.zeros((TM, LANES), F32)
        for k, hk in enumerate(hits):
            lp = jnp.where(lane == k, jnp.sum(jnp.where(hk, pos, 0.0), axis=-1, keepdims=True), lp)
        lp_ref[j * TM:(j + 1) * TM, :] = lp.astype(I32)
        lpt_ref[j * 8:(j + 1) * 8, :] = lp.T[:8].astype(I32)
        runs = jnp.where(row == 0, colsum, jnp.where(row == 1, loff, jnp.where(row == 2, base, 0.0)))
        runs_ref[j] = runs.astype(I32)
        base = base + colsum
    base_scr[...] = base


def _route(cnt, idx):
    n_tiles = idx.shape[0] // TM
    assert n_tiles % ROUTE_TILES == 0
    rt = ROUTE_TILES
    return pl.pallas_call(
        _route_kernel, grid=(n_tiles // rt,),
        in_specs=[pl.BlockSpec((1, LANES), lambda t: (0, 0)), pl.BlockSpec((rt * TM, LANES), lambda t: (t, 0))],
        out_specs=[pl.BlockSpec((rt * TM, LANES), lambda t: (t, 0)),
                   pl.BlockSpec((rt * 8, TM), lambda t: (t, 0)),
                   pl.BlockSpec((rt, 8, LANES), lambda t: (t, 0, 0)),
                   pl.BlockSpec((8, LANES), lambda t: (0, 0))],
        out_shape=[jax.ShapeDtypeStruct((n_tiles * TM, LANES), I32),
                   jax.ShapeDtypeStruct((n_tiles * 8, TM), I32),
                   jax.ShapeDtypeStruct((n_tiles, 8, LANES), I32),
                   jax.ShapeDtypeStruct((8, LANES), I32)],
        scratch_shapes=[pltpu.VMEM((1, LANES), F32)],
        compiler_params=_cparams(1), name="moe_route",
    )(cnt, idx)


def _tile_rows(ref, r, n):
    return ref.at[:, pl.ds(r, n), :]


def _run_copy(src, dst, r_src, r_dst, n, sem, wait):
    for b in reversed(range(TM.bit_length())):
        size = 1 << b
        off = (n >> (b + 1)) << (b + 1)

        def piece(off=off, size=size):
            cp = pltpu.make_async_copy(_tile_rows(src, r_src + off, size),
                                       _tile_rows(dst, r_dst + off, size), sem)
            if wait:
                cp.wait()
            else:
                cp.start()

        if isinstance(n, int):
            if n & size:
                piece()
        else:
            pl.when((n & size) != 0)(piece)


def _load_rows(ref3, rows=None):
    rows = ref3.shape[1] if rows is None else rows
    parts = []
    for s in range(SLABS):
        word = ref3[s, :rows, :]
        parts.append(lax.bitcast_convert_type(lax.shift_left(word, 16), F32))
        parts.append(lax.bitcast_convert_type(word & HIGH_HALF, F32))
    return jnp.concatenate(parts, axis=1).astype(BF16)


def _store_rows(ref3, val):
    rows = val.shape[0]
    bits = lax.bitcast_convert_type(val, I32)
    for s in range(SLABS):
        low = lax.shift_right_logical(bits[:, (2 * s) * LANES:(2 * s + 1) * LANES], 16)
        ref3[s, :rows, :] = bits[:, (2 * s + 1) * LANES:(2 * s + 2) * LANES] | low


def _dispatch_kernel(len_ref, pos_ref, row_ref, cnt_ref, first_ref, end_ref, lpt_ref, h2_ref, dst,
                     buf, zbuf, sem, zsem):
    n_rows = dst.shape[1]
    step = pl.program_id(0)
    slot = step % 2
    cur = buf.at[slot]
    lpt = lpt_ref[...]
    p = lax.broadcasted_iota(I32, (TM * TOPK, TM), 0)
    pick = lpt[0:1, :] == p
    for k in range(1, TOPK):
        pick = pick | (lpt[k:k + 1, :] == p)
    perm = jnp.where(pick, 1.0, 0.0).astype(BF16)
    _store_rows(cur, jnp.dot(perm, h2_ref[...], preferred_element_type=F32))

    def per_expert(e, carry):
        _run_copy(cur, dst, pos_ref[e], row_ref[e], len_ref[e], sem.at[slot], False)
        return carry

    lax.fori_loop(0, NE, per_expert, 0)

    def drain(s):
        pltpu.make_async_copy(buf.at[s], _tile_rows(dst, 0, TM * TOPK), sem.at[s]).wait()

    @pl.when(step > 0)
    def _():
        drain(1 - slot)

    @pl.when(step == pl.num_programs(0) - 1)
    def _():
        drain(slot)

    @pl.when(step == 0)
    def _():
        zbuf[...] = jnp.zeros(zbuf.shape, I32)
        n_tail = (n_rows - end_ref[NE - 1]) // TM
        for wait in (False, True):
            def pad(e, carry, wait=wait):
                lo = first_ref[e] + cnt_ref[e]
                _run_copy(zbuf, dst, 0, lo, end_ref[e] - lo, zsem, wait)
                return carry

            lax.fori_loop(0, NE, pad, 0)

            def tail(j, carry, wait=wait):
                _run_copy(zbuf, dst, 0, end_ref[NE - 1] + j * TM, TM, zsem, wait)
                return carry

            lax.fori_loop(0, n_tail, tail, 0)


def _dispatch(h2, lpt, run_len, run_pos, run_row, cnt, first, end, n_rows):
    assert BLK % TM == 0 and BLK < 2 * TM + 1
    n_tiles = h2.shape[0] // TM
    smem = pl.BlockSpec(memory_space=pltpu.SMEM)
    per_tile = pl.BlockSpec((LANES,), lambda t: (t,), memory_space=pltpu.SMEM)
    return pl.pallas_call(
        _dispatch_kernel, grid=(n_tiles,),
        in_specs=[per_tile, per_tile, per_tile, smem, smem, smem,
                  pl.BlockSpec((8, TM), lambda t: (t, 0)), _tile_spec(D, TM)],
        out_specs=pl.BlockSpec(memory_space=pl.ANY),
        out_shape=jax.ShapeDtypeStruct((SLABS, n_rows, LANES), I32),
        scratch_shapes=[pltpu.VMEM((2, SLABS, TM * TOPK, LANES), I32),
                        pltpu.VMEM((SLABS, BLK, LANES), I32),
                        pltpu.SemaphoreType.DMA((2,)), pltpu.SemaphoreType.DMA],
        compiler_params=_cparams(1), name="moe_dispatch",
    )(run_len, run_pos, run_row, cnt, first, end, lpt, h2)


def _expert_kernel(be_ref, nu_ref, nr_ref, nxt_ref, par_ref, xb_ref, wu_hbm, bu_ref, wd_hbm, bd_ref, yb_ref,
                   wuf, wdf, wub, wdb, sem, *, layer):
    i = pl.program_id(0)
    used = i < nu_ref[0]
    half = nr_ref[i] <= BLK // 2
    slot = par_ref[i]

    def weights(e, s):
        return (pltpu.make_async_copy(wu_hbm.at[layer, e], wuf.at[s], sem.at[s, 0]),
                pltpu.make_async_copy(wd_hbm.at[layer, e], wdf.at[s], sem.at[s, 1]))

    @pl.when(i == 0)
    def _():
        for cp in weights(be_ref[0], slot):
            cp.start()

    @pl.when(used & ((i == 0) | (be_ref[i] != be_ref[jnp.maximum(i - 1, 0)])))
    def _():
        for cp in weights(be_ref[i], slot):
            cp.wait()
        wub[...] = wuf[slot].astype(BF16)
        wdb[...] = wdf[slot].astype(BF16)

        @pl.when(nxt_ref[i] >= 0)
        def _():
            for cp in weights(nxt_ref[i], 1 - slot):
                cp.start()

    def swiglu_rows(rows):
        z = jnp.dot(_load_rows(xb_ref, rows), wub[...], preferred_element_type=F32) + bu_ref[0, 0]
        glu = jnp.minimum(z[:, :FE], LIMIT)
        lin = jnp.clip(z[:, FE:], -LIMIT, LIMIT)
        act = glu * jax.nn.sigmoid(ALPHA * glu) * (lin + 1.0)
        y = jnp.dot(act.astype(BF16), wdb[...], preferred_element_type=F32) + bd_ref[0, 0]
        _store_rows(yb_ref, y.astype(BF16).astype(F32))

    @pl.when(used & jnp.logical_not(half))
    def _():
        swiglu_rows(BLK)

    @pl.when(used & half)
    def _():
        swiglu_rows(BLK // 2)
        yb_ref[:, BLK // 2:, :] = jnp.zeros((SLABS, BLK // 2, LANES), I32)

    @pl.when(jnp.logical_not(used))
    def _():
        yb_ref[...] = jnp.zeros(yb_ref.shape, I32)


def _experts(layer, xb, blk_e, n_used, blk_rows, blk_next, blk_par, wu, bu, wd, bd):
    n_blocks = blk_e.shape[0]
    grid_spec = pltpu.PrefetchScalarGridSpec(
        num_scalar_prefetch=5, grid=(n_blocks,),
        in_specs=[pl.BlockSpec((SLABS, BLK, LANES), lambda i, be, nu, *_: (0, jnp.minimum(i, nu[0] - 1), 0)),
                  pl.BlockSpec(memory_space=pl.ANY),
                  pl.BlockSpec((1, 1, 1, 2 * FE), lambda i, be, *_: (layer, be[i], 0, 0)),
                  pl.BlockSpec(memory_space=pl.ANY),
                  pl.BlockSpec((1, 1, 1, D), lambda i, be, *_: (layer, be[i], 0, 0))],
        out_specs=pl.BlockSpec((SLABS, BLK, LANES), lambda i, *_: (0, i, 0)),
        scratch_shapes=[pltpu.VMEM((2, D, 2 * FE), F32), pltpu.VMEM((2, FE, D), F32),
                        pltpu.VMEM((D, 2 * FE), BF16), pltpu.VMEM((FE, D), BF16),
                        pltpu.SemaphoreType.DMA((2, 2))])
    return pl.pallas_call(
        functools.partial(_expert_kernel, layer=layer), grid_spec=grid_spec,
        out_shape=jax.ShapeDtypeStruct((SLABS, n_blocks * BLK, LANES), I32),
        compiler_params=_cparams(1), name="experts",
    )(blk_e, n_used, blk_rows, blk_next, blk_par, xb, wu, bu, wd, bd)


def _combine_kernel(len_ref, pos_ref, row_ref, len_nx, pos_nx, row_nx, yb, lp_ref, x_ref, gate_ref,
                    g2p, g2s, fg_ref, *outs_and_scratch, final):
    *o_ref, ybuf, sem = outs_and_scratch
    step = pl.program_id(0)
    slot = step % 2
    is_p = step < T_P // TM

    def fetch(len_r, pos_r, row_r, s):
        def per_expert(e, carry):
            _run_copy(yb, ybuf.at[s], row_r[e], pos_r[e], len_r[e], sem.at[s], False)
            return carry

        lax.fori_loop(0, NE, per_expert, 0)

    @pl.when(step == 0)
    def _():
        fetch(len_ref, pos_ref, row_ref, slot)

    @pl.when(step + 1 < pl.num_programs(0))
    def _():
        fetch(len_nx, pos_nx, row_nx, 1 - slot)

    cur = ybuf.at[slot]
    pltpu.make_async_copy(_tile_rows(yb, 0, TM * TOPK), cur, sem.at[slot]).wait()

    rows = x_ref.shape[0]
    lp = lp_ref[...]
    gates = gate_ref[...]
    p = lax.broadcasted_iota(I32, (rows, rows * TOPK), 1)
    weights = jnp.zeros((rows, rows * TOPK), F32)
    for k in range(TOPK):
        weights = jnp.where(lp[:, k:k + 1] == p, gates[:, k:k + 1], weights)
    acc = jnp.dot(weights.astype(BF16), _load_rows(cur), preferred_element_type=F32)
    xn = x_ref[...] + _pick(is_p, g2p, g2s) * acc
    if not final:
        o_ref[0][...] = xn
        return
    y = _rms(xn, fg_ref[...])
    op_ref, os_ref = o_ref

    @pl.when(is_p)
    def _():
        op_ref[...] = y

    @pl.when(jnp.logical_not(is_p))
    def _():
        os_ref[...] = y


def _combine(x, yb, lp, run_len, run_pos, run_row, gates, mod_p, mod_s, final_g, final):
    n_tiles = x.shape[0] // TM
    n_prompt = T_P // TM
    if final:
        out_specs = [pl.BlockSpec((TM, D), lambda t: (jnp.minimum(t, n_prompt - 1), 0)),
                     pl.BlockSpec((TM, D), lambda t: (jnp.maximum(t - n_prompt, 0), 0))]
        out_shape = [jax.ShapeDtypeStruct((T_P, D), F32), jax.ShapeDtypeStruct((x.shape[0] - T_P, D), F32)]
    else:
        out_specs, out_shape = _tile_spec(D, TM), jax.ShapeDtypeStruct(x.shape, F32)
    per_tile = pl.BlockSpec((LANES,), lambda t: (t,), memory_space=pltpu.SMEM)
    next_tile = pl.BlockSpec((LANES,), lambda t: (jnp.minimum(t + 1, n_tiles - 1),),
                             memory_space=pltpu.SMEM)
    in_specs = ([per_tile, per_tile, per_tile, next_tile, next_tile, next_tile,
                 pl.BlockSpec(memory_space=pl.ANY),
                 _tile_spec(LANES, TM), _tile_spec(D, TM), _tile_spec(LANES, TM)]
                + _mod_specs(5, TM) + [_full_spec((1, D))])
    return pl.pallas_call(
        functools.partial(_combine_kernel, final=final), grid=(n_tiles,), in_specs=in_specs,
        out_specs=out_specs, out_shape=out_shape,
        scratch_shapes=[pltpu.VMEM((2, SLABS, TM * TOPK, LANES), I32), pltpu.SemaphoreType.DMA((2,))],
        compiler_params=_cparams(1), name="moe_combine",
    )(run_len, run_pos, run_row, run_len, run_pos, run_row, yb, lp, x, gates, mod_p, mod_s, final_g)


def _block_tables(cnt, first, end, n_blocks):
    experts = jnp.arange(NE, dtype=I32)
    blk_first = jnp.arange(n_blocks, dtype=I32) * BLK
    blk_e = jnp.minimum(jnp.sum((end[None, :] <= blk_first[:, None]).astype(I32), axis=1), NE - 1)
    n_used = end[NE - 1:] // BLK
    has_rows = cnt > 0
    later = has_rows[None, :] & (experts[None, :] > experts[:, None])
    next_e = jnp.min(jnp.where(later, experts[None, :], NE), axis=1)
    next_e = jnp.where(next_e == NE, -1, next_e)
    parity = (jnp.cumsum(has_rows.astype(I32)) - 1) % 2
    own = blk_e[:, None] == experts[None, :]
    of_block = lambda per_expert: jnp.sum(jnp.where(own, per_expert[None, :], 0), axis=1)
    blk_rows = jnp.clip(of_block(first + cnt) - blk_first, 0, BLK)
    return blk_e, n_used, blk_rows, of_block(next_e), of_block(parity)


def _moe(layer, x, h2, idx, gates, cnt_all, mod_p, mod_s, wu, bu, wd, bd, final_g, final):
    lp, lpt, runs, meta = _route(cnt_all, idx)
    cnt, first, end = meta[0, :NE], meta[1, :NE], meta[2, :NE]
    run_len, run_pos, run_row = (runs[:, r, :].reshape(-1) for r in range(3))
    xb = _dispatch(h2, lpt, run_len, run_pos, run_row, cnt, first, end, N_ROWS)
    yb = _experts(layer, xb, *_block_tables(cnt, first, end, N_BLOCKS), wu, bu, wd, bd)
    return _combine(x, yb, lp, run_len, run_pos, run_row, gates, mod_p, mod_s, final_g, final)


def _rope_tables():
    half = DK // 2
    inv = 1.0 / (ROPE_BASE ** jnp.linspace(0.0, 1.0, half, dtype=F32))

    def tab(pos):
        ang = pos.astype(F32)[:, None] * inv[None, :]
        cos, sin = jnp.cos(ang), jnp.sin(ang)
        return jnp.concatenate([cos, cos], -1), jnp.concatenate([-sin, sin], -1)

    cp, sp = tab(jnp.arange(L_P, dtype=I32))
    cs, ss = tab(PAST + jnp.arange(L_S, dtype=I32))
    cos = jnp.concatenate([jnp.tile(cp, (NB_P, 1)), jnp.tile(cs, (NB_S, 1))], 0)
    sin = jnp.concatenate([jnp.tile(sp, (NB_P, 1)), jnp.tile(ss, (NB_S, 1))], 0)
    return cos, sin


def _decay_tables(cl):
    lg = jnp.log(1.0 - 2.0 ** (-5.0 - jnp.arange(H, dtype=F32)))
    r = jnp.arange(CHUNK)
    idx = (r % cl).astype(F32)
    diff = idx[:, None] - idx[None, :]
    same = (r[:, None] // cl) == (r[None, :] // cl)
    decay = jnp.where((same & (diff >= 0))[None],
                      jnp.exp(lg[:, None, None] * jnp.maximum(diff, 0.0)[None]), 0.0)
    qdec = jnp.exp(lg[None, :] * (idx[:, None] + 1.0))
    kdec = jnp.exp(lg[None, :] * (cl - 1.0 - idx[:, None]))
    wide = lambda a: jnp.repeat(a, DK, axis=1)
    return {"decay": decay, "qdec": wide(qdec), "kdec": wide(kdec), "gam": jnp.exp(lg * cl)}


def kernel(x_prompt, x_sample, c_prompt, c_sample, state_ret, w_mod, b_mod, norm1_g, norm2_g,
           sgu_w_in, sgu_ln_g, sgu_ln_b, sgu_w_s, sgu_b_s, sgu_w_out, ret_w_in, ret_norm_g, ret_w_out,
           moe_w_router, moe_b_router, moe_w_up, moe_b_up, moe_w_down, moe_b_down, final_g):
    x = jnp.concatenate([x_prompt.reshape(T_P, D), x_sample.reshape(T_S, D)], 0)
    mod = _modulation(jnp.concatenate([c_prompt, c_sample], 0), w_mod, b_mod)
    cos_tab, sin_tab = _rope_tables()
    tabs_p = _decay_tables(CHUNK)
    tabs_s = _decay_tables(L_S)
    wr_pad = jnp.pad(moe_w_router, ((0, 0), (0, 0), (0, LANES - NE))).astype(BF16)
    br_pad = jnp.pad(moe_b_router, ((0, 0), (0, LANES - NE)), constant_values=-1e30)
    fg = final_g.reshape(1, D)
    b_up = moe_b_up.reshape(DEPTH, NE, 1, 2 * FE)
    b_down = moe_b_down.reshape(DEPTH, NE, 1, D)

    ret_p, v_rows = [], []
    s_all = lax.empty(state_ret.shape, F32)
    for i in range(DEPTH):
        j = i // 2
        mod_p = mod[i, :NB_P].reshape(NB_P, 1, 6 * D)
        mod_s = jnp.repeat(mod[i, NB_P:], L_S, axis=0)
        n1g = norm1_g[i].reshape(1, D)
        n2g = norm2_g[i].reshape(1, D)
        wr = wr_pad[i]
        br = br_pad[i].reshape(1, LANES)
        if i % 2 == 0:
            mixw = jnp.stack([sgu_w_s[j], jnp.tile(sgu_w_s[j][:, :L_S, :L_S], (1, SB, SB))])
            bias_p = jnp.repeat(sgu_b_s[j].T, SGU_GD, axis=1)
            bias_s = jnp.tile(bias_p[:L_S], (SB, 1))
            x, h2, idx, gates, cnt, v = _sgu_layer(
                x, mod_p, mod_s, n1g, n2g, sgu_w_in[j].astype(BF16), sgu_ln_g[j].reshape(1, SGU_W),
                sgu_ln_b[j].reshape(1, SGU_W), mixw, jnp.stack([bias_p, bias_s]),
                sgu_w_out[j].astype(BF16), wr, br)
            v_rows.append(v.reshape(NB_S, L_S, SGU_W))
        else:
            q, k, v, sg = _ret_proj(x, mod_p, mod_s, n1g, ret_w_in[j].astype(BF16), cos_tab, sin_tab)
            ng = ret_norm_g[j].reshape(1, VD)
            y, s_p = _ret_prompt(q, k, v, sg, tabs_p, ng)
            y, s_all = _ret_sample(j, q, k, v, sg, tabs_s, ng, state_ret, y, s_all)
            ret_p.append(s_p)
            x, h2, idx, gates, cnt = _ret_out(y, ret_w_out[j].astype(BF16), x, mod_p, mod_s, n2g, wr, br)
        x = _moe(i, x, h2, idx, gates, cnt, mod_p, mod_s, moe_w_up, b_up, moe_w_down, b_down,
                 fg, final=(i == DEPTH - 1))
    y_prompt, y_sample = x
    return (y_prompt.reshape(NB_P, L_P, D), y_sample.reshape(NB_S, L_S, D), jnp.stack(ret_p), s_all,
            jnp.stack(v_rows))
```

```python
import functools

import jax
import jax.numpy as jnp
from jax import lax
from jax.experimental import pallas as pl
from jax.experimental.pallas import tpu as pltpu

F32 = jnp.float32
BF16 = jnp.bfloat16
I32 = jnp.int32

D = 1024
NB_P, L_P = 8, 2048
NB_S, L_S = 128, 8
PAST = 16384
DEPTH = 4
T_P = NB_P * L_P
T_S = NB_S * L_S
T = T_P + T_S
SGU_W = 2 * D
SGU_G = 8
SGU_GD = SGU_W // SGU_G
CHUNK = 128
H = 8
DK = D // H
DV = 2 * DK
QD = H * DK
VD = H * DV
RET_IN = 2 * QD + 2 * VD
ROPE_BASE = 10000.0
NE = 32
TOPK = 4
FE = D
ALPHA = 1.702
LIMIT = 7.0
EPS = 1e-6

LANES = 128
SLABS = D // (2 * LANES)
HIGH_HALF = -65536
TD = 512
NPT = T_P // TD
NT = T // TD
TM = 256
BLK = 512
TK = T * TOPK
N_BLOCKS = -(-(TK + NE * (BLK - 1)) // BLK)
N_ROWS = N_BLOCKS * BLK
SB = 16
ROUTE_TILES = 4
SH = 2
RET_CHUNKS = 4
VMEM_LIMIT = 56 * 1024 * 1024


def _cparams(n_axes):
    return pltpu.CompilerParams(dimension_semantics=("arbitrary",) * n_axes,
                                vmem_limit_bytes=VMEM_LIMIT)


def _rms(x, g):
    return (x * lax.rsqrt(jnp.mean(x * x, axis=-1, keepdims=True) + EPS)) * g


def _pick(is_p, p_ref, s_ref):
    return jnp.where(is_p, p_ref[0], s_ref[...])


def _mod_specs(j, tm=TD):
    per_batch, n_prompt = L_P // tm, T_P // tm
    return [
        pl.BlockSpec((1, 1, D), lambda t: (jnp.minimum(t // per_batch, NB_P - 1), 0, j)),
        pl.BlockSpec((tm, D), lambda t: (jnp.maximum(t - n_prompt, 0), j), pipeline_mode=pl.Buffered(1)),
    ]


def _tile_spec(width, tm=TD):
    return pl.BlockSpec((tm, width), lambda t: (t, 0))


def _full_spec(shape):
    return pl.BlockSpec(shape, lambda *_: (0,) * len(shape), pipeline_mode=pl.Buffered(1))


def _mod_kernel(c_ref, w_ref, b_ref, o_ref):
    c = c_ref[...]
    cs = (c * jax.nn.sigmoid(c)).astype(BF16)
    o_ref[0] = jnp.dot(cs, w_ref[0].astype(BF16), preferred_element_type=F32) + b_ref[0]


def _modulation(c_all, w_mod, b_mod):
    tn = 1536
    nb = c_all.shape[0]
    return pl.pallas_call(
        _mod_kernel,
        grid=(DEPTH, 6 * D // tn),
        in_specs=[
            pl.BlockSpec((nb, D), lambda l, n: (0, 0)),
            pl.BlockSpec((1, D, tn), lambda l, n: (l, 0, n)),
            pl.BlockSpec((1, 1, tn), lambda l, n: (l, 0, n)),
        ],
        out_specs=pl.BlockSpec((1, nb, tn), lambda l, n: (l, 0, n)),
        out_shape=jax.ShapeDtypeStruct((DEPTH, nb, 6 * D), F32),
        compiler_params=_cparams(2),
        name="modulation",
    )(c_all, w_mod, b_mod.reshape(DEPTH, 1, 6 * D))


def _tail(is_p, x, y, g1p, g1s, sh2p, sh2s, sc2p, sc2s, n2g, wr, br, xo_ref, h2_ref, idx_ref, gate_ref,
          cnt_ref):
    xn = x + _pick(is_p, g1p, g1s) * y
    xo_ref[...] = xn
    h2 = (_rms(xn, n2g[...]) * (1.0 + _pick(is_p, sc2p, sc2s)) + _pick(is_p, sh2p, sh2s)).astype(BF16)
    h2_ref[...] = h2
    logit = lax.dot_general(wr[...], h2, (((1,), (1,)), ((), ())), preferred_element_type=F32)[:NE] + br[...]
    expert = lax.broadcasted_iota(I32, logit.shape, 0)
    vals, ids = [], []
    for _ in range(TOPK):
        m = jnp.max(logit, axis=0, keepdims=True)
        sel = jnp.min(jnp.where(logit == m, expert, NE), axis=0, keepdims=True)
        vals.append(m)
        ids.append(sel)
        logit = jnp.where(expert == sel, -jnp.inf, logit)
    es = [jnp.exp(v - vals[0]) for v in vals]
    tot = (es[0] + es[1]) + (es[2] + es[3])
    slot = lax.broadcasted_iota(I32, (LANES, logit.shape[1]), 0)
    idx_t = jnp.zeros(slot.shape, F32)
    gate_t = jnp.zeros(slot.shape, F32)
    for k in range(TOPK):
        idx_t = jnp.where(slot == k, ids[k].astype(F32), idx_t)
        gate_t = jnp.where(slot == k, es[k] / tot, gate_t)
    idx_out = idx_t.T.astype(I32)
    idx_ref[...] = idx_out
    gate_ref[...] = gate_t.T
    lane = lax.broadcasted_iota(I32, idx_out.shape, 1)
    chosen = jnp.zeros(idx_out.shape, F32)
    for k in range(TOPK):
        chosen = chosen + jnp.where(lane == idx_out[:, k:k + 1], 1.0, 0.0)

    @pl.when(pl.program_id(0) == 0)
    def _():
        cnt_ref[...] = jnp.zeros(cnt_ref.shape, F32)

    cnt_ref[...] += jnp.sum(chosen, axis=0, keepdims=True)


def _tail_in_specs():
    return (_mod_specs(2) + _mod_specs(3) + _mod_specs(4)
            + [_full_spec((1, D)), _full_spec((LANES, D)), _full_spec((NE, 1))])


def _tail_out_specs():
    return [_tile_spec(D), _tile_spec(D), _tile_spec(LANES), _tile_spec(LANES),
            pl.BlockSpec((1, LANES), lambda t: (0, 0))]


def _tail_out_shapes():
    return [jax.ShapeDtypeStruct((T, D), F32), jax.ShapeDtypeStruct((T, D), BF16),
            jax.ShapeDtypeStruct((T, LANES), I32), jax.ShapeDtypeStruct((T, LANES), F32),
            jax.ShapeDtypeStruct((1, LANES), F32)]


def _sgu_kernel(x_ref, sh1p, sh1s, sc1p, sc1s, n1g, win, lng, lnb, mixw, mixb, wout,
                g1p, g1s, sh2p, sh2s, sc2p, sc2s, n2g, wr, br,
                xo_ref, h2_ref, idx_ref, gate_ref, cnt_ref, v_ref, y_scr):
    t = pl.program_id(0)
    is_p = t < NPT
    x = x_ref[...]
    h = _rms(x, n1g[...]) * (1.0 + _pick(is_p, sc1p, sc1s)) + _pick(is_p, sh1p, sh1s)
    z = jnp.dot(h.astype(BF16), win[...], preferred_element_type=F32)
    z = 0.5 * z * (1.0 + lax.erf(z * (0.5 ** 0.5)))
    u = z[:, :SGU_W]
    v = z[:, SGU_W:]
    vc = v - jnp.mean(v, axis=-1, keepdims=True)
    vn = vc * lax.rsqrt(jnp.mean(vc * vc, axis=-1, keepdims=True) + EPS) * lng[...] + lnb[...]

    @pl.when(t >= NPT)
    def _():
        v_ref[...] = vn

    vb = vn.astype(BF16)
    ri = lax.broadcasted_iota(I32, (CHUNK, CHUNK), 0)
    ci = lax.broadcasted_iota(I32, (CHUNK, CHUNK), 1)
    causal = ci <= ri
    shift = jnp.broadcast_to(jnp.where(is_p, 7, 3), ri.shape)
    keep = causal & (lax.shift_right_logical(ri, shift) == lax.shift_right_logical(ci, shift))
    for g in range(SGU_G):
        wg = jnp.where(keep, mixw[0, g], 0.0).astype(BF16)
        for c in range(TD // CHUNK):
            rows = slice(c * CHUNK, (c + 1) * CHUNK)
            cols = slice(g * SGU_GD, (g + 1) * SGU_GD)
            mixed = jnp.dot(wg, vb[rows, cols], preferred_element_type=F32) + mixb[0, :, cols]
            y_scr[rows, cols] = (u[rows, cols] * mixed).astype(BF16)
    y = jnp.dot(y_scr[...], wout[...], preferred_element_type=F32)
    _tail(is_p, x, y, g1p, g1s, sh2p, sh2s, sc2p, sc2s, n2g, wr, br, xo_ref, h2_ref, idx_ref, gate_ref,
          cnt_ref)


def _sgu_layer(x, mod_p, mod_s, n1g, n2g, win, lng, lnb, mixw, mixb, wout, wr, br):
    sel = lambda t: (jnp.where(t < NPT, 0, 1), 0, 0, 0)
    in_specs = ([_tile_spec(D)] + _mod_specs(0) + _mod_specs(1)
                + [_full_spec((1, D)), _full_spec((D, 2 * SGU_W)), _full_spec((1, SGU_W)),
                   _full_spec((1, SGU_W)),
                   pl.BlockSpec((1, SGU_G, CHUNK, CHUNK), sel),
                   pl.BlockSpec((1, CHUNK, SGU_W), lambda t: (jnp.where(t < NPT, 0, 1), 0, 0)),
                   _full_spec((SGU_W, D))]
                + _tail_in_specs())
    out_specs = _tail_out_specs() + [pl.BlockSpec((TD, SGU_W), lambda t: (jnp.maximum(t - NPT, 0), 0))]
    out_shape = _tail_out_shapes() + [jax.ShapeDtypeStruct((T_S, SGU_W), F32)]
    return pl.pallas_call(
        _sgu_kernel, grid=(NT,), in_specs=in_specs, out_specs=out_specs, out_shape=out_shape,
        scratch_shapes=[pltpu.VMEM((TD, SGU_W), BF16)],
        compiler_params=_cparams(1), name="sgu_layer",
    )(x, mod_p, mod_s, mod_p, mod_s, n1g, win, lng, lnb, mixw, mixb, wout,
      mod_p, mod_s, mod_p, mod_s, mod_p, mod_s, n2g, wr, br)


def _ret_proj_kernel(x_ref, sh1p, sh1s, sc1p, sc1s, n1g, win, cos_ref, sin_ref,
                     q_ref, k_ref, v_ref, sg_ref):
    t = pl.program_id(0)
    is_p = t < NPT
    x = x_ref[...]
    h = _rms(x, n1g[...]) * (1.0 + _pick(is_p, sc1p, sc1s)) + _pick(is_p, sh1p, sh1s)
    p = jnp.dot(h.astype(BF16), win[...], preferred_element_type=F32)
    cos = cos_ref[...]
    sin = sin_ref[...]
    for hd in range(H):
        cq = slice(hd * DK, (hd + 1) * DK)
        ck = slice(QD + hd * DK, QD + (hd + 1) * DK)
        qh = p[:, cq]
        kh = p[:, ck]
        q_ref[:, cq] = (qh * cos + pltpu.roll(qh, DK // 2, 1) * sin).astype(BF16)
        k_ref[:, cq] = ((kh * cos + pltpu.roll(kh, DK // 2, 1) * sin) * (DK ** -0.5)).astype(BF16)
    v_ref[...] = p[:, 2 * QD:2 * QD + VD].astype(BF16)
    g = p[:, 2 * QD + VD:]
    sg_ref[...] = (g * jax.nn.sigmoid(g)).astype(BF16)


def _ret_proj(x, mod_p, mod_s, n1g, win, cos_tab, sin_tab):
    in_specs = ([_tile_spec(D)] + _mod_specs(0) + _mod_specs(1)
                + [_full_spec((1, D)), _full_spec((D, RET_IN)), _tile_spec(DK), _tile_spec(DK)])
    out_specs = [_tile_spec(QD), _tile_spec(QD), _tile_spec(VD), _tile_spec(VD)]
    out_shape = [jax.ShapeDtypeStruct((T, QD), BF16), jax.ShapeDtypeStruct((T, QD), BF16),
                 jax.ShapeDtypeStruct((T, VD), BF16), jax.ShapeDtypeStruct((T, VD), BF16)]
    return pl.pallas_call(
        _ret_proj_kernel, grid=(NT,), in_specs=in_specs, out_specs=out_specs, out_shape=out_shape,
        compiler_params=_cparams(1), name="ret_proj",
    )(x, mod_p, mod_s, mod_p, mod_s, n1g, win, cos_tab, sin_tab)


def _ret_intra(q, k, v, decay, qdec, kdec):
    s = lax.dot_general(q, k, (((1,), (1,)), ((), ())), preferred_element_type=F32) * decay
    o = jnp.dot(s.astype(BF16), v, preferred_element_type=F32)
    return o, q.astype(F32) * qdec, k.astype(F32) * kdec


def _ret_finish(o, sg, ng):
    on = o * lax.rsqrt(jnp.mean(o * o, axis=-1, keepdims=True) + EPS)
    return (sg.astype(F32) * (on * ng)).astype(BF16)


def _ret_prompt_kernel(gam_ref, q_ref, k_ref, v_ref, sg_ref, dec_ref, qdec_ref, kdec_ref, ng_ref,
                       y_ref, s_ref):
    c = pl.program_id(1)

    @pl.when(c == 0)
    def _():
        s_ref[...] = jnp.zeros(s_ref.shape, F32)

    cks = [slice(hd * DK, (hd + 1) * DK) for hd in range(H)]
    cvs = [slice(hd * DV, (hd + 1) * DV) for hd in range(H)]
    for j in range(RET_CHUNKS):
        rows = slice(j * CHUNK, (j + 1) * CHUNK)
        vs = [v_ref[rows, cv] for cv in cvs]
        intra = [_ret_intra(q_ref[rows, ck], k_ref[rows, ck], v, dec_ref[hd], qdec_ref[:, ck], kdec_ref[:, ck])
                 for hd, (ck, v) in enumerate(zip(cks, vs))]
        olds = [s_ref[0, hd] for hd in range(H)]
        outs = [o + jnp.dot(qd.astype(BF16), s_old.astype(BF16), preferred_element_type=F32)
                for (o, qd, _), s_old in zip(intra, olds)]
        for hd in range(H):
            s_ref[0, hd] = gam_ref[hd] * olds[hd] + lax.dot_general(
                intra[hd][2].astype(BF16), vs[hd], (((0,), (0,)), ((), ())), preferred_element_type=F32)
        for hd, cv in enumerate(cvs):
            y_ref[rows, cv] = _ret_finish(outs[hd], sg_ref[rows, cv], ng_ref[:, cv])


def _ret_prompt(q, k, v, sg, tabs, ng):
    rows = RET_CHUNKS * CHUNK
    nc = L_P // rows
    row = lambda b, c: (b * nc + c, 0)
    smem = pl.BlockSpec(memory_space=pltpu.SMEM)
    in_specs = [smem,
                pl.BlockSpec((rows, QD), row), pl.BlockSpec((rows, QD), row),
                pl.BlockSpec((rows, VD), row), pl.BlockSpec((rows, VD), row),
                _full_spec((H, CHUNK, CHUNK)), _full_spec((CHUNK, QD)), _full_spec((CHUNK, QD)),
                _full_spec((1, VD))]
    out_specs = [pl.BlockSpec((rows, VD), row),
                 pl.BlockSpec((1, H, DK, DV), lambda b, c: (b, 0, 0, 0))]
    out_shape = [jax.ShapeDtypeStruct((T, VD), BF16), jax.ShapeDtypeStruct((NB_P, H, DK, DV), F32)]
    return pl.pallas_call(
        _ret_prompt_kernel, grid=(NB_P, nc), in_specs=in_specs, out_specs=out_specs, out_shape=out_shape,
        compiler_params=_cparams(2), name="ret_prompt",
    )(tabs["gam"], q, k, v, sg, tabs["decay"], tabs["qdec"], tabs["kdec"], ng)


def _ret_sample_kernel(gam_ref, q_ref, k_ref, v_ref, sg_ref, dec_ref, qdec_ref, kdec_ref, ng_ref,
                       s_in_ref, y_in_ref, s_all_ref, y_ref, s_out_ref):
    del y_in_ref, s_all_ref
    s_in_ref = s_in_ref.at[0]
    s_out_ref = s_out_ref.at[0]
    row_batch = lax.broadcasted_iota(I32, (CHUNK, DK), 0) // L_S
    for i in range(SH):
        ck = slice(i * DK, (i + 1) * DK)
        cv = slice(i * DV, (i + 1) * DV)
        v = v_ref[:, cv]
        o, qd, kd = _ret_intra(q_ref[:, ck], k_ref[:, ck], v, dec_ref[i], qdec_ref[:, ck], kdec_ref[:, ck])
        gam = gam_ref[pl.program_id(1) * SH + i]
        cross = [jnp.dot(qd[b * L_S:(b + 1) * L_S].astype(BF16), s_in_ref[b, i].astype(BF16),
                         preferred_element_type=F32) for b in range(SB)]
        for b in range(SB):
            kb = jnp.where(row_batch == b, kd, 0.0).astype(BF16)
            s_out_ref[b, i] = gam * s_in_ref[b, i] + lax.dot_general(
                kb, v, (((0,), (0,)), ((), ())), preferred_element_type=F32)
        o = o + jnp.concatenate(cross, axis=0)
        y_ref[:, cv] = _ret_finish(o, sg_ref[:, cv], ng_ref[:, cv])


def _ret_sample(j, q, k, v, sg, tabs, ng, s_in, y_prev, s_all):
    base = T_P // CHUNK
    rq = lambda g, h: (base + g, h)
    st = pl.BlockSpec((1, SB, SH, DK, DV), lambda g, h: (j, g, h, 0, 0))
    smem = pl.BlockSpec(memory_space=pltpu.SMEM)
    in_specs = [smem,
                pl.BlockSpec((CHUNK, SH * DK), rq), pl.BlockSpec((CHUNK, SH * DK), rq),
                pl.BlockSpec((CHUNK, SH * DV), rq), pl.BlockSpec((CHUNK, SH * DV), rq),
                pl.BlockSpec((SH, CHUNK, CHUNK), lambda g, h: (h, 0, 0)),
                pl.BlockSpec((CHUNK, SH * DK), lambda g, h: (0, h)),
                pl.BlockSpec((CHUNK, SH * DK), lambda g, h: (0, h)),
                pl.BlockSpec((1, SH * DV), lambda g, h: (0, h)),
                st, pl.BlockSpec(memory_space=pl.ANY), pl.BlockSpec(memory_space=pl.ANY)]
    out_specs = [pl.BlockSpec((CHUNK, SH * DV), rq), st]
    out_shape = [jax.ShapeDtypeStruct((T, VD), BF16), jax.ShapeDtypeStruct(s_all.shape, F32)]
    return pl.pallas_call(
        _ret_sample_kernel, grid=(NB_S // SB, H // SH), in_specs=in_specs, out_specs=out_specs,
        out_shape=out_shape, input_output_aliases={10: 0, 11: 1},
        compiler_params=_cparams(2), name="ret_sample",
    )(tabs["gam"], q, k, v, sg, tabs["decay"], tabs["qdec"], tabs["kdec"], ng, s_in, y_prev, s_all)


def _ret_out_kernel(y_ref, wout, x_ref, g1p, g1s, sh2p, sh2s, sc2p, sc2s, n2g, wr, br,
                    xo_ref, h2_ref, idx_ref, gate_ref, cnt_ref):
    is_p = pl.program_id(0) < NPT
    y = jnp.dot(y_ref[...], wout[...], preferred_element_type=F32)
    _tail(is_p, x_ref[...], y, g1p, g1s, sh2p, sh2s, sc2p, sc2s, n2g, wr, br,
          xo_ref, h2_ref, idx_ref, gate_ref, cnt_ref)


def _ret_out(y, wout, x, mod_p, mod_s, n2g, wr, br):
    in_specs = [_tile_spec(VD), _full_spec((VD, D)), _tile_spec(D)] + _tail_in_specs()
    return pl.pallas_call(
        _ret_out_kernel, grid=(NT,), in_specs=in_specs, out_specs=_tail_out_specs(),
        out_shape=_tail_out_shapes(), compiler_params=_cparams(1), name="ret_out",
    )(y, wout, x, mod_p, mod_s, mod_p, mod_s, mod_p, mod_s, n2g, wr, br)


def _lane_prefix(v, lane1):
    s = 1
    while s < LANES:
        v = v + jnp.where(lane1 >= s, pltpu.roll(v, s, 1), 0.0)
        s *= 2
    return v


def _route_kernel(cnt_ref, idx_ref, lp_ref, lpt_ref, runs_ref, meta_ref, base_scr):
    lane = lax.broadcasted_iota(I32, (TM, LANES), 1)
    lane1 = lax.broadcasted_iota(I32, (1, LANES), 1)
    row = lax.broadcasted_iota(I32, (8, LANES), 0)

    @pl.when(pl.program_id(0) == 0)
    def _():
        cnt = cnt_ref[...]
        padded = (((cnt.astype(I32) + (BLK - 1)) // BLK) * BLK).astype(F32)
        end = _lane_prefix(padded, lane1)
        base_scr[...] = end - padded
        meta = jnp.where(row == 0, cnt, jnp.where(row == 1, end - padded, jnp.where(row == 2, end, 0.0)))
        meta_ref[...] = meta.astype(I32)

    ri = lax.broadcasted_iota(I32, (TM, TM), 0)
    ci = lax.broadcasted_iota(I32, (TM, TM), 1)
    before = jnp.where(ci < ri, 1.0, 0.0).astype(BF16)
    base = base_scr[...]
    for j in range(ROUTE_TILES):
        idx = idx_ref[j * TM:(j + 1) * TM, :]
        hits = [lane == idx[:, k:k + 1] for k in range(TOPK)]
        chosen = jnp.zeros((TM, LANES), F32)
        for hk in hits:
            chosen = chosen + jnp.where(hk, 1.0, 0.0)
        colsum = jnp.sum(chosen, axis=0, keepdims=True)
        loff = _lane_prefix(colsum, lane1) - colsum
        pos = jnp.dot(before, chosen.astype(BF16), preferred_element_type=F32) + loff
        lp = jnp.zeros((TM, LANES), F32)
        for k, hk in enumerate(hits):
            lp = jnp.where(lane == k, jnp.sum(jnp.where(hk, pos, 0.0), axis=-1, keepdims=True), lp)
        lp_ref[j * TM:(j + 1) * TM, :] = lp.astype(I32)
        lpt_ref[j * 8:(j + 1) * 8, :] = lp.T[:8].astype(I32)
        runs = jnp.where(row == 0, colsum, jnp.where(row == 1, loff, jnp.where(row == 2, base, 0.0)))
        runs_ref[j] = runs.astype(I32)
        base = base + colsum
    base_scr[...] = base


def _route(cnt, idx):
    n_tiles = idx.shape[0] // TM
    assert n_tiles % ROUTE_TILES == 0
    rt = ROUTE_TILES
    return pl.pallas_call(
        _route_kernel, grid=(n_tiles // rt,),
        in_specs=[pl.BlockSpec((1, LANES), lambda t: (0, 0)), pl.BlockSpec((rt * TM, LANES), lambda t: (t, 0))],
        out_specs=[pl.BlockSpec((rt * TM, LANES), lambda t: (t, 0)),
                   pl.BlockSpec((rt * 8, TM), lambda t: (t, 0)),
                   pl.BlockSpec((rt, 8, LANES), lambda t: (t, 0, 0)),
                   pl.BlockSpec((8, LANES), lambda t: (0, 0))],
        out_shape=[jax.ShapeDtypeStruct((n_tiles * TM, LANES), I32),
                   jax.ShapeDtypeStruct((n_tiles * 8, TM), I32),
                   jax.ShapeDtypeStruct((n_tiles, 8, LANES), I32),
                   jax.ShapeDtypeStruct((8, LANES), I32)],
        scratch_shapes=[pltpu.VMEM((1, LANES), F32)],
        compiler_params=_cparams(1), name="moe_route",
    )(cnt, idx)


def _tile_rows(ref, r, n):
    return ref.at[:, pl.ds(r, n), :]


def _run_copy(src, dst, r_src, r_dst, n, sem, wait):
    for b in reversed(range(TM.bit_length())):
        size = 1 << b
        off = (n >> (b + 1)) << (b + 1)

        def piece(off=off, size=size):
            cp = pltpu.make_async_copy(_tile_rows(src, r_src + off, size),
                                       _tile_rows(dst, r_dst + off, size), sem)
            if wait:
                cp.wait()
            else:
                cp.start()

        if isinstance(n, int):
            if n & size:
                piece()
        else:
            pl.when((n & size) != 0)(piece)


def _load_rows(ref3, rows=None):
    rows = ref3.shape[1] if rows is None else rows
    parts = []
    for s in range(SLABS):
        word = ref3[s, :rows, :]
        parts.append(lax.bitcast_convert_type(lax.shift_left(word, 16), F32))
        parts.append(lax.bitcast_convert_type(word & HIGH_HALF, F32))
    return jnp.concatenate(parts, axis=1).astype(BF16)


def _store_rows(ref3, val):
    rows = val.shape[0]
    bits = lax.bitcast_convert_type(val, I32)
    for s in range(SLABS):
        low = lax.shift_right_logical(bits[:, (2 * s) * LANES:(2 * s + 1) * LANES], 16)
        ref3[s, :rows, :] = bits[:, (2 * s + 1) * LANES:(2 * s + 2) * LANES] | low


def _dispatch_kernel(len_ref, pos_ref, row_ref, cnt_ref, first_ref, end_ref, lpt_ref, h2_ref, dst,
                     buf, zbuf, sem, zsem):
    n_rows = dst.shape[1]
    step = pl.program_id(0)
    slot = step % 2
    cur = buf.at[slot]
    lpt = lpt_ref[...]
    p = lax.broadcasted_iota(I32, (TM * TOPK, TM), 0)
    pick = lpt[0:1, :] == p
    for k in range(1, TOPK):
        pick = pick | (lpt[k:k + 1, :] == p)
    perm = jnp.where(pick, 1.0, 0.0).astype(BF16)
    _store_rows(cur, jnp.dot(perm, h2_ref[...], preferred_element_type=F32))

    def per_expert(e, carry):
        _run_copy(cur, dst, pos_ref[e], row_ref[e], len_ref[e], sem.at[slot], False)
        return carry

    lax.fori_loop(0, NE, per_expert, 0)

    def drain(s):
        pltpu.make_async_copy(buf.at[s], _tile_rows(dst, 0, TM * TOPK), sem.at[s]).wait()

    @pl.when(step > 0)
    def _():
        drain(1 - slot)

    @pl.when(step == pl.num_programs(0) - 1)
    def _():
        drain(slot)

    @pl.when(step == 0)
    def _():
        zbuf[...] = jnp.zeros(zbuf.shape, I32)
        n_tail = (n_rows - end_ref[NE - 1]) // TM
        for wait in (False, True):
            def pad(e, carry, wait=wait):
                lo = first_ref[e] + cnt_ref[e]
                _run_copy(zbuf, dst, 0, lo, end_ref[e] - lo, zsem, wait)
                return carry

            lax.fori_loop(0, NE, pad, 0)

            def tail(j, carry, wait=wait):
                _run_copy(zbuf, dst, 0, end_ref[NE - 1] + j * TM, TM, zsem, wait)
                return carry

            lax.fori_loop(0, n_tail, tail, 0)


def _dispatch(h2, lpt, run_len, run_pos, run_row, cnt, first, end, n_rows):
    assert BLK % TM == 0 and BLK < 2 * TM + 1
    n_tiles = h2.shape[0] // TM
    smem = pl.BlockSpec(memory_space=pltpu.SMEM)
    per_tile = pl.BlockSpec((LANES,), lambda t: (t,), memory_space=pltpu.SMEM)
    return pl.pallas_call(
        _dispatch_kernel, grid=(n_tiles,),
        in_specs=[per_tile, per_tile, per_tile, smem, smem, smem,
                  pl.BlockSpec((8, TM), lambda t: (t, 0)), _tile_spec(D, TM)],
        out_specs=pl.BlockSpec(memory_space=pl.ANY),
        out_shape=jax.ShapeDtypeStruct((SLABS, n_rows, LANES), I32),
        scratch_shapes=[pltpu.VMEM((2, SLABS, TM * TOPK, LANES), I32),
                        pltpu.VMEM((SLABS, BLK, LANES), I32),
                        pltpu.SemaphoreType.DMA((2,)), pltpu.SemaphoreType.DMA],
        compiler_params=_cparams(1), name="moe_dispatch",
    )(run_len, run_pos, run_row, cnt, first, end, lpt, h2)


def _expert_kernel(be_ref, nu_ref, nr_ref, nxt_ref, par_ref, xb_ref, wu_hbm, bu_ref, wd_hbm, bd_ref, yb_ref,
                   wuf, wdf, wub, wdb, sem, *, layer):
    i = pl.program_id(0)
    used = i < nu_ref[0]
    half = nr_ref[i] <= BLK // 2
    slot = par_ref[i]

    def weights(e, s):
        return (pltpu.make_async_copy(wu_hbm.at[layer, e], wuf.at[s], sem.at[s, 0]),
                pltpu.make_async_copy(wd_hbm.at[layer, e], wdf.at[s], sem.at[s, 1]))

    @pl.when(i == 0)
    def _():
        for cp in weights(be_ref[0], slot):
            cp.start()

    @pl.when(used & ((i == 0) | (be_ref[i] != be_ref[jnp.maximum(i - 1, 0)])))
    def _():
        for cp in weights(be_ref[i], slot):
            cp.wait()
        wub[...] = wuf[slot].astype(BF16)
        wdb[...] = wdf[slot].astype(BF16)

        @pl.when(nxt_ref[i] >= 0)
        def _():
            for cp in weights(nxt_ref[i], 1 - slot):
                cp.start()

    def swiglu_rows(rows):
        z = jnp.dot(_load_rows(xb_ref, rows), wub[...], preferred_element_type=F32) + bu_ref[0, 0]
        glu = jnp.minimum(z[:, :FE], LIMIT)
        lin = jnp.clip(z[:, FE:], -LIMIT, LIMIT)
        act = glu * jax.nn.sigmoid(ALPHA * glu) * (lin + 1.0)
        y = jnp.dot(act.astype(BF16), wdb[...], preferred_element_type=F32) + bd_ref[0, 0]
        _store_rows(yb_ref, y.astype(BF16).astype(F32))

    @pl.when(used & jnp.logical_not(half))
    def _():
        swiglu_rows(BLK)

    @pl.when(used & half)
    def _():
        swiglu_rows(BLK // 2)
        yb_ref[:, BLK // 2:, :] = jnp.zeros((SLABS, BLK // 2, LANES), I32)

    @pl.when(jnp.logical_not(used))
    def _():
        yb_ref[...] = jnp.zeros(yb_ref.shape, I32)


def _experts(layer, xb, blk_e, n_used, blk_rows, blk_next, blk_par, wu, bu, wd, bd):
    n_blocks = blk_e.shape[0]
    grid_spec = pltpu.PrefetchScalarGridSpec(
        num_scalar_prefetch=5, grid=(n_blocks,),
        in_specs=[pl.BlockSpec((SLABS, BLK, LANES), lambda i, be, nu, *_: (0, jnp.minimum(i, nu[0] - 1), 0)),
                  pl.BlockSpec(memory_space=pl.ANY),
                  pl.BlockSpec((1, 1, 1, 2 * FE), lambda i, be, *_: (layer, be[i], 0, 0)),
                  pl.BlockSpec(memory_space=pl.ANY),
                  pl.BlockSpec((1, 1, 1, D), lambda i, be, *_: (layer, be[i], 0, 0))],
        out_specs=pl.BlockSpec((SLABS, BLK, LANES), lambda i, *_: (0, i, 0)),
        scratch_shapes=[pltpu.VMEM((2, D, 2 * FE), F32), pltpu.VMEM((2, FE, D), F32),
                        pltpu.VMEM((D, 2 * FE), BF16), pltpu.VMEM((FE, D), BF16),
                        pltpu.SemaphoreType.DMA((2, 2))])
    return pl.pallas_call(
        functools.partial(_expert_kernel, layer=layer), grid_spec=grid_spec,
        out_shape=jax.ShapeDtypeStruct((SLABS, n_blocks * BLK, LANES), I32),
        compiler_params=_cparams(1), name="experts",
    )(blk_e, n_used, blk_rows, blk_next, blk_par, xb, wu, bu, wd, bd)


def _combine_kernel(len_ref, pos_ref, row_ref, len_nx, pos_nx, row_nx, yb, lp_ref, x_ref, gate_ref,
                    g2p, g2s, fg_ref, *outs_and_scratch, final):
    *o_ref, ybuf, sem = outs_and_scratch
    step = pl.program_id(0)
    slot = step % 2
    is_p = step < T_P // TM

    def fetch(len_r, pos_r, row_r, s):
        def per_expert(e, carry):
            _run_copy(yb, ybuf.at[s], row_r[e], pos_r[e], len_r[e], sem.at[s], False)
            return carry

        lax.fori_loop(0, NE, per_expert, 0)

    @pl.when(step == 0)
    def _():
        fetch(len_ref, pos_ref, row_ref, slot)

    @pl.when(step + 1 < pl.num_programs(0))
    def _():
        fetch(len_nx, pos_nx, row_nx, 1 - slot)

    cur = ybuf.at[slot]
    pltpu.make_async_copy(_tile_rows(yb, 0, TM * TOPK), cur, sem.at[slot]).wait()

    rows = x_ref.shape[0]
    lp = lp_ref[...]
    gates = gate_ref[...]
    p = lax.broadcasted_iota(I32, (rows, rows * TOPK), 1)
    weights = jnp.zeros((rows, rows * TOPK), F32)
    for k in range(TOPK):
        weights = jnp.where(lp[:, k:k + 1] == p, gates[:, k:k + 1], weights)
    acc = jnp.dot(weights.astype(BF16), _load_rows(cur), preferred_element_type=F32)
    xn = x_ref[...] + _pick(is_p, g2p, g2s) * acc
    if not final:
        o_ref[0][...] = xn
        return
    y = _rms(xn, fg_ref[...])
    op_ref, os_ref = o_ref

    @pl.when(is_p)
    def _():
        op_ref[...] = y

    @pl.when(jnp.logical_not(is_p))
    def _():
        os_ref[...] = y


def _combine(x, yb, lp, run_len, run_pos, run_row, gates, mod_p, mod_s, final_g, final):
    n_tiles = x.shape[0] // TM
    n_prompt = T_P // TM
    if final:
        out_specs = [pl.BlockSpec((TM, D), lambda t: (jnp.minimum(t, n_prompt - 1), 0)),
                     pl.BlockSpec((TM, D), lambda t: (jnp.maximum(t - n_prompt, 0), 0))]
        out_shape = [jax.ShapeDtypeStruct((T_P, D), F32), jax.ShapeDtypeStruct((x.shape[0] - T_P, D), F32)]
    else:
        out_specs, out_shape = _tile_spec(D, TM), jax.ShapeDtypeStruct(x.shape, F32)
    per_tile = pl.BlockSpec((LANES,), lambda t: (t,), memory_space=pltpu.SMEM)
    next_tile = pl.BlockSpec((LANES,), lambda t: (jnp.minimum(t + 1, n_tiles - 1),),
                             memory_space=pltpu.SMEM)
    in_specs = ([per_tile, per_tile, per_tile, next_tile, next_tile, next_tile,
                 pl.BlockSpec(memory_space=pl.ANY),
                 _tile_spec(LANES, TM), _tile_spec(D, TM), _tile_spec(LANES, TM)]
                + _mod_specs(5, TM) + [_full_spec((1, D))])
    return pl.pallas_call(
        functools.partial(_combine_kernel, final=final), grid=(n_tiles,), in_specs=in_specs,
        out_specs=out_specs, out_shape=out_shape,
        scratch_shapes=[pltpu.VMEM((2, SLABS, TM * TOPK, LANES), I32), pltpu.SemaphoreType.DMA((2,))],
        compiler_params=_cparams(1), name="moe_combine",
    )(run_len, run_pos, run_row, run_len, run_pos, run_row, yb, lp, x, gates, mod_p, mod_s, final_g)


def _block_tables(cnt, first, end, n_blocks):
    experts = jnp.arange(NE, dtype=I32)
    blk_first = jnp.arange(n_blocks, dtype=I32) * BLK
    blk_e = jnp.minimum(jnp.sum((end[None, :] <= blk_first[:, None]).astype(I32), axis=1), NE - 1)
    n_used = end[NE - 1:] // BLK
    has_rows = cnt > 0
    later = has_rows[None, :] & (experts[None, :] > experts[:, None])
    next_e = jnp.min(jnp.where(later, experts[None, :], NE), axis=1)
    next_e = jnp.where(next_e == NE, -1, next_e)
    parity = (jnp.cumsum(has_rows.astype(I32)) - 1) % 2
    own = blk_e[:, None] == experts[None, :]
    of_block = lambda per_expert: jnp.sum(jnp.where(own, per_expert[None, :], 0), axis=1)
    blk_rows = jnp.clip(of_block(first + cnt) - blk_first, 0, BLK)
    return blk_e, n_used, blk_rows, of_block(next_e), of_block(parity)


def _moe(layer, x, h2, idx, gates, cnt_all, mod_p, mod_s, wu, bu, wd, bd, final_g, final):
    lp, lpt, runs, meta = _route(cnt_all, idx)
    cnt, first, end = meta[0, :NE], meta[1, :NE], meta[2, :NE]
    run_len, run_pos, run_row = (runs[:, r, :].reshape(-1) for r in range(3))
    xb = _dispatch(h2, lpt, run_len, run_pos, run_row, cnt, first, end, N_ROWS)
    yb = _experts(layer, xb, *_block_tables(cnt, first, end, N_BLOCKS), wu, bu, wd, bd)
    return _combine(x, yb, lp, run_len, run_pos, run_row, gates, mod_p, mod_s, final_g, final)


def _rope_tables():
    half = DK // 2
    inv = 1.0 / (ROPE_BASE ** jnp.linspace(0.0, 1.0, half, dtype=F32))

    def tab(pos):
        ang = pos.astype(F32)[:, None] * inv[None, :]
        cos, sin = jnp.cos(ang), jnp.sin(ang)
        return jnp.concatenate([cos, cos], -1), jnp.concatenate([-sin, sin], -1)

    cp, sp = tab(jnp.arange(L_P, dtype=I32))
    cs, ss = tab(PAST + jnp.arange(L_S, dtype=I32))
    cos = jnp.concatenate([jnp.tile(cp, (NB_P, 1)), jnp.tile(cs, (NB_S, 1))], 0)
    sin = jnp.concatenate([jnp.tile(sp, (NB_P, 1)), jnp.tile(ss, (NB_S, 1))], 0)
    return cos, sin


def _decay_tables(cl):
    lg = jnp.log(1.0 - 2.0 ** (-5.0 - jnp.arange(H, dtype=F32)))
    r = jnp.arange(CHUNK)
    idx = (r % cl).astype(F32)
    diff = idx[:, None] - idx[None, :]
    same = (r[:, None] // cl) == (r[None, :] // cl)
    decay = jnp.where((same & (diff >= 0))[None],
                      jnp.exp(lg[:, None, None] * jnp.maximum(diff, 0.0)[None]), 0.0)
    qdec = jnp.exp(lg[None, :] * (idx[:, None] + 1.0))
    kdec = jnp.exp(lg[None, :] * (cl - 1.0 - idx[:, None]))
    wide = lambda a: jnp.repeat(a, DK, axis=1)
    return {"decay": decay, "qdec": wide(qdec), "kdec": wide(kdec), "gam": jnp.exp(lg * cl)}


def kernel(x_prompt, x_sample, c_prompt, c_sample, state_ret, w_mod, b_mod, norm1_g, norm2_g,
           sgu_w_in, sgu_ln_g, sgu_ln_b, sgu_w_s, sgu_b_s, sgu_w_out, ret_w_in, ret_norm_g, ret_w_out,
           moe_w_router, moe_b_router, moe_w_up, moe_b_up, moe_w_down, moe_b_down, final_g):
    x = jnp.concatenate([x_prompt.reshape(T_P, D), x_sample.reshape(T_S, D)], 0)
    mod = _modulation(jnp.concatenate([c_prompt, c_sample], 0), w_mod, b_mod)
    cos_tab, sin_tab = _rope_tables()
    tabs_p = _decay_tables(CHUNK)
    tabs_s = _decay_tables(L_S)
    wr_rows = jnp.pad(jnp.swapaxes(moe_w_router, 1, 2), ((0, 0), (0, LANES - NE), (0, 0))).astype(BF16)
    fg = final_g.reshape(1, D)
    b_up = moe_b_up.reshape(DEPTH, NE, 1, 2 * FE)
    b_down = moe_b_down.reshape(DEPTH, NE, 1, D)

    ret_p, v_rows = [], []
    s_all = lax.empty(state_ret.shape, F32)
    for i in range(DEPTH):
        j = i // 2
        mod_p = mod[i, :NB_P].reshape(NB_P, 1, 6 * D)
        mod_s = jnp.repeat(mod[i, NB_P:], L_S, axis=0)
        n1g = norm1_g[i].reshape(1, D)
        n2g = norm2_g[i].reshape(1, D)
        wr = wr_rows[i]
        br = moe_b_router[i].reshape(NE, 1)
        if i % 2 == 0:
            mixw = jnp.stack([sgu_w_s[j], jnp.tile(sgu_w_s[j][:, :L_S, :L_S], (1, SB, SB))])
            bias_p = jnp.repeat(sgu_b_s[j].T, SGU_GD, axis=1)
            bias_s = jnp.tile(bias_p[:L_S], (SB, 1))
            x, h2, idx, gates, cnt, v = _sgu_layer(
                x, mod_p, mod_s, n1g, n2g, sgu_w_in[j].astype(BF16), sgu_ln_g[j].reshape(1, SGU_W),
                sgu_ln_b[j].reshape(1, SGU_W), mixw, jnp.stack([bias_p, bias_s]),
                sgu_w_out[j].astype(BF16), wr, br)
            v_rows.append(v.reshape(NB_S, L_S, SGU_W))
        else:
            q, k, v, sg = _ret_proj(x, mod_p, mod_s, n1g, ret_w_in[j].astype(BF16), cos_tab, sin_tab)
            ng = ret_norm_g[j].reshape(1, VD)
            y, s_p = _ret_prompt(q, k, v, sg, tabs_p, ng)
            y, s_all = _ret_sample(j, q, k, v, sg, tabs_s, ng, state_ret, y, s_all)
            ret_p.append(s_p)
            x, h2, idx, gates, cnt = _ret_out(y, ret_w_out[j].astype(BF16), x, mod_p, mod_s, n2g, wr, br)
        x = _moe(i, x, h2, idx, gates, cnt, mod_p, mod_s, moe_w_up, b_up, moe_w_down, b_down,
                 fg, final=(i == DEPTH - 1))
    y_prompt, y_sample = x
    return (y_prompt.reshape(NB_P, L_P, D), y_sample.reshape(NB_S, L_S, D), jnp.stack(ret_p), s_all,
            jnp.stack(v_rows))
```

```python
import functools

import jax
import jax.numpy as jnp
from jax import lax
from jax.experimental import pallas as pl
from jax.experimental.pallas import tpu as pltpu

F32 = jnp.float32
BF16 = jnp.bfloat16
I32 = jnp.int32

D = 1024
NB_P, L_P = 8, 2048
NB_S, L_S = 128, 8
PAST = 16384
DEPTH = 4
T_P = NB_P * L_P
T_S = NB_S * L_S
T = T_P + T_S
SGU_W = 2 * D
SGU_G = 8
SGU_GD = SGU_W // SGU_G
CHUNK = 128
H = 8
DK = D // H
DV = 2 * DK
QD = H * DK
VD = H * DV
RET_IN = 2 * QD + 2 * VD
ROPE_BASE = 10000.0
NE = 32
TOPK = 4
FE = D
ALPHA = 1.702
LIMIT = 7.0
EPS = 1e-6

LANES = 128
SLABS = D // (2 * LANES)
HIGH_HALF = -65536
TD = 512
NPT = T_P // TD
NT = T // TD
TM = 256
BLK = 512
TK = T * TOPK
N_BLOCKS = -(-(TK + NE * (BLK - 1)) // BLK)
N_ROWS = N_BLOCKS * BLK
SB = 16
ROUTE_TILES = 4
SH = 2
RET_CHUNKS = 4
VMEM_LIMIT = 56 * 1024 * 1024


def _cparams(n_axes):
    return pltpu.CompilerParams(dimension_semantics=("arbitrary",) * n_axes,
                                vmem_limit_bytes=VMEM_LIMIT)


def _rms(x, g):
    return (x * lax.rsqrt(jnp.mean(x * x, axis=-1, keepdims=True) + EPS)) * g


def _pick(is_p, p_ref, s_ref):
    return jnp.where(is_p, p_ref[0], s_ref[0])


def _mod_specs(layer, j, tm=TD):
    per_batch, n_prompt = L_P // tm, T_P // tm
    return [
        pl.BlockSpec((1, 1, D), lambda t: (layer * NB_P + jnp.minimum(t // per_batch, NB_P - 1), 0, j)),
        pl.BlockSpec((1, tm, D), lambda t: (layer, jnp.maximum(t - n_prompt, 0), j),
                     pipeline_mode=pl.Buffered(1)),
    ]


def _tile_spec(width, tm=TD):
    return pl.BlockSpec((tm, width), lambda t: (t, 0))


def _full_spec(shape):
    return pl.BlockSpec(shape, lambda *_: (0,) * len(shape), pipeline_mode=pl.Buffered(1))


def _mod_kernel(c_ref, w_ref, b_ref, p_ref, s_ref):
    c = c_ref[...]
    cs = (c * jax.nn.sigmoid(c)).astype(BF16)
    mod = jnp.dot(cs, w_ref[0].astype(BF16), preferred_element_type=F32) + b_ref[0]
    p_ref[0] = mod[:NB_P]
    per_batch = mod[NB_P:]
    s_ref[0] = jnp.broadcast_to(per_batch[:, None, :], (NB_S, L_S, per_batch.shape[-1])).reshape(T_S, -1)


def _modulation(c_all, w_mod, b_mod):
    tn = 1536
    nb = c_all.shape[0]
    return pl.pallas_call(
        _mod_kernel,
        grid=(DEPTH, 6 * D // tn),
        in_specs=[
            pl.BlockSpec((nb, D), lambda l, n: (0, 0)),
            pl.BlockSpec((1, D, tn), lambda l, n: (l, 0, n)),
            pl.BlockSpec((1, 1, tn), lambda l, n: (l, 0, n)),
        ],
        out_specs=[pl.BlockSpec((1, NB_P, tn), lambda l, n: (l, 0, n)),
                   pl.BlockSpec((1, T_S, tn), lambda l, n: (l, 0, n))],
        out_shape=[jax.ShapeDtypeStruct((DEPTH, NB_P, 6 * D), F32),
                   jax.ShapeDtypeStruct((DEPTH, T_S, 6 * D), F32)],
        compiler_params=_cparams(2),
        name="modulation",
    )(c_all, w_mod, b_mod.reshape(DEPTH, 1, 6 * D))


def _tail(is_p, x, y, g1p, g1s, sh2p, sh2s, sc2p, sc2s, n2g, wr, br, xo_ref, h2_ref, idx_ref, gate_ref,
          cnt_ref):
    xn = x + _pick(is_p, g1p, g1s) * y
    xo_ref[...] = xn
    h2 = (_rms(xn, n2g[...]) * (1.0 + _pick(is_p, sc2p, sc2s)) + _pick(is_p, sh2p, sh2s)).astype(BF16)
    h2_ref[...] = h2
    logit = lax.dot_general(wr[...], h2, (((1,), (1,)), ((), ())), preferred_element_type=F32)[:NE] + br[...]
    expert = lax.broadcasted_iota(I32, logit.shape, 0)
    vals, ids = [], []
    for _ in range(TOPK):
        m = jnp.max(logit, axis=0, keepdims=True)
        sel = jnp.min(jnp.where(logit == m, expert, NE), axis=0, keepdims=True)
        vals.append(m)
        ids.append(sel)
        logit = jnp.where(expert == sel, -jnp.inf, logit)
    es = [jnp.exp(v - vals[0]) for v in vals]
    tot = (es[0] + es[1]) + (es[2] + es[3])
    slot = lax.broadcasted_iota(I32, (LANES, logit.shape[1]), 0)
    idx_t = jnp.zeros(slot.shape, F32)
    gate_t = jnp.zeros(slot.shape, F32)
    for k in range(TOPK):
        idx_t = jnp.where(slot == k, ids[k].astype(F32), idx_t)
        gate_t = jnp.where(slot == k, es[k] / tot, gate_t)
    idx_out = idx_t.T.astype(I32)
    idx_ref[...] = idx_out
    gate_ref[...] = gate_t.T
    lane = lax.broadcasted_iota(I32, idx_out.shape, 1)
    chosen = jnp.zeros(idx_out.shape, F32)
    for k in range(TOPK):
        chosen = chosen + jnp.where(lane == idx_out[:, k:k + 1], 1.0, 0.0)

    @pl.when(pl.program_id(0) == 0)
    def _():
        cnt_ref[...] = jnp.zeros(cnt_ref.shape, F32)

    cnt_ref[...] += jnp.sum(chosen, axis=0, keepdims=True)


def _tail_in_specs(layer):
    return (_mod_specs(layer, 2) + _mod_specs(layer, 3) + _mod_specs(layer, 4)
            + [_full_spec((1, D)), _full_spec((LANES, D)), _full_spec((NE, 1))])


def _tail_out_specs():
    return [_tile_spec(D), _tile_spec(D), _tile_spec(LANES), _tile_spec(LANES),
            pl.BlockSpec((1, LANES), lambda t: (0, 0))]


def _tail_out_shapes():
    return [jax.ShapeDtypeStruct((T, D), F32), jax.ShapeDtypeStruct((T, D), BF16),
            jax.ShapeDtypeStruct((T, LANES), I32), jax.ShapeDtypeStruct((T, LANES), F32),
            jax.ShapeDtypeStruct((1, LANES), F32)]


def _sgu_kernel(x_ref, sh1p, sh1s, sc1p, sc1s, n1g, win, lng, lnb, mixw, mixb, wout,
                g1p, g1s, sh2p, sh2s, sc2p, sc2s, n2g, wr, br,
                xo_ref, h2_ref, idx_ref, gate_ref, cnt_ref, v_ref, y_scr):
    t = pl.program_id(0)
    is_p = t < NPT
    x = x_ref[...]
    h = _rms(x, n1g[...]) * (1.0 + _pick(is_p, sc1p, sc1s)) + _pick(is_p, sh1p, sh1s)
    z = jnp.dot(h.astype(BF16), win[...], preferred_element_type=F32)
    z = 0.5 * z * (1.0 + lax.erf(z * (0.5 ** 0.5)))
    u = z[:, :SGU_W]
    v = z[:, SGU_W:]
    vc = v - jnp.mean(v, axis=-1, keepdims=True)
    vn = vc * lax.rsqrt(jnp.mean(vc * vc, axis=-1, keepdims=True) + EPS) * lng[...] + lnb[...]

    @pl.when(t >= NPT)
    def _():
        v_ref[...] = vn

    vb = vn.astype(BF16)
    ri = lax.broadcasted_iota(I32, (CHUNK, CHUNK), 0)
    ci = lax.broadcasted_iota(I32, (CHUNK, CHUNK), 1)
    causal = ci <= ri
    shift = jnp.broadcast_to(jnp.where(is_p, 7, 3), ri.shape)
    keep = causal & (lax.shift_right_logical(ri, shift) == lax.shift_right_logical(ci, shift))
    for g in range(SGU_G):
        wg = jnp.where(keep, mixw[0, g], 0.0).astype(BF16)
        for c in range(TD // CHUNK):
            rows = slice(c * CHUNK, (c + 1) * CHUNK)
            cols = slice(g * SGU_GD, (g + 1) * SGU_GD)
            mixed = jnp.dot(wg, vb[rows, cols], preferred_element_type=F32) + mixb[0, :, cols]
            y_scr[rows, cols] = (u[rows, cols] * mixed).astype(BF16)
    y = jnp.dot(y_scr[...], wout[...], preferred_element_type=F32)
    _tail(is_p, x, y, g1p, g1s, sh2p, sh2s, sc2p, sc2s, n2g, wr, br, xo_ref, h2_ref, idx_ref, gate_ref,
          cnt_ref)


def _sgu_layer(layer, x, mod_p, mod_s, n1g, n2g, win, lng, lnb, mixw, mixb, wout, wr, br):
    sel = lambda t: (jnp.where(t < NPT, 0, 1), 0, 0, 0)
    in_specs = ([_tile_spec(D)] + _mod_specs(layer, 0) + _mod_specs(layer, 1)
                + [_full_spec((1, D)), _full_spec((D, 2 * SGU_W)), _full_spec((1, SGU_W)),
                   _full_spec((1, SGU_W)),
                   pl.BlockSpec((1, SGU_G, CHUNK, CHUNK), sel),
                   pl.BlockSpec((1, CHUNK, SGU_W), lambda t: (jnp.where(t < NPT, 0, 1), 0, 0)),
                   _full_spec((SGU_W, D))]
                + _tail_in_specs(layer))
    out_specs = _tail_out_specs() + [pl.BlockSpec((TD, SGU_W), lambda t: (jnp.maximum(t - NPT, 0), 0))]
    out_shape = _tail_out_shapes() + [jax.ShapeDtypeStruct((T_S, SGU_W), F32)]
    return pl.pallas_call(
        _sgu_kernel, grid=(NT,), in_specs=in_specs, out_specs=out_specs, out_shape=out_shape,
        scratch_shapes=[pltpu.VMEM((TD, SGU_W), BF16)],
        compiler_params=_cparams(1), name="sgu_layer",
    )(x, mod_p, mod_s, mod_p, mod_s, n1g, win, lng, lnb, mixw, mixb, wout,
      mod_p, mod_s, mod_p, mod_s, mod_p, mod_s, n2g, wr, br)


def _ret_proj_kernel(x_ref, sh1p, sh1s, sc1p, sc1s, n1g, win, cos_ref, sin_ref,
                     q_ref, k_ref, v_ref, sg_ref):
    t = pl.program_id(0)
    is_p = t < NPT
    x = x_ref[...]
    h = _rms(x, n1g[...]) * (1.0 + _pick(is_p, sc1p, sc1s)) + _pick(is_p, sh1p, sh1s)
    p = jnp.dot(h.astype(BF16), win[...], preferred_element_type=F32)
    cos = cos_ref[...]
    sin = sin_ref[...]
    for hd in range(H):
        cq = slice(hd * DK, (hd + 1) * DK)
        ck = slice(QD + hd * DK, QD + (hd + 1) * DK)
        qh = p[:, cq]
        kh = p[:, ck]
        q_ref[:, cq] = (qh * cos + pltpu.roll(qh, DK // 2, 1) * sin).astype(BF16)
        k_ref[:, cq] = ((kh * cos + pltpu.roll(kh, DK // 2, 1) * sin) * (DK ** -0.5)).astype(BF16)
    v_ref[...] = p[:, 2 * QD:2 * QD + VD].astype(BF16)
    g = p[:, 2 * QD + VD:]
    sg_ref[...] = (g * jax.nn.sigmoid(g)).astype(BF16)


def _ret_proj(layer, x, mod_p, mod_s, n1g, win, cos_tab, sin_tab):
    in_specs = ([_tile_spec(D)] + _mod_specs(layer, 0) + _mod_specs(layer, 1)
                + [_full_spec((1, D)), _full_spec((D, RET_IN)), _tile_spec(DK), _tile_spec(DK)])
    out_specs = [_tile_spec(QD), _tile_spec(QD), _tile_spec(VD), _tile_spec(VD)]
    out_shape = [jax.ShapeDtypeStruct((T, QD), BF16), jax.ShapeDtypeStruct((T, QD), BF16),
                 jax.ShapeDtypeStruct((T, VD), BF16), jax.ShapeDtypeStruct((T, VD), BF16)]
    return pl.pallas_call(
        _ret_proj_kernel, grid=(NT,), in_specs=in_specs, out_specs=out_specs, out_shape=out_shape,
        compiler_params=_cparams(1), name="ret_proj",
    )(x, mod_p, mod_s, mod_p, mod_s, n1g, win, cos_tab, sin_tab)


def _ret_intra(q, k, v, decay, qdec, kdec):
    s = lax.dot_general(q, k, (((1,), (1,)), ((), ())), preferred_element_type=F32) * decay
    o = jnp.dot(s.astype(BF16), v, preferred_element_type=F32)
    return o, q.astype(F32) * qdec, k.astype(F32) * kdec


def _ret_finish(o, sg, ng):
    on = o * lax.rsqrt(jnp.mean(o * o, axis=-1, keepdims=True) + EPS)
    return (sg.astype(F32) * (on * ng)).astype(BF16)


def _ret_prompt_kernel(gam_ref, q_ref, k_ref, v_ref, sg_ref, dec_ref, qdec_ref, kdec_ref, ng_ref,
                       y_ref, s_ref):
    c = pl.program_id(1)

    @pl.when(c == 0)
    def _():
        s_ref[...] = jnp.zeros(s_ref.shape, F32)

    cks = [slice(hd * DK, (hd + 1) * DK) for hd in range(H)]
    cvs = [slice(hd * DV, (hd + 1) * DV) for hd in range(H)]
    for j in range(RET_CHUNKS):
        rows = slice(j * CHUNK, (j + 1) * CHUNK)
        vs = [v_ref[rows, cv] for cv in cvs]
        intra = [_ret_intra(q_ref[rows, ck], k_ref[rows, ck], v, dec_ref[hd], qdec_ref[:, ck], kdec_ref[:, ck])
                 for hd, (ck, v) in enumerate(zip(cks, vs))]
        olds = [s_ref[0, hd] for hd in range(H)]
        outs = [o + jnp.dot(qd.astype(BF16), s_old.astype(BF16), preferred_element_type=F32)
                for (o, qd, _), s_old in zip(intra, olds)]
        for hd in range(H):
            s_ref[0, hd] = gam_ref[hd] * olds[hd] + lax.dot_general(
                intra[hd][2].astype(BF16), vs[hd], (((0,), (0,)), ((), ())), preferred_element_type=F32)
        for hd, cv in enumerate(cvs):
            y_ref[rows, cv] = _ret_finish(outs[hd], sg_ref[rows, cv], ng_ref[:, cv])


def _ret_prompt(q, k, v, sg, tabs, ng):
    rows = RET_CHUNKS * CHUNK
    nc = L_P // rows
    row = lambda b, c: (b * nc + c, 0)
    smem = pl.BlockSpec(memory_space=pltpu.SMEM)
    in_specs = [smem,
                pl.BlockSpec((rows, QD), row), pl.BlockSpec((rows, QD), row),
                pl.BlockSpec((rows, VD), row), pl.BlockSpec((rows, VD), row),
                _full_spec((H, CHUNK, CHUNK)), _full_spec((CHUNK, QD)), _full_spec((CHUNK, QD)),
                _full_spec((1, VD))]
    out_specs = [pl.BlockSpec((rows, VD), row),
                 pl.BlockSpec((1, H, DK, DV), lambda b, c: (b, 0, 0, 0))]
    out_shape = [jax.ShapeDtypeStruct((T, VD), BF16), jax.ShapeDtypeStruct((NB_P, H, DK, DV), F32)]
    return pl.pallas_call(
        _ret_prompt_kernel, grid=(NB_P, nc), in_specs=in_specs, out_specs=out_specs, out_shape=out_shape,
        compiler_params=_cparams(2), name="ret_prompt",
    )(tabs["gam"], q, k, v, sg, tabs["decay"], tabs["qdec"], tabs["kdec"], ng)


def _ret_sample_kernel(gam_ref, q_ref, k_ref, v_ref, sg_ref, dec_ref, qdec_ref, kdec_ref, ng_ref,
                       s_in_ref, y_in_ref, s_all_ref, y_ref, s_out_ref):
    del y_in_ref, s_all_ref
    s_in_ref = s_in_ref.at[0]
    s_out_ref = s_out_ref.at[0]
    row_batch = lax.broadcasted_iota(I32, (CHUNK, DK), 0) // L_S
    for i in range(SH):
        ck = slice(i * DK, (i + 1) * DK)
        cv = slice(i * DV, (i + 1) * DV)
        v = v_ref[:, cv]
        o, qd, kd = _ret_intra(q_ref[:, ck], k_ref[:, ck], v, dec_ref[i], qdec_ref[:, ck], kdec_ref[:, ck])
        gam = gam_ref[pl.program_id(1) * SH + i]
        cross = [jnp.dot(qd[b * L_S:(b + 1) * L_S].astype(BF16), s_in_ref[b, i].astype(BF16),
                         preferred_element_type=F32) for b in range(SB)]
        for b in range(SB):
            kb = jnp.where(row_batch == b, kd, 0.0).astype(BF16)
            s_out_ref[b, i] = gam * s_in_ref[b, i] + lax.dot_general(
                kb, v, (((0,), (0,)), ((), ())), preferred_element_type=F32)
        o = o + jnp.concatenate(cross, axis=0)
        y_ref[:, cv] = _ret_finish(o, sg_ref[:, cv], ng_ref[:, cv])


def _ret_sample(j, q, k, v, sg, tabs, ng, s_in, y_prev, s_all):
    base = T_P // CHUNK
    rq = lambda g, h: (base + g, h)
    st = pl.BlockSpec((1, SB, SH, DK, DV), lambda g, h: (j, g, h, 0, 0))
    smem = pl.BlockSpec(memory_space=pltpu.SMEM)
    in_specs = [smem,
                pl.BlockSpec((CHUNK, SH * DK), rq), pl.BlockSpec((CHUNK, SH * DK), rq),
                pl.BlockSpec((CHUNK, SH * DV), rq), pl.BlockSpec((CHUNK, SH * DV), rq),
                pl.BlockSpec((SH, CHUNK, CHUNK), lambda g, h: (h, 0, 0)),
                pl.BlockSpec((CHUNK, SH * DK), lambda g, h: (0, h)),
                pl.BlockSpec((CHUNK, SH * DK), lambda g, h: (0, h)),
                pl.BlockSpec((1, SH * DV), lambda g, h: (0, h)),
                st, pl.BlockSpec(memory_space=pl.ANY), pl.BlockSpec(memory_space=pl.ANY)]
    out_specs = [pl.BlockSpec((CHUNK, SH * DV), rq), st]
    out_shape = [jax.ShapeDtypeStruct((T, VD), BF16), jax.ShapeDtypeStruct(s_all.shape, F32)]
    return pl.pallas_call(
        _ret_sample_kernel, grid=(NB_S // SB, H // SH), in_specs=in_specs, out_specs=out_specs,
        out_shape=out_shape, input_output_aliases={10: 0, 11: 1},
        compiler_params=_cparams(2), name="ret_sample",
    )(tabs["gam"], q, k, v, sg, tabs["decay"], tabs["qdec"], tabs["kdec"], ng, s_in, y_prev, s_all)


def _ret_out_kernel(y_ref, wout, x_ref, g1p, g1s, sh2p, sh2s, sc2p, sc2s, n2g, wr, br,
                    xo_ref, h2_ref, idx_ref, gate_ref, cnt_ref):
    is_p = pl.program_id(0) < NPT
    y = jnp.dot(y_ref[...], wout[...], preferred_element_type=F32)
    _tail(is_p, x_ref[...], y, g1p, g1s, sh2p, sh2s, sc2p, sc2s, n2g, wr, br,
          xo_ref, h2_ref, idx_ref, gate_ref, cnt_ref)


def _ret_out(layer, y, wout, x, mod_p, mod_s, n2g, wr, br):
    in_specs = [_tile_spec(VD), _full_spec((VD, D)), _tile_spec(D)] + _tail_in_specs(layer)
    return pl.pallas_call(
        _ret_out_kernel, grid=(NT,), in_specs=in_specs, out_specs=_tail_out_specs(),
        out_shape=_tail_out_shapes(), compiler_params=_cparams(1), name="ret_out",
    )(y, wout, x, mod_p, mod_s, mod_p, mod_s, mod_p, mod_s, n2g, wr, br)


def _lane_prefix(v, lane1):
    s = 1
    while s < LANES:
        v = v + jnp.where(lane1 >= s, pltpu.roll(v, s, 1), 0.0)
        s *= 2
    return v


def _route_kernel(cnt_ref, idx_ref, lp_ref, lpt_ref, runs_ref, meta_ref, base_scr):
    lane = lax.broadcasted_iota(I32, (TM, LANES), 1)
    lane1 = lax.broadcasted_iota(I32, (1, LANES), 1)
    row = lax.broadcasted_iota(I32, (8, LANES), 0)

    @pl.when(pl.program_id(0) == 0)
    def _():
        cnt = cnt_ref[...]
        padded = (((cnt.astype(I32) + (BLK - 1)) // BLK) * BLK).astype(F32)
        end = _lane_prefix(padded, lane1)
        base_scr[...] = end - padded
        meta = jnp.where(row == 0, cnt, jnp.where(row == 1, end - padded, jnp.where(row == 2, end, 0.0)))
        meta_ref[...] = meta.astype(I32)

    ri = lax.broadcasted_iota(I32, (TM, TM), 0)
    ci = lax.broadcasted_iota(I32, (TM, TM), 1)
    before = jnp.where(ci < ri, 1.0, 0.0).astype(BF16)
    base = base_scr[...]
    for j in range(ROUTE_TILES):
        idx = idx_ref[j * TM:(j + 1) * TM, :]
        hits = [lane == idx[:, k:k + 1] for k in range(TOPK)]
        chosen = jnp.zeros((TM, LANES), F32)
        for hk in hits:
            chosen = chosen + jnp.where(hk, 1.0, 0.0)
        colsum = jnp.sum(chosen, axis=0, keepdims=True)
        loff = _lane_prefix(colsum, lane1) - colsum
        pos = jnp.dot(before, chosen.astype(BF16), preferred_element_type=F32) + loff
        lp = jnp.zeros((TM, LANES), F32)
        for k, hk in enumerate(hits):
            lp = jnp.where(lane == k, jnp.sum(jnp.where(hk, pos, 0.0), axis=-1, keepdims=True), lp)
        lp_ref[j * TM:(j + 1) * TM, :] = lp.astype(I32)
        lpt_ref[j * 8:(j + 1) * 8, :] = lp.T[:8].astype(I32)
        runs = jnp.where(row == 0, colsum, jnp.where(row == 1, loff, jnp.where(row == 2, base, 0.0)))
        runs_ref[j] = runs.astype(I32)
        base = base + colsum
    base_scr[...] = base


def _route(cnt, idx):
    n_tiles = idx.shape[0] // TM
    assert n_tiles % ROUTE_TILES == 0
    rt = ROUTE_TILES
    return pl.pallas_call(
        _route_kernel, grid=(n_tiles // rt,),
        in_specs=[pl.BlockSpec((1, LANES), lambda t: (0, 0)), pl.BlockSpec((rt * TM, LANES), lambda t: (t, 0))],
        out_specs=[pl.BlockSpec((rt * TM, LANES), lambda t: (t, 0)),
                   pl.BlockSpec((rt * 8, TM), lambda t: (t, 0)),
                   pl.BlockSpec((rt, 8, LANES), lambda t: (t, 0, 0)),
                   pl.BlockSpec((8, LANES), lambda t: (0, 0))],
        out_shape=[jax.ShapeDtypeStruct((n_tiles * TM, LANES), I32),
                   jax.ShapeDtypeStruct((n_tiles * 8, TM), I32),
                   jax.ShapeDtypeStruct((n_tiles, 8, LANES), I32),
                   jax.ShapeDtypeStruct((8, LANES), I32)],
        scratch_shapes=[pltpu.VMEM((1, LANES), F32)],
        compiler_params=_cparams(1), name="moe_route",
    )(cnt, idx)


def _tile_rows(ref, r, n):
    return ref.at[:, pl.ds(r, n), :]


def _run_copy(src, dst, r_src, r_dst, n, sem, wait):
    for b in reversed(range(TM.bit_length())):
        size = 1 << b
        off = (n >> (b + 1)) << (b + 1)

        def piece(off=off, size=size):
            cp = pltpu.make_async_copy(_tile_rows(src, r_src + off, size),
                                       _tile_rows(dst, r_dst + off, size), sem)
            if wait:
                cp.wait()
            else:
                cp.start()

        if isinstance(n, int):
            if n & size:
                piece()
        else:
            pl.when((n & size) != 0)(piece)


def _load_rows(ref3, rows=None):
    rows = ref3.shape[1] if rows is None else rows
    parts = []
    for s in range(SLABS):
        word = ref3[s, :rows, :]
        parts.append(lax.bitcast_convert_type(lax.shift_left(word, 16), F32))
        parts.append(lax.bitcast_convert_type(word & HIGH_HALF, F32))
    return jnp.concatenate(parts, axis=1).astype(BF16)


def _store_rows(ref3, val):
    rows = val.shape[0]
    bits = lax.bitcast_convert_type(val, I32)
    for s in range(SLABS):
        low = lax.shift_right_logical(bits[:, (2 * s) * LANES:(2 * s + 1) * LANES], 16)
        ref3[s, :rows, :] = bits[:, (2 * s + 1) * LANES:(2 * s + 2) * LANES] | low


def _dispatch_kernel(len_ref, pos_ref, row_ref, cnt_ref, first_ref, end_ref, lpt_ref, h2_ref, dst,
                     buf, zbuf, sem, zsem):
    n_rows = dst.shape[1]
    step = pl.program_id(0)
    slot = step % 2
    cur = buf.at[slot]
    lpt = lpt_ref[...]
    p = lax.broadcasted_iota(I32, (TM * TOPK, TM), 0)
    pick = lpt[0:1, :] == p
    for k in range(1, TOPK):
        pick = pick | (lpt[k:k + 1, :] == p)
    perm = jnp.where(pick, 1.0, 0.0).astype(BF16)
    _store_rows(cur, jnp.dot(perm, h2_ref[...], preferred_element_type=F32))

    def per_expert(e, carry):
        _run_copy(cur, dst, pos_ref[e], row_ref[e], len_ref[e], sem.at[slot], False)
        return carry

    lax.fori_loop(0, NE, per_expert, 0)

    def drain(s):
        pltpu.make_async_copy(buf.at[s], _tile_rows(dst, 0, TM * TOPK), sem.at[s]).wait()

    @pl.when(step > 0)
    def _():
        drain(1 - slot)

    @pl.when(step == pl.num_programs(0) - 1)
    def _():
        drain(slot)

    @pl.when(step == 0)
    def _():
        zbuf[...] = jnp.zeros(zbuf.shape, I32)
        n_tail = (n_rows - end_ref[NE - 1]) // TM
        for wait in (False, True):
            def pad(e, carry, wait=wait):
                lo = first_ref[e] + cnt_ref[e]
                _run_copy(zbuf, dst, 0, lo, end_ref[e] - lo, zsem, wait)
                return carry

            lax.fori_loop(0, NE, pad, 0)

            def tail(j, carry, wait=wait):
                _run_copy(zbuf, dst, 0, end_ref[NE - 1] + j * TM, TM, zsem, wait)
                return carry

            lax.fori_loop(0, n_tail, tail, 0)


def _dispatch(h2, lpt, run_len, run_pos, run_row, cnt, first, end, n_rows):
    assert BLK % TM == 0 and BLK < 2 * TM + 1
    n_tiles = h2.shape[0] // TM
    smem = pl.BlockSpec(memory_space=pltpu.SMEM)
    per_tile = pl.BlockSpec((LANES,), lambda t: (t,), memory_space=pltpu.SMEM)
    return pl.pallas_call(
        _dispatch_kernel, grid=(n_tiles,),
        in_specs=[per_tile, per_tile, per_tile, smem, smem, smem,
                  pl.BlockSpec((8, TM), lambda t: (t, 0)), _tile_spec(D, TM)],
        out_specs=pl.BlockSpec(memory_space=pl.ANY),
        out_shape=jax.ShapeDtypeStruct((SLABS, n_rows, LANES), I32),
        scratch_shapes=[pltpu.VMEM((2, SLABS, TM * TOPK, LANES), I32),
                        pltpu.VMEM((SLABS, BLK, LANES), I32),
                        pltpu.SemaphoreType.DMA((2,)), pltpu.SemaphoreType.DMA],
        compiler_params=_cparams(1), name="moe_dispatch",
    )(run_len, run_pos, run_row, cnt, first, end, lpt, h2)


def _expert_kernel(be_ref, nu_ref, nr_ref, nxt_ref, par_ref, xb_ref, wu_hbm, bu_ref, wd_hbm, bd_ref, yb_ref,
                   wuf, wdf, wub, wdb, sem, *, layer):
    i = pl.program_id(0)
    used = i < nu_ref[0]
    half = nr_ref[i] <= BLK // 2
    slot = par_ref[i]

    def weights(e, s):
        return (pltpu.make_async_copy(wu_hbm.at[layer, e], wuf.at[s], sem.at[s, 0]),
                pltpu.make_async_copy(wd_hbm.at[layer, e], wdf.at[s], sem.at[s, 1]))

    @pl.when(i == 0)
    def _():
        for cp in weights(be_ref[0], slot):
            cp.start()

    @pl.when(used & ((i == 0) | (be_ref[i] != be_ref[jnp.maximum(i - 1, 0)])))
    def _():
        for cp in weights(be_ref[i], slot):
            cp.wait()
        wub[...] = wuf[slot].astype(BF16)
        wdb[...] = wdf[slot].astype(BF16)

        @pl.when(nxt_ref[i] >= 0)
        def _():
            for cp in weights(nxt_ref[i], 1 - slot):
                cp.start()

    def swiglu_rows(rows):
        z = jnp.dot(_load_rows(xb_ref, rows), wub[...], preferred_element_type=F32) + bu_ref[0, 0]
        glu = jnp.minimum(z[:, :FE], LIMIT)
        lin = jnp.clip(z[:, FE:], -LIMIT, LIMIT)
        act = glu * jax.nn.sigmoid(ALPHA * glu) * (lin + 1.0)
        y = jnp.dot(act.astype(BF16), wdb[...], preferred_element_type=F32) + bd_ref[0, 0]
        _store_rows(yb_ref, y.astype(BF16).astype(F32))

    @pl.when(used & jnp.logical_not(half))
    def _():
        swiglu_rows(BLK)

    @pl.when(used & half)
    def _():
        swiglu_rows(BLK // 2)
        yb_ref[:, BLK // 2:, :] = jnp.zeros((SLABS, BLK // 2, LANES), I32)

    @pl.when(jnp.logical_not(used))
    def _():
        yb_ref[...] = jnp.zeros(yb_ref.shape, I32)


def _experts(layer, xb, blk_e, n_used, blk_rows, blk_next, blk_par, wu, bu, wd, bd):
    n_blocks = blk_e.shape[0]
    grid_spec = pltpu.PrefetchScalarGridSpec(
        num_scalar_prefetch=5, grid=(n_blocks,),
        in_specs=[pl.BlockSpec((SLABS, BLK, LANES), lambda i, be, nu, *_: (0, jnp.minimum(i, nu[0] - 1), 0)),
                  pl.BlockSpec(memory_space=pl.ANY),
                  pl.BlockSpec((1, 1, 1, 2 * FE), lambda i, be, *_: (layer, be[i], 0, 0)),
                  pl.BlockSpec(memory_space=pl.ANY),
                  pl.BlockSpec((1, 1, 1, D), lambda i, be, *_: (layer, be[i], 0, 0))],
        out_specs=pl.BlockSpec((SLABS, BLK, LANES), lambda i, *_: (0, i, 0)),
        scratch_shapes=[pltpu.VMEM((2, D, 2 * FE), F32), pltpu.VMEM((2, FE, D), F32),
                        pltpu.VMEM((D, 2 * FE), BF16), pltpu.VMEM((FE, D), BF16),
                        pltpu.SemaphoreType.DMA((2, 2))])
    return pl.pallas_call(
        functools.partial(_expert_kernel, layer=layer), grid_spec=grid_spec,
        out_shape=jax.ShapeDtypeStruct((SLABS, n_blocks * BLK, LANES), I32),
        compiler_params=_cparams(1), name="experts",
    )(blk_e, n_used, blk_rows, blk_next, blk_par, xb, wu, bu, wd, bd)


def _combine_kernel(len_ref, pos_ref, row_ref, len_nx, pos_nx, row_nx, yb, lp_ref, x_ref, gate_ref,
                    g2p, g2s, fg_ref, *outs_and_scratch, final):
    *o_ref, ybuf, sem = outs_and_scratch
    step = pl.program_id(0)
    slot = step % 2
    is_p = step < T_P // TM

    def fetch(len_r, pos_r, row_r, s):
        def per_expert(e, carry):
            _run_copy(yb, ybuf.at[s], row_r[e], pos_r[e], len_r[e], sem.at[s], False)
            return carry

        lax.fori_loop(0, NE, per_expert, 0)

    @pl.when(step == 0)
    def _():
        fetch(len_ref, pos_ref, row_ref, slot)

    @pl.when(step + 1 < pl.num_programs(0))
    def _():
        fetch(len_nx, pos_nx, row_nx, 1 - slot)

    cur = ybuf.at[slot]
    pltpu.make_async_copy(_tile_rows(yb, 0, TM * TOPK), cur, sem.at[slot]).wait()

    rows = x_ref.shape[0]
    lp = lp_ref[...]
    gates = gate_ref[...]
    p = lax.broadcasted_iota(I32, (rows, rows * TOPK), 1)
    weights = jnp.zeros((rows, rows * TOPK), F32)
    for k in range(TOPK):
        weights = jnp.where(lp[:, k:k + 1] == p, gates[:, k:k + 1], weights)
    acc = jnp.dot(weights.astype(BF16), _load_rows(cur), preferred_element_type=F32)
    xn = x_ref[...] + _pick(is_p, g2p, g2s) * acc
    if not final:
        o_ref[0][...] = xn
        return
    y = _rms(xn, fg_ref[...])
    op_ref, os_ref = o_ref

    @pl.when(is_p)
    def _():
        op_ref[...] = y

    @pl.when(jnp.logical_not(is_p))
    def _():
        os_ref[...] = y


def _combine(layer, x, yb, lp, run_len, run_pos, run_row, gates, mod_p, mod_s, final_g, final):
    n_tiles = x.shape[0] // TM
    n_prompt = T_P // TM
    if final:
        out_specs = [pl.BlockSpec((TM, D), lambda t: (jnp.minimum(t, n_prompt - 1), 0)),
                     pl.BlockSpec((TM, D), lambda t: (jnp.maximum(t - n_prompt, 0), 0))]
        out_shape = [jax.ShapeDtypeStruct((T_P, D), F32), jax.ShapeDtypeStruct((x.shape[0] - T_P, D), F32)]
    else:
        out_specs, out_shape = _tile_spec(D, TM), jax.ShapeDtypeStruct(x.shape, F32)
    per_tile = pl.BlockSpec((LANES,), lambda t: (t,), memory_space=pltpu.SMEM)
    next_tile = pl.BlockSpec((LANES,), lambda t: (jnp.minimum(t + 1, n_tiles - 1),),
                             memory_space=pltpu.SMEM)
    in_specs = ([per_tile, per_tile, per_tile, next_tile, next_tile, next_tile,
                 pl.BlockSpec(memory_space=pl.ANY),
                 _tile_spec(LANES, TM), _tile_spec(D, TM), _tile_spec(LANES, TM)]
                + _mod_specs(layer, 5, TM) + [_full_spec((1, D))])
    return pl.pallas_call(
        functools.partial(_combine_kernel, final=final), grid=(n_tiles,), in_specs=in_specs,
        out_specs=out_specs, out_shape=out_shape,
        scratch_shapes=[pltpu.VMEM((2, SLABS, TM * TOPK, LANES), I32), pltpu.SemaphoreType.DMA((2,))],
        compiler_params=_cparams(1), name="moe_combine",
    )(run_len, run_pos, run_row, run_len, run_pos, run_row, yb, lp, x, gates, mod_p, mod_s, final_g)


def _block_tables(cnt, first, end, n_blocks):
    experts = jnp.arange(NE, dtype=I32)
    blk_first = jnp.arange(n_blocks, dtype=I32) * BLK
    blk_e = jnp.minimum(jnp.sum((end[None, :] <= blk_first[:, None]).astype(I32), axis=1), NE - 1)
    n_used = end[NE - 1:] // BLK
    has_rows = cnt > 0
    later = has_rows[None, :] & (experts[None, :] > experts[:, None])
    next_e = jnp.min(jnp.where(later, experts[None, :], NE), axis=1)
    next_e = jnp.where(next_e == NE, -1, next_e)
    parity = (jnp.cumsum(has_rows.astype(I32)) - 1) % 2
    own = blk_e[:, None] == experts[None, :]
    of_block = lambda per_expert: jnp.sum(jnp.where(own, per_expert[None, :], 0), axis=1)
    blk_rows = jnp.clip(of_block(first + cnt) - blk_first, 0, BLK)
    return blk_e, n_used, blk_rows, of_block(next_e), of_block(parity)


def _moe(layer, x, h2, idx, gates, cnt_all, mod_p, mod_s, wu, bu, wd, bd, final_g, final):
    lp, lpt, runs, meta = _route(cnt_all, idx)
    cnt, first, end = meta[0, :NE], meta[1, :NE], meta[2, :NE]
    run_len, run_pos, run_row = (runs[:, r, :].reshape(-1) for r in range(3))
    xb = _dispatch(h2, lpt, run_len, run_pos, run_row, cnt, first, end, N_ROWS)
    yb = _experts(layer, xb, *_block_tables(cnt, first, end, N_BLOCKS), wu, bu, wd, bd)
    return _combine(layer, x, yb, lp, run_len, run_pos, run_row, gates, mod_p, mod_s, final_g, final)


def _rope_tables():
    half = DK // 2
    inv = 1.0 / (ROPE_BASE ** jnp.linspace(0.0, 1.0, half, dtype=F32))

    def tab(pos):
        ang = pos.astype(F32)[:, None] * inv[None, :]
        cos, sin = jnp.cos(ang), jnp.sin(ang)
        return jnp.concatenate([cos, cos], -1), jnp.concatenate([-sin, sin], -1)

    cp, sp = tab(jnp.arange(L_P, dtype=I32))
    cs, ss = tab(PAST + jnp.arange(L_S, dtype=I32))
    cos = jnp.concatenate([jnp.tile(cp, (NB_P, 1)), jnp.tile(cs, (NB_S, 1))], 0)
    sin = jnp.concatenate([jnp.tile(sp, (NB_P, 1)), jnp.tile(ss, (NB_S, 1))], 0)
    return cos, sin


def _decay_tables(cl):
    lg = jnp.log(1.0 - 2.0 ** (-5.0 - jnp.arange(H, dtype=F32)))
    r = jnp.arange(CHUNK)
    idx = (r % cl).astype(F32)
    diff = idx[:, None] - idx[None, :]
    same = (r[:, None] // cl) == (r[None, :] // cl)
    decay = jnp.where((same & (diff >= 0))[None],
                      jnp.exp(lg[:, None, None] * jnp.maximum(diff, 0.0)[None]), 0.0)
    qdec = jnp.exp(lg[None, :] * (idx[:, None] + 1.0))
    kdec = jnp.exp(lg[None, :] * (cl - 1.0 - idx[:, None]))
    wide = lambda a: jnp.repeat(a, DK, axis=1)
    return {"decay": decay, "qdec": wide(qdec), "kdec": wide(kdec), "gam": jnp.exp(lg * cl)}


def kernel(x_prompt, x_sample, c_prompt, c_sample, state_ret, w_mod, b_mod, norm1_g, norm2_g,
           sgu_w_in, sgu_ln_g, sgu_ln_b, sgu_w_s, sgu_b_s, sgu_w_out, ret_w_in, ret_norm_g, ret_w_out,
           moe_w_router, moe_b_router, moe_w_up, moe_b_up, moe_w_down, moe_b_down, final_g):
    x = jnp.concatenate([x_prompt.reshape(T_P, D), x_sample.reshape(T_S, D)], 0)
    mod_p, mod_s = _modulation(jnp.concatenate([c_prompt, c_sample], 0), w_mod, b_mod)
    mod_p = mod_p.reshape(DEPTH * NB_P, 1, 6 * D)
    cos_tab, sin_tab = _rope_tables()
    tabs_p = _decay_tables(CHUNK)
    tabs_s = _decay_tables(L_S)
    wr_rows = jnp.pad(jnp.swapaxes(moe_w_router, 1, 2), ((0, 0), (0, LANES - NE), (0, 0))).astype(BF16)
    fg = final_g.reshape(1, D)
    b_up = moe_b_up.reshape(DEPTH, NE, 1, 2 * FE)
    b_down = moe_b_down.reshape(DEPTH, NE, 1, D)

    ret_p, v_rows = [], []
    s_all = lax.empty(state_ret.shape, F32)
    for i in range(DEPTH):
        j = i // 2
        n1g = norm1_g[i].reshape(1, D)
        n2g = norm2_g[i].reshape(1, D)
        wr = wr_rows[i]
        br = moe_b_router[i].reshape(NE, 1)
        if i % 2 == 0:
            mixw = jnp.stack([sgu_w_s[j], jnp.tile(sgu_w_s[j][:, :L_S, :L_S], (1, SB, SB))])
            bias_p = jnp.repeat(sgu_b_s[j].T, SGU_GD, axis=1)
            bias_s = jnp.tile(bias_p[:L_S], (SB, 1))
            x, h2, idx, gates, cnt, v = _sgu_layer(
                i, x, mod_p, mod_s, n1g, n2g, sgu_w_in[j].astype(BF16), sgu_ln_g[j].reshape(1, SGU_W),
                sgu_ln_b[j].reshape(1, SGU_W), mixw, jnp.stack([bias_p, bias_s]),
                sgu_w_out[j].astype(BF16), wr, br)
            v_rows.append(v.reshape(NB_S, L_S, SGU_W))
        else:
            q, k, v, sg = _ret_proj(i, x, mod_p, mod_s, n1g, ret_w_in[j].astype(BF16), cos_tab, sin_tab)
            ng = ret_norm_g[j].reshape(1, VD)
            y, s_p = _ret_prompt(q, k, v, sg, tabs_p, ng)
            y, s_all = _ret_sample(j, q, k, v, sg, tabs_s, ng, state_ret, y, s_all)
            ret_p.append(s_p)
            x, h2, idx, gates, cnt = _ret_out(i, y, ret_w_out[j].astype(BF16), x, mod_p, mod_s, n2g, wr, br)
        x = _moe(i, x, h2, idx, gates, cnt, mod_p, mod_s, moe_w_up, b_up, moe_w_down, b_down,
                 fg, final=(i == DEPTH - 1))
    y_prompt, y_sample = x
    return (y_prompt.reshape(NB_P, L_P, D), y_sample.reshape(NB_S, L_S, D), jnp.stack(ret_p), s_all,
            jnp.stack(v_rows))
```

```python
import functools

import jax
import jax.numpy as jnp
from jax import lax
from jax.experimental import pallas as pl
from jax.experimental.pallas import tpu as pltpu

F32 = jnp.float32
BF16 = jnp.bfloat16
I32 = jnp.int32

D = 1024
NB_P, L_P = 8, 2048
NB_S, L_S = 128, 8
PAST = 16384
DEPTH = 4
T_P = NB_P * L_P
T_S = NB_S * L_S
T = T_P + T_S
SGU_W = 2 * D
SGU_G = 8
SGU_GD = SGU_W // SGU_G
CHUNK = 128
H = 8
DK = D // H
DV = 2 * DK
QD = H * DK
VD = H * DV
RET_IN = 2 * QD + 2 * VD
ROPE_BASE = 10000.0
NE = 32
TOPK = 4
FE = D
ALPHA = 1.702
LIMIT = 7.0
EPS = 1e-6

LANES = 128
SLABS = D // (2 * LANES)
HIGH_HALF = -65536
TD = 512
NPT = T_P // TD
NT = T // TD
TM = 256
BLK = 512
TK = T * TOPK
N_BLOCKS = -(-(TK + NE * (BLK - 1)) // BLK)
N_ROWS = N_BLOCKS * BLK
SB = 16
ROUTE_TILES = 4
SH = 4
RET_CHUNKS = 8
VMEM_LIMIT = 56 * 1024 * 1024


def _cparams(n_axes):
    return pltpu.CompilerParams(dimension_semantics=("arbitrary",) * n_axes,
                                vmem_limit_bytes=VMEM_LIMIT)


def _rms(x, g):
    return (x * lax.rsqrt(jnp.mean(x * x, axis=-1, keepdims=True) + EPS)) * g


def _pick(is_p, p_ref, s_ref):
    return jnp.where(is_p, p_ref[0], s_ref[0])


def _mod_specs(layer, j, tm=TD):
    per_batch, n_prompt = L_P // tm, T_P // tm
    return [
        pl.BlockSpec((1, 1, D), lambda t: (layer * NB_P + jnp.minimum(t // per_batch, NB_P - 1), 0, j)),
        pl.BlockSpec((1, tm, D), lambda t: (layer, jnp.maximum(t - n_prompt, 0), j),
                     pipeline_mode=pl.Buffered(1)),
    ]


def _tile_spec(width, tm=TD):
    return pl.BlockSpec((tm, width), lambda t: (t, 0))


def _full_spec(shape):
    return pl.BlockSpec(shape, lambda *_: (0,) * len(shape), pipeline_mode=pl.Buffered(1))


def _mod_kernel(c_ref, w_ref, b_ref, p_ref, s_ref):
    c = c_ref[...]
    cs = (c * jax.nn.sigmoid(c)).astype(BF16)
    mod = jnp.dot(cs, w_ref[0].astype(BF16), preferred_element_type=F32) + b_ref[0]
    p_ref[0] = mod[:NB_P]
    per_batch = mod[NB_P:]
    s_ref[0] = jnp.broadcast_to(per_batch[:, None, :], (NB_S, L_S, per_batch.shape[-1])).reshape(T_S, -1)


def _modulation(c_all, w_mod, b_mod):
    tn = 1536
    nb = c_all.shape[0]
    return pl.pallas_call(
        _mod_kernel,
        grid=(DEPTH, 6 * D // tn),
        in_specs=[
            pl.BlockSpec((nb, D), lambda l, n: (0, 0)),
            pl.BlockSpec((1, D, tn), lambda l, n: (l, 0, n)),
            pl.BlockSpec((1, 1, tn), lambda l, n: (l, 0, n)),
        ],
        out_specs=[pl.BlockSpec((1, NB_P, tn), lambda l, n: (l, 0, n)),
                   pl.BlockSpec((1, T_S, tn), lambda l, n: (l, 0, n))],
        out_shape=[jax.ShapeDtypeStruct((DEPTH, NB_P, 6 * D), F32),
                   jax.ShapeDtypeStruct((DEPTH, T_S, 6 * D), F32)],
        compiler_params=_cparams(2),
        name="modulation",
    )(c_all, w_mod, b_mod.reshape(DEPTH, 1, 6 * D))


def _tail(is_p, x, y, g1p, g1s, sh2p, sh2s, sc2p, sc2s, n2g, wr, br, xo_ref, h2_ref, idx_ref, gate_ref,
          cnt_ref):
    xn = x + _pick(is_p, g1p, g1s) * y
    xo_ref[...] = xn
    h2 = (_rms(xn, n2g[...]) * (1.0 + _pick(is_p, sc2p, sc2s)) + _pick(is_p, sh2p, sh2s)).astype(BF16)
    h2_ref[...] = h2
    logit = lax.dot_general(wr[...], h2, (((1,), (1,)), ((), ())), preferred_element_type=F32)[:NE] + br[...]
    expert = lax.broadcasted_iota(I32, logit.shape, 0)
    vals, ids = [], []
    for _ in range(TOPK):
        m = jnp.max(logit, axis=0, keepdims=True)
        sel = jnp.min(jnp.where(logit == m, expert, NE), axis=0, keepdims=True)
        vals.append(m)
        ids.append(sel)
        logit = jnp.where(expert == sel, -jnp.inf, logit)
    es = [jnp.exp(v - vals[0]) for v in vals]
    tot = (es[0] + es[1]) + (es[2] + es[3])
    slot = lax.broadcasted_iota(I32, (LANES, logit.shape[1]), 0)
    idx_t = jnp.zeros(slot.shape, F32)
    gate_t = jnp.zeros(slot.shape, F32)
    for k in range(TOPK):
        idx_t = jnp.where(slot == k, ids[k].astype(F32), idx_t)
        gate_t = jnp.where(slot == k, es[k] / tot, gate_t)
    idx_out = idx_t.T.astype(I32)
    idx_ref[...] = idx_out
    gate_ref[...] = gate_t.T
    lane = lax.broadcasted_iota(I32, idx_out.shape, 1)
    chosen = jnp.zeros(idx_out.shape, F32)
    for k in range(TOPK):
        chosen = chosen + jnp.where(lane == idx_out[:, k:k + 1], 1.0, 0.0)

    @pl.when(pl.program_id(0) == 0)
    def _():
        cnt_ref[...] = jnp.zeros(cnt_ref.shape, F32)

    cnt_ref[...] += jnp.sum(chosen, axis=0, keepdims=True)


def _tail_in_specs(layer):
    return (_mod_specs(layer, 2) + _mod_specs(layer, 3) + _mod_specs(layer, 4)
            + [_full_spec((1, D)), _full_spec((LANES, D)), _full_spec((NE, 1))])


def _tail_out_specs():
    return [_tile_spec(D), _tile_spec(D), _tile_spec(LANES), _tile_spec(LANES),
            pl.BlockSpec((1, LANES), lambda t: (0, 0))]


def _tail_out_shapes():
    return [jax.ShapeDtypeStruct((T, D), F32), jax.ShapeDtypeStruct((T, D), BF16),
            jax.ShapeDtypeStruct((T, LANES), I32), jax.ShapeDtypeStruct((T, LANES), F32),
            jax.ShapeDtypeStruct((1, LANES), F32)]


def _sgu_kernel(x_ref, sh1p, sh1s, sc1p, sc1s, n1g, win, lng, lnb, mixw, mixb, wout,
                g1p, g1s, sh2p, sh2s, sc2p, sc2s, n2g, wr, br,
                xo_ref, h2_ref, idx_ref, gate_ref, cnt_ref, v_ref, y_scr):
    t = pl.program_id(0)
    is_p = t < NPT
    x = x_ref[...]
    h = _rms(x, n1g[...]) * (1.0 + _pick(is_p, sc1p, sc1s)) + _pick(is_p, sh1p, sh1s)
    z = jnp.dot(h.astype(BF16), win[...], preferred_element_type=F32)
    z = 0.5 * z * (1.0 + lax.erf(z * (0.5 ** 0.5)))
    u = z[:, :SGU_W]
    v = z[:, SGU_W:]
    vc = v - jnp.mean(v, axis=-1, keepdims=True)
    vn = vc * lax.rsqrt(jnp.mean(vc * vc, axis=-1, keepdims=True) + EPS) * lng[...] + lnb[...]

    @pl.when(t >= NPT)
    def _():
        v_ref[...] = vn

    vb = vn.astype(BF16)
    ri = lax.broadcasted_iota(I32, (CHUNK, CHUNK), 0)
    ci = lax.broadcasted_iota(I32, (CHUNK, CHUNK), 1)
    causal = ci <= ri
    shift = jnp.broadcast_to(jnp.where(is_p, 7, 3), ri.shape)
    keep = causal & (lax.shift_right_logical(ri, shift) == lax.shift_right_logical(ci, shift))
    for g in range(SGU_G):
        wg = jnp.where(keep, mixw[0, g], 0.0).astype(BF16)
        for c in range(TD // CHUNK):
            rows = slice(c * CHUNK, (c + 1) * CHUNK)
            cols = slice(g * SGU_GD, (g + 1) * SGU_GD)
            mixed = jnp.dot(wg, vb[rows, cols], preferred_element_type=F32) + mixb[0, :, cols]
            y_scr[rows, cols] = (u[rows, cols] * mixed).astype(BF16)
    y = jnp.dot(y_scr[...], wout[...], preferred_element_type=F32)
    _tail(is_p, x, y, g1p, g1s, sh2p, sh2s, sc2p, sc2s, n2g, wr, br, xo_ref, h2_ref, idx_ref, gate_ref,
          cnt_ref)


def _sgu_layer(layer, x, mod_p, mod_s, n1g, n2g, win, lng, lnb, mixw, mixb, wout, wr, br):
    sel = lambda t: (jnp.where(t < NPT, 0, 1), 0, 0, 0)
    in_specs = ([_tile_spec(D)] + _mod_specs(layer, 0) + _mod_specs(layer, 1)
                + [_full_spec((1, D)), _full_spec((D, 2 * SGU_W)), _full_spec((1, SGU_W)),
                   _full_spec((1, SGU_W)),
                   pl.BlockSpec((1, SGU_G, CHUNK, CHUNK), sel),
                   pl.BlockSpec((1, CHUNK, SGU_W), lambda t: (jnp.where(t < NPT, 0, 1), 0, 0)),
                   _full_spec((SGU_W, D))]
                + _tail_in_specs(layer))
    out_specs = _tail_out_specs() + [pl.BlockSpec((TD, SGU_W), lambda t: (jnp.maximum(t - NPT, 0), 0))]
    out_shape = _tail_out_shapes() + [jax.ShapeDtypeStruct((T_S, SGU_W), F32)]
    return pl.pallas_call(
        _sgu_kernel, grid=(NT,), in_specs=in_specs, out_specs=out_specs, out_shape=out_shape,
        scratch_shapes=[pltpu.VMEM((TD, SGU_W), BF16)],
        compiler_params=_cparams(1), name="sgu_layer",
    )(x, mod_p, mod_s, mod_p, mod_s, n1g, win, lng, lnb, mixw, mixb, wout,
      mod_p, mod_s, mod_p, mod_s, mod_p, mod_s, n2g, wr, br)


def _ret_proj_kernel(x_ref, sh1p, sh1s, sc1p, sc1s, n1g, win, cos_ref, sin_ref,
                     q_ref, k_ref, v_ref, sg_ref):
    t = pl.program_id(0)
    is_p = t < NPT
    x = x_ref[...]
    h = _rms(x, n1g[...]) * (1.0 + _pick(is_p, sc1p, sc1s)) + _pick(is_p, sh1p, sh1s)
    p = jnp.dot(h.astype(BF16), win[...], preferred_element_type=F32)
    cos = cos_ref[...]
    sin = sin_ref[...]
    for hd in range(H):
        cq = slice(hd * DK, (hd + 1) * DK)
        ck = slice(QD + hd * DK, QD + (hd + 1) * DK)
        qh = p[:, cq]
        kh = p[:, ck]
        q_ref[:, cq] = (qh * cos + pltpu.roll(qh, DK // 2, 1) * sin).astype(BF16)
        k_ref[:, cq] = ((kh * cos + pltpu.roll(kh, DK // 2, 1) * sin) * (DK ** -0.5)).astype(BF16)
    v_ref[...] = p[:, 2 * QD:2 * QD + VD].astype(BF16)
    g = p[:, 2 * QD + VD:]
    sg_ref[...] = (g * jax.nn.sigmoid(g)).astype(BF16)


def _ret_proj(layer, x, mod_p, mod_s, n1g, win, cos_tab, sin_tab):
    in_specs = ([_tile_spec(D)] + _mod_specs(layer, 0) + _mod_specs(layer, 1)
                + [_full_spec((1, D)), _full_spec((D, RET_IN)), _tile_spec(DK), _tile_spec(DK)])
    out_specs = [_tile_spec(QD), _tile_spec(QD), _tile_spec(VD), _tile_spec(VD)]
    out_shape = [jax.ShapeDtypeStruct((T, QD), BF16), jax.ShapeDtypeStruct((T, QD), BF16),
                 jax.ShapeDtypeStruct((T, VD), BF16), jax.ShapeDtypeStruct((T, VD), BF16)]
    return pl.pallas_call(
        _ret_proj_kernel, grid=(NT,), in_specs=in_specs, out_specs=out_specs, out_shape=out_shape,
        compiler_params=_cparams(1), name="ret_proj",
    )(x, mod_p, mod_s, mod_p, mod_s, n1g, win, cos_tab, sin_tab)


def _ret_intra(q, k, v, decay, qdec, kdec):
    s = lax.dot_general(q, k, (((1,), (1,)), ((), ())), preferred_element_type=F32) * decay
    o = jnp.dot(s.astype(BF16), v, preferred_element_type=F32)
    return o, q.astype(F32) * qdec, k.astype(F32) * kdec


def _ret_finish(o, sg, ng):
    on = o * lax.rsqrt(jnp.mean(o * o, axis=-1, keepdims=True) + EPS)
    return (sg.astype(F32) * (on * ng)).astype(BF16)


def _ret_prompt_kernel(gam_ref, q_ref, k_ref, v_ref, sg_ref, dec_ref, qdec_ref, kdec_ref, ng_ref,
                       y_ref, s_ref):
    c = pl.program_id(1)

    @pl.when(c == 0)
    def _():
        s_ref[...] = jnp.zeros(s_ref.shape, F32)

    cks = [slice(hd * DK, (hd + 1) * DK) for hd in range(H)]
    cvs = [slice(hd * DV, (hd + 1) * DV) for hd in range(H)]
    for j in range(RET_CHUNKS):
        rows = slice(j * CHUNK, (j + 1) * CHUNK)
        vs = [v_ref[rows, cv] for cv in cvs]
        intra = [_ret_intra(q_ref[rows, ck], k_ref[rows, ck], v, dec_ref[hd], qdec_ref[:, ck], kdec_ref[:, ck])
                 for hd, (ck, v) in enumerate(zip(cks, vs))]
        olds = [s_ref[0, hd] for hd in range(H)]
        outs = [o + jnp.dot(qd.astype(BF16), s_old.astype(BF16), preferred_element_type=F32)
                for (o, qd, _), s_old in zip(intra, olds)]
        for hd in range(H):
            s_ref[0, hd] = gam_ref[hd] * olds[hd] + lax.dot_general(
                intra[hd][2].astype(BF16), vs[hd], (((0,), (0,)), ((), ())), preferred_element_type=F32)
        for hd, cv in enumerate(cvs):
            y_ref[rows, cv] = _ret_finish(outs[hd], sg_ref[rows, cv], ng_ref[:, cv])


def _ret_prompt(q, k, v, sg, tabs, ng):
    rows = RET_CHUNKS * CHUNK
    nc = L_P // rows
    row = lambda b, c: (b * nc + c, 0)
    smem = pl.BlockSpec(memory_space=pltpu.SMEM)
    in_specs = [smem,
                pl.BlockSpec((rows, QD), row), pl.BlockSpec((rows, QD), row),
                pl.BlockSpec((rows, VD), row), pl.BlockSpec((rows, VD), row),
                _full_spec((H, CHUNK, CHUNK)), _full_spec((CHUNK, QD)), _full_spec((CHUNK, QD)),
                _full_spec((1, VD))]
    out_specs = [pl.BlockSpec((rows, VD), row),
                 pl.BlockSpec((1, H, DK, DV), lambda b, c: (b, 0, 0, 0))]
    out_shape = [jax.ShapeDtypeStruct((T, VD), BF16), jax.ShapeDtypeStruct((NB_P, H, DK, DV), F32)]
    return pl.pallas_call(
        _ret_prompt_kernel, grid=(NB_P, nc), in_specs=in_specs, out_specs=out_specs, out_shape=out_shape,
        compiler_params=_cparams(2), name="ret_prompt",
    )(tabs["gam"], q, k, v, sg, tabs["decay"], tabs["qdec"], tabs["kdec"], ng)


def _ret_sample_kernel(gam_ref, q_ref, k_ref, v_ref, sg_ref, dec_ref, qdec_ref, kdec_ref, ng_ref,
                       s_in_ref, y_in_ref, s_all_ref, y_ref, s_out_ref):
    del y_in_ref, s_all_ref
    s_in_ref = s_in_ref.at[0]
    s_out_ref = s_out_ref.at[0]
    row_batch = lax.broadcasted_iota(I32, (CHUNK, DK), 0) // L_S
    for i in range(SH):
        ck = slice(i * DK, (i + 1) * DK)
        cv = slice(i * DV, (i + 1) * DV)
        v = v_ref[:, cv]
        o, qd, kd = _ret_intra(q_ref[:, ck], k_ref[:, ck], v, dec_ref[i], qdec_ref[:, ck], kdec_ref[:, ck])
        gam = gam_ref[pl.program_id(1) * SH + i]
        cross = [jnp.dot(qd[b * L_S:(b + 1) * L_S].astype(BF16), s_in_ref[b, i].astype(BF16),
                         preferred_element_type=F32) for b in range(SB)]
        for b in range(SB):
            kb = jnp.where(row_batch == b, kd, 0.0).astype(BF16)
            s_out_ref[b, i] = gam * s_in_ref[b, i] + lax.dot_general(
                kb, v, (((0,), (0,)), ((), ())), preferred_element_type=F32)
        o = o + jnp.concatenate(cross, axis=0)
        y_ref[:, cv] = _ret_finish(o, sg_ref[:, cv], ng_ref[:, cv])


def _ret_sample(j, q, k, v, sg, tabs, ng, s_in, y_prev, s_all):
    base = T_P // CHUNK
    rq = lambda g, h: (base + g, h)
    st = pl.BlockSpec((1, SB, SH, DK, DV), lambda g, h: (j, g, h, 0, 0))
    smem = pl.BlockSpec(memory_space=pltpu.SMEM)
    in_specs = [smem,
                pl.BlockSpec((CHUNK, SH * DK), rq), pl.BlockSpec((CHUNK, SH * DK), rq),
                pl.BlockSpec((CHUNK, SH * DV), rq), pl.BlockSpec((CHUNK, SH * DV), rq),
                pl.BlockSpec((SH, CHUNK, CHUNK), lambda g, h: (h, 0, 0)),
                pl.BlockSpec((CHUNK, SH * DK), lambda g, h: (0, h)),
                pl.BlockSpec((CHUNK, SH * DK), lambda g, h: (0, h)),
                pl.BlockSpec((1, SH * DV), lambda g, h: (0, h)),
                st, pl.BlockSpec(memory_space=pl.ANY), pl.BlockSpec(memory_space=pl.ANY)]
    out_specs = [pl.BlockSpec((CHUNK, SH * DV), rq), st]
    out_shape = [jax.ShapeDtypeStruct((T, VD), BF16), jax.ShapeDtypeStruct(s_all.shape, F32)]
    return pl.pallas_call(
        _ret_sample_kernel, grid=(NB_S // SB, H // SH), in_specs=in_specs, out_specs=out_specs,
        out_shape=out_shape, input_output_aliases={10: 0, 11: 1},
        compiler_params=_cparams(2), name="ret_sample",
    )(tabs["gam"], q, k, v, sg, tabs["decay"], tabs["qdec"], tabs["kdec"], ng, s_in, y_prev, s_all)


def _ret_out_kernel(y_ref, wout, x_ref, g1p, g1s, sh2p, sh2s, sc2p, sc2s, n2g, wr, br,
                    xo_ref, h2_ref, idx_ref, gate_ref, cnt_ref):
    is_p = pl.program_id(0) < NPT
    y = jnp.dot(y_ref[...], wout[...], preferred_element_type=F32)
    _tail(is_p, x_ref[...], y, g1p, g1s, sh2p, sh2s, sc2p, sc2s, n2g, wr, br,
          xo_ref, h2_ref, idx_ref, gate_ref, cnt_ref)


def _ret_out(layer, y, wout, x, mod_p, mod_s, n2g, wr, br):
    in_specs = [_tile_spec(VD), _full_spec((VD, D)), _tile_spec(D)] + _tail_in_specs(layer)
    return pl.pallas_call(
        _ret_out_kernel, grid=(NT,), in_specs=in_specs, out_specs=_tail_out_specs(),
        out_shape=_tail_out_shapes(), compiler_params=_cparams(1), name="ret_out",
    )(y, wout, x, mod_p, mod_s, mod_p, mod_s, mod_p, mod_s, n2g, wr, br)


def _lane_prefix(v, lane1):
    s = 1
    while s < LANES:
        v = v + jnp.where(lane1 >= s, pltpu.roll(v, s, 1), 0.0)
        s *= 2
    return v


def _route_kernel(cnt_ref, idx_ref, lp_ref, lpt_ref, runs_ref, meta_ref, base_scr):
    lane = lax.broadcasted_iota(I32, (TM, LANES), 1)
    lane1 = lax.broadcasted_iota(I32, (1, LANES), 1)
    row = lax.broadcasted_iota(I32, (8, LANES), 0)

    @pl.when(pl.program_id(0) == 0)
    def _():
        cnt = cnt_ref[...]
        padded = (((cnt.astype(I32) + (BLK - 1)) // BLK) * BLK).astype(F32)
        end = _lane_prefix(padded, lane1)
        base_scr[...] = end - padded
        meta = jnp.where(row == 0, cnt, jnp.where(row == 1, end - padded, jnp.where(row == 2, end, 0.0)))
        meta_ref[...] = meta.astype(I32)

    ri = lax.broadcasted_iota(I32, (TM, TM), 0)
    ci = lax.broadcasted_iota(I32, (TM, TM), 1)
    before = jnp.where(ci < ri, 1.0, 0.0).astype(BF16)
    base = base_scr[...]
    for j in range(ROUTE_TILES):
        idx = idx_ref[j * TM:(j + 1) * TM, :]
        hits = [lane == idx[:, k:k + 1] for k in range(TOPK)]
        chosen = jnp.zeros((TM, LANES), F32)
        for hk in hits:
            chosen = chosen + jnp.where(hk, 1.0, 0.0)
        colsum = jnp.sum(chosen, axis=0, keepdims=True)
        loff = _lane_prefix(colsum, lane1) - colsum
        pos = jnp.dot(before, chosen.astype(BF16), preferred_element_type=F32) + loff
        lp = jnp.zeros((TM, LANES), F32)
        for k, hk in enumerate(hits):
            lp = jnp.where(lane == k, jnp.sum(jnp.where(hk, pos, 0.0), axis=-1, keepdims=True), lp)
        lp_ref[j * TM:(j + 1) * TM, :] = lp.astype(I32)
        lpt_ref[j * 8:(j + 1) * 8, :] = lp.T[:8].astype(I32)
        runs = jnp.where(row == 0, colsum, jnp.where(row == 1, loff, jnp.where(row == 2, base, 0.0)))
        runs_ref[j] = runs.astype(I32)
        base = base + colsum
    base_scr[...] = base


def _route(cnt, idx):
    n_tiles = idx.shape[0] // TM
    assert n_tiles % ROUTE_TILES == 0
    rt = ROUTE_TILES
    return pl.pallas_call(
        _route_kernel, grid=(n_tiles // rt,),
        in_specs=[pl.BlockSpec((1, LANES), lambda t: (0, 0)), pl.BlockSpec((rt * TM, LANES), lambda t: (t, 0))],
        out_specs=[pl.BlockSpec((rt * TM, LANES), lambda t: (t, 0)),
                   pl.BlockSpec((rt * 8, TM), lambda t: (t, 0)),
                   pl.BlockSpec((rt, 8, LANES), lambda t: (t, 0, 0)),
                   pl.BlockSpec((8, LANES), lambda t: (0, 0))],
        out_shape=[jax.ShapeDtypeStruct((n_tiles * TM, LANES), I32),
                   jax.ShapeDtypeStruct((n_tiles * 8, TM), I32),
                   jax.ShapeDtypeStruct((n_tiles, 8, LANES), I32),
                   jax.ShapeDtypeStruct((8, LANES), I32)],
        scratch_shapes=[pltpu.VMEM((1, LANES), F32)],
        compiler_params=_cparams(1), name="moe_route",
    )(cnt, idx)


def _tile_rows(ref, r, n):
    return ref.at[:, pl.ds(r, n), :]


def _run_copy(src, dst, r_src, r_dst, n, sem, wait):
    for b in reversed(range(TM.bit_length())):
        size = 1 << b
        off = (n >> (b + 1)) << (b + 1)

        def piece(off=off, size=size):
            cp = pltpu.make_async_copy(_tile_rows(src, r_src + off, size),
                                       _tile_rows(dst, r_dst + off, size), sem)
            if wait:
                cp.wait()
            else:
                cp.start()

        if isinstance(n, int):
            if n & size:
                piece()
        else:
            pl.when((n & size) != 0)(piece)


def _load_rows(ref3, rows=None):
    rows = ref3.shape[1] if rows is None else rows
    parts = []
    for s in range(SLABS):
        word = ref3[s, :rows, :]
        parts.append(lax.bitcast_convert_type(lax.shift_left(word, 16), F32))
        parts.append(lax.bitcast_convert_type(word & HIGH_HALF, F32))
    return jnp.concatenate(parts, axis=1).astype(BF16)


def _store_rows(ref3, val):
    rows = val.shape[0]
    bits = lax.bitcast_convert_type(val, I32)
    for s in range(SLABS):
        low = lax.shift_right_logical(bits[:, (2 * s) * LANES:(2 * s + 1) * LANES], 16)
        ref3[s, :rows, :] = bits[:, (2 * s + 1) * LANES:(2 * s + 2) * LANES] | low


def _dispatch_kernel(len_ref, pos_ref, row_ref, cnt_ref, first_ref, end_ref, lpt_ref, h2_ref, dst,
                     buf, zbuf, sem, zsem):
    n_rows = dst.shape[1]
    step = pl.program_id(0)
    slot = step % 2
    cur = buf.at[slot]
    lpt = lpt_ref[...]
    p = lax.broadcasted_iota(I32, (TM * TOPK, TM), 0)
    pick = lpt[0:1, :] == p
    for k in range(1, TOPK):
        pick = pick | (lpt[k:k + 1, :] == p)
    perm = jnp.where(pick, 1.0, 0.0).astype(BF16)
    _store_rows(cur, jnp.dot(perm, h2_ref[...], preferred_element_type=F32))

    def per_expert(e, carry):
        _run_copy(cur, dst, pos_ref[e], row_ref[e], len_ref[e], sem.at[slot], False)
        return carry

    lax.fori_loop(0, NE, per_expert, 0)

    def drain(s):
        pltpu.make_async_copy(buf.at[s], _tile_rows(dst, 0, TM * TOPK), sem.at[s]).wait()

    @pl.when(step > 0)
    def _():
        drain(1 - slot)

    @pl.when(step == pl.num_programs(0) - 1)
    def _():
        drain(slot)

    @pl.when(step == 0)
    def _():
        zbuf[...] = jnp.zeros(zbuf.shape, I32)
        n_tail = (n_rows - end_ref[NE - 1]) // TM
        for wait in (False, True):
            def pad(e, carry, wait=wait):
                lo = first_ref[e] + cnt_ref[e]
                _run_copy(zbuf, dst, 0, lo, end_ref[e] - lo, zsem, wait)
                return carry

            lax.fori_loop(0, NE, pad, 0)

            def tail(j, carry, wait=wait):
                _run_copy(zbuf, dst, 0, end_ref[NE - 1] + j * TM, TM, zsem, wait)
                return carry

            lax.fori_loop(0, n_tail, tail, 0)


def _dispatch(h2, lpt, run_len, run_pos, run_row, cnt, first, end, n_rows):
    assert BLK % TM == 0 and BLK < 2 * TM + 1
    n_tiles = h2.shape[0] // TM
    smem = pl.BlockSpec(memory_space=pltpu.SMEM)
    per_tile = pl.BlockSpec((LANES,), lambda t: (t,), memory_space=pltpu.SMEM)
    return pl.pallas_call(
        _dispatch_kernel, grid=(n_tiles,),
        in_specs=[per_tile, per_tile, per_tile, smem, smem, smem,
                  pl.BlockSpec((8, TM), lambda t: (t, 0)), _tile_spec(D, TM)],
        out_specs=pl.BlockSpec(memory_space=pl.ANY),
        out_shape=jax.ShapeDtypeStruct((SLABS, n_rows, LANES), I32),
        scratch_shapes=[pltpu.VMEM((2, SLABS, TM * TOPK, LANES), I32),
                        pltpu.VMEM((SLABS, BLK, LANES), I32),
                        pltpu.SemaphoreType.DMA((2,)), pltpu.SemaphoreType.DMA],
        compiler_params=_cparams(1), name="moe_dispatch",
    )(run_len, run_pos, run_row, cnt, first, end, lpt, h2)


def _expert_kernel(be_ref, nu_ref, nr_ref, nxt_ref, par_ref, xb_ref, wu_hbm, bu_ref, wd_hbm, bd_ref, yb_ref,
                   wuf, wdf, wub, wdb, sem, *, layer):
    i = pl.program_id(0)
    used = i < nu_ref[0]
    half = nr_ref[i] <= BLK // 2
    slot = par_ref[i]

    def weights(e, s):
        return (pltpu.make_async_copy(wu_hbm.at[layer, e], wuf.at[s], sem.at[s, 0]),
                pltpu.make_async_copy(wd_hbm.at[layer, e], wdf.at[s], sem.at[s, 1]))

    @pl.when(i == 0)
    def _():
        for cp in weights(be_ref[0], slot):
            cp.start()

    @pl.when(used & ((i == 0) | (be_ref[i] != be_ref[jnp.maximum(i - 1, 0)])))
    def _():
        for cp in weights(be_ref[i], slot):
            cp.wait()
        wub[...] = wuf[slot].astype(BF16)
        wdb[...] = wdf[slot].astype(BF16)

        @pl.when(nxt_ref[i] >= 0)
        def _():
            for cp in weights(nxt_ref[i], 1 - slot):
                cp.start()

    def swiglu_rows(rows):
        z = jnp.dot(_load_rows(xb_ref, rows), wub[...], preferred_element_type=F32) + bu_ref[0, 0]
        glu = jnp.minimum(z[:, :FE], LIMIT)
        lin = jnp.clip(z[:, FE:], -LIMIT, LIMIT)
        act = glu * jax.nn.sigmoid(ALPHA * glu) * (lin + 1.0)
        y = jnp.dot(act.astype(BF16), wdb[...], preferred_element_type=F32) + bd_ref[0, 0]
        _store_rows(yb_ref, y.astype(BF16).astype(F32))

    @pl.when(used & jnp.logical_not(half))
    def _():
        swiglu_rows(BLK)

    @pl.when(used & half)
    def _():
        swiglu_rows(BLK // 2)
        yb_ref[:, BLK // 2:, :] = jnp.zeros((SLABS, BLK // 2, LANES), I32)

    @pl.when(jnp.logical_not(used))
    def _():
        yb_ref[...] = jnp.zeros(yb_ref.shape, I32)


def _experts(layer, xb, blk_e, n_used, blk_rows, blk_next, blk_par, wu, bu, wd, bd):
    n_blocks = blk_e.shape[0]
    grid_spec = pltpu.PrefetchScalarGridSpec(
        num_scalar_prefetch=5, grid=(n_blocks,),
        in_specs=[pl.BlockSpec((SLABS, BLK, LANES), lambda i, be, nu, *_: (0, jnp.minimum(i, nu[0] - 1), 0)),
                  pl.BlockSpec(memory_space=pl.ANY),
                  pl.BlockSpec((1, 1, 1, 2 * FE), lambda i, be, *_: (layer, be[i], 0, 0)),
                  pl.BlockSpec(memory_space=pl.ANY),
                  pl.BlockSpec((1, 1, 1, D), lambda i, be, *_: (layer, be[i], 0, 0))],
        out_specs=pl.BlockSpec((SLABS, BLK, LANES), lambda i, *_: (0, i, 0)),
        scratch_shapes=[pltpu.VMEM((2, D, 2 * FE), F32), pltpu.VMEM((2, FE, D), F32),
                        pltpu.VMEM((D, 2 * FE), BF16), pltpu.VMEM((FE, D), BF16),
                        pltpu.SemaphoreType.DMA((2, 2))])
    return pl.pallas_call(
        functools.partial(_expert_kernel, layer=layer), grid_spec=grid_spec,
        out_shape=jax.ShapeDtypeStruct((SLABS, n_blocks * BLK, LANES), I32),
        compiler_params=_cparams(1), name="experts",
    )(blk_e, n_used, blk_rows, blk_next, blk_par, xb, wu, bu, wd, bd)


def _combine_kernel(len_ref, pos_ref, row_ref, len_nx, pos_nx, row_nx, yb, lp_ref, x_ref, gate_ref,
                    g2p, g2s, fg_ref, *outs_and_scratch, final):
    *o_ref, ybuf, sem = outs_and_scratch
    step = pl.program_id(0)
    slot = step % 2
    is_p = step < T_P // TM

    def fetch(len_r, pos_r, row_r, s):
        def per_expert(e, carry):
            _run_copy(yb, ybuf.at[s], row_r[e], pos_r[e], len_r[e], sem.at[s], False)
            return carry

        lax.fori_loop(0, NE, per_expert, 0)

    @pl.when(step == 0)
    def _():
        fetch(len_ref, pos_ref, row_ref, slot)

    @pl.when(step + 1 < pl.num_programs(0))
    def _():
        fetch(len_nx, pos_nx, row_nx, 1 - slot)

    cur = ybuf.at[slot]
    pltpu.make_async_copy(_tile_rows(yb, 0, TM * TOPK), cur, sem.at[slot]).wait()

    rows = x_ref.shape[0]
    lp = lp_ref[...]
    gates = gate_ref[...]
    p = lax.broadcasted_iota(I32, (rows, rows * TOPK), 1)
    weights = jnp.zeros((rows, rows * TOPK), F32)
    for k in range(TOPK):
        weights = jnp.where(lp[:, k:k + 1] == p, gates[:, k:k + 1], weights)
    acc = jnp.dot(weights.astype(BF16), _load_rows(cur), preferred_element_type=F32)
    xn = x_ref[...] + _pick(is_p, g2p, g2s) * acc
    if not final:
        o_ref[0][...] = xn
        return
    y = _rms(xn, fg_ref[...])
    op_ref, os_ref = o_ref

    @pl.when(is_p)
    def _():
        op_ref[...] = y

    @pl.when(jnp.logical_not(is_p))
    def _():
        os_ref[...] = y


def _combine(layer, x, yb, lp, run_len, run_pos, run_row, gates, mod_p, mod_s, final_g, final):
    n_tiles = x.shape[0] // TM
    n_prompt = T_P // TM
    if final:
        out_specs = [pl.BlockSpec((TM, D), lambda t: (jnp.minimum(t, n_prompt - 1), 0)),
                     pl.BlockSpec((TM, D), lambda t: (jnp.maximum(t - n_prompt, 0), 0))]
        out_shape = [jax.ShapeDtypeStruct((T_P, D), F32), jax.ShapeDtypeStruct((x.shape[0] - T_P, D), F32)]
    else:
        out_specs, out_shape = _tile_spec(D, TM), jax.ShapeDtypeStruct(x.shape, F32)
    per_tile = pl.BlockSpec((LANES,), lambda t: (t,), memory_space=pltpu.SMEM)
    next_tile = pl.BlockSpec((LANES,), lambda t: (jnp.minimum(t + 1, n_tiles - 1),),
                             memory_space=pltpu.SMEM)
    in_specs = ([per_tile, per_tile, per_tile, next_tile, next_tile, next_tile,
                 pl.BlockSpec(memory_space=pl.ANY),
                 _tile_spec(LANES, TM), _tile_spec(D, TM), _tile_spec(LANES, TM)]
                + _mod_specs(layer, 5, TM) + [_full_spec((1, D))])
    return pl.pallas_call(
        functools.partial(_combine_kernel, final=final), grid=(n_tiles,), in_specs=in_specs,
        out_specs=out_specs, out_shape=out_shape,
        scratch_shapes=[pltpu.VMEM((2, SLABS, TM * TOPK, LANES), I32), pltpu.SemaphoreType.DMA((2,))],
        compiler_params=_cparams(1), name="moe_combine",
    )(run_len, run_pos, run_row, run_len, run_pos, run_row, yb, lp, x, gates, mod_p, mod_s, final_g)


def _block_tables(cnt, first, end, n_blocks):
    experts = jnp.arange(NE, dtype=I32)
    blk_first = jnp.arange(n_blocks, dtype=I32) * BLK
    blk_e = jnp.minimum(jnp.sum((end[None, :] <= blk_first[:, None]).astype(I32), axis=1), NE - 1)
    n_used = end[NE - 1:] // BLK
    has_rows = cnt > 0
    later = has_rows[None, :] & (experts[None, :] > experts[:, None])
    next_e = jnp.min(jnp.where(later, experts[None, :], NE), axis=1)
    next_e = jnp.where(next_e == NE, -1, next_e)
    parity = (jnp.cumsum(has_rows.astype(I32)) - 1) % 2
    own = blk_e[:, None] == experts[None, :]
    of_block = lambda per_expert: jnp.sum(jnp.where(own, per_expert[None, :], 0), axis=1)
    blk_rows = jnp.clip(of_block(first + cnt) - blk_first, 0, BLK)
    return blk_e, n_used, blk_rows, of_block(next_e), of_block(parity)


def _moe(layer, x, h2, idx, gates, cnt_all, mod_p, mod_s, wu, bu, wd, bd, final_g, final):
    lp, lpt, runs, meta = _route(cnt_all, idx)
    cnt, first, end = meta[0, :NE], meta[1, :NE], meta[2, :NE]
    run_len, run_pos, run_row = (runs[:, r, :].reshape(-1) for r in range(3))
    xb = _dispatch(h2, lpt, run_len, run_pos, run_row, cnt, first, end, N_ROWS)
    yb = _experts(layer, xb, *_block_tables(cnt, first, end, N_BLOCKS), wu, bu, wd, bd)
    return _combine(layer, x, yb, lp, run_len, run_pos, run_row, gates, mod_p, mod_s, final_g, final)


def _rope_tables():
    half = DK // 2
    inv = 1.0 / (ROPE_BASE ** jnp.linspace(0.0, 1.0, half, dtype=F32))

    def tab(pos):
        ang = pos.astype(F32)[:, None] * inv[None, :]
        cos, sin = jnp.cos(ang), jnp.sin(ang)
        return jnp.concatenate([cos, cos], -1), jnp.concatenate([-sin, sin], -1)

    cp, sp = tab(jnp.arange(L_P, dtype=I32))
    cs, ss = tab(PAST + jnp.arange(L_S, dtype=I32))
    cos = jnp.concatenate([jnp.tile(cp, (NB_P, 1)), jnp.tile(cs, (NB_S, 1))], 0)
    sin = jnp.concatenate([jnp.tile(sp, (NB_P, 1)), jnp.tile(ss, (NB_S, 1))], 0)
    return cos, sin


def _decay_tables(cl):
    lg = jnp.log(1.0 - 2.0 ** (-5.0 - jnp.arange(H, dtype=F32)))
    r = jnp.arange(CHUNK)
    idx = (r % cl).astype(F32)
    diff = idx[:, None] - idx[None, :]
    same = (r[:, None] // cl) == (r[None, :] // cl)
    decay = jnp.where((same & (diff >= 0))[None],
                      jnp.exp(lg[:, None, None] * jnp.maximum(diff, 0.0)[None]), 0.0)
    qdec = jnp.exp(lg[None, :] * (idx[:, None] + 1.0))
    kdec = jnp.exp(lg[None, :] * (cl - 1.0 - idx[:, None]))
    wide = lambda a: jnp.repeat(a, DK, axis=1)
    return {"decay": decay, "qdec": wide(qdec), "kdec": wide(kdec), "gam": jnp.exp(lg * cl)}


def kernel(x_prompt, x_sample, c_prompt, c_sample, state_ret, w_mod, b_mod, norm1_g, norm2_g,
           sgu_w_in, sgu_ln_g, sgu_ln_b, sgu_w_s, sgu_b_s, sgu_w_out, ret_w_in, ret_norm_g, ret_w_out,
           moe_w_router, moe_b_router, moe_w_up, moe_b_up, moe_w_down, moe_b_down, final_g):
    x = jnp.concatenate([x_prompt.reshape(T_P, D), x_sample.reshape(T_S, D)], 0)
    mod_p, mod_s = _modulation(jnp.concatenate([c_prompt, c_sample], 0), w_mod, b_mod)
    mod_p = mod_p.reshape(DEPTH * NB_P, 1, 6 * D)
    cos_tab, sin_tab = _rope_tables()
    tabs_p = _decay_tables(CHUNK)
    tabs_s = _decay_tables(L_S)
    wr_rows = jnp.pad(jnp.swapaxes(moe_w_router, 1, 2), ((0, 0), (0, LANES - NE), (0, 0))).astype(BF16)
    fg = final_g.reshape(1, D)
    b_up = moe_b_up.reshape(DEPTH, NE, 1, 2 * FE)
    b_down = moe_b_down.reshape(DEPTH, NE, 1, D)

    ret_p, v_rows = [], []
    s_all = lax.empty(state_ret.shape, F32)
    for i in range(DEPTH):
        j = i // 2
        n1g = norm1_g[i].reshape(1, D)
        n2g = norm2_g[i].reshape(1, D)
        wr = wr_rows[i]
        br = moe_b_router[i].reshape(NE, 1)
        if i % 2 == 0:
            mixw = jnp.stack([sgu_w_s[j], jnp.tile(sgu_w_s[j][:, :L_S, :L_S], (1, SB, SB))])
            bias_p = jnp.repeat(sgu_b_s[j].T, SGU_GD, axis=1)
            bias_s = jnp.tile(bias_p[:L_S], (SB, 1))
            x, h2, idx, gates, cnt, v = _sgu_layer(
                i, x, mod_p, mod_s, n1g, n2g, sgu_w_in[j].astype(BF16), sgu_ln_g[j].reshape(1, SGU_W),
                sgu_ln_b[j].reshape(1, SGU_W), mixw, jnp.stack([bias_p, bias_s]),
                sgu_w_out[j].astype(BF16), wr, br)
            v_rows.append(v.reshape(NB_S, L_S, SGU_W))
        else:
            q, k, v, sg = _ret_proj(i, x, mod_p, mod_s, n1g, ret_w_in[j].astype(BF16), cos_tab, sin_tab)
            ng = ret_norm_g[j].reshape(1, VD)
            y, s_p = _ret_prompt(q, k, v, sg, tabs_p, ng)
            y, s_all = _ret_sample(j, q, k, v, sg, tabs_s, ng, state_ret, y, s_all)
            ret_p.append(s_p)
            x, h2, idx, gates, cnt = _ret_out(i, y, ret_w_out[j].astype(BF16), x, mod_p, mod_s, n2g, wr, br)
        x = _moe(i, x, h2, idx, gates, cnt, mod_p, mod_s, moe_w_up, b_up, moe_w_down, b_down,
                 fg, final=(i == DEPTH - 1))
    y_prompt, y_sample = x
    return (y_prompt.reshape(NB_P, L_P, D), y_sample.reshape(NB_S, L_S, D), jnp.stack(ret_p), s_all,
            jnp.stack(v_rows))
```

```python
import functools

import jax
import jax.numpy as jnp
from jax import lax
from jax.experimental import pallas as pl
from jax.experimental.pallas import tpu as pltpu

F32 = jnp.float32
BF16 = jnp.bfloat16
I32 = jnp.int32

D = 1024
NB_P, L_P = 8, 2048
NB_S, L_S = 128, 8
PAST = 16384
DEPTH = 4
T_P = NB_P * L_P
T_S = NB_S * L_S
T = T_P + T_S
SGU_W = 2 * D
SGU_G = 8
SGU_GD = SGU_W // SGU_G
CHUNK = 128
H = 8
DK = D // H
DV = 2 * DK
QD = H * DK
VD = H * DV
RET_IN = 2 * QD + 2 * VD
ROPE_BASE = 10000.0
NE = 32
TOPK = 4
FE = D
ALPHA = 1.702
LIMIT = 7.0
EPS = 1e-6

LANES = 128
SLABS = D // (2 * LANES)
HIGH_HALF = -65536
TD = 512
NPT = T_P // TD
NT = T // TD
TM = 256
BLK = 512
TK = T * TOPK
N_BLOCKS = -(-(TK + NE * (BLK - 1)) // BLK)
N_ROWS = N_BLOCKS * BLK
SB = 16
ROUTE_TILES = 4
SH = 4
RET_CHUNKS = 8
VMEM_LIMIT = 58 * 1024 * 1024


def _cparams(n_axes):
    return pltpu.CompilerParams(dimension_semantics=("arbitrary",) * n_axes,
                                vmem_limit_bytes=VMEM_LIMIT)


def _rms(x, g):
    return (x * lax.rsqrt(jnp.mean(x * x, axis=-1, keepdims=True) + EPS)) * g


def _pick(is_p, p_ref, s_ref):
    return jnp.where(is_p, p_ref[0], s_ref[0])


def _mod_specs(layer, j, tm=TD):
    per_batch, n_prompt = L_P // tm, T_P // tm
    return [
        pl.BlockSpec((1, 1, D), lambda t: (layer * NB_P + jnp.minimum(t // per_batch, NB_P - 1), 0, j)),
        pl.BlockSpec((1, tm, D), lambda t: (layer, jnp.maximum(t - n_prompt, 0), j),
                     pipeline_mode=pl.Buffered(1)),
    ]


def _tile_spec(width, tm=TD):
    return pl.BlockSpec((tm, width), lambda t: (t, 0))


def _full_spec(shape):
    return pl.BlockSpec(shape, lambda *_: (0,) * len(shape), pipeline_mode=pl.Buffered(1))


def _mod_kernel(c_ref, w_ref, b_ref, p_ref, s_ref):
    c = c_ref[...]
    cs = (c * jax.nn.sigmoid(c)).astype(BF16)
    mod = jnp.dot(cs, w_ref[0].astype(BF16), preferred_element_type=F32) + b_ref[0]
    p_ref[0] = mod[:NB_P]
    per_batch = mod[NB_P:]
    s_ref[0] = jnp.broadcast_to(per_batch[:, None, :], (NB_S, L_S, per_batch.shape[-1])).reshape(T_S, -1)


def _modulation(c_all, w_mod, b_mod):
    tn = 1536
    nb = c_all.shape[0]
    return pl.pallas_call(
        _mod_kernel,
        grid=(DEPTH, 6 * D // tn),
        in_specs=[
            pl.BlockSpec((nb, D), lambda l, n: (0, 0)),
            pl.BlockSpec((1, D, tn), lambda l, n: (l, 0, n)),
            pl.BlockSpec((1, 1, tn), lambda l, n: (l, 0, n)),
        ],
        out_specs=[pl.BlockSpec((1, NB_P, tn), lambda l, n: (l, 0, n)),
                   pl.BlockSpec((1, T_S, tn), lambda l, n: (l, 0, n))],
        out_shape=[jax.ShapeDtypeStruct((DEPTH, NB_P, 6 * D), F32),
                   jax.ShapeDtypeStruct((DEPTH, T_S, 6 * D), F32)],
        compiler_params=_cparams(2),
        name="modulation",
    )(c_all, w_mod, b_mod.reshape(DEPTH, 1, 6 * D))


def _tail(is_p, x, y, g1p, g1s, sh2p, sh2s, sc2p, sc2s, n2g, wr, br, xo_ref, h2_ref, idx_ref, gate_ref,
          cnt_ref):
    xn = x + _pick(is_p, g1p, g1s) * y
    xo_ref[...] = xn
    h2 = (_rms(xn, n2g[...]) * (1.0 + _pick(is_p, sc2p, sc2s)) + _pick(is_p, sh2p, sh2s)).astype(BF16)
    h2_ref[...] = h2
    logit = lax.dot_general(wr[...], h2, (((1,), (1,)), ((), ())), preferred_element_type=F32)[:NE] + br[...]
    expert = lax.broadcasted_iota(I32, logit.shape, 0)
    vals, ids = [], []
    for _ in range(TOPK):
        m = jnp.max(logit, axis=0, keepdims=True)
        sel = jnp.min(jnp.where(logit == m, expert, NE), axis=0, keepdims=True)
        vals.append(m)
        ids.append(sel)
        logit = jnp.where(expert == sel, -jnp.inf, logit)
    es = [jnp.exp(v - vals[0]) for v in vals]
    tot = (es[0] + es[1]) + (es[2] + es[3])
    slot = lax.broadcasted_iota(I32, (LANES, logit.shape[1]), 0)
    idx_t = jnp.zeros(slot.shape, F32)
    gate_t = jnp.zeros(slot.shape, F32)
    for k in range(TOPK):
        idx_t = jnp.where(slot == k, ids[k].astype(F32), idx_t)
        gate_t = jnp.where(slot == k, es[k] / tot, gate_t)
    idx_out = idx_t.T.astype(I32)
    idx_ref[...] = idx_out
    gate_ref[...] = gate_t.T
    lane = lax.broadcasted_iota(I32, idx_out.shape, 1)
    chosen = jnp.zeros(idx_out.shape, F32)
    for k in range(TOPK):
        chosen = chosen + jnp.where(lane == idx_out[:, k:k + 1], 1.0, 0.0)

    @pl.when(pl.program_id(0) == 0)
    def _():
        cnt_ref[...] = jnp.zeros(cnt_ref.shape, F32)

    cnt_ref[...] += jnp.sum(chosen, axis=0, keepdims=True)


def _tail_in_specs(layer):
    return (_mod_specs(layer, 2) + _mod_specs(layer, 3) + _mod_specs(layer, 4)
            + [_full_spec((1, D)), _full_spec((LANES, D)), _full_spec((NE, 1))])


def _tail_out_specs():
    return [_tile_spec(D), _tile_spec(D), _tile_spec(LANES), _tile_spec(LANES),
            pl.BlockSpec((1, LANES), lambda t: (0, 0))]


def _tail_out_shapes():
    return [jax.ShapeDtypeStruct((T, D), F32), jax.ShapeDtypeStruct((T, D), BF16),
            jax.ShapeDtypeStruct((T, LANES), I32), jax.ShapeDtypeStruct((T, LANES), F32),
            jax.ShapeDtypeStruct((1, LANES), F32)]


def _sgu_kernel(x_ref, xs_ref, sh1p, sh1s, sc1p, sc1s, n1g, win, lng, lnb, mixw, mixb, wout,
                g1p, g1s, sh2p, sh2s, sc2p, sc2s, n2g, wr, br,
                xo_ref, h2_ref, idx_ref, gate_ref, cnt_ref, v_ref, y_scr, *, split):
    t = pl.program_id(0)
    is_p = t < NPT
    x = jnp.where(is_p, x_ref[...], xs_ref[...]) if split else x_ref[...]
    h = _rms(x, n1g[...]) * (1.0 + _pick(is_p, sc1p, sc1s)) + _pick(is_p, sh1p, sh1s)
    z = jnp.dot(h.astype(BF16), win[...], preferred_element_type=F32)
    z = 0.5 * z * (1.0 + lax.erf(z * (0.5 ** 0.5)))
    u = z[:, :SGU_W]
    v = z[:, SGU_W:]
    vc = v - jnp.mean(v, axis=-1, keepdims=True)
    vn = vc * lax.rsqrt(jnp.mean(vc * vc, axis=-1, keepdims=True) + EPS) * lng[...] + lnb[...]

    @pl.when(t >= NPT)
    def _():
        v_ref[...] = vn

    vb = vn.astype(BF16)
    ri = lax.broadcasted_iota(I32, (CHUNK, CHUNK), 0)
    ci = lax.broadcasted_iota(I32, (CHUNK, CHUNK), 1)
    causal = ci <= ri
    shift = jnp.broadcast_to(jnp.where(is_p, 7, 3), ri.shape)
    keep = causal & (lax.shift_right_logical(ri, shift) == lax.shift_right_logical(ci, shift))
    for g in range(SGU_G):
        wg = jnp.where(keep, mixw[0, g], 0.0).astype(BF16)
        for c in range(TD // CHUNK):
            rows = slice(c * CHUNK, (c + 1) * CHUNK)
            cols = slice(g * SGU_GD, (g + 1) * SGU_GD)
            mixed = jnp.dot(wg, vb[rows, cols], preferred_element_type=F32) + mixb[0, :, cols]
            y_scr[rows, cols] = (u[rows, cols] * mixed).astype(BF16)
    y = jnp.dot(y_scr[...], wout[...], preferred_element_type=F32)
    _tail(is_p, x, y, g1p, g1s, sh2p, sh2s, sc2p, sc2s, n2g, wr, br, xo_ref, h2_ref, idx_ref, gate_ref,
          cnt_ref)


def _sgu_layer(layer, x, mod_p, mod_s, n1g, n2g, win, lng, lnb, mixw, mixb, wout, wr, br):
    sel = lambda t: (jnp.where(t < NPT, 0, 1), 0, 0, 0)
    split = isinstance(x, tuple)
    if split:
        x, xs = x
        x_specs = [pl.BlockSpec((TD, D), lambda t: (jnp.minimum(t, NPT - 1), 0)),
                   pl.BlockSpec((TD, D), lambda t: (jnp.maximum(t - NPT, 0), 0), pipeline_mode=pl.Buffered(1))]
    else:
        xs = x
        x_specs = [_tile_spec(D), pl.BlockSpec((TD, D), lambda t: (0, 0), pipeline_mode=pl.Buffered(1))]
    in_specs = (x_specs + _mod_specs(layer, 0) + _mod_specs(layer, 1)
                + [_full_spec((1, D)), _full_spec((D, 2 * SGU_W)), _full_spec((1, SGU_W)),
                   _full_spec((1, SGU_W)),
                   pl.BlockSpec((1, SGU_G, CHUNK, CHUNK), sel),
                   pl.BlockSpec((1, CHUNK, SGU_W), lambda t: (jnp.where(t < NPT, 0, 1), 0, 0)),
                   _full_spec((SGU_W, D))]
                + _tail_in_specs(layer))
    out_specs = _tail_out_specs() + [pl.BlockSpec((TD, SGU_W), lambda t: (jnp.maximum(t - NPT, 0), 0))]
    out_shape = _tail_out_shapes() + [jax.ShapeDtypeStruct((T_S, SGU_W), F32)]
    return pl.pallas_call(
        functools.partial(_sgu_kernel, split=split), grid=(NT,), in_specs=in_specs, out_specs=out_specs,
        out_shape=out_shape, scratch_shapes=[pltpu.VMEM((TD, SGU_W), BF16)],
        compiler_params=_cparams(1), name="sgu_layer",
    )(x, xs, mod_p, mod_s, mod_p, mod_s, n1g, win, lng, lnb, mixw, mixb, wout,
      mod_p, mod_s, mod_p, mod_s, mod_p, mod_s, n2g, wr, br)


def _ret_proj_kernel(x_ref, sh1p, sh1s, sc1p, sc1s, n1g, win, cos_ref, sin_ref,
                     q_ref, k_ref, v_ref, sg_ref):
    t = pl.program_id(0)
    is_p = t < NPT
    x = x_ref[...]
    h = _rms(x, n1g[...]) * (1.0 + _pick(is_p, sc1p, sc1s)) + _pick(is_p, sh1p, sh1s)
    p = jnp.dot(h.astype(BF16), win[...], preferred_element_type=F32)
    cos = cos_ref[...]
    sin = sin_ref[...]
    for hd in range(H):
        cq = slice(hd * DK, (hd + 1) * DK)
        ck = slice(QD + hd * DK, QD + (hd + 1) * DK)
        qh = p[:, cq]
        kh = p[:, ck]
        q_ref[:, cq] = (qh * cos + pltpu.roll(qh, DK // 2, 1) * sin).astype(BF16)
        k_ref[:, cq] = ((kh * cos + pltpu.roll(kh, DK // 2, 1) * sin) * (DK ** -0.5)).astype(BF16)
    v_ref[...] = p[:, 2 * QD:2 * QD + VD].astype(BF16)
    g = p[:, 2 * QD + VD:]
    sg_ref[...] = (g * jax.nn.sigmoid(g)).astype(BF16)


def _ret_proj(layer, x, mod_p, mod_s, n1g, win, cos_tab, sin_tab):
    in_specs = ([_tile_spec(D)] + _mod_specs(layer, 0) + _mod_specs(layer, 1)
                + [_full_spec((1, D)), _full_spec((D, RET_IN)), _tile_spec(DK), _tile_spec(DK)])
    out_specs = [_tile_spec(QD), _tile_spec(QD), _tile_spec(VD), _tile_spec(VD)]
    out_shape = [jax.ShapeDtypeStruct((T, QD), BF16), jax.ShapeDtypeStruct((T, QD), BF16),
                 jax.ShapeDtypeStruct((T, VD), BF16), jax.ShapeDtypeStruct((T, VD), BF16)]
    return pl.pallas_call(
        _ret_proj_kernel, grid=(NT,), in_specs=in_specs, out_specs=out_specs, out_shape=out_shape,
        compiler_params=_cparams(1), name="ret_proj",
    )(x, mod_p, mod_s, mod_p, mod_s, n1g, win, cos_tab, sin_tab)


def _ret_intra(q, k, v, decay, qdec, kdec):
    s = lax.dot_general(q, k, (((1,), (1,)), ((), ())), preferred_element_type=F32) * decay
    o = jnp.dot(s.astype(BF16), v, preferred_element_type=F32)
    return o, q.astype(F32) * qdec, k.astype(F32) * kdec


def _ret_finish(o, sg, ng):
    on = o * lax.rsqrt(jnp.mean(o * o, axis=-1, keepdims=True) + EPS)
    return (sg.astype(F32) * (on * ng)).astype(BF16)


def _ret_prompt_kernel(gam_ref, q_ref, k_ref, v_ref, sg_ref, dec_ref, qdec_ref, kdec_ref, ng_ref,
                       y_ref, s_ref):
    c = pl.program_id(1)

    @pl.when(c == 0)
    def _():
        s_ref[...] = jnp.zeros(s_ref.shape, F32)

    cks = [slice(hd * DK, (hd + 1) * DK) for hd in range(H)]
    cvs = [slice(hd * DV, (hd + 1) * DV) for hd in range(H)]
    for j in range(RET_CHUNKS):
        rows = slice(j * CHUNK, (j + 1) * CHUNK)
        vs = [v_ref[rows, cv] for cv in cvs]
        intra = [_ret_intra(q_ref[rows, ck], k_ref[rows, ck], v, dec_ref[hd], qdec_ref[:, ck], kdec_ref[:, ck])
                 for hd, (ck, v) in enumerate(zip(cks, vs))]
        olds = [s_ref[0, hd] for hd in range(H)]
        outs = [o + jnp.dot(qd.astype(BF16), s_old.astype(BF16), preferred_element_type=F32)
                for (o, qd, _), s_old in zip(intra, olds)]
        for hd in range(H):
            s_ref[0, hd] = gam_ref[hd] * olds[hd] + lax.dot_general(
                intra[hd][2].astype(BF16), vs[hd], (((0,), (0,)), ((), ())), preferred_element_type=F32)
        for hd, cv in enumerate(cvs):
            y_ref[rows, cv] = _ret_finish(outs[hd], sg_ref[rows, cv], ng_ref[:, cv])


def _ret_prompt(q, k, v, sg, tabs, ng):
    rows = RET_CHUNKS * CHUNK
    nc = L_P // rows
    row = lambda b, c: (b * nc + c, 0)
    smem = pl.BlockSpec(memory_space=pltpu.SMEM)
    in_specs = [smem,
                pl.BlockSpec((rows, QD), row), pl.BlockSpec((rows, QD), row),
                pl.BlockSpec((rows, VD), row), pl.BlockSpec((rows, VD), row),
                _full_spec((H, CHUNK, CHUNK)), _full_spec((CHUNK, QD)), _full_spec((CHUNK, QD)),
                _full_spec((1, VD))]
    out_specs = [pl.BlockSpec((rows, VD), row),
                 pl.BlockSpec((1, H, DK, DV), lambda b, c: (b, 0, 0, 0))]
    out_shape = [jax.ShapeDtypeStruct((T, VD), BF16), jax.ShapeDtypeStruct((NB_P, H, DK, DV), F32)]
    return pl.pallas_call(
        _ret_prompt_kernel, grid=(NB_P, nc), in_specs=in_specs, out_specs=out_specs, out_shape=out_shape,
        compiler_params=_cparams(2), name="ret_prompt",
    )(tabs["gam"], q, k, v, sg, tabs["decay"], tabs["qdec"], tabs["kdec"], ng)


def _ret_sample_kernel(gam_ref, q_ref, k_ref, v_ref, sg_ref, dec_ref, qdec_ref, kdec_ref, ng_ref,
                       s_in_ref, y_in_ref, s_all_ref, y_ref, s_out_ref):
    del y_in_ref, s_all_ref
    s_in_ref = s_in_ref.at[0]
    s_out_ref = s_out_ref.at[0]
    row_batch = lax.broadcasted_iota(I32, (CHUNK, DK), 0) // L_S
    for i in range(SH):
        ck = slice(i * DK, (i + 1) * DK)
        cv = slice(i * DV, (i + 1) * DV)
        v = v_ref[:, cv]
        o, qd, kd = _ret_intra(q_ref[:, ck], k_ref[:, ck], v, dec_ref[i], qdec_ref[:, ck], kdec_ref[:, ck])
        gam = gam_ref[pl.program_id(1) * SH + i]
        cross = [jnp.dot(qd[b * L_S:(b + 1) * L_S].astype(BF16), s_in_ref[b, i].astype(BF16),
                         preferred_element_type=F32) for b in range(SB)]
        for b in range(SB):
            kb = jnp.where(row_batch == b, kd, 0.0).astype(BF16)
            s_out_ref[b, i] = gam * s_in_ref[b, i] + lax.dot_general(
                kb, v, (((0,), (0,)), ((), ())), preferred_element_type=F32)
        o = o + jnp.concatenate(cross, axis=0)
        y_ref[:, cv] = _ret_finish(o, sg_ref[:, cv], ng_ref[:, cv])


def _ret_sample(j, q, k, v, sg, tabs, ng, s_in, y_prev, s_all):
    base = T_P // CHUNK
    rq = lambda g, h: (base + g, h)
    st = pl.BlockSpec((1, SB, SH, DK, DV), lambda g, h: (j, g, h, 0, 0))
    smem = pl.BlockSpec(memory_space=pltpu.SMEM)
    in_specs = [smem,
                pl.BlockSpec((CHUNK, SH * DK), rq), pl.BlockSpec((CHUNK, SH * DK), rq),
                pl.BlockSpec((CHUNK, SH * DV), rq), pl.BlockSpec((CHUNK, SH * DV), rq),
                pl.BlockSpec((SH, CHUNK, CHUNK), lambda g, h: (h, 0, 0)),
                pl.BlockSpec((CHUNK, SH * DK), lambda g, h: (0, h)),
                pl.BlockSpec((CHUNK, SH * DK), lambda g, h: (0, h)),
                pl.BlockSpec((1, SH * DV), lambda g, h: (0, h)),
                st, pl.BlockSpec(memory_space=pl.ANY), pl.BlockSpec(memory_space=pl.ANY)]
    out_specs = [pl.BlockSpec((CHUNK, SH * DV), rq), st]
    out_shape = [jax.ShapeDtypeStruct((T, VD), BF16), jax.ShapeDtypeStruct(s_all.shape, F32)]
    return pl.pallas_call(
        _ret_sample_kernel, grid=(NB_S // SB, H // SH), in_specs=in_specs, out_specs=out_specs,
        out_shape=out_shape, input_output_aliases={10: 0, 11: 1},
        compiler_params=_cparams(2), name="ret_sample",
    )(tabs["gam"], q, k, v, sg, tabs["decay"], tabs["qdec"], tabs["kdec"], ng, s_in, y_prev, s_all)


def _ret_out_kernel(y_ref, wout, x_ref, g1p, g1s, sh2p, sh2s, sc2p, sc2s, n2g, wr, br,
                    xo_ref, h2_ref, idx_ref, gate_ref, cnt_ref):
    is_p = pl.program_id(0) < NPT
    y = jnp.dot(y_ref[...], wout[...], preferred_element_type=F32)
    _tail(is_p, x_ref[...], y, g1p, g1s, sh2p, sh2s, sc2p, sc2s, n2g, wr, br,
          xo_ref, h2_ref, idx_ref, gate_ref, cnt_ref)


def _ret_out(layer, y, wout, x, mod_p, mod_s, n2g, wr, br):
    in_specs = [_tile_spec(VD), _full_spec((VD, D)), _tile_spec(D)] + _tail_in_specs(layer)
    return pl.pallas_call(
        _ret_out_kernel, grid=(NT,), in_specs=in_specs, out_specs=_tail_out_specs(),
        out_shape=_tail_out_shapes(), compiler_params=_cparams(1), name="ret_out",
    )(y, wout, x, mod_p, mod_s, mod_p, mod_s, mod_p, mod_s, n2g, wr, br)


def _lane_prefix(v, lane1):
    s = 1
    while s < LANES:
        v = v + jnp.where(lane1 >= s, pltpu.roll(v, s, 1), 0.0)
        s *= 2
    return v


def _route_kernel(cnt_ref, idx_ref, lp_ref, lpt_ref, runs_ref, meta_ref, base_scr):
    lane = lax.broadcasted_iota(I32, (TM, LANES), 1)
    lane1 = lax.broadcasted_iota(I32, (1, LANES), 1)
    row = lax.broadcasted_iota(I32, (8, LANES), 0)

    @pl.when(pl.program_id(0) == 0)
    def _():
        cnt = cnt_ref[...]
        padded = (((cnt.astype(I32) + (BLK - 1)) // BLK) * BLK).astype(F32)
        end = _lane_prefix(padded, lane1)
        base_scr[...] = end - padded
        meta = jnp.where(row == 0, cnt, jnp.where(row == 1, end - padded, jnp.where(row == 2, end, 0.0)))
        meta_ref[...] = meta.astype(I32)

    ri = lax.broadcasted_iota(I32, (TM, TM), 0)
    ci = lax.broadcasted_iota(I32, (TM, TM), 1)
    before = jnp.where(ci < ri, 1.0, 0.0).astype(BF16)
    base = base_scr[...]
    for j in range(ROUTE_TILES):
        idx = idx_ref[j * TM:(j + 1) * TM, :]
        hits = [lane == idx[:, k:k + 1] for k in range(TOPK)]
        chosen = jnp.zeros((TM, LANES), F32)
        for hk in hits:
            chosen = chosen + jnp.where(hk, 1.0, 0.0)
        colsum = jnp.sum(chosen, axis=0, keepdims=True)
        loff = _lane_prefix(colsum, lane1) - colsum
        pos = jnp.dot(before, chosen.astype(BF16), preferred_element_type=F32) + loff
        lp = jnp.zeros((TM, LANES), F32)
        for k, hk in enumerate(hits):
            lp = jnp.where(lane == k, jnp.sum(jnp.where(hk, pos, 0.0), axis=-1, keepdims=True), lp)
        lp_ref[j * TM:(j + 1) * TM, :] = lp.astype(I32)
        lpt_ref[j * 8:(j + 1) * 8, :] = lp.T[:8].astype(I32)
        runs = jnp.where(row == 0, colsum, jnp.where(row == 1, loff, jnp.where(row == 2, base, 0.0)))
        runs_ref[j] = runs.astype(I32)
        base = base + colsum
    base_scr[...] = base


def _route(cnt, idx):
    n_tiles = idx.shape[0] // TM
    assert n_tiles % ROUTE_TILES == 0
    rt = ROUTE_TILES
    return pl.pallas_call(
        _route_kernel, grid=(n_tiles // rt,),
        in_specs=[pl.BlockSpec((1, LANES), lambda t: (0, 0)), pl.BlockSpec((rt * TM, LANES), lambda t: (t, 0))],
        out_specs=[pl.BlockSpec((rt * TM, LANES), lambda t: (t, 0)),
                   pl.BlockSpec((rt * 8, TM), lambda t: (t, 0)),
                   pl.BlockSpec((rt, 8, LANES), lambda t: (t, 0, 0)),
                   pl.BlockSpec((8, LANES), lambda t: (0, 0))],
        out_shape=[jax.ShapeDtypeStruct((n_tiles * TM, LANES), I32),
                   jax.ShapeDtypeStruct((n_tiles * 8, TM), I32),
                   jax.ShapeDtypeStruct((n_tiles, 8, LANES), I32),
                   jax.ShapeDtypeStruct((8, LANES), I32)],
        scratch_shapes=[pltpu.VMEM((1, LANES), F32)],
        compiler_params=_cparams(1), name="moe_route",
    )(cnt, idx)


def _tile_rows(ref, r, n):
    return ref.at[:, pl.ds(r, n), :]


def _run_copy(src, dst, r_src, r_dst, n, sem, wait):
    for b in reversed(range(TM.bit_length())):
        size = 1 << b
        off = (n >> (b + 1)) << (b + 1)

        def piece(off=off, size=size):
            cp = pltpu.make_async_copy(_tile_rows(src, r_src + off, size),
                                       _tile_rows(dst, r_dst + off, size), sem)
            if wait:
                cp.wait()
            else:
                cp.start()

        if isinstance(n, int):
            if n & size:
                piece()
        else:
            pl.when((n & size) != 0)(piece)


def _load_rows(ref3, rows=None):
    rows = ref3.shape[1] if rows is None else rows
    parts = []
    for s in range(SLABS):
        word = ref3[s, :rows, :]
        parts.append(lax.bitcast_convert_type(lax.shift_left(word, 16), F32))
        parts.append(lax.bitcast_convert_type(word & HIGH_HALF, F32))
    return jnp.concatenate(parts, axis=1).astype(BF16)


def _store_rows(ref3, val):
    rows = val.shape[0]
    bits = lax.bitcast_convert_type(val, I32)
    for s in range(SLABS):
        low = lax.shift_right_logical(bits[:, (2 * s) * LANES:(2 * s + 1) * LANES], 16)
        ref3[s, :rows, :] = bits[:, (2 * s + 1) * LANES:(2 * s + 2) * LANES] | low


def _dispatch_kernel(len_ref, pos_ref, row_ref, cnt_ref, first_ref, end_ref, lpt_ref, h2_ref, dst,
                     buf, zbuf, sem, zsem):
    n_rows = dst.shape[1]
    step = pl.program_id(0)
    slot = step % 2
    cur = buf.at[slot]
    lpt = lpt_ref[...]
    p = lax.broadcasted_iota(I32, (TM * TOPK, TM), 0)
    pick = lpt[0:1, :] == p
    for k in range(1, TOPK):
        pick = pick | (lpt[k:k + 1, :] == p)
    perm = jnp.where(pick, 1.0, 0.0).astype(BF16)
    _store_rows(cur, jnp.dot(perm, h2_ref[...], preferred_element_type=F32))

    def per_expert(e, carry):
        _run_copy(cur, dst, pos_ref[e], row_ref[e], len_ref[e], sem.at[slot], False)
        return carry

    lax.fori_loop(0, NE, per_expert, 0)

    def drain(s):
        pltpu.make_async_copy(buf.at[s], _tile_rows(dst, 0, TM * TOPK), sem.at[s]).wait()

    @pl.when(step > 0)
    def _():
        drain(1 - slot)

    @pl.when(step == pl.num_programs(0) - 1)
    def _():
        drain(slot)

    @pl.when(step == 0)
    def _():
        zbuf[...] = jnp.zeros(zbuf.shape, I32)
        n_tail = (n_rows - end_ref[NE - 1]) // TM
        for wait in (False, True):
            def pad(e, carry, wait=wait):
                lo = first_ref[e] + cnt_ref[e]
                _run_copy(zbuf, dst, 0, lo, end_ref[e] - lo, zsem, wait)
                return carry

            lax.fori_loop(0, NE, pad, 0)

            def tail(j, carry, wait=wait):
                _run_copy(zbuf, dst, 0, end_ref[NE - 1] + j * TM, TM, zsem, wait)
                return carry

            lax.fori_loop(0, n_tail, tail, 0)


def _dispatch(h2, lpt, run_len, run_pos, run_row, cnt, first, end, n_rows):
    assert BLK % TM == 0 and BLK < 2 * TM + 1
    n_tiles = h2.shape[0] // TM
    smem = pl.BlockSpec(memory_space=pltpu.SMEM)
    per_tile = pl.BlockSpec((LANES,), lambda t: (t,), memory_space=pltpu.SMEM)
    return pl.pallas_call(
        _dispatch_kernel, grid=(n_tiles,),
        in_specs=[per_tile, per_tile, per_tile, smem, smem, smem,
                  pl.BlockSpec((8, TM), lambda t: (t, 0)), _tile_spec(D, TM)],
        out_specs=pl.BlockSpec(memory_space=pl.ANY),
        out_shape=jax.ShapeDtypeStruct((SLABS, n_rows, LANES), I32),
        scratch_shapes=[pltpu.VMEM((2, SLABS, TM * TOPK, LANES), I32),
                        pltpu.VMEM((SLABS, BLK, LANES), I32),
                        pltpu.SemaphoreType.DMA((2,)), pltpu.SemaphoreType.DMA],
        compiler_params=_cparams(1), name="moe_dispatch",
    )(run_len, run_pos, run_row, cnt, first, end, lpt, h2)


def _expert_kernel(be_ref, nu_ref, nr_ref, nxt_ref, par_ref, xb_ref, wu_hbm, bu_ref, wd_hbm, bd_ref, yb_ref,
                   wuf, wdf, wub, wdb, sem, *, layer):
    i = pl.program_id(0)
    used = i < nu_ref[0]
    half = nr_ref[i] <= BLK // 2
    slot = par_ref[i]

    def weights(e, s):
        return (pltpu.make_async_copy(wu_hbm.at[layer, e], wuf.at[s], sem.at[s, 0]),
                pltpu.make_async_copy(wd_hbm.at[layer, e], wdf.at[s], sem.at[s, 1]))

    @pl.when(i == 0)
    def _():
        for cp in weights(be_ref[0], slot):
            cp.start()

    @pl.when(used & ((i == 0) | (be_ref[i] != be_ref[jnp.maximum(i - 1, 0)])))
    def _():
        for cp in weights(be_ref[i], slot):
            cp.wait()
        wub[...] = wuf[slot].astype(BF16)
        wdb[...] = wdf[slot].astype(BF16)

        @pl.when(nxt_ref[i] >= 0)
        def _():
            for cp in weights(nxt_ref[i], 1 - slot):
                cp.start()

    def swiglu_rows(rows):
        z = jnp.dot(_load_rows(xb_ref, rows), wub[...], preferred_element_type=F32) + bu_ref[0, 0]
        glu = jnp.minimum(z[:, :FE], LIMIT)
        lin = jnp.clip(z[:, FE:], -LIMIT, LIMIT)
        act = glu * jax.nn.sigmoid(ALPHA * glu) * (lin + 1.0)
        y = jnp.dot(act.astype(BF16), wdb[...], preferred_element_type=F32) + bd_ref[0, 0]
        _store_rows(yb_ref, y.astype(BF16).astype(F32))

    @pl.when(used & jnp.logical_not(half))
    def _():
        swiglu_rows(BLK)

    @pl.when(used & half)
    def _():
        swiglu_rows(BLK // 2)
        yb_ref[:, BLK // 2:, :] = jnp.zeros((SLABS, BLK // 2, LANES), I32)

    @pl.when(jnp.logical_not(used))
    def _():
        yb_ref[...] = jnp.zeros(yb_ref.shape, I32)


def _experts(layer, xb, blk_e, n_used, blk_rows, blk_next, blk_par, wu, bu, wd, bd):
    n_blocks = blk_e.shape[0]
    grid_spec = pltpu.PrefetchScalarGridSpec(
        num_scalar_prefetch=5, grid=(n_blocks,),
        in_specs=[pl.BlockSpec((SLABS, BLK, LANES), lambda i, be, nu, *_: (0, jnp.minimum(i, nu[0] - 1), 0)),
                  pl.BlockSpec(memory_space=pl.ANY),
                  pl.BlockSpec((1, 1, 1, 2 * FE), lambda i, be, *_: (layer, be[i], 0, 0)),
                  pl.BlockSpec(memory_space=pl.ANY),
                  pl.BlockSpec((1, 1, 1, D), lambda i, be, *_: (layer, be[i], 0, 0))],
        out_specs=pl.BlockSpec((SLABS, BLK, LANES), lambda i, *_: (0, i, 0)),
        scratch_shapes=[pltpu.VMEM((2, D, 2 * FE), F32), pltpu.VMEM((2, FE, D), F32),
                        pltpu.VMEM((D, 2 * FE), BF16), pltpu.VMEM((FE, D), BF16),
                        pltpu.SemaphoreType.DMA((2, 2))])
    return pl.pallas_call(
        functools.partial(_expert_kernel, layer=layer), grid_spec=grid_spec,
        out_shape=jax.ShapeDtypeStruct((SLABS, n_blocks * BLK, LANES), I32),
        compiler_params=_cparams(1), name="experts",
    )(blk_e, n_used, blk_rows, blk_next, blk_par, xb, wu, bu, wd, bd)


def _combine_kernel(len_ref, pos_ref, row_ref, len_nx, pos_nx, row_nx, yb, lp_ref, x_ref, gate_ref,
                    g2p, g2s, fg_ref, *outs_and_scratch, final):
    *o_ref, ybuf, sem = outs_and_scratch
    step = pl.program_id(0)
    slot = step % 2
    is_p = step < T_P // TM

    def fetch(len_r, pos_r, row_r, s):
        def per_expert(e, carry):
            _run_copy(yb, ybuf.at[s], row_r[e], pos_r[e], len_r[e], sem.at[s], False)
            return carry

        lax.fori_loop(0, NE, per_expert, 0)

    @pl.when(step == 0)
    def _():
        fetch(len_ref, pos_ref, row_ref, slot)

    @pl.when(step + 1 < pl.num_programs(0))
    def _():
        fetch(len_nx, pos_nx, row_nx, 1 - slot)

    cur = ybuf.at[slot]
    pltpu.make_async_copy(_tile_rows(yb, 0, TM * TOPK), cur, sem.at[slot]).wait()

    rows = x_ref.shape[0]
    lp = lp_ref[...]
    gates = gate_ref[...]
    p = lax.broadcasted_iota(I32, (rows, rows * TOPK), 1)
    weights = jnp.zeros((rows, rows * TOPK), F32)
    for k in range(TOPK):
        weights = jnp.where(lp[:, k:k + 1] == p, gates[:, k:k + 1], weights)
    acc = jnp.dot(weights.astype(BF16), _load_rows(cur), preferred_element_type=F32)
    xn = x_ref[...] + _pick(is_p, g2p, g2s) * acc
    if not final:
        o_ref[0][...] = xn
        return
    y = _rms(xn, fg_ref[...])
    op_ref, os_ref = o_ref

    @pl.when(is_p)
    def _():
        op_ref[...] = y

    @pl.when(jnp.logical_not(is_p))
    def _():
        os_ref[...] = y


def _combine(layer, x, yb, lp, run_len, run_pos, run_row, gates, mod_p, mod_s, final_g, final):
    n_tiles = x.shape[0] // TM
    n_prompt = T_P // TM
    if final:
        out_specs = [pl.BlockSpec((TM, D), lambda t: (jnp.minimum(t, n_prompt - 1), 0)),
                     pl.BlockSpec((TM, D), lambda t: (jnp.maximum(t - n_prompt, 0), 0))]
        out_shape = [jax.ShapeDtypeStruct((T_P, D), F32), jax.ShapeDtypeStruct((x.shape[0] - T_P, D), F32)]
    else:
        out_specs, out_shape = _tile_spec(D, TM), jax.ShapeDtypeStruct(x.shape, F32)
    per_tile = pl.BlockSpec((LANES,), lambda t: (t,), memory_space=pltpu.SMEM)
    next_tile = pl.BlockSpec((LANES,), lambda t: (jnp.minimum(t + 1, n_tiles - 1),),
                             memory_space=pltpu.SMEM)
    in_specs = ([per_tile, per_tile, per_tile, next_tile, next_tile, next_tile,
                 pl.BlockSpec(memory_space=pl.ANY),
                 _tile_spec(LANES, TM), _tile_spec(D, TM), _tile_spec(LANES, TM)]
                + _mod_specs(layer, 5, TM) + [_full_spec((1, D))])
    return pl.pallas_call(
        functools.partial(_combine_kernel, final=final), grid=(n_tiles,), in_specs=in_specs,
        out_specs=out_specs, out_shape=out_shape,
        scratch_shapes=[pltpu.VMEM((2, SLABS, TM * TOPK, LANES), I32), pltpu.SemaphoreType.DMA((2,))],
        compiler_params=_cparams(1), name="moe_combine",
    )(run_len, run_pos, run_row, run_len, run_pos, run_row, yb, lp, x, gates, mod_p, mod_s, final_g)


def _block_tables(cnt, first, end, n_blocks):
    experts = jnp.arange(NE, dtype=I32)
    blk_first = jnp.arange(n_blocks, dtype=I32) * BLK
    blk_e = jnp.minimum(jnp.sum((end[None, :] <= blk_first[:, None]).astype(I32), axis=1), NE - 1)
    n_used = end[NE - 1:] // BLK
    has_rows = cnt > 0
    later = has_rows[None, :] & (experts[None, :] > experts[:, None])
    next_e = jnp.min(jnp.where(later, experts[None, :], NE), axis=1)
    next_e = jnp.where(next_e == NE, -1, next_e)
    parity = (jnp.cumsum(has_rows.astype(I32)) - 1) % 2
    own = blk_e[:, None] == experts[None, :]
    of_block = lambda per_expert: jnp.sum(jnp.where(own, per_expert[None, :], 0), axis=1)
    blk_rows = jnp.clip(of_block(first + cnt) - blk_first, 0, BLK)
    return blk_e, n_used, blk_rows, of_block(next_e), of_block(parity)


def _moe(layer, x, h2, idx, gates, cnt_all, mod_p, mod_s, wu, bu, wd, bd, final_g, final):
    lp, lpt, runs, meta = _route(cnt_all, idx)
    cnt, first, end = meta[0, :NE], meta[1, :NE], meta[2, :NE]
    run_len, run_pos, run_row = (runs[:, r, :].reshape(-1) for r in range(3))
    xb = _dispatch(h2, lpt, run_len, run_pos, run_row, cnt, first, end, N_ROWS)
    yb = _experts(layer, xb, *_block_tables(cnt, first, end, N_BLOCKS), wu, bu, wd, bd)
    return _combine(layer, x, yb, lp, run_len, run_pos, run_row, gates, mod_p, mod_s, final_g, final)


def _rope_tables():
    half = DK // 2
    inv = 1.0 / (ROPE_BASE ** jnp.linspace(0.0, 1.0, half, dtype=F32))

    def tab(pos):
        ang = pos.astype(F32)[:, None] * inv[None, :]
        cos, sin = jnp.cos(ang), jnp.sin(ang)
        return jnp.concatenate([cos, cos], -1), jnp.concatenate([-sin, sin], -1)

    cp, sp = tab(jnp.arange(L_P, dtype=I32))
    cs, ss = tab(PAST + jnp.arange(L_S, dtype=I32))
    cos = jnp.concatenate([jnp.tile(cp, (NB_P, 1)), jnp.tile(cs, (NB_S, 1))], 0)
    sin = jnp.concatenate([jnp.tile(sp, (NB_P, 1)), jnp.tile(ss, (NB_S, 1))], 0)
    return cos, sin


def _decay_tables(cl):
    lg = jnp.log(1.0 - 2.0 ** (-5.0 - jnp.arange(H, dtype=F32)))
    r = jnp.arange(CHUNK)
    idx = (r % cl).astype(F32)
    diff = idx[:, None] - idx[None, :]
    same = (r[:, None] // cl) == (r[None, :] // cl)
    decay = jnp.where((same & (diff >= 0))[None],
                      jnp.exp(lg[:, None, None] * jnp.maximum(diff, 0.0)[None]), 0.0)
    qdec = jnp.exp(lg[None, :] * (idx[:, None] + 1.0))
    kdec = jnp.exp(lg[None, :] * (cl - 1.0 - idx[:, None]))
    wide = lambda a: jnp.repeat(a, DK, axis=1)
    return {"decay": decay, "qdec": wide(qdec), "kdec": wide(kdec), "gam": jnp.exp(lg * cl)}


def kernel(x_prompt, x_sample, c_prompt, c_sample, state_ret, w_mod, b_mod, norm1_g, norm2_g,
           sgu_w_in, sgu_ln_g, sgu_ln_b, sgu_w_s, sgu_b_s, sgu_w_out, ret_w_in, ret_norm_g, ret_w_out,
           moe_w_router, moe_b_router, moe_w_up, moe_b_up, moe_w_down, moe_b_down, final_g):
    x = (x_prompt.reshape(T_P, D), x_sample.reshape(T_S, D))
    mod_p, mod_s = _modulation(jnp.concatenate([c_prompt, c_sample], 0), w_mod, b_mod)
    mod_p = mod_p.reshape(DEPTH * NB_P, 1, 6 * D)
    cos_tab, sin_tab = _rope_tables()
    tabs_p = _decay_tables(CHUNK)
    tabs_s = _decay_tables(L_S)
    wr_rows = jnp.pad(jnp.swapaxes(moe_w_router, 1, 2), ((0, 0), (0, LANES - NE), (0, 0))).astype(BF16)
    fg = final_g.reshape(1, D)
    b_up = moe_b_up.reshape(DEPTH, NE, 1, 2 * FE)
    b_down = moe_b_down.reshape(DEPTH, NE, 1, D)

    ret_p, v_rows = [], []
    s_all = lax.empty(state_ret.shape, F32)
    for i in range(DEPTH):
        j = i // 2
        n1g = norm1_g[i].reshape(1, D)
        n2g = norm2_g[i].reshape(1, D)
        wr = wr_rows[i]
        br = moe_b_router[i].reshape(NE, 1)
        if i % 2 == 0:
            mixw = jnp.stack([sgu_w_s[j], jnp.tile(sgu_w_s[j][:, :L_S, :L_S], (1, SB, SB))])
            bias_p = jnp.repeat(sgu_b_s[j].T, SGU_GD, axis=1)
            bias_s = jnp.tile(bias_p[:L_S], (SB, 1))
            x, h2, idx, gates, cnt, v = _sgu_layer(
                i, x, mod_p, mod_s, n1g, n2g, sgu_w_in[j].astype(BF16), sgu_ln_g[j].reshape(1, SGU_W),
                sgu_ln_b[j].reshape(1, SGU_W), mixw, jnp.stack([bias_p, bias_s]),
                sgu_w_out[j].astype(BF16), wr, br)
            v_rows.append(v.reshape(NB_S, L_S, SGU_W))
        else:
            q, k, v, sg = _ret_proj(i, x, mod_p, mod_s, n1g, ret_w_in[j].astype(BF16), cos_tab, sin_tab)
            ng = ret_norm_g[j].reshape(1, VD)
            y, s_p = _ret_prompt(q, k, v, sg, tabs_p, ng)
            y, s_all = _ret_sample(j, q, k, v, sg, tabs_s, ng, state_ret, y, s_all)
            ret_p.append(s_p)
            x, h2, idx, gates, cnt = _ret_out(i, y, ret_w_out[j].astype(BF16), x, mod_p, mod_s, n2g, wr, br)
        x = _moe(i, x, h2, idx, gates, cnt, mod_p, mod_s, moe_w_up, b_up, moe_w_down, b_down,
                 fg, final=(i == DEPTH - 1))
    y_prompt, y_sample = x
    return (y_prompt.reshape(NB_P, L_P, D), y_sample.reshape(NB_S, L_S, D), jnp.stack(ret_p), s_all,
            jnp.stack(v_rows))
```

```python
import functools

import jax
import jax.numpy as jnp
from jax import lax
from jax.experimental import pallas as pl
from jax.experimental.pallas import tpu as pltpu

F32 = jnp.float32
BF16 = jnp.bfloat16
I32 = jnp.int32

D = 1024
NB_P, L_P = 8, 2048
NB_S, L_S = 128, 8
PAST = 16384
DEPTH = 4
T_P = NB_P * L_P
T_S = NB_S * L_S
T = T_P + T_S
SGU_W = 2 * D
SGU_G = 8
SGU_GD = SGU_W // SGU_G
CHUNK = 128
H = 8
DK = D // H
DV = 2 * DK
QD = H * DK
VD = H * DV
RET_IN = 2 * QD + 2 * VD
ROPE_BASE = 10000.0
NE = 32
TOPK = 4
FE = D
ALPHA = 1.702
LIMIT = 7.0
EPS = 1e-6

LANES = 128
SLABS = D // (2 * LANES)
HIGH_HALF = -65536
TD = 512
NPT = T_P // TD
NT = T // TD
TM = 256
BLK = 512
TK = T * TOPK
N_BLOCKS = -(-(TK + NE * (BLK - 1)) // BLK)
N_ROWS = N_BLOCKS * BLK
SB = 16
ROUTE_TILES = 4
SH = 4
RET_CHUNKS = 8
VMEM_LIMIT = 58 * 1024 * 1024


def _cparams(n_axes):
    return pltpu.CompilerParams(dimension_semantics=("arbitrary",) * n_axes,
                                vmem_limit_bytes=VMEM_LIMIT)


def _rms(x, g):
    return (x * lax.rsqrt(jnp.mean(x * x, axis=-1, keepdims=True) + EPS)) * g


def _pick(is_p, p_ref, s_ref):
    return jnp.where(is_p, p_ref[0], s_ref[0])


def _mod_specs(layer, j, tm=TD):
    per_batch, n_prompt = L_P // tm, T_P // tm
    return [
        pl.BlockSpec((1, 1, D), lambda t: (layer * NB_P + jnp.minimum(t // per_batch, NB_P - 1), 0, j)),
        pl.BlockSpec((1, tm, D), lambda t: (layer, jnp.maximum(t - n_prompt, 0), j),
                     pipeline_mode=pl.Buffered(1)),
    ]


def _tile_spec(width, tm=TD):
    return pl.BlockSpec((tm, width), lambda t: (t, 0))


def _full_spec(shape):
    return pl.BlockSpec(shape, lambda *_: (0,) * len(shape), pipeline_mode=pl.Buffered(1))


def _mod_kernel(c_ref, w_ref, b_ref, p_ref, s_ref):
    c = c_ref[...]
    cs = (c * jax.nn.sigmoid(c)).astype(BF16)
    mod = jnp.dot(cs, w_ref[0].astype(BF16), preferred_element_type=F32) + b_ref[0]
    p_ref[0] = mod[:NB_P]
    per_batch = mod[NB_P:]
    s_ref[0] = jnp.broadcast_to(per_batch[:, None, :], (NB_S, L_S, per_batch.shape[-1])).reshape(T_S, -1)


def _modulation(c_all, w_mod, b_mod):
    tn = 1536
    nb = c_all.shape[0]
    return pl.pallas_call(
        _mod_kernel,
        grid=(DEPTH, 6 * D // tn),
        in_specs=[
            pl.BlockSpec((nb, D), lambda l, n: (0, 0)),
            pl.BlockSpec((1, D, tn), lambda l, n: (l, 0, n)),
            pl.BlockSpec((1, 1, tn), lambda l, n: (l, 0, n)),
        ],
        out_specs=[pl.BlockSpec((1, NB_P, tn), lambda l, n: (l, 0, n)),
                   pl.BlockSpec((1, T_S, tn), lambda l, n: (l, 0, n))],
        out_shape=[jax.ShapeDtypeStruct((DEPTH, NB_P, 6 * D), F32),
                   jax.ShapeDtypeStruct((DEPTH, T_S, 6 * D), F32)],
        compiler_params=_cparams(2),
        name="modulation",
    )(c_all, w_mod, b_mod.reshape(DEPTH, 1, 6 * D))


def _tail(is_p, x, y, g1p, g1s, sh2p, sh2s, sc2p, sc2s, n2g, wr, br, xo_ref, h2_ref, idx_ref, gate_ref,
          cnt_ref):
    xn = x + _pick(is_p, g1p, g1s) * y
    xo_ref[...] = xn
    h2 = (_rms(xn, n2g[...]) * (1.0 + _pick(is_p, sc2p, sc2s)) + _pick(is_p, sh2p, sh2s)).astype(BF16)
    h2_ref[...] = h2
    logit = lax.dot_general(wr[...], h2, (((1,), (1,)), ((), ())), preferred_element_type=F32)[:NE] + br[...]
    expert = lax.broadcasted_iota(I32, logit.shape, 0)
    vals, ids = [], []
    for _ in range(TOPK):
        m = jnp.max(logit, axis=0, keepdims=True)
        sel = jnp.min(jnp.where(logit == m, expert, NE), axis=0, keepdims=True)
        vals.append(m)
        ids.append(sel)
        logit = jnp.where(expert == sel, -jnp.inf, logit)
    es = [jnp.exp(v - vals[0]) for v in vals]
    tot = (es[0] + es[1]) + (es[2] + es[3])
    slot = lax.broadcasted_iota(I32, (LANES, logit.shape[1]), 0)
    idx_t = jnp.zeros(slot.shape, F32)
    gate_t = jnp.zeros(slot.shape, F32)
    for k in range(TOPK):
        idx_t = jnp.where(slot == k, ids[k].astype(F32), idx_t)
        gate_t = jnp.where(slot == k, es[k] / tot, gate_t)
    idx_out = idx_t.T.astype(I32)
    idx_ref[...] = idx_out
    gate_ref[...] = gate_t.T
    lane = lax.broadcasted_iota(I32, idx_out.shape, 1)
    chosen = jnp.zeros(idx_out.shape, F32)
    for k in range(TOPK):
        chosen = chosen + jnp.where(lane == idx_out[:, k:k + 1], 1.0, 0.0)

    @pl.when(pl.program_id(0) == 0)
    def _():
        cnt_ref[...] = jnp.zeros(cnt_ref.shape, F32)

    cnt_ref[...] += jnp.sum(chosen, axis=0, keepdims=True)


def _tail_in_specs(layer):
    return (_mod_specs(layer, 2) + _mod_specs(layer, 3) + _mod_specs(layer, 4)
            + [_full_spec((1, D)), _full_spec((LANES, D)), _full_spec((NE, 1))])


def _tail_out_specs():
    return [_tile_spec(D), _tile_spec(D), _tile_spec(LANES), _tile_spec(LANES),
            pl.BlockSpec((1, LANES), lambda t: (0, 0))]


def _tail_out_shapes():
    return [jax.ShapeDtypeStruct((T, D), F32), jax.ShapeDtypeStruct((T, D), BF16),
            jax.ShapeDtypeStruct((T, LANES), I32), jax.ShapeDtypeStruct((T, LANES), F32),
            jax.ShapeDtypeStruct((1, LANES), F32)]


def _sgu_kernel(x_ref, xs_ref, sh1p, sh1s, sc1p, sc1s, n1g, win, lng, lnb, mixw, mixb, wout,
                g1p, g1s, sh2p, sh2s, sc2p, sc2s, n2g, wr, br,
                xo_ref, h2_ref, idx_ref, gate_ref, cnt_ref, v_ref, y_scr, *, split):
    t = pl.program_id(0)
    is_p = t < NPT
    x = jnp.where(is_p, x_ref[...], xs_ref[...]) if split else x_ref[...]
    h = _rms(x, n1g[...]) * (1.0 + _pick(is_p, sc1p, sc1s)) + _pick(is_p, sh1p, sh1s)
    z = jnp.dot(h.astype(BF16), win[...], preferred_element_type=F32)
    z = 0.5 * z * (1.0 + lax.erf(z * (0.5 ** 0.5)))
    u = z[:, :SGU_W]
    v = z[:, SGU_W:]
    vc = v - jnp.mean(v, axis=-1, keepdims=True)
    vn = vc * lax.rsqrt(jnp.mean(vc * vc, axis=-1, keepdims=True) + EPS) * lng[...] + lnb[...]

    @pl.when(t >= NPT)
    def _():
        v_ref[...] = vn

    vb = vn.astype(BF16)
    ri = lax.broadcasted_iota(I32, (CHUNK, CHUNK), 0)
    ci = lax.broadcasted_iota(I32, (CHUNK, CHUNK), 1)
    causal = ci <= ri
    shift = jnp.broadcast_to(jnp.where(is_p, 7, 3), ri.shape)
    keep = causal & (lax.shift_right_logical(ri, shift) == lax.shift_right_logical(ci, shift))
    for g in range(SGU_G):
        wg = jnp.where(keep, mixw[0, g], 0.0).astype(BF16)
        for c in range(TD // CHUNK):
            rows = slice(c * CHUNK, (c + 1) * CHUNK)
            cols = slice(g * SGU_GD, (g + 1) * SGU_GD)
            mixed = jnp.dot(wg, vb[rows, cols], preferred_element_type=F32) + mixb[0, :, cols]
            y_scr[rows, cols] = (u[rows, cols] * mixed).astype(BF16)
    y = jnp.dot(y_scr[...], wout[...], preferred_element_type=F32)
    _tail(is_p, x, y, g1p, g1s, sh2p, sh2s, sc2p, sc2s, n2g, wr, br, xo_ref, h2_ref, idx_ref, gate_ref,
          cnt_ref)


def _sgu_layer(layer, x, mod_p, mod_s, n1g, n2g, win, lng, lnb, mixw, mixb, wout, wr, br):
    sel = lambda t: (jnp.where(t < NPT, 0, 1), 0, 0, 0)
    split = isinstance(x, tuple)
    if split:
        x, xs = x
        x_specs = [pl.BlockSpec((TD, D), lambda t: (jnp.minimum(t, NPT - 1), 0)),
                   pl.BlockSpec((TD, D), lambda t: (jnp.maximum(t - NPT, 0), 0), pipeline_mode=pl.Buffered(1))]
    else:
        xs = x
        x_specs = [_tile_spec(D), pl.BlockSpec((TD, D), lambda t: (0, 0), pipeline_mode=pl.Buffered(1))]
    in_specs = (x_specs + _mod_specs(layer, 0) + _mod_specs(layer, 1)
                + [_full_spec((1, D)), _full_spec((D, 2 * SGU_W)), _full_spec((1, SGU_W)),
                   _full_spec((1, SGU_W)),
                   pl.BlockSpec((1, SGU_G, CHUNK, CHUNK), sel),
                   pl.BlockSpec((1, CHUNK, SGU_W), lambda t: (jnp.where(t < NPT, 0, 1), 0, 0)),
                   _full_spec((SGU_W, D))]
                + _tail_in_specs(layer))
    out_specs = _tail_out_specs() + [pl.BlockSpec((TD, SGU_W), lambda t: (jnp.maximum(t - NPT, 0), 0))]
    out_shape = _tail_out_shapes() + [jax.ShapeDtypeStruct((T_S, SGU_W), F32)]
    return pl.pallas_call(
        functools.partial(_sgu_kernel, split=split), grid=(NT,), in_specs=in_specs, out_specs=out_specs,
        out_shape=out_shape, scratch_shapes=[pltpu.VMEM((TD, SGU_W), BF16)],
        compiler_params=_cparams(1), name="sgu_layer",
    )(x, xs, mod_p, mod_s, mod_p, mod_s, n1g, win, lng, lnb, mixw, mixb, wout,
      mod_p, mod_s, mod_p, mod_s, mod_p, mod_s, n2g, wr, br)


def _ret_proj_kernel(x_ref, sh1p, sh1s, sc1p, sc1s, n1g, win, cos_ref, sin_ref,
                     q_ref, k_ref, v_ref, sg_ref):
    t = pl.program_id(0)
    is_p = t < NPT
    x = x_ref[...]
    h = _rms(x, n1g[...]) * (1.0 + _pick(is_p, sc1p, sc1s)) + _pick(is_p, sh1p, sh1s)
    p = jnp.dot(h.astype(BF16), win[...], preferred_element_type=F32)
    cos = cos_ref[...]
    sin = sin_ref[...]
    for hd in range(H):
        cq = slice(hd * DK, (hd + 1) * DK)
        ck = slice(QD + hd * DK, QD + (hd + 1) * DK)
        qh = p[:, cq]
        kh = p[:, ck]
        q_ref[:, cq] = (qh * cos + pltpu.roll(qh, DK // 2, 1) * sin).astype(BF16)
        k_ref[:, cq] = ((kh * cos + pltpu.roll(kh, DK // 2, 1) * sin) * (DK ** -0.5)).astype(BF16)
    v_ref[...] = p[:, 2 * QD:2 * QD + VD].astype(BF16)
    g = p[:, 2 * QD + VD:]
    sg_ref[...] = (g * jax.nn.sigmoid(g)).astype(BF16)


def _ret_proj(layer, x, mod_p, mod_s, n1g, win, cos_tab, sin_tab):
    in_specs = ([_tile_spec(D)] + _mod_specs(layer, 0) + _mod_specs(layer, 1)
                + [_full_spec((1, D)), _full_spec((D, RET_IN)), _tile_spec(DK), _tile_spec(DK)])
    out_specs = [_tile_spec(QD), _tile_spec(QD), _tile_spec(VD), _tile_spec(VD)]
    out_shape = [jax.ShapeDtypeStruct((T, QD), BF16), jax.ShapeDtypeStruct((T, QD), BF16),
                 jax.ShapeDtypeStruct((T, VD), BF16), jax.ShapeDtypeStruct((T, VD), BF16)]
    return pl.pallas_call(
        _ret_proj_kernel, grid=(NT,), in_specs=in_specs, out_specs=out_specs, out_shape=out_shape,
        compiler_params=_cparams(1), name="ret_proj",
    )(x, mod_p, mod_s, mod_p, mod_s, n1g, win, cos_tab, sin_tab)


def _ret_intra(q, k, v, decay, qdec, kdec):
    s = lax.dot_general(q, k, (((1,), (1,)), ((), ())), preferred_element_type=F32) * decay
    o = jnp.dot(s.astype(BF16), v, preferred_element_type=F32)
    return o, q.astype(F32) * qdec, k.astype(F32) * kdec


def _ret_finish(o, sg, ng):
    on = o * lax.rsqrt(jnp.mean(o * o, axis=-1, keepdims=True) + EPS)
    return (sg.astype(F32) * (on * ng)).astype(BF16)


def _ret_prompt_kernel(gam_ref, q_ref, k_ref, v_ref, sg_ref, dec_ref, qdec_ref, kdec_ref, ng_ref,
                       y_ref, s_ref):
    c = pl.program_id(1)

    @pl.when(c == 0)
    def _():
        s_ref[...] = jnp.zeros(s_ref.shape, F32)

    cks = [slice(hd * DK, (hd + 1) * DK) for hd in range(H)]
    cvs = [slice(hd * DV, (hd + 1) * DV) for hd in range(H)]
    for j in range(RET_CHUNKS):
        rows = slice(j * CHUNK, (j + 1) * CHUNK)
        vs = [v_ref[rows, cv] for cv in cvs]
        intra = [_ret_intra(q_ref[rows, ck], k_ref[rows, ck], v, dec_ref[hd], qdec_ref[:, ck], kdec_ref[:, ck])
                 for hd, (ck, v) in enumerate(zip(cks, vs))]
        olds = [s_ref[0, hd] for hd in range(H)]
        outs = [o + jnp.dot(qd.astype(BF16), s_old.astype(BF16), preferred_element_type=F32)
                for (o, qd, _), s_old in zip(intra, olds)]
        for hd in range(H):
            s_ref[0, hd] = gam_ref[hd] * olds[hd] + lax.dot_general(
                intra[hd][2].astype(BF16), vs[hd], (((0,), (0,)), ((), ())), preferred_element_type=F32)
        for hd, cv in enumerate(cvs):
            y_ref[rows, cv] = _ret_finish(outs[hd], sg_ref[rows, cv], ng_ref[:, cv])


def _ret_prompt(q, k, v, sg, tabs, ng):
    rows = RET_CHUNKS * CHUNK
    nc = L_P // rows
    row = lambda b, c: (b * nc + c, 0)
    smem = pl.BlockSpec(memory_space=pltpu.SMEM)
    in_specs = [smem,
                pl.BlockSpec((rows, QD), row), pl.BlockSpec((rows, QD), row),
                pl.BlockSpec((rows, VD), row), pl.BlockSpec((rows, VD), row),
                _full_spec((H, CHUNK, CHUNK)), _full_spec((CHUNK, QD)), _full_spec((CHUNK, QD)),
                _full_spec((1, VD))]
    out_specs = [pl.BlockSpec((rows, VD), row),
                 pl.BlockSpec((1, H, DK, DV), lambda b, c: (b, 0, 0, 0))]
    out_shape = [jax.ShapeDtypeStruct((T, VD), BF16), jax.ShapeDtypeStruct((NB_P, H, DK, DV), F32)]
    return pl.pallas_call(
        _ret_prompt_kernel, grid=(NB_P, nc), in_specs=in_specs, out_specs=out_specs, out_shape=out_shape,
        compiler_params=_cparams(2), name="ret_prompt",
    )(tabs["gam"], q, k, v, sg, tabs["decay"], tabs["qdec"], tabs["kdec"], ng)


def _ret_sample_kernel(gam_ref, q_ref, k_ref, v_ref, sg_ref, dec_ref, qdec_ref, kdec_ref, ng_ref,
                       s_in_ref, y_in_ref, s_all_ref, y_ref, s_out_ref):
    del y_in_ref, s_all_ref
    s_in_ref = s_in_ref.at[0]
    s_out_ref = s_out_ref.at[0]
    row_batch = lax.broadcasted_iota(I32, (CHUNK, DK), 0) // L_S
    for i in range(SH):
        ck = slice(i * DK, (i + 1) * DK)
        cv = slice(i * DV, (i + 1) * DV)
        v = v_ref[:, cv]
        o, qd, kd = _ret_intra(q_ref[:, ck], k_ref[:, ck], v, dec_ref[i], qdec_ref[:, ck], kdec_ref[:, ck])
        gam = gam_ref[pl.program_id(1) * SH + i]
        cross = [jnp.dot(qd[b * L_S:(b + 1) * L_S].astype(BF16), s_in_ref[b, i].astype(BF16),
                         preferred_element_type=F32) for b in range(SB)]
        for b in range(SB):
            kb = jnp.where(row_batch == b, kd, 0.0).astype(BF16)
            s_out_ref[b, i] = gam * s_in_ref[b, i] + lax.dot_general(
                kb, v, (((0,), (0,)), ((), ())), preferred_element_type=F32)
        o = o + jnp.concatenate(cross, axis=0)
        y_ref[:, cv] = _ret_finish(o, sg_ref[:, cv], ng_ref[:, cv])


def _ret_sample(j, q, k, v, sg, tabs, ng, s_in, y_prev, s_all):
    base = T_P // CHUNK
    rq = lambda g, h: (base + g, h)
    st = pl.BlockSpec((1, SB, SH, DK, DV), lambda g, h: (j, g, h, 0, 0))
    smem = pl.BlockSpec(memory_space=pltpu.SMEM)
    in_specs = [smem,
                pl.BlockSpec((CHUNK, SH * DK), rq), pl.BlockSpec((CHUNK, SH * DK), rq),
                pl.BlockSpec((CHUNK, SH * DV), rq), pl.BlockSpec((CHUNK, SH * DV), rq),
                pl.BlockSpec((SH, CHUNK, CHUNK), lambda g, h: (h, 0, 0)),
                pl.BlockSpec((CHUNK, SH * DK), lambda g, h: (0, h)),
                pl.BlockSpec((CHUNK, SH * DK), lambda g, h: (0, h)),
                pl.BlockSpec((1, SH * DV), lambda g, h: (0, h)),
                st, pl.BlockSpec(memory_space=pl.ANY), pl.BlockSpec(memory_space=pl.ANY)]
    out_specs = [pl.BlockSpec((CHUNK, SH * DV), rq), st]
    out_shape = [jax.ShapeDtypeStruct((T, VD), BF16), jax.ShapeDtypeStruct(s_all.shape, F32)]
    return pl.pallas_call(
        _ret_sample_kernel, grid=(NB_S // SB, H // SH), in_specs=in_specs, out_specs=out_specs,
        out_shape=out_shape, input_output_aliases={10: 0, 11: 1},
        compiler_params=_cparams(2), name="ret_sample",
    )(tabs["gam"], q, k, v, sg, tabs["decay"], tabs["qdec"], tabs["kdec"], ng, s_in, y_prev, s_all)


def _ret_out_kernel(y_ref, wout, x_ref, g1p, g1s, sh2p, sh2s, sc2p, sc2s, n2g, wr, br,
                    xo_ref, h2_ref, idx_ref, gate_ref, cnt_ref):
    is_p = pl.program_id(0) < NPT
    y = jnp.dot(y_ref[...], wout[...], preferred_element_type=F32)
    _tail(is_p, x_ref[...], y, g1p, g1s, sh2p, sh2s, sc2p, sc2s, n2g, wr, br,
          xo_ref, h2_ref, idx_ref, gate_ref, cnt_ref)


def _ret_out(layer, y, wout, x, mod_p, mod_s, n2g, wr, br):
    in_specs = [_tile_spec(VD), _full_spec((VD, D)), _tile_spec(D)] + _tail_in_specs(layer)
    return pl.pallas_call(
        _ret_out_kernel, grid=(NT,), in_specs=in_specs, out_specs=_tail_out_specs(),
        out_shape=_tail_out_shapes(), compiler_params=_cparams(1), name="ret_out",
    )(y, wout, x, mod_p, mod_s, mod_p, mod_s, mod_p, mod_s, n2g, wr, br)


def _lane_prefix(v, lane1):
    s = 1
    while s < LANES:
        v = v + jnp.where(lane1 >= s, pltpu.roll(v, s, 1), 0.0)
        s *= 2
    return v


def _route_kernel(cnt_ref, idx_ref, lp_ref, lpt_ref, runs_ref, meta_ref, base_scr):
    lane = lax.broadcasted_iota(I32, (TM, LANES), 1)
    lane1 = lax.broadcasted_iota(I32, (1, LANES), 1)
    row = lax.broadcasted_iota(I32, (8, LANES), 0)

    @pl.when(pl.program_id(0) == 0)
    def _():
        cnt = cnt_ref[...]
        padded = (((cnt.astype(I32) + (BLK - 1)) // BLK) * BLK).astype(F32)
        end = _lane_prefix(padded, lane1)
        base_scr[...] = end - padded
        meta = jnp.where(row == 0, cnt, jnp.where(row == 1, end - padded, jnp.where(row == 2, end, 0.0)))
        meta_ref[...] = meta.astype(I32)

    ri = lax.broadcasted_iota(I32, (TM, TM), 0)
    ci = lax.broadcasted_iota(I32, (TM, TM), 1)
    before = jnp.where(ci < ri, 1.0, 0.0).astype(BF16)
    base = base_scr[...]
    for j in range(ROUTE_TILES):
        idx = idx_ref[j * TM:(j + 1) * TM, :]
        hits = [lane == idx[:, k:k + 1] for k in range(TOPK)]
        chosen = jnp.zeros((TM, LANES), F32)
        for hk in hits:
            chosen = chosen + jnp.where(hk, 1.0, 0.0)
        colsum = jnp.sum(chosen, axis=0, keepdims=True)
        loff = _lane_prefix(colsum, lane1) - colsum
        pos = jnp.dot(before, chosen.astype(BF16), preferred_element_type=F32) + loff
        lp = jnp.zeros((TM, LANES), F32)
        for k, hk in enumerate(hits):
            lp = jnp.where(lane == k, jnp.sum(jnp.where(hk, pos, 0.0), axis=-1, keepdims=True), lp)
        lp_ref[j * TM:(j + 1) * TM, :] = lp.astype(I32)
        lpt_ref[j * 8:(j + 1) * 8, :] = lp.T[:8].astype(I32)
        runs = jnp.where(row == 0, colsum, jnp.where(row == 1, loff, jnp.where(row == 2, base, 0.0)))
        runs_ref[j] = runs.astype(I32)
        base = base + colsum
    base_scr[...] = base


def _route(cnt, idx):
    n_tiles = idx.shape[0] // TM
    assert n_tiles % ROUTE_TILES == 0
    rt = ROUTE_TILES
    return pl.pallas_call(
        _route_kernel, grid=(n_tiles // rt,),
        in_specs=[pl.BlockSpec((1, LANES), lambda t: (0, 0)), pl.BlockSpec((rt * TM, LANES), lambda t: (t, 0))],
        out_specs=[pl.BlockSpec((rt * TM, LANES), lambda t: (t, 0)),
                   pl.BlockSpec((rt * 8, TM), lambda t: (t, 0)),
                   pl.BlockSpec((rt, 8, LANES), lambda t: (t, 0, 0)),
                   pl.BlockSpec((8, LANES), lambda t: (0, 0))],
        out_shape=[jax.ShapeDtypeStruct((n_tiles * TM, LANES), I32),
                   jax.ShapeDtypeStruct((n_tiles * 8, TM), I32),
                   jax.ShapeDtypeStruct((n_tiles, 8, LANES), I32),
                   jax.ShapeDtypeStruct((8, LANES), I32)],
        scratch_shapes=[pltpu.VMEM((1, LANES), F32)],
        compiler_params=_cparams(1), name="moe_route",
    )(cnt, idx)


def _tile_rows(ref, r, n):
    return ref.at[:, pl.ds(r, n), :]


def _run_copy(src, dst, r_src, r_dst, n, sem, wait):
    for b in reversed(range(TM.bit_length())):
        size = 1 << b
        off = (n >> (b + 1)) << (b + 1)

        def piece(off=off, size=size):
            cp = pltpu.make_async_copy(_tile_rows(src, r_src + off, size),
                                       _tile_rows(dst, r_dst + off, size), sem)
            if wait:
                cp.wait()
            else:
                cp.start()

        if isinstance(n, int):
            if n & size:
                piece()
        else:
            pl.when((n & size) != 0)(piece)


def _load_rows(ref3, rows=None):
    rows = ref3.shape[1] if rows is None else rows
    parts = []
    for s in range(SLABS):
        word = ref3[s, :rows, :]
        parts.append(lax.bitcast_convert_type(lax.shift_left(word, 16), F32))
        parts.append(lax.bitcast_convert_type(word & HIGH_HALF, F32))
    return jnp.concatenate(parts, axis=1).astype(BF16)


def _store_rows(ref3, val):
    rows = val.shape[0]
    bits = lax.bitcast_convert_type(val, I32)
    for s in range(SLABS):
        low = lax.shift_right_logical(bits[:, (2 * s) * LANES:(2 * s + 1) * LANES], 16)
        ref3[s, :rows, :] = bits[:, (2 * s + 1) * LANES:(2 * s + 2) * LANES] | low


def _dispatch_kernel(len_ref, pos_ref, row_ref, cnt_ref, first_ref, end_ref, lpt_ref, h2_ref, dst,
                     buf, zbuf, sem, zsem):
    n_rows = dst.shape[1]
    step = pl.program_id(0)
    slot = step % 2
    cur = buf.at[slot]
    lpt = lpt_ref[...]
    p = lax.broadcasted_iota(I32, (TM * TOPK, TM), 0)
    pick = lpt[0:1, :] == p
    for k in range(1, TOPK):
        pick = pick | (lpt[k:k + 1, :] == p)
    perm = jnp.where(pick, 1.0, 0.0).astype(BF16)
    _store_rows(cur, jnp.dot(perm, h2_ref[...], preferred_element_type=F32))

    def per_expert(e, carry):
        _run_copy(cur, dst, pos_ref[e], row_ref[e], len_ref[e], sem.at[slot], False)
        return carry

    lax.fori_loop(0, NE, per_expert, 0, unroll=8)

    def drain(s):
        pltpu.make_async_copy(buf.at[s], _tile_rows(dst, 0, TM * TOPK), sem.at[s]).wait()

    @pl.when(step > 0)
    def _():
        drain(1 - slot)

    @pl.when(step == pl.num_programs(0) - 1)
    def _():
        drain(slot)

    @pl.when(step == 0)
    def _():
        zbuf[...] = jnp.zeros(zbuf.shape, I32)
        n_tail = (n_rows - end_ref[NE - 1]) // TM
        for wait in (False, True):
            def pad(e, carry, wait=wait):
                lo = first_ref[e] + cnt_ref[e]
                _run_copy(zbuf, dst, 0, lo, end_ref[e] - lo, zsem, wait)
                return carry

            lax.fori_loop(0, NE, pad, 0)

            def tail(j, carry, wait=wait):
                _run_copy(zbuf, dst, 0, end_ref[NE - 1] + j * TM, TM, zsem, wait)
                return carry

            lax.fori_loop(0, n_tail, tail, 0)


def _dispatch(h2, lpt, run_len, run_pos, run_row, cnt, first, end, n_rows):
    assert BLK % TM == 0 and BLK < 2 * TM + 1
    n_tiles = h2.shape[0] // TM
    smem = pl.BlockSpec(memory_space=pltpu.SMEM)
    per_tile = pl.BlockSpec((LANES,), lambda t: (t,), memory_space=pltpu.SMEM)
    return pl.pallas_call(
        _dispatch_kernel, grid=(n_tiles,),
        in_specs=[per_tile, per_tile, per_tile, smem, smem, smem,
                  pl.BlockSpec((8, TM), lambda t: (t, 0)), _tile_spec(D, TM)],
        out_specs=pl.BlockSpec(memory_space=pl.ANY),
        out_shape=jax.ShapeDtypeStruct((SLABS, n_rows, LANES), I32),
        scratch_shapes=[pltpu.VMEM((2, SLABS, TM * TOPK, LANES), I32),
                        pltpu.VMEM((SLABS, BLK, LANES), I32),
                        pltpu.SemaphoreType.DMA((2,)), pltpu.SemaphoreType.DMA],
        compiler_params=_cparams(1), name="moe_dispatch",
    )(run_len, run_pos, run_row, cnt, first, end, lpt, h2)


def _expert_kernel(be_ref, nu_ref, nr_ref, nxt_ref, par_ref, xb_ref, wu_hbm, bu_ref, wd_hbm, bd_ref, yb_ref,
                   wuf, wdf, wub, wdb, sem, *, layer):
    i = pl.program_id(0)
    used = i < nu_ref[0]
    half = nr_ref[i] <= BLK // 2
    slot = par_ref[i]

    def weights(e, s):
        return (pltpu.make_async_copy(wu_hbm.at[layer, e], wuf.at[s], sem.at[s, 0]),
                pltpu.make_async_copy(wd_hbm.at[layer, e], wdf.at[s], sem.at[s, 1]))

    @pl.when(i == 0)
    def _():
        for cp in weights(be_ref[0], slot):
            cp.start()

    @pl.when(used & ((i == 0) | (be_ref[i] != be_ref[jnp.maximum(i - 1, 0)])))
    def _():
        for cp in weights(be_ref[i], slot):
            cp.wait()
        wub[...] = wuf[slot].astype(BF16)
        wdb[...] = wdf[slot].astype(BF16)

        @pl.when(nxt_ref[i] >= 0)
        def _():
            for cp in weights(nxt_ref[i], 1 - slot):
                cp.start()

    def swiglu_rows(rows):
        z = jnp.dot(_load_rows(xb_ref, rows), wub[...], preferred_element_type=F32) + bu_ref[0, 0]
        glu = jnp.minimum(z[:, :FE], LIMIT)
        lin = jnp.clip(z[:, FE:], -LIMIT, LIMIT)
        act = glu * jax.nn.sigmoid(ALPHA * glu) * (lin + 1.0)
        y = jnp.dot(act.astype(BF16), wdb[...], preferred_element_type=F32) + bd_ref[0, 0]
        _store_rows(yb_ref, y.astype(BF16).astype(F32))

    @pl.when(used & jnp.logical_not(half))
    def _():
        swiglu_rows(BLK)

    @pl.when(used & half)
    def _():
        swiglu_rows(BLK // 2)
        yb_ref[:, BLK // 2:, :] = jnp.zeros((SLABS, BLK // 2, LANES), I32)

    @pl.when(jnp.logical_not(used))
    def _():
        yb_ref[...] = jnp.zeros(yb_ref.shape, I32)


def _experts(layer, xb, blk_e, n_used, blk_rows, blk_next, blk_par, wu, bu, wd, bd):
    n_blocks = blk_e.shape[0]
    grid_spec = pltpu.PrefetchScalarGridSpec(
        num_scalar_prefetch=5, grid=(n_blocks,),
        in_specs=[pl.BlockSpec((SLABS, BLK, LANES), lambda i, be, nu, *_: (0, jnp.minimum(i, nu[0] - 1), 0)),
                  pl.BlockSpec(memory_space=pl.ANY),
                  pl.BlockSpec((1, 1, 1, 2 * FE), lambda i, be, *_: (layer, be[i], 0, 0)),
                  pl.BlockSpec(memory_space=pl.ANY),
                  pl.BlockSpec((1, 1, 1, D), lambda i, be, *_: (layer, be[i], 0, 0))],
        out_specs=pl.BlockSpec((SLABS, BLK, LANES), lambda i, *_: (0, i, 0)),
        scratch_shapes=[pltpu.VMEM((2, D, 2 * FE), F32), pltpu.VMEM((2, FE, D), F32),
                        pltpu.VMEM((D, 2 * FE), BF16), pltpu.VMEM((FE, D), BF16),
                        pltpu.SemaphoreType.DMA((2, 2))])
    return pl.pallas_call(
        functools.partial(_expert_kernel, layer=layer), grid_spec=grid_spec,
        out_shape=jax.ShapeDtypeStruct((SLABS, n_blocks * BLK, LANES), I32),
        compiler_params=_cparams(1), name="experts",
    )(blk_e, n_used, blk_rows, blk_next, blk_par, xb, wu, bu, wd, bd)


def _combine_kernel(len_ref, pos_ref, row_ref, len_nx, pos_nx, row_nx, yb, lp_ref, x_ref, gate_ref,
                    g2p, g2s, fg_ref, *outs_and_scratch, final):
    *o_ref, ybuf, sem = outs_and_scratch
    step = pl.program_id(0)
    slot = step % 2
    is_p = step < T_P // TM

    def fetch(len_r, pos_r, row_r, s):
        def per_expert(e, carry):
            _run_copy(yb, ybuf.at[s], row_r[e], pos_r[e], len_r[e], sem.at[s], False)
            return carry

        lax.fori_loop(0, NE, per_expert, 0, unroll=8)

    @pl.when(step == 0)
    def _():
        fetch(len_ref, pos_ref, row_ref, slot)

    @pl.when(step + 1 < pl.num_programs(0))
    def _():
        fetch(len_nx, pos_nx, row_nx, 1 - slot)

    cur = ybuf.at[slot]
    pltpu.make_async_copy(_tile_rows(yb, 0, TM * TOPK), cur, sem.at[slot]).wait()

    rows = x_ref.shape[0]
    lp = lp_ref[...]
    gates = gate_ref[...]
    p = lax.broadcasted_iota(I32, (rows, rows * TOPK), 1)
    weights = jnp.zeros((rows, rows * TOPK), F32)
    for k in range(TOPK):
        weights = jnp.where(lp[:, k:k + 1] == p, gates[:, k:k + 1], weights)
    acc = jnp.dot(weights.astype(BF16), _load_rows(cur), preferred_element_type=F32)
    xn = x_ref[...] + _pick(is_p, g2p, g2s) * acc
    if not final:
        o_ref[0][...] = xn
        return
    y = _rms(xn, fg_ref[...])
    op_ref, os_ref = o_ref

    @pl.when(is_p)
    def _():
        op_ref[...] = y

    @pl.when(jnp.logical_not(is_p))
    def _():
        os_ref[...] = y


def _combine(layer, x, yb, lp, run_len, run_pos, run_row, gates, mod_p, mod_s, final_g, final):
    n_tiles = x.shape[0] // TM
    n_prompt = T_P // TM
    if final:
        out_specs = [pl.BlockSpec((TM, D), lambda t: (jnp.minimum(t, n_prompt - 1), 0)),
                     pl.BlockSpec((TM, D), lambda t: (jnp.maximum(t - n_prompt, 0), 0))]
        out_shape = [jax.ShapeDtypeStruct((T_P, D), F32), jax.ShapeDtypeStruct((x.shape[0] - T_P, D), F32)]
    else:
        out_specs, out_shape = _tile_spec(D, TM), jax.ShapeDtypeStruct(x.shape, F32)
    per_tile = pl.BlockSpec((LANES,), lambda t: (t,), memory_space=pltpu.SMEM)
    next_tile = pl.BlockSpec((LANES,), lambda t: (jnp.minimum(t + 1, n_tiles - 1),),
                             memory_space=pltpu.SMEM)
    in_specs = ([per_tile, per_tile, per_tile, next_tile, next_tile, next_tile,
                 pl.BlockSpec(memory_space=pl.ANY),
                 _tile_spec(LANES, TM), _tile_spec(D, TM), _tile_spec(LANES, TM)]
                + _mod_specs(layer, 5, TM) + [_full_spec((1, D))])
    return pl.pallas_call(
        functools.partial(_combine_kernel, final=final), grid=(n_tiles,), in_specs=in_specs,
        out_specs=out_specs, out_shape=out_shape,
        scratch_shapes=[pltpu.VMEM((2, SLABS, TM * TOPK, LANES), I32), pltpu.SemaphoreType.DMA((2,))],
        compiler_params=_cparams(1), name="moe_combine",
    )(run_len, run_pos, run_row, run_len, run_pos, run_row, yb, lp, x, gates, mod_p, mod_s, final_g)


def _block_tables(cnt, first, end, n_blocks):
    experts = jnp.arange(NE, dtype=I32)
    blk_first = jnp.arange(n_blocks, dtype=I32) * BLK
    blk_e = jnp.minimum(jnp.sum((end[None, :] <= blk_first[:, None]).astype(I32), axis=1), NE - 1)
    n_used = end[NE - 1:] // BLK
    has_rows = cnt > 0
    later = has_rows[None, :] & (experts[None, :] > experts[:, None])
    next_e = jnp.min(jnp.where(later, experts[None, :], NE), axis=1)
    next_e = jnp.where(next_e == NE, -1, next_e)
    parity = (jnp.cumsum(has_rows.astype(I32)) - 1) % 2
    own = blk_e[:, None] == experts[None, :]
    of_block = lambda per_expert: jnp.sum(jnp.where(own, per_expert[None, :], 0), axis=1)
    blk_rows = jnp.clip(of_block(first + cnt) - blk_first, 0, BLK)
    return blk_e, n_used, blk_rows, of_block(next_e), of_block(parity)


def _moe(layer, x, h2, idx, gates, cnt_all, mod_p, mod_s, wu, bu, wd, bd, final_g, final):
    lp, lpt, runs, meta = _route(cnt_all, idx)
    cnt, first, end = meta[0, :NE], meta[1, :NE], meta[2, :NE]
    run_len, run_pos, run_row = (runs[:, r, :].reshape(-1) for r in range(3))
    xb = _dispatch(h2, lpt, run_len, run_pos, run_row, cnt, first, end, N_ROWS)
    yb = _experts(layer, xb, *_block_tables(cnt, first, end, N_BLOCKS), wu, bu, wd, bd)
    return _combine(layer, x, yb, lp, run_len, run_pos, run_row, gates, mod_p, mod_s, final_g, final)


def _rope_tables():
    half = DK // 2
    inv = 1.0 / (ROPE_BASE ** jnp.linspace(0.0, 1.0, half, dtype=F32))

    def tab(pos):
        ang = pos.astype(F32)[:, None] * inv[None, :]
        cos, sin = jnp.cos(ang), jnp.sin(ang)
        return jnp.concatenate([cos, cos], -1), jnp.concatenate([-sin, sin], -1)

    cp, sp = tab(jnp.arange(L_P, dtype=I32))
    cs, ss = tab(PAST + jnp.arange(L_S, dtype=I32))
    cos = jnp.concatenate([jnp.tile(cp, (NB_P, 1)), jnp.tile(cs, (NB_S, 1))], 0)
    sin = jnp.concatenate([jnp.tile(sp, (NB_P, 1)), jnp.tile(ss, (NB_S, 1))], 0)
    return cos, sin


def _decay_tables(cl):
    lg = jnp.log(1.0 - 2.0 ** (-5.0 - jnp.arange(H, dtype=F32)))
    r = jnp.arange(CHUNK)
    idx = (r % cl).astype(F32)
    diff = idx[:, None] - idx[None, :]
    same = (r[:, None] // cl) == (r[None, :] // cl)
    decay = jnp.where((same & (diff >= 0))[None],
                      jnp.exp(lg[:, None, None] * jnp.maximum(diff, 0.0)[None]), 0.0)
    qdec = jnp.exp(lg[None, :] * (idx[:, None] + 1.0))
    kdec = jnp.exp(lg[None, :] * (cl - 1.0 - idx[:, None]))
    wide = lambda a: jnp.repeat(a, DK, axis=1)
    return {"decay": decay, "qdec": wide(qdec), "kdec": wide(kdec), "gam": jnp.exp(lg * cl)}


def kernel(x_prompt, x_sample, c_prompt, c_sample, state_ret, w_mod, b_mod, norm1_g, norm2_g,
           sgu_w_in, sgu_ln_g, sgu_ln_b, sgu_w_s, sgu_b_s, sgu_w_out, ret_w_in, ret_norm_g, ret_w_out,
           moe_w_router, moe_b_router, moe_w_up, moe_b_up, moe_w_down, moe_b_down, final_g):
    x = (x_prompt.reshape(T_P, D), x_sample.reshape(T_S, D))
    mod_p, mod_s = _modulation(jnp.concatenate([c_prompt, c_sample], 0), w_mod, b_mod)
    mod_p = mod_p.reshape(DEPTH * NB_P, 1, 6 * D)
    cos_tab, sin_tab = _rope_tables()
    tabs_p = _decay_tables(CHUNK)
    tabs_s = _decay_tables(L_S)
    wr_rows = jnp.pad(jnp.swapaxes(moe_w_router, 1, 2), ((0, 0), (0, LANES - NE), (0, 0))).astype(BF16)
    fg = final_g.reshape(1, D)
    b_up = moe_b_up.reshape(DEPTH, NE, 1, 2 * FE)
    b_down = moe_b_down.reshape(DEPTH, NE, 1, D)

    ret_p, v_rows = [], []
    s_all = lax.empty(state_ret.shape, F32)
    for i in range(DEPTH):
        j = i // 2
        n1g = norm1_g[i].reshape(1, D)
        n2g = norm2_g[i].reshape(1, D)
        wr = wr_rows[i]
        br = moe_b_router[i].reshape(NE, 1)
        if i % 2 == 0:
            mixw = jnp.stack([sgu_w_s[j], jnp.tile(sgu_w_s[j][:, :L_S, :L_S], (1, SB, SB))])
            bias_p = jnp.repeat(sgu_b_s[j].T, SGU_GD, axis=1)
            bias_s = jnp.tile(bias_p[:L_S], (SB, 1))
            x, h2, idx, gates, cnt, v = _sgu_layer(
                i, x, mod_p, mod_s, n1g, n2g, sgu_w_in[j].astype(BF16), sgu_ln_g[j].reshape(1, SGU_W),
                sgu_ln_b[j].reshape(1, SGU_W), mixw, jnp.stack([bias_p, bias_s]),
                sgu_w_out[j].astype(BF16), wr, br)
            v_rows.append(v.reshape(NB_S, L_S, SGU_W))
        else:
            q, k, v, sg = _ret_proj(i, x, mod_p, mod_s, n1g, ret_w_in[j].astype(BF16), cos_tab, sin_tab)
            ng = ret_norm_g[j].reshape(1, VD)
            y, s_p = _ret_prompt(q, k, v, sg, tabs_p, ng)
            y, s_all = _ret_sample(j, q, k, v, sg, tabs_s, ng, state_ret, y, s_all)
            ret_p.append(s_p)
            x, h2, idx, gates, cnt = _ret_out(i, y, ret_w_out[j].astype(BF16), x, mod_p, mod_s, n2g, wr, br)
        x = _moe(i, x, h2, idx, gates, cnt, mod_p, mod_s, moe_w_up, b_up, moe_w_down, b_down,
                 fg, final=(i == DEPTH - 1))
    y_prompt, y_sample = x
    return (y_prompt.reshape(NB_P, L_P, D), y_sample.reshape(NB_S, L_S, D), jnp.stack(ret_p), s_all,
            jnp.stack(v_rows))
```

```python
import functools

import jax
import jax.numpy as jnp
from jax import lax
from jax.experimental import pallas as pl
from jax.experimental.pallas import tpu as pltpu

F32 = jnp.float32
BF16 = jnp.bfloat16
I32 = jnp.int32

D = 1024
NB_P, L_P = 8, 2048
NB_S, L_S = 128, 8
PAST = 16384
DEPTH = 4
T_P = NB_P * L_P
T_S = NB_S * L_S
T = T_P + T_S
SGU_W = 2 * D
SGU_G = 8
SGU_GD = SGU_W // SGU_G
CHUNK = 128
H = 8
DK = D // H
DV = 2 * DK
QD = H * DK
VD = H * DV
RET_IN = 2 * QD + 2 * VD
ROPE_BASE = 10000.0
NE = 32
TOPK = 4
FE = D
ALPHA = 1.702
LIMIT = 7.0
EPS = 1e-6

LANES = 128
SLABS = D // (2 * LANES)
HIGH_HALF = -65536
TD = 512
NPT = T_P // TD
NT = T // TD
TM = 256
BLK = 512
TK = T * TOPK
N_BLOCKS = -(-(TK + NE * (BLK - 1)) // BLK)
N_ROWS = N_BLOCKS * BLK
SB = 16
ROUTE_TILES = 4
SH = 4
RET_CHUNKS = 8
VMEM_LIMIT = 58 * 1024 * 1024


def _cparams(n_axes):
    return pltpu.CompilerParams(dimension_semantics=("arbitrary",) * n_axes,
                                vmem_limit_bytes=VMEM_LIMIT)


def _rms(x, g):
    return (x * lax.rsqrt(jnp.mean(x * x, axis=-1, keepdims=True) + EPS)) * g


def _pick(is_p, p_ref, s_ref):
    return jnp.where(is_p, p_ref[0], s_ref[0])


def _mod_specs(layer, j, tm=TD):
    per_batch, n_prompt = L_P // tm, T_P // tm
    return [
        pl.BlockSpec((1, 1, D), lambda t: (layer * NB_P + jnp.minimum(t // per_batch, NB_P - 1), 0, j)),
        pl.BlockSpec((1, tm, D), lambda t: (layer, jnp.maximum(t - n_prompt, 0), j),
                     pipeline_mode=pl.Buffered(1)),
    ]


def _tile_spec(width, tm=TD):
    return pl.BlockSpec((tm, width), lambda t: (t, 0))


def _full_spec(shape):
    return pl.BlockSpec(shape, lambda *_: (0,) * len(shape), pipeline_mode=pl.Buffered(1))


def _mod_kernel(c_ref, w_ref, b_ref, p_ref, s_ref):
    c = c_ref[...]
    cs = (c * jax.nn.sigmoid(c)).astype(BF16)
    mod = jnp.dot(cs, w_ref[0].astype(BF16), preferred_element_type=F32) + b_ref[0]
    p_ref[0] = mod[:NB_P]
    per_batch = mod[NB_P:]
    s_ref[0] = jnp.broadcast_to(per_batch[:, None, :], (NB_S, L_S, per_batch.shape[-1])).reshape(T_S, -1)


def _modulation(c_all, w_mod, b_mod):
    tn = 1536
    nb = c_all.shape[0]
    return pl.pallas_call(
        _mod_kernel,
        grid=(DEPTH, 6 * D // tn),
        in_specs=[
            pl.BlockSpec((nb, D), lambda l, n: (0, 0)),
            pl.BlockSpec((1, D, tn), lambda l, n: (l, 0, n)),
            pl.BlockSpec((1, 1, tn), lambda l, n: (l, 0, n)),
        ],
        out_specs=[pl.BlockSpec((1, NB_P, tn), lambda l, n: (l, 0, n)),
                   pl.BlockSpec((1, T_S, tn), lambda l, n: (l, 0, n))],
        out_shape=[jax.ShapeDtypeStruct((DEPTH, NB_P, 6 * D), F32),
                   jax.ShapeDtypeStruct((DEPTH, T_S, 6 * D), F32)],
        compiler_params=_cparams(2),
        name="modulation",
    )(c_all, w_mod, b_mod.reshape(DEPTH, 1, 6 * D))


def _tail(is_p, x, y, g1p, g1s, sh2p, sh2s, sc2p, sc2s, n2g, wr, br, xo_ref, h2_ref, idx_ref, gate_ref,
          cnt_ref):
    xn = x + _pick(is_p, g1p, g1s) * y
    xo_ref[...] = xn
    h2 = (_rms(xn, n2g[...]) * (1.0 + _pick(is_p, sc2p, sc2s)) + _pick(is_p, sh2p, sh2s)).astype(BF16)
    h2_ref[...] = h2
    logit = lax.dot_general(wr[...], h2, (((1,), (1,)), ((), ())), preferred_element_type=F32)[:NE] + br[...]
    expert = lax.broadcasted_iota(I32, logit.shape, 0)
    vals, ids = [], []
    for _ in range(TOPK):
        m = jnp.max(logit, axis=0, keepdims=True)
        sel = jnp.min(jnp.where(logit == m, expert, NE), axis=0, keepdims=True)
        vals.append(m)
        ids.append(sel)
        logit = jnp.where(expert == sel, -jnp.inf, logit)
    es = [jnp.exp(v - vals[0]) for v in vals]
    tot = (es[0] + es[1]) + (es[2] + es[3])
    slot = lax.broadcasted_iota(I32, (LANES, logit.shape[1]), 0)
    idx_t = jnp.zeros(slot.shape, F32)
    gate_t = jnp.zeros(slot.shape, F32)
    for k in range(TOPK):
        idx_t = jnp.where(slot == k, ids[k].astype(F32), idx_t)
        gate_t = jnp.where(slot == k, es[k] / tot, gate_t)
    idx_out = idx_t.T.astype(I32)
    idx_ref[...] = idx_out
    gate_ref[...] = gate_t.T
    lane = lax.broadcasted_iota(I32, idx_out.shape, 1)
    chosen = jnp.zeros(idx_out.shape, F32)
    for k in range(TOPK):
        chosen = chosen + jnp.where(lane == idx_out[:, k:k + 1], 1.0, 0.0)

    @pl.when(pl.program_id(0) == 0)
    def _():
        cnt_ref[...] = jnp.zeros(cnt_ref.shape, F32)

    cnt_ref[...] += jnp.sum(chosen, axis=0, keepdims=True)


def _tail_in_specs(layer):
    return (_mod_specs(layer, 2) + _mod_specs(layer, 3) + _mod_specs(layer, 4)
            + [_full_spec((1, D)), _full_spec((LANES, D)), _full_spec((NE, 1))])


def _tail_out_specs():
    return [_tile_spec(D), _tile_spec(D), _tile_spec(LANES), _tile_spec(LANES),
            pl.BlockSpec((1, LANES), lambda t: (0, 0))]


def _tail_out_shapes():
    return [jax.ShapeDtypeStruct((T, D), F32), jax.ShapeDtypeStruct((T, D), BF16),
            jax.ShapeDtypeStruct((T, LANES), I32), jax.ShapeDtypeStruct((T, LANES), F32),
            jax.ShapeDtypeStruct((1, LANES), F32)]


def _sgu_kernel(x_ref, xs_ref, sh1p, sh1s, sc1p, sc1s, n1g, win, lng, lnb, mixw, mixb, wout,
                g1p, g1s, sh2p, sh2s, sc2p, sc2s, n2g, wr, br,
                xo_ref, h2_ref, idx_ref, gate_ref, cnt_ref, v_ref, y_scr, *, split):
    t = pl.program_id(0)
    is_p = t < NPT
    x = jnp.where(is_p, x_ref[...], xs_ref[...]) if split else x_ref[...]
    h = _rms(x, n1g[...]) * (1.0 + _pick(is_p, sc1p, sc1s)) + _pick(is_p, sh1p, sh1s)
    z = jnp.dot(h.astype(BF16), win[...], preferred_element_type=F32)
    z = 0.5 * z * (1.0 + lax.erf(z * (0.5 ** 0.5)))
    u = z[:, :SGU_W]
    v = z[:, SGU_W:]
    vc = v - jnp.mean(v, axis=-1, keepdims=True)
    vn = vc * lax.rsqrt(jnp.mean(vc * vc, axis=-1, keepdims=True) + EPS) * lng[...] + lnb[...]

    @pl.when(t >= NPT)
    def _():
        v_ref[...] = vn

    vb = vn.astype(BF16)
    ri = lax.broadcasted_iota(I32, (CHUNK, CHUNK), 0)
    ci = lax.broadcasted_iota(I32, (CHUNK, CHUNK), 1)
    causal = ci <= ri
    shift = jnp.broadcast_to(jnp.where(is_p, 7, 3), ri.shape)
    keep = causal & (lax.shift_right_logical(ri, shift) == lax.shift_right_logical(ci, shift))
    for g in range(SGU_G):
        wg = jnp.where(keep, mixw[0, g], 0.0).astype(BF16)
        for c in range(TD // CHUNK):
            rows = slice(c * CHUNK, (c + 1) * CHUNK)
            cols = slice(g * SGU_GD, (g + 1) * SGU_GD)
            mixed = jnp.dot(wg, vb[rows, cols], preferred_element_type=F32) + mixb[0, :, cols]
            y_scr[rows, cols] = (u[rows, cols] * mixed).astype(BF16)
    y = jnp.dot(y_scr[...], wout[...], preferred_element_type=F32)
    _tail(is_p, x, y, g1p, g1s, sh2p, sh2s, sc2p, sc2s, n2g, wr, br, xo_ref, h2_ref, idx_ref, gate_ref,
          cnt_ref)


def _sgu_layer(layer, x, mod_p, mod_s, n1g, n2g, win, lng, lnb, mixw, mixb, wout, wr, br):
    sel = lambda t: (jnp.where(t < NPT, 0, 1), 0, 0, 0)
    split = isinstance(x, tuple)
    if split:
        x, xs = x
        x_specs = [pl.BlockSpec((TD, D), lambda t: (jnp.minimum(t, NPT - 1), 0)),
                   pl.BlockSpec((TD, D), lambda t: (jnp.maximum(t - NPT, 0), 0), pipeline_mode=pl.Buffered(1))]
    else:
        xs = x
        x_specs = [_tile_spec(D), pl.BlockSpec((TD, D), lambda t: (0, 0), pipeline_mode=pl.Buffered(1))]
    in_specs = (x_specs + _mod_specs(layer, 0) + _mod_specs(layer, 1)
                + [_full_spec((1, D)), _full_spec((D, 2 * SGU_W)), _full_spec((1, SGU_W)),
                   _full_spec((1, SGU_W)),
                   pl.BlockSpec((1, SGU_G, CHUNK, CHUNK), sel),
                   pl.BlockSpec((1, CHUNK, SGU_W), lambda t: (jnp.where(t < NPT, 0, 1), 0, 0)),
                   _full_spec((SGU_W, D))]
                + _tail_in_specs(layer))
    out_specs = _tail_out_specs() + [pl.BlockSpec((TD, SGU_W), lambda t: (jnp.maximum(t - NPT, 0), 0))]
    out_shape = _tail_out_shapes() + [jax.ShapeDtypeStruct((T_S, SGU_W), F32)]
    return pl.pallas_call(
        functools.partial(_sgu_kernel, split=split), grid=(NT,), in_specs=in_specs, out_specs=out_specs,
        out_shape=out_shape, scratch_shapes=[pltpu.VMEM((TD, SGU_W), BF16)],
        compiler_params=_cparams(1), name="sgu_layer",
    )(x, xs, mod_p, mod_s, mod_p, mod_s, n1g, win, lng, lnb, mixw, mixb, wout,
      mod_p, mod_s, mod_p, mod_s, mod_p, mod_s, n2g, wr, br)


def _ret_proj_kernel(x_ref, sh1p, sh1s, sc1p, sc1s, n1g, win, cos_ref, sin_ref,
                     q_ref, k_ref, v_ref, sg_ref):
    t = pl.program_id(0)
    is_p = t < NPT
    x = x_ref[...]
    h = _rms(x, n1g[...]) * (1.0 + _pick(is_p, sc1p, sc1s)) + _pick(is_p, sh1p, sh1s)
    p = jnp.dot(h.astype(BF16), win[...], preferred_element_type=F32)
    cos = cos_ref[...]
    sin = sin_ref[...]
    for hd in range(H):
        cq = slice(hd * DK, (hd + 1) * DK)
        ck = slice(QD + hd * DK, QD + (hd + 1) * DK)
        qh = p[:, cq]
        kh = p[:, ck]
        q_ref[:, cq] = (qh * cos + pltpu.roll(qh, DK // 2, 1) * sin).astype(BF16)
        k_ref[:, cq] = ((kh * cos + pltpu.roll(kh, DK // 2, 1) * sin) * (DK ** -0.5)).astype(BF16)
    v_ref[...] = p[:, 2 * QD:2 * QD + VD].astype(BF16)
    g = p[:, 2 * QD + VD:]
    sg_ref[...] = (g * jax.nn.sigmoid(g)).astype(BF16)


def _ret_proj(layer, x, mod_p, mod_s, n1g, win, cos_tab, sin_tab):
    in_specs = ([_tile_spec(D)] + _mod_specs(layer, 0) + _mod_specs(layer, 1)
                + [_full_spec((1, D)), _full_spec((D, RET_IN)), _tile_spec(DK), _tile_spec(DK)])
    out_specs = [_tile_spec(QD), _tile_spec(QD), _tile_spec(VD), _tile_spec(VD)]
    out_shape = [jax.ShapeDtypeStruct((T, QD), BF16), jax.ShapeDtypeStruct((T, QD), BF16),
                 jax.ShapeDtypeStruct((T, VD), BF16), jax.ShapeDtypeStruct((T, VD), BF16)]
    return pl.pallas_call(
        _ret_proj_kernel, grid=(NT,), in_specs=in_specs, out_specs=out_specs, out_shape=out_shape,
        compiler_params=_cparams(1), name="ret_proj",
    )(x, mod_p, mod_s, mod_p, mod_s, n1g, win, cos_tab, sin_tab)


def _ret_intra(q, k, v, decay, qdec, kdec):
    s = lax.dot_general(q, k, (((1,), (1,)), ((), ())), preferred_element_type=F32) * decay
    o = jnp.dot(s.astype(BF16), v, preferred_element_type=F32)
    return o, q.astype(F32) * qdec, k.astype(F32) * kdec


def _ret_finish(o, sg, ng):
    on = o * lax.rsqrt(jnp.mean(o * o, axis=-1, keepdims=True) + EPS)
    return (sg.astype(F32) * (on * ng)).astype(BF16)


def _ret_prompt_kernel(gam_ref, q_ref, k_ref, v_ref, sg_ref, dec_ref, qdec_ref, kdec_ref, ng_ref,
                       y_ref, s_ref):
    c = pl.program_id(1)

    @pl.when(c == 0)
    def _():
        s_ref[...] = jnp.zeros(s_ref.shape, F32)

    cks = [slice(hd * DK, (hd + 1) * DK) for hd in range(H)]
    cvs = [slice(hd * DV, (hd + 1) * DV) for hd in range(H)]
    for j in range(RET_CHUNKS):
        rows = slice(j * CHUNK, (j + 1) * CHUNK)
        vs = [v_ref[rows, cv] for cv in cvs]
        intra = [_ret_intra(q_ref[rows, ck], k_ref[rows, ck], v, dec_ref[hd], qdec_ref[:, ck], kdec_ref[:, ck])
                 for hd, (ck, v) in enumerate(zip(cks, vs))]
        olds = [s_ref[0, hd] for hd in range(H)]
        outs = [o + jnp.dot(qd.astype(BF16), s_old.astype(BF16), preferred_element_type=F32)
                for (o, qd, _), s_old in zip(intra, olds)]
        for hd in range(H):
            s_ref[0, hd] = gam_ref[hd] * olds[hd] + lax.dot_general(
                intra[hd][2].astype(BF16), vs[hd], (((0,), (0,)), ((), ())), preferred_element_type=F32)
        for hd, cv in enumerate(cvs):
            y_ref[rows, cv] = _ret_finish(outs[hd], sg_ref[rows, cv], ng_ref[:, cv])


def _ret_prompt(q, k, v, sg, tabs, ng):
    rows = RET_CHUNKS * CHUNK
    nc = L_P // rows
    row = lambda b, c: (b * nc + c, 0)
    smem = pl.BlockSpec(memory_space=pltpu.SMEM)
    in_specs = [smem,
                pl.BlockSpec((rows, QD), row), pl.BlockSpec((rows, QD), row),
                pl.BlockSpec((rows, VD), row), pl.BlockSpec((rows, VD), row),
                _full_spec((H, CHUNK, CHUNK)), _full_spec((CHUNK, QD)), _full_spec((CHUNK, QD)),
                _full_spec((1, VD))]
    out_specs = [pl.BlockSpec((rows, VD), row),
                 pl.BlockSpec((1, H, DK, DV), lambda b, c: (b, 0, 0, 0))]
    out_shape = [jax.ShapeDtypeStruct((T, VD), BF16), jax.ShapeDtypeStruct((NB_P, H, DK, DV), F32)]
    return pl.pallas_call(
        _ret_prompt_kernel, grid=(NB_P, nc), in_specs=in_specs, out_specs=out_specs, out_shape=out_shape,
        compiler_params=_cparams(2), name="ret_prompt",
    )(tabs["gam"], q, k, v, sg, tabs["decay"], tabs["qdec"], tabs["kdec"], ng)


def _ret_sample_kernel(gam_ref, q_ref, k_ref, v_ref, sg_ref, dec_ref, qdec_ref, kdec_ref, ng_ref,
                       s_in_ref, y_in_ref, s_all_ref, y_ref, s_out_ref):
    del y_in_ref, s_all_ref
    s_in_ref = s_in_ref.at[0]
    s_out_ref = s_out_ref.at[0]
    row_batch = lax.broadcasted_iota(I32, (CHUNK, DK), 0) // L_S
    for i in range(SH):
        ck = slice(i * DK, (i + 1) * DK)
        cv = slice(i * DV, (i + 1) * DV)
        v = v_ref[:, cv]
        o, qd, kd = _ret_intra(q_ref[:, ck], k_ref[:, ck], v, dec_ref[i], qdec_ref[:, ck], kdec_ref[:, ck])
        gam = gam_ref[pl.program_id(1) * SH + i]
        cross = [jnp.dot(qd[b * L_S:(b + 1) * L_S].astype(BF16), s_in_ref[b, i].astype(BF16),
                         preferred_element_type=F32) for b in range(SB)]
        for b in range(SB):
            kb = jnp.where(row_batch == b, kd, 0.0).astype(BF16)
            s_out_ref[b, i] = gam * s_in_ref[b, i] + lax.dot_general(
                kb, v, (((0,), (0,)), ((), ())), preferred_element_type=F32)
        o = o + jnp.concatenate(cross, axis=0)
        y_ref[:, cv] = _ret_finish(o, sg_ref[:, cv], ng_ref[:, cv])


def _ret_sample(j, q, k, v, sg, tabs, ng, s_in, y_prev, s_all):
    base = T_P // CHUNK
    rq = lambda g, h: (base + g, h)
    st = pl.BlockSpec((1, SB, SH, DK, DV), lambda g, h: (j, g, h, 0, 0))
    smem = pl.BlockSpec(memory_space=pltpu.SMEM)
    in_specs = [smem,
                pl.BlockSpec((CHUNK, SH * DK), rq), pl.BlockSpec((CHUNK, SH * DK), rq),
                pl.BlockSpec((CHUNK, SH * DV), rq), pl.BlockSpec((CHUNK, SH * DV), rq),
                pl.BlockSpec((SH, CHUNK, CHUNK), lambda g, h: (h, 0, 0)),
                pl.BlockSpec((CHUNK, SH * DK), lambda g, h: (0, h)),
                pl.BlockSpec((CHUNK, SH * DK), lambda g, h: (0, h)),
                pl.BlockSpec((1, SH * DV), lambda g, h: (0, h)),
                st, pl.BlockSpec(memory_space=pl.ANY), pl.BlockSpec(memory_space=pl.ANY)]
    out_specs = [pl.BlockSpec((CHUNK, SH * DV), rq), st]
    out_shape = [jax.ShapeDtypeStruct((T, VD), BF16), jax.ShapeDtypeStruct(s_all.shape, F32)]
    return pl.pallas_call(
        _ret_sample_kernel, grid=(NB_S // SB, H // SH), in_specs=in_specs, out_specs=out_specs,
        out_shape=out_shape, input_output_aliases={10: 0, 11: 1},
        compiler_params=_cparams(2), name="ret_sample",
    )(tabs["gam"], q, k, v, sg, tabs["decay"], tabs["qdec"], tabs["kdec"], ng, s_in, y_prev, s_all)


def _ret_out_kernel(y_ref, wout, x_ref, g1p, g1s, sh2p, sh2s, sc2p, sc2s, n2g, wr, br,
                    xo_ref, h2_ref, idx_ref, gate_ref, cnt_ref):
    is_p = pl.program_id(0) < NPT
    y = jnp.dot(y_ref[...], wout[...], preferred_element_type=F32)
    _tail(is_p, x_ref[...], y, g1p, g1s, sh2p, sh2s, sc2p, sc2s, n2g, wr, br,
          xo_ref, h2_ref, idx_ref, gate_ref, cnt_ref)


def _ret_out(layer, y, wout, x, mod_p, mod_s, n2g, wr, br):
    in_specs = [_tile_spec(VD), _full_spec((VD, D)), _tile_spec(D)] + _tail_in_specs(layer)
    return pl.pallas_call(
        _ret_out_kernel, grid=(NT,), in_specs=in_specs, out_specs=_tail_out_specs(),
        out_shape=_tail_out_shapes(), compiler_params=_cparams(1), name="ret_out",
    )(y, wout, x, mod_p, mod_s, mod_p, mod_s, mod_p, mod_s, n2g, wr, br)


def _lane_prefix(v, lane1):
    s = 1
    while s < LANES:
        v = v + jnp.where(lane1 >= s, pltpu.roll(v, s, 1), 0.0)
        s *= 2
    return v


def _route_kernel(cnt_ref, idx_ref, lp_ref, lpt_ref, runs_ref, meta_ref, base_scr):
    lane = lax.broadcasted_iota(I32, (TM, LANES), 1)
    lane1 = lax.broadcasted_iota(I32, (1, LANES), 1)
    row = lax.broadcasted_iota(I32, (8, LANES), 0)

    @pl.when(pl.program_id(0) == 0)
    def _():
        cnt = cnt_ref[...]
        padded = (((cnt.astype(I32) + (BLK - 1)) // BLK) * BLK).astype(F32)
        end = _lane_prefix(padded, lane1)
        base_scr[...] = end - padded
        meta = jnp.where(row == 0, cnt, jnp.where(row == 1, end - padded, jnp.where(row == 2, end, 0.0)))
        meta_ref[...] = meta.astype(I32)

    ri = lax.broadcasted_iota(I32, (TM, TM), 0)
    ci = lax.broadcasted_iota(I32, (TM, TM), 1)
    before = jnp.where(ci < ri, 1.0, 0.0).astype(BF16)
    base = base_scr[...]
    for j in range(ROUTE_TILES):
        idx = idx_ref[j * TM:(j + 1) * TM, :]
        hits = [lane == idx[:, k:k + 1] for k in range(TOPK)]
        chosen = jnp.zeros((TM, LANES), F32)
        for hk in hits:
            chosen = chosen + jnp.where(hk, 1.0, 0.0)
        colsum = jnp.sum(chosen, axis=0, keepdims=True)
        loff = _lane_prefix(colsum, lane1) - colsum
        pos = jnp.dot(before, chosen.astype(BF16), preferred_element_type=F32) + loff
        lp = jnp.zeros((TM, LANES), F32)
        for k, hk in enumerate(hits):
            lp = jnp.where(lane == k, jnp.sum(jnp.where(hk, pos, 0.0), axis=-1, keepdims=True), lp)
        lp_ref[j * TM:(j + 1) * TM, :] = lp.astype(I32)
        lpt_ref[j * 8:(j + 1) * 8, :] = lp.T[:8].astype(I32)
        runs = jnp.where(row == 0, colsum, jnp.where(row == 1, loff, jnp.where(row == 2, base, 0.0)))
        runs_ref[j] = runs.astype(I32)
        base = base + colsum
    base_scr[...] = base


def _route(cnt, idx):
    n_tiles = idx.shape[0] // TM
    assert n_tiles % ROUTE_TILES == 0
    rt = ROUTE_TILES
    return pl.pallas_call(
        _route_kernel, grid=(n_tiles // rt,),
        in_specs=[pl.BlockSpec((1, LANES), lambda t: (0, 0)), pl.BlockSpec((rt * TM, LANES), lambda t: (t, 0))],
        out_specs=[pl.BlockSpec((rt * TM, LANES), lambda t: (t, 0)),
                   pl.BlockSpec((rt * 8, TM), lambda t: (t, 0)),
                   pl.BlockSpec((rt, 8, LANES), lambda t: (t, 0, 0)),
                   pl.BlockSpec((8, LANES), lambda t: (0, 0))],
        out_shape=[jax.ShapeDtypeStruct((n_tiles * TM, LANES), I32),
                   jax.ShapeDtypeStruct((n_tiles * 8, TM), I32),
                   jax.ShapeDtypeStruct((n_tiles, 8, LANES), I32),
                   jax.ShapeDtypeStruct((8, LANES), I32)],
        scratch_shapes=[pltpu.VMEM((1, LANES), F32)],
        compiler_params=_cparams(1), name="moe_route",
    )(cnt, idx)


def _tile_rows(ref, r, n):
    return ref.at[:, pl.ds(r, n), :]


def _run_copy(src, dst, r_src, r_dst, n, sem, wait):
    off = 0
    for b in reversed(range(TM.bit_length())):
        size = 1 << b
        take = n & size

        def piece(off=off, size=size):
            cp = pltpu.make_async_copy(_tile_rows(src, r_src + off, size),
                                       _tile_rows(dst, r_dst + off, size), sem)
            if wait:
                cp.wait()
            else:
                cp.start()

        if isinstance(n, int):
            if take:
                piece()
        else:
            pl.when(take != 0)(piece)
        off = off + take


def _load_rows(ref3, rows=None):
    rows = ref3.shape[1] if rows is None else rows
    parts = []
    for s in range(SLABS):
        word = ref3[s, :rows, :]
        parts.append(lax.bitcast_convert_type(lax.shift_left(word, 16), F32))
        parts.append(lax.bitcast_convert_type(word & HIGH_HALF, F32))
    return jnp.concatenate(parts, axis=1).astype(BF16)


def _store_rows(ref3, val):
    rows = val.shape[0]
    bits = lax.bitcast_convert_type(val, I32)
    for s in range(SLABS):
        low = lax.shift_right_logical(bits[:, (2 * s) * LANES:(2 * s + 1) * LANES], 16)
        ref3[s, :rows, :] = bits[:, (2 * s + 1) * LANES:(2 * s + 2) * LANES] | low


def _dispatch_kernel(len_ref, pos_ref, row_ref, cnt_ref, first_ref, end_ref, lpt_ref, h2_ref, dst,
                     buf, zbuf, sem, zsem):
    n_rows = dst.shape[1]
    step = pl.program_id(0)
    slot = step % 2
    cur = buf.at[slot]
    lpt = lpt_ref[...]
    p = lax.broadcasted_iota(I32, (TM * TOPK, TM), 0)
    pick = lpt[0:1, :] == p
    for k in range(1, TOPK):
        pick = pick | (lpt[k:k + 1, :] == p)
    perm = jnp.where(pick, 1.0, 0.0).astype(BF16)
    _store_rows(cur, jnp.dot(perm, h2_ref[...], preferred_element_type=F32))

    def per_expert(e, carry):
        _run_copy(cur, dst, pos_ref[e], row_ref[e], len_ref[e], sem.at[slot], False)
        return carry

    lax.fori_loop(0, NE, per_expert, 0, unroll=16)

    def drain(s):
        pltpu.make_async_copy(buf.at[s], _tile_rows(dst, 0, TM * TOPK), sem.at[s]).wait()

    @pl.when(step > 0)
    def _():
        drain(1 - slot)

    @pl.when(step == pl.num_programs(0) - 1)
    def _():
        drain(slot)

    @pl.when(step == 0)
    def _():
        zbuf[...] = jnp.zeros(zbuf.shape, I32)
        n_tail = (n_rows - end_ref[NE - 1]) // TM
        for wait in (False, True):
            def pad(e, carry, wait=wait):
                lo = first_ref[e] + cnt_ref[e]
                _run_copy(zbuf, dst, 0, lo, end_ref[e] - lo, zsem, wait)
                return carry

            lax.fori_loop(0, NE, pad, 0)

            def tail(j, carry, wait=wait):
                _run_copy(zbuf, dst, 0, end_ref[NE - 1] + j * TM, TM, zsem, wait)
                return carry

            lax.fori_loop(0, n_tail, tail, 0)


def _dispatch(h2, lpt, run_len, run_pos, run_row, cnt, first, end, n_rows):
    assert BLK % TM == 0 and BLK < 2 * TM + 1
    n_tiles = h2.shape[0] // TM
    smem = pl.BlockSpec(memory_space=pltpu.SMEM)
    per_tile = pl.BlockSpec((LANES,), lambda t: (t,), memory_space=pltpu.SMEM)
    return pl.pallas_call(
        _dispatch_kernel, grid=(n_tiles,),
        in_specs=[per_tile, per_tile, per_tile, smem, smem, smem,
                  pl.BlockSpec((8, TM), lambda t: (t, 0)), _tile_spec(D, TM)],
        out_specs=pl.BlockSpec(memory_space=pl.ANY),
        out_shape=jax.ShapeDtypeStruct((SLABS, n_rows, LANES), I32),
        scratch_shapes=[pltpu.VMEM((2, SLABS, TM * TOPK, LANES), I32),
                        pltpu.VMEM((SLABS, BLK, LANES), I32),
                        pltpu.SemaphoreType.DMA((2,)), pltpu.SemaphoreType.DMA],
        compiler_params=_cparams(1), name="moe_dispatch",
    )(run_len, run_pos, run_row, cnt, first, end, lpt, h2)


def _expert_kernel(be_ref, nu_ref, nr_ref, nxt_ref, par_ref, xb_ref, wu_hbm, bu_ref, wd_hbm, bd_ref, yb_ref,
                   wuf, wdf, wub, wdb, sem, *, layer):
    i = pl.program_id(0)
    used = i < nu_ref[0]
    half = nr_ref[i] <= BLK // 2
    slot = par_ref[i]

    def weights(e, s):
        return (pltpu.make_async_copy(wu_hbm.at[layer, e], wuf.at[s], sem.at[s, 0]),
                pltpu.make_async_copy(wd_hbm.at[layer, e], wdf.at[s], sem.at[s, 1]))

    @pl.when(i == 0)
    def _():
        for cp in weights(be_ref[0], slot):
            cp.start()

    @pl.when(used & ((i == 0) | (be_ref[i] != be_ref[jnp.maximum(i - 1, 0)])))
    def _():
        for cp in weights(be_ref[i], slot):
            cp.wait()
        wub[...] = wuf[slot].astype(BF16)
        wdb[...] = wdf[slot].astype(BF16)

        @pl.when(nxt_ref[i] >= 0)
        def _():
            for cp in weights(nxt_ref[i], 1 - slot):
                cp.start()

    def swiglu_rows(rows):
        z = jnp.dot(_load_rows(xb_ref, rows), wub[...], preferred_element_type=F32) + bu_ref[0, 0]
        glu = jnp.minimum(z[:, :FE], LIMIT)
        lin = jnp.clip(z[:, FE:], -LIMIT, LIMIT)
        act = glu * jax.nn.sigmoid(ALPHA * glu) * (lin + 1.0)
        y = jnp.dot(act.astype(BF16), wdb[...], preferred_element_type=F32) + bd_ref[0, 0]
        _store_rows(yb_ref, y.astype(BF16).astype(F32))

    @pl.when(used & jnp.logical_not(half))
    def _():
        swiglu_rows(BLK)

    @pl.when(used & half)
    def _():
        swiglu_rows(BLK // 2)
        yb_ref[:, BLK // 2:, :] = jnp.zeros((SLABS, BLK // 2, LANES), I32)

    @pl.when(jnp.logical_not(used))
    def _():
        yb_ref[...] = jnp.zeros(yb_ref.shape, I32)


def _experts(layer, xb, blk_e, n_used, blk_rows, blk_next, blk_par, wu, bu, wd, bd):
    n_blocks = blk_e.shape[0]
    grid_spec = pltpu.PrefetchScalarGridSpec(
        num_scalar_prefetch=5, grid=(n_blocks,),
        in_specs=[pl.BlockSpec((SLABS, BLK, LANES), lambda i, be, nu, *_: (0, jnp.minimum(i, nu[0] - 1), 0)),
                  pl.BlockSpec(memory_space=pl.ANY),
                  pl.BlockSpec((1, 1, 1, 2 * FE), lambda i, be, *_: (layer, be[i], 0, 0)),
                  pl.BlockSpec(memory_space=pl.ANY),
                  pl.BlockSpec((1, 1, 1, D), lambda i, be, *_: (layer, be[i], 0, 0))],
        out_specs=pl.BlockSpec((SLABS, BLK, LANES), lambda i, *_: (0, i, 0)),
        scratch_shapes=[pltpu.VMEM((2, D, 2 * FE), F32), pltpu.VMEM((2, FE, D), F32),
                        pltpu.VMEM((D, 2 * FE), BF16), pltpu.VMEM((FE, D), BF16),
                        pltpu.SemaphoreType.DMA((2, 2))])
    return pl.pallas_call(
        functools.partial(_expert_kernel, layer=layer), grid_spec=grid_spec,
        out_shape=jax.ShapeDtypeStruct((SLABS, n_blocks * BLK, LANES), I32),
        compiler_params=_cparams(1), name="experts",
    )(blk_e, n_used, blk_rows, blk_next, blk_par, xb, wu, bu, wd, bd)


def _combine_kernel(len_ref, pos_ref, row_ref, len_nx, pos_nx, row_nx, yb, lp_ref, x_ref, gate_ref,
                    g2p, g2s, fg_ref, *outs_and_scratch, final):
    *o_ref, ybuf, sem = outs_and_scratch
    step = pl.program_id(0)
    slot = step % 2
    is_p = step < T_P // TM

    def fetch(len_r, pos_r, row_r, s):
        def per_expert(e, carry):
            _run_copy(yb, ybuf.at[s], row_r[e], pos_r[e], len_r[e], sem.at[s], False)
            return carry

        lax.fori_loop(0, NE, per_expert, 0, unroll=16)

    @pl.when(step == 0)
    def _():
        fetch(len_ref, pos_ref, row_ref, slot)

    @pl.when(step + 1 < pl.num_programs(0))
    def _():
        fetch(len_nx, pos_nx, row_nx, 1 - slot)

    cur = ybuf.at[slot]
    pltpu.make_async_copy(_tile_rows(yb, 0, TM * TOPK), cur, sem.at[slot]).wait()

    rows = x_ref.shape[0]
    lp = lp_ref[...]
    gates = gate_ref[...]
    p = lax.broadcasted_iota(I32, (rows, rows * TOPK), 1)
    weights = jnp.zeros((rows, rows * TOPK), F32)
    for k in range(TOPK):
        weights = jnp.where(lp[:, k:k + 1] == p, gates[:, k:k + 1], weights)
    acc = jnp.dot(weights.astype(BF16), _load_rows(cur), preferred_element_type=F32)
    xn = x_ref[...] + _pick(is_p, g2p, g2s) * acc
    if not final:
        o_ref[0][...] = xn
        return
    y = _rms(xn, fg_ref[...])
    op_ref, os_ref = o_ref

    @pl.when(is_p)
    def _():
        op_ref[...] = y

    @pl.when(jnp.logical_not(is_p))
    def _():
        os_ref[...] = y


def _combine(layer, x, yb, lp, run_len, run_pos, run_row, gates, mod_p, mod_s, final_g, final):
    n_tiles = x.shape[0] // TM
    n_prompt = T_P // TM
    if final:
        out_specs = [pl.BlockSpec((TM, D), lambda t: (jnp.minimum(t, n_prompt - 1), 0)),
                     pl.BlockSpec((TM, D), lambda t: (jnp.maximum(t - n_prompt, 0), 0))]
        out_shape = [jax.ShapeDtypeStruct((T_P, D), F32), jax.ShapeDtypeStruct((x.shape[0] - T_P, D), F32)]
    else:
        out_specs, out_shape = _tile_spec(D, TM), jax.ShapeDtypeStruct(x.shape, F32)
    per_tile = pl.BlockSpec((LANES,), lambda t: (t,), memory_space=pltpu.SMEM)
    next_tile = pl.BlockSpec((LANES,), lambda t: (jnp.minimum(t + 1, n_tiles - 1),),
                             memory_space=pltpu.SMEM)
    in_specs = ([per_tile, per_tile, per_tile, next_tile, next_tile, next_tile,
                 pl.BlockSpec(memory_space=pl.ANY),
                 _tile_spec(LANES, TM), _tile_spec(D, TM), _tile_spec(LANES, TM)]
                + _mod_specs(layer, 5, TM) + [_full_spec((1, D))])
    return pl.pallas_call(
        functools.partial(_combine_kernel, final=final), grid=(n_tiles,), in_specs=in_specs,
        out_specs=out_specs, out_shape=out_shape,
        scratch_shapes=[pltpu.VMEM((2, SLABS, TM * TOPK, LANES), I32), pltpu.SemaphoreType.DMA((2,))],
        compiler_params=_cparams(1), name="moe_combine",
    )(run_len, run_pos, run_row, run_len, run_pos, run_row, yb, lp, x, gates, mod_p, mod_s, final_g)


def _block_tables(cnt, first, end, n_blocks):
    experts = jnp.arange(NE, dtype=I32)
    blk_first = jnp.arange(n_blocks, dtype=I32) * BLK
    blk_e = jnp.minimum(jnp.sum((end[None, :] <= blk_first[:, None]).astype(I32), axis=1), NE - 1)
    n_used = end[NE - 1:] // BLK
    has_rows = cnt > 0
    later = has_rows[None, :] & (experts[None, :] > experts[:, None])
    next_e = jnp.min(jnp.where(later, experts[None, :], NE), axis=1)
    next_e = jnp.where(next_e == NE, -1, next_e)
    parity = (jnp.cumsum(has_rows.astype(I32)) - 1) % 2
    own = blk_e[:, None] == experts[None, :]
    of_block = lambda per_expert: jnp.sum(jnp.where(own, per_expert[None, :], 0), axis=1)
    blk_rows = jnp.clip(of_block(first + cnt) - blk_first, 0, BLK)
    return blk_e, n_used, blk_rows, of_block(next_e), of_block(parity)


def _moe(layer, x, h2, idx, gates, cnt_all, mod_p, mod_s, wu, bu, wd, bd, final_g, final):
    lp, lpt, runs, meta = _route(cnt_all, idx)
    cnt, first, end = meta[0, :NE], meta[1, :NE], meta[2, :NE]
    run_len, run_pos, run_row = (runs[:, r, :].reshape(-1) for r in range(3))
    xb = _dispatch(h2, lpt, run_len, run_pos, run_row, cnt, first, end, N_ROWS)
    yb = _experts(layer, xb, *_block_tables(cnt, first, end, N_BLOCKS), wu, bu, wd, bd)
    return _combine(layer, x, yb, lp, run_len, run_pos, run_row, gates, mod_p, mod_s, final_g, final)


def _rope_tables():
    half = DK // 2
    inv = 1.0 / (ROPE_BASE ** jnp.linspace(0.0, 1.0, half, dtype=F32))

    def tab(pos):
        ang = pos.astype(F32)[:, None] * inv[None, :]
        cos, sin = jnp.cos(ang), jnp.sin(ang)
        return jnp.concatenate([cos, cos], -1), jnp.concatenate([-sin, sin], -1)

    cp, sp = tab(jnp.arange(L_P, dtype=I32))
    cs, ss = tab(PAST + jnp.arange(L_S, dtype=I32))
    cos = jnp.concatenate([jnp.tile(cp, (NB_P, 1)), jnp.tile(cs, (NB_S, 1))], 0)
    sin = jnp.concatenate([jnp.tile(sp, (NB_P, 1)), jnp.tile(ss, (NB_S, 1))], 0)
    return cos, sin


def _decay_tables(cl):
    lg = jnp.log(1.0 - 2.0 ** (-5.0 - jnp.arange(H, dtype=F32)))
    r = jnp.arange(CHUNK)
    idx = (r % cl).astype(F32)
    diff = idx[:, None] - idx[None, :]
    same = (r[:, None] // cl) == (r[None, :] // cl)
    decay = jnp.where((same & (diff >= 0))[None],
                      jnp.exp(lg[:, None, None] * jnp.maximum(diff, 0.0)[None]), 0.0)
    qdec = jnp.exp(lg[None, :] * (idx[:, None] + 1.0))
    kdec = jnp.exp(lg[None, :] * (cl - 1.0 - idx[:, None]))
    wide = lambda a: jnp.repeat(a, DK, axis=1)
    return {"decay": decay, "qdec": wide(qdec), "kdec": wide(kdec), "gam": jnp.exp(lg * cl)}


def kernel(x_prompt, x_sample, c_prompt, c_sample, state_ret, w_mod, b_mod, norm1_g, norm2_g,
           sgu_w_in, sgu_ln_g, sgu_ln_b, sgu_w_s, sgu_b_s, sgu_w_out, ret_w_in, ret_norm_g, ret_w_out,
           moe_w_router, moe_b_router, moe_w_up, moe_b_up, moe_w_down, moe_b_down, final_g):
    x = (x_prompt.reshape(T_P, D), x_sample.reshape(T_S, D))
    mod_p, mod_s = _modulation(jnp.concatenate([c_prompt, c_sample], 0), w_mod, b_mod)
    mod_p = mod_p.reshape(DEPTH * NB_P, 1, 6 * D)
    cos_tab, sin_tab = _rope_tables()
    tabs_p = _decay_tables(CHUNK)
    tabs_s = _decay_tables(L_S)
    wr_rows = jnp.pad(jnp.swapaxes(moe_w_router, 1, 2), ((0, 0), (0, LANES - NE), (0, 0))).astype(BF16)
    fg = final_g.reshape(1, D)
    b_up = moe_b_up.reshape(DEPTH, NE, 1, 2 * FE)
    b_down = moe_b_down.reshape(DEPTH, NE, 1, D)

    ret_p, v_rows = [], []
    s_all = lax.empty(state_ret.shape, F32)
    for i in range(DEPTH):
        j = i // 2
        n1g = norm1_g[i].reshape(1, D)
        n2g = norm2_g[i].reshape(1, D)
        wr = wr_rows[i]
        br = moe_b_router[i].reshape(NE, 1)
        if i % 2 == 0:
            mixw = jnp.stack([sgu_w_s[j], jnp.tile(sgu_w_s[j][:, :L_S, :L_S], (1, SB, SB))])
            bias_p = jnp.repeat(sgu_b_s[j].T, SGU_GD, axis=1)
            bias_s = jnp.tile(bias_p[:L_S], (SB, 1))
            x, h2, idx, gates, cnt, v = _sgu_layer(
                i, x, mod_p, mod_s, n1g, n2g, sgu_w_in[j].astype(BF16), sgu_ln_g[j].reshape(1, SGU_W),
                sgu_ln_b[j].reshape(1, SGU_W), mixw, jnp.stack([bias_p, bias_s]),
                sgu_w_out[j].astype(BF16), wr, br)
            v_rows.append(v.reshape(NB_S, L_S, SGU_W))
        else:
            q, k, v, sg = _ret_proj(i, x, mod_p, mod_s, n1g, ret_w_in[j].astype(BF16), cos_tab, sin_tab)
            ng = ret_norm_g[j].reshape(1, VD)
            y, s_p = _ret_prompt(q, k, v, sg, tabs_p, ng)
            y, s_all = _ret_sample(j, q, k, v, sg, tabs_s, ng, state_ret, y, s_all)
            ret_p.append(s_p)
            x, h2, idx, gates, cnt = _ret_out(i, y, ret_w_out[j].astype(BF16), x, mod_p, mod_s, n2g, wr, br)
        x = _moe(i, x, h2, idx, gates, cnt, mod_p, mod_s, moe_w_up, b_up, moe_w_down, b_down,
                 fg, final=(i == DEPTH - 1))
    y_prompt, y_sample = x
    return (y_prompt.reshape(NB_P, L_P, D), y_sample.reshape(NB_S, L_S, D), jnp.stack(ret_p), s_all,
            jnp.stack(v_rows))
```
